```python
import math
import jax, jax.numpy as jnp
from jax import lax
import numpy as np

D_MODEL = 2048
BATCH = 1
SEQ = 16384
DEPTH = 2

EPS = 1e-6
D_MIX = D_MODEL
GDN_HEADS = 8
GDN_DK = 128
GDN_DV = 128
GDN_CONV = 4
GDN_CHUNK = 64
MLA_HEADS = 4
MLA_Q_RANK = 448
MLA_KV_RANK = 128
MLA_NOPE = 128
MLA_ROPE = 64
MLA_DV = 128
MLA_BLOCK = 128
ROPE_THETA = 10000.0
SWA_HEADS = 8
SWA_KV_HEADS = 2
SWA_DH = 64
SWA_WINDOW = 128
D_FF = 5632
FFN_CONV = 3

GDN_QK = GDN_HEADS * GDN_DK
GDN_V = GDN_HEADS * GDN_DV
MLA_OUT = MLA_HEADS * MLA_DV
SWA_OUT = SWA_HEADS * SWA_DH
SWA_KV = SWA_KV_HEADS * SWA_DH
IN_SIZES = (GDN_QK, GDN_QK, GDN_V, GDN_V, GDN_HEADS, GDN_HEADS,
            MLA_Q_RANK, MLA_KV_RANK, MLA_ROPE,
            SWA_OUT, SWA_KV, SWA_KV)
D_IN = int(sum(IN_SIZES))
IN_SPLITS = tuple(int(s) for s in np.cumsum(IN_SIZES)[:-1])
N_MOD = 6

kernel_name = 'hybrid_gdn_mla_swa_parallel_heads'


def rmsnorm(x, w):
    xf = x.astype(jnp.float32)
    y = xf * lax.rsqrt(jnp.mean(xf * xf, axis=-1, keepdims=True) + EPS)
    return (y * w.astype(jnp.float32)).astype(x.dtype)


def l2norm(x):
    xf = x.astype(jnp.float32)
    return xf * lax.rsqrt(jnp.sum(xf * xf, axis=-1, keepdims=True) + EPS)


def causal_dwconv(x, w):
    K = w.shape[0]
    S = x.shape[1]
    xp = jnp.pad(x, ((0, 0), (K - 1, 0), (0, 0)))
    return sum(xp[:, j:j + S] * w[j] for j in range(K))


def rope(x, positions):
    half = x.shape[-1] // 2
    inv = ROPE_THETA ** (-jnp.arange(half, dtype=jnp.float32) / half)
    ang = positions.astype(jnp.float32)[..., None, None] * inv
    cos, sin = jnp.cos(ang), jnp.sin(ang)
    xf = x.astype(jnp.float32)
    x1, x2 = xf[..., :half], xf[..., half:]
    return jnp.concatenate([x1 * cos - x2 * sin, x2 * cos + x1 * sin], axis=-1).astype(x.dtype)


def alibi_slopes(n):
    return 2.0 ** (-8.0 * (jnp.arange(n, dtype=jnp.float32) + 1.0) / n)


def gated_delta_rule_chunked(q, k, v, g, beta):
    B, S, H, DK = q.shape
    DV = v.shape[-1]
    C = GDN_CHUNK
    N = S // C

    def chunks(t):
        t = t.astype(jnp.float32).reshape((B, N, C, H) + t.shape[3:])
        return jnp.moveaxis(t, 3, 1)

    q = chunks(q) * (DK ** -0.5)
    k = chunks(k)
    v = chunks(v)
    beta = chunks(beta)
    gc = jnp.cumsum(chunks(g), axis=-1)
    incl = jnp.tril(jnp.ones((C, C), bool))
    strict = jnp.tril(jnp.ones((C, C), bool), -1)
    diff = gc[..., :, None] - gc[..., None, :]
    decay = jnp.where(incl, jnp.exp(jnp.where(incl, diff, 0.0)), 0.0)
    k_beta = k * beta[..., None]
    kk = jnp.einsum('bhnid,bhnjd->bhnij', k_beta, k)
    a_mat = jnp.eye(C, dtype=jnp.float32) + jnp.where(strict, kk * decay, 0.0)
    u = lax.linalg.triangular_solve(a_mat, v * beta[..., None], left_side=True, lower=True,
                                    unit_diagonal=True)
    w = lax.linalg.triangular_solve(a_mat, k_beta * jnp.exp(gc)[..., None], left_side=True,
                                    lower=True, unit_diagonal=True)
    qk = jnp.einsum('bhnid,bhnjd->bhnij', q, k) * decay
    q_dec = q * jnp.exp(gc)[..., None]
    k_tail = k * jnp.exp(gc[..., -1:] - gc)[..., None]
    g_tot = jnp.exp(gc[..., -1])
    xs = tuple(jnp.moveaxis(t, 2, 0) for t in (u, w, qk, q_dec, k_tail, g_tot))

    def step(state, inp):
        u_n, w_n, qk_n, qd_n, kt_n, gt_n = inp
        v_new = u_n - jnp.einsum('bhck,bhkv->bhcv', w_n, state)
        o_n = (jnp.einsum('bhck,bhkv->bhcv', qd_n, state)
               + jnp.einsum('bhij,bhjv->bhiv', qk_n, v_new))
        state = state * gt_n[..., None, None] + jnp.einsum('bhck,bhcv->bhkv', kt_n, v_new)
        return state, o_n

    s0 = jnp.zeros((B, H, DK, DV), jnp.float32)
    _, o = lax.scan(step, s0, xs)
    return o.transpose(1, 0, 3, 2, 4).reshape(B, S, H, DV)


def gdn_mixer(q, k, v, z, a, b, conv_w, a_log, dt_bias, norm_w):
    B, S, _ = q.shape
    qkv = jax.nn.silu(causal_dwconv(jnp.concatenate([q, k, v], axis=-1), conv_w))
    q, k, v = jnp.split(qkv, (GDN_QK, 2 * GDN_QK), axis=-1)
    q = l2norm(q.reshape(B, S, GDN_HEADS, GDN_DK))
    k = l2norm(k.reshape(B, S, GDN_HEADS, GDN_DK))
    v = v.reshape(B, S, GDN_HEADS, GDN_DV)
    g = -jnp.exp(a_log.astype(jnp.float32)) * jax.nn.softplus(
        a.astype(jnp.float32) + dt_bias.astype(jnp.float32))
    beta = jax.nn.sigmoid(b.astype(jnp.float32))
    o = gated_delta_rule_chunked(q, k, v, g, beta)
    o = rmsnorm(o, norm_w) * jax.nn.silu(z.astype(jnp.float32).reshape(B, S, GDN_HEADS, GDN_DV))
    return o.reshape(B, S, GDN_V).astype(z.dtype)


def mla_mixer(c_q, c_kv, k_rope_raw, positions, q_norm_w, w_uq, kv_norm_w, w_ukv):
    B, S, _ = c_q.shape
    H = MLA_HEADS
    dqk = MLA_NOPE + MLA_ROPE
    q = (rmsnorm(c_q, q_norm_w) @ w_uq).reshape(B, S, H, dqk)
    q = jnp.concatenate([q[..., :MLA_NOPE], rope(q[..., MLA_NOPE:], positions)], axis=-1)
    q = q * (dqk ** -0.5)
    kv = (rmsnorm(c_kv, kv_norm_w) @ w_ukv).reshape(B, S, H, MLA_NOPE + MLA_DV)
    k_nope, v = kv[..., :MLA_NOPE], kv[..., MLA_NOPE:]
    k_pe = rope(k_rope_raw[:, :, None, :], positions)
    k = jnp.concatenate([k_nope, jnp.broadcast_to(k_pe, (B, S, H, MLA_ROPE))], axis=-1)
    nb = S // MLA_BLOCK
    q_blocks = q.reshape(B, nb, MLA_BLOCK, H, dqk).transpose(1, 0, 2, 3, 4)
    key_idx = jnp.arange(S)

    def block(args):
        qb, n = args
        s = jnp.einsum('bqhd,bkhd->bhqk', qb, k).astype(jnp.float32)
        q_idx = n * MLA_BLOCK + jnp.arange(MLA_BLOCK)
        s = jnp.where(key_idx[None, :] <= q_idx[:, None], s, -jnp.inf)
        p = jax.nn.softmax(s, axis=-1).astype(v.dtype)
        return jnp.einsum('bhqk,bkhd->bqhd', p, v)

    out = lax.map(block, (q_blocks, jnp.arange(nb)))
    return out.transpose(1, 0, 2, 3, 4).reshape(B, S, MLA_OUT)


def swa_mixer(q, k, v, sinks):
    B, S, _ = q.shape
    W = SWA_WINDOW
    nb = S // W
    G = SWA_HEADS // SWA_KV_HEADS
    qb = q.reshape(B, nb, W, SWA_KV_HEADS, G, SWA_DH)
    kb = k.reshape(B, nb, W, SWA_KV_HEADS, SWA_DH)
    vb = v.reshape(B, nb, W, SWA_KV_HEADS, SWA_DH)

    def with_prev(t):
        prev = jnp.concatenate([jnp.zeros_like(t[:, :1]), t[:, :-1]], axis=1)
        return jnp.concatenate([prev, t], axis=2)

    kk, vv = with_prev(kb), with_prev(vb)
    s = jnp.einsum('bnqhgd,bnkhd->bnhgqk', qb, kk).astype(jnp.float32) * (SWA_DH ** -0.5)
    qi = jnp.arange(W)[:, None]
    kj = jnp.arange(2 * W)[None, :]
    dist = (qi + W - kj).astype(jnp.float32)
    key_pos = jnp.arange(nb)[:, None] * W - W + kj
    valid = ((dist >= 0) & (dist < W))[None] & (key_pos >= 0)[:, None, :]
    slopes = alibi_slopes(SWA_HEADS).reshape(SWA_KV_HEADS, G)
    s = s - slopes[:, :, None, None] * dist
    s = jnp.where(valid[None, :, None, None], s, -jnp.inf)
    sink = jnp.broadcast_to(sinks.astype(jnp.float32).reshape(SWA_KV_HEADS, G)[:, :, None, None],
                            s.shape[:-1] + (1,))
    p = jax.nn.softmax(jnp.concatenate([s, sink], axis=-1), axis=-1)[..., :-1]
    o = jnp.einsum('bnhgqk,bnkhd->bnqhgd', p.astype(v.dtype), vv)
    return o.reshape(B, S, SWA_OUT)


def setup_inputs(seed: int = 0) -> dict:
    key = jax.random.key(seed)
    ks = jax.random.split(key, 24)
    f32 = jnp.float32
    L = DEPTH

    def nrm(k, shape, scale):
        return jax.random.normal(k, shape, f32) * scale

    def gain(k, shape):
        return 1.0 + 0.05 * jax.random.normal(k, shape, f32)

    x = nrm(ks[0], (BATCH, SEQ, D_MODEL), 1.0)
    c = nrm(ks[1], (BATCH, D_MODEL), 1.0)
    start = jax.random.randint(ks[2], (BATCH, 1), 0, 1024, jnp.int32)
    positions = start + jnp.arange(SEQ, dtype=jnp.int32)[None, :]
    dt = jnp.exp(jax.random.uniform(ks[11], (L, GDN_HEADS), f32, math.log(1e-3), math.log(1e-1)))
    return {
        'x': x,
        'c': c,
        'positions': positions,
        'ada_w': nrm(ks[3], (L, D_MODEL, N_MOD * D_MODEL), 0.5 * D_MODEL ** -0.5),
        'ada_b': nrm(ks[4], (L, N_MOD * D_MODEL), 0.02),
        'mix_pre_norm': gain(ks[5], (L, D_MODEL)),
        'mix_post_norm': gain(ks[6], (L, D_MODEL)),
        'w_in': nrm(ks[7], (L, D_MODEL, D_IN), D_MODEL ** -0.5),
        'w_out': nrm(ks[8], (L, D_MIX, D_MODEL), D_MIX ** -0.5),
        'gdn_conv': nrm(ks[9], (L, GDN_CONV, 2 * GDN_QK + GDN_V), GDN_CONV ** -0.5),
        'gdn_a_log': jnp.log(jax.random.uniform(ks[10], (L, GDN_HEADS), f32, 1.0, 16.0)),
        'gdn_dt_bias': dt + jnp.log(-jnp.expm1(-dt)),
        'gdn_norm': gain(ks[12], (L, GDN_DV)),
        'mla_q_norm': gain(ks[13], (L, MLA_Q_RANK)),
        'mla_w_uq': nrm(ks[14], (L, MLA_Q_RANK, MLA_HEADS * (MLA_NOPE + MLA_ROPE)), MLA_Q_RANK ** -0.5),
        'mla_kv_norm': gain(ks[15], (L, MLA_KV_RANK)),
        'mla_w_ukv': nrm(ks[16], (L, MLA_KV_RANK, MLA_HEADS * (MLA_NOPE + MLA_DV)), MLA_KV_RANK ** -0.5),
        'swa_sinks': nrm(ks[17], (L, SWA_HEADS), 1.0),
        'ffn_pre_norm': gain(ks[18], (L, D_MODEL)),
        'ffn_post_norm': gain(ks[19], (L, D_MODEL)),
        'ffn_w_up': nrm(ks[20], (L, D_MODEL, 2 * D_FF), D_MODEL ** -0.5),
        'ffn_conv': nrm(ks[21], (L, FFN_CONV, 2 * D_FF), FFN_CONV ** -0.5),
        'ffn_conv_b': nrm(ks[22], (L, 2 * D_FF), 0.02),
        'ffn_w_down': nrm(ks[23], (L, D_FF, D_MODEL), D_FF ** -0.5),
    }


def reference(x, c, positions, ada_w, ada_b, mix_pre_norm, mix_post_norm, w_in, w_out,
              gdn_conv, gdn_a_log, gdn_dt_bias, gdn_norm, mla_q_norm, mla_w_uq, mla_kv_norm,
              mla_w_ukv, swa_sinks, ffn_pre_norm, ffn_post_norm, ffn_w_up, ffn_conv, ffn_conv_b,
              ffn_w_down):
    B, S, D = x.shape
    c_act = jax.nn.silu(c)
    for l in range(DEPTH):
        mod = (c_act @ ada_w[l] + ada_b[l]).reshape(B, N_MOD, D)
        shift1, scale1, gate1, shift2, scale2, gate2 = (mod[:, i][:, None, :] for i in range(N_MOD))

        h = rmsnorm(x, mix_pre_norm[l]) * (1.0 + scale1) + shift1
        (a_q, a_k, a_v, a_z, a_a, a_b, b_cq, b_ckv, b_krope,
         c_q, c_k, c_v) = jnp.split(h @ w_in[l], IN_SPLITS, axis=-1)
        o_a = gdn_mixer(a_q, a_k, a_v, a_z, a_a, a_b, gdn_conv[l], gdn_a_log[l], gdn_dt_bias[l],
                        gdn_norm[l])
        o_b = mla_mixer(b_cq, b_ckv, b_krope, positions, mla_q_norm[l], mla_w_uq[l],
                        mla_kv_norm[l], mla_w_ukv[l])
        o_c = swa_mixer(c_q, c_k, c_v, swa_sinks[l])
        mix = jnp.concatenate([o_a, o_b, o_c], axis=-1) @ w_out[l]
        x = x + gate1 * rmsnorm(mix, mix_post_norm[l])

        h = rmsnorm(x, ffn_pre_norm[l]) * (1.0 + scale2) + shift2
        u = causal_dwconv(h @ ffn_w_up[l], ffn_conv[l]) + ffn_conv_b[l]
        gate_br, up = u[..., :D_FF], u[..., D_FF:]
        y = (jax.nn.gelu(gate_br, approximate=True) * up) @ ffn_w_down[l]
        x = x + gate2 * rmsnorm(y, ffn_post_norm[l])
    return x
```

```python
import functools
import math

import numpy as np
import jax
import jax.numpy as jnp
from jax import lax
from jax.experimental import pallas as pl
from jax.experimental.pallas import tpu as pltpu

F32 = jnp.float32
BF16 = jnp.bfloat16

EPS = 1e-6
N_MOD = 6
GDN_HEADS = 8
GDN_DK = 128
GDN_DV = 128
GDN_CONV = 4
GDN_CHUNK = 64
GDN_QK = GDN_HEADS * GDN_DK
GDN_V = GDN_HEADS * GDN_DV
MLA_HEADS = 4
MLA_Q_RANK = 448
MLA_Q_RANK_PAD = 512
MLA_KV_RANK = 128
MLA_NOPE = 128
MLA_ROPE = 64
MLA_DV = 128
MLA_QK_PAD = 256
ROPE_THETA = 10000.0
SWA_HEADS = 8
SWA_KV_HEADS = 2
SWA_GROUP = SWA_HEADS // SWA_KV_HEADS
SWA_DH = 64
SWA_WINDOW = 128
SWA_OUT = SWA_HEADS * SWA_DH
SWA_KV = SWA_KV_HEADS * SWA_DH
FFN_CONV = 3

LANES = 128
GDN_GROUP = 256
VMEM_LIMIT_MB = 56

P_QKV = 0
P_Z = 3072
P_CQ = 4096
P_SWQ = 4608
P_CKV = 5120
P_KROPE = 5248
P_SWK = 5376
P_SWV = 5504
P_WIDTH = 5632


def _params(semantics):
    return pltpu.CompilerParams(dimension_semantics=semantics, vmem_limit_bytes=VMEM_LIMIT_MB << 20)


def _sigmoid(x):
    return 1.0 / (1.0 + jnp.exp(-x))


def _silu(x):
    return x * _sigmoid(x)


def _softplus(x):
    return jnp.maximum(x, 0.0) + jnp.log(1.0 + jnp.exp(-jnp.abs(x)))


def _rms(x):
    return x * lax.rsqrt(jnp.mean(x * x, axis=-1, keepdims=True) + EPS)


def _dot(a, b):
    return jnp.dot(a, b, preferred_element_type=F32)


def _dot_nt(a, b):
    return lax.dot_general(a, b, (((1,), (1,)), ((), ())), preferred_element_type=F32)


def _dot_tn(a, b):
    return lax.dot_general(a, b, (((0,), (0,)), ((), ())), preferred_element_type=F32)


def _mod_kernel(c_ref, w_ref, b_ref, o_ref, *, d, kc):
    cact = _silu(c_ref[...])
    acc = b_ref[0]
    for k0 in range(0, d, kc):
        acc = acc + jnp.sum(w_ref[0, k0:k0 + kc, :] * cact[k0:k0 + kc], axis=0, keepdims=True)
    o_ref[0] = acc


def _adaln_mod(c, ada_w, ada_b):
    depth, d, n = ada_w.shape
    tn = 1024
    return pl.pallas_call(
        functools.partial(_mod_kernel, d=d, kc=256),
        out_shape=jax.ShapeDtypeStruct((depth, 1, n), F32),
        grid=(depth, n // tn),
        in_specs=[pl.BlockSpec((d, 1), lambda l, j: (0, 0)),
                  pl.BlockSpec((1, d, tn), lambda l, j: (l, 0, j)),
                  pl.BlockSpec((1, 1, tn), lambda l, j: (l, 0, j))],
        out_specs=pl.BlockSpec((1, 1, tn), lambda l, j: (l, 0, j)),
        compiler_params=_params(("parallel", "parallel")),
        name="adaln_mod",
    )(c.reshape(d, 1), ada_w, ada_b.reshape(depth, 1, n))


def _norm_mod(x, w, scale, shift):
    return _rms(x) * w * (1.0 + scale) + shift


def _norm_mod_kernel(x_ref, w_ref, sc_ref, sh_ref, h_ref):
    h_ref[...] = _norm_mod(x_ref[...], w_ref[...], sc_ref[...], sh_ref[...]).astype(BF16)


def _pre_norm(x, w, scale, shift):
    s, d = x.shape
    tm = min(512, s)
    row = pl.BlockSpec((1, d), lambda i: (0, 0))
    return pl.pallas_call(
        _norm_mod_kernel,
        out_shape=jax.ShapeDtypeStruct((s, d), BF16),
        grid=(s // tm,),
        in_specs=[pl.BlockSpec((tm, d), lambda i: (i, 0)), row, row, row],
        out_specs=pl.BlockSpec((tm, d), lambda i: (i, 0)),
        compiler_params=_params(("parallel",)),
        name="pre_norm",
    )(x, w, scale, shift)


def _inproj_kernel(h_ref, w_ref, wab_ref, p_ref, ab_ref):
    h = h_ref[...]
    p_ref[...] = _dot(h, w_ref[...]).astype(BF16)

    @pl.when(pl.program_id(1) == 0)
    def _():
        ab_ref[...] = _dot(h, wab_ref[...])


def _in_proj(h, w_p, w_ab):
    s, d = h.shape
    n = w_p.shape[1]
    tm, tn = min(1024, s), 512
    return pl.pallas_call(
        _inproj_kernel,
        out_shape=(jax.ShapeDtypeStruct((s, n), BF16), jax.ShapeDtypeStruct((s, LANES), F32)),
        grid=(s // tm, n // tn),
        in_specs=[pl.BlockSpec((tm, d), lambda i, j: (i, 0)),
                  pl.BlockSpec((d, tn), lambda i, j: (0, j)),
                  pl.BlockSpec((d, LANES), lambda i, j: (0, 0))],
        out_specs=(pl.BlockSpec((tm, tn), lambda i, j: (i, j)),
                   pl.BlockSpec((tm, LANES), lambda i, j: (i, 0))),
        compiler_params=_params(("parallel", "arbitrary")),
        name="in_proj",
    )(h, w_p, w_ab)


def _gdn_prep_kernel(x_ref, halo_ref, cw_ref, ab_ref, alog_ref, dtb_ref, tri_ref,
                     q_ref, k_ref, v_ref, gcb_ref, gct_ref, xp_ref, *, tm):
    i = pl.program_id(0)
    outs = (q_ref, k_ref, v_ref)
    for grp in range(3):
        c0 = grp * GDN_QK
        halo = halo_ref[:, c0:c0 + GDN_QK].astype(F32)
        xp_ref[0:8, :] = jnp.where(i > 0, halo, 0.0)
        xp_ref[8:8 + tm, :] = x_ref[:, c0:c0 + GDN_QK].astype(F32)
        y = cw_ref[3:4, c0:c0 + GDN_QK] * xp_ref[8:8 + tm, :]
        for j in range(GDN_CONV - 1):
            off = 8 - (GDN_CONV - 1) + j
            y = y + cw_ref[j:j + 1, c0:c0 + GDN_QK] * xp_ref[off:off + tm, :]
        y = _silu(y)
        for h in range(GDN_HEADS):
            yh = y[:, h * GDN_DK:(h + 1) * GDN_DK]
            if grp < 2:
                yh = yh * lax.rsqrt(jnp.sum(yh * yh, axis=-1, keepdims=True) + EPS)
            if grp == 0:
                yh = yh * (GDN_DK ** -0.5)
            outs[grp][:, h * GDN_DK:(h + 1) * GDN_DK] = yh.astype(BF16)

    ab = ab_ref[...]
    g = -jnp.exp(alog_ref[...]) * _softplus(ab + dtb_ref[...])
    gc = jnp.dot(tri_ref[...], g, preferred_element_type=F32, precision=lax.Precision.HIGHEST)
    lane = lax.broadcasted_iota(jnp.int32, (1, LANES), 1)
    gcb = jnp.where(lane < GDN_HEADS, gc, _sigmoid(ab))
    gcb_ref[...] = gcb
    gct_ref[...] = gcb.T[0:2 * GDN_HEADS, :]


def _gdn_prep(p, ab, conv_w, a_log, dt_bias):
    s = p.shape[0]
    tm = min(512, s)
    c3 = 3 * GDN_QK
    alog_row = jnp.zeros((1, LANES), F32).at[0, :GDN_HEADS].set(a_log)
    dtb_row = jnp.zeros((1, LANES), F32).at[0, :GDN_HEADS].set(dt_bias)
    r = np.arange(tm)
    tri = jnp.asarray(((r[:, None] >= r[None, :]) & (r[:, None] // GDN_CHUNK == r[None, :] // GDN_CHUNK))
                      .astype(np.float32))
    hb = tm // 8
    row = lambda i: (i, 0)
    return pl.pallas_call(
        functools.partial(_gdn_prep_kernel, tm=tm),
        out_shape=(jax.ShapeDtypeStruct((s, GDN_QK), BF16), jax.ShapeDtypeStruct((s, GDN_QK), BF16),
                   jax.ShapeDtypeStruct((s, GDN_V), BF16), jax.ShapeDtypeStruct((s, LANES), F32),
                   jax.ShapeDtypeStruct((2 * GDN_HEADS, s), F32)),
        grid=(s // tm,),
        in_specs=[pl.BlockSpec((tm, c3), row),
                  pl.BlockSpec((8, c3), lambda i: (jnp.maximum(i * hb - 1, 0), 0)),
                  pl.BlockSpec((GDN_CONV, c3), lambda i: (0, 0)),
                  pl.BlockSpec((tm, LANES), row),
                  pl.BlockSpec((1, LANES), lambda i: (0, 0)),
                  pl.BlockSpec((1, LANES), lambda i: (0, 0)),
                  pl.BlockSpec((tm, tm), lambda i: (0, 0))],
        out_specs=(pl.BlockSpec((tm, GDN_QK), row), pl.BlockSpec((tm, GDN_QK), row),
                   pl.BlockSpec((tm, GDN_V), row), pl.BlockSpec((tm, LANES), row),
                   pl.BlockSpec((2 * GDN_HEADS, tm), lambda i: (0, i))),
        scratch_shapes=[pltpu.VMEM((tm + 8, GDN_QK), F32)],
        compiler_params=_params(("parallel",)),
        name="gdn_prep",
    )(p, p, conv_w, ab, alog_row, dtb_row, tri)


def _gdn_local_kernel(q_ref, k_ref, v_ref, gcb_ref, gct_ref,
                      u_ref, w_ref, qd_ref, kt_ref, qk_ref):
    n = GDN_GROUP
    c = GDN_CHUNK
    ri = lax.broadcasted_iota(jnp.int32, (n, n), 0)
    ci = lax.broadcasted_iota(jnp.int32, (n, n), 1)
    same_chunk = (ri // c) == (ci // c)
    incl = same_chunk & (ri >= ci)
    strict = same_chunk & (ri > ci)
    eye = (ri == ci).astype(F32)
    last_sel = (ci == (ri // c) * c + (c - 1)).astype(F32)
    gcb = gcb_ref[...]
    gc_last_all = jnp.dot(last_sel, gcb, preferred_element_type=F32, precision=lax.Precision.HIGHEST)
    for h in range(GDN_HEADS):
        sl = slice(h * GDN_DK, (h + 1) * GDN_DK)
        q = q_ref[:, sl].astype(F32)
        k = k_ref[:, sl].astype(F32)
        v = v_ref[:, sl].astype(F32)
        gc_col = gcb[:, h:h + 1]
        beta = gcb[:, GDN_HEADS + h:GDN_HEADS + h + 1]
        gc_row = gct_ref[h:h + 1, :]
        gc_last = gc_last_all[:, h:h + 1]
        diff = gc_col - gc_row
        decay = jnp.where(incl, jnp.exp(jnp.where(incl, diff, 0.0)), 0.0)
        kb = k * beta
        k16 = k.astype(BF16)
        kk = _dot_nt(kb.astype(BF16), k16)
        lmat = jnp.where(strict, kk * decay, 0.0)
        t = eye - jnp.where((ri // 2) == (ci // 2), lmat, 0.0)
        b = 2
        while b < c:
            cb = jnp.where(((ri // (2 * b)) == (ci // (2 * b))) & ((ri // b) != (ci // b)), lmat, 0.0)
            t16 = t.astype(BF16)
            t = t - _dot(_dot(t16, cb.astype(BF16)).astype(BF16), t16)
            b *= 2
        eg = jnp.exp(gc_col)
        rhs = jnp.concatenate([v * beta, kb * eg], axis=-1).astype(BF16)
        uw = _dot(t.astype(BF16), rhs)
        u_ref[:, sl] = uw[:, :GDN_DV].astype(BF16)
        w_ref[:, sl] = uw[:, GDN_DV:].astype(BF16)
        qd_ref[:, sl] = (q * eg).astype(BF16)
        kt_ref[:, sl] = (k * jnp.exp(gc_last - gc_col)).astype(BF16)
        qk = _dot_nt(q.astype(BF16), k16) * decay
        qk_c = qk[0:c, 0:c]
        for j in range(1, n // c):
            qk_c = jnp.concatenate([qk_c, qk[j * c:(j + 1) * c, j * c:(j + 1) * c]], axis=0)
        qk_ref[:, h * c:(h + 1) * c] = qk_c.astype(BF16)


def _gdn_local(q, k, v, gcb, gct):
    s = q.shape[0]
    n = GDN_GROUP
    row = lambda i: (i, 0)
    wide = pl.BlockSpec((n, GDN_QK), row)
    return pl.pallas_call(
        _gdn_local_kernel,
        out_shape=(jax.ShapeDtypeStruct((s, GDN_V), BF16), jax.ShapeDtypeStruct((s, GDN_QK), BF16),
                   jax.ShapeDtypeStruct((s, GDN_QK), BF16), jax.ShapeDtypeStruct((s, GDN_QK), BF16),
                   jax.ShapeDtypeStruct((s, GDN_HEADS * GDN_CHUNK), BF16)),
        grid=(s // n,),
        in_specs=[wide, wide, wide, pl.BlockSpec((n, LANES), row),
                  pl.BlockSpec((2 * GDN_HEADS, n), lambda i: (0, i))],
        out_specs=(wide, wide, wide, wide, pl.BlockSpec((n, GDN_HEADS * GDN_CHUNK), row)),
        compiler_params=_params(("parallel",)),
        name="gdn_local",
    )(q, k, v, gcb, gct)


def _gdn_scan_kernel(u_ref, w_ref, qd_ref, kt_ref, qk_ref, gcb_ref, z_ref, nw_ref, o_ref, st_ref, *, rb):
    c = GDN_CHUNK

    @pl.when(pl.program_id(0) == 0)
    def _():
        st_ref[...] = jnp.zeros_like(st_ref)

    nw = nw_ref[...]

    def chunk(ci, carry):
        r0 = pl.multiple_of(ci * c, c)
        rows = pl.ds(r0, c)
        gt_row = jnp.exp(gcb_ref[pl.ds(r0 + c - 1, 1), :])
        for h in range(GDN_HEADS):
            sl = slice(h * GDN_DK, (h + 1) * GDN_DK)
            st = st_ref[h]
            st16 = st.astype(BF16)
            wq = jnp.concatenate([w_ref[rows, sl], qd_ref[rows, sl]], axis=0)
            r1 = _dot(wq, st16)
            v_new = u_ref[rows, sl].astype(F32) - r1[0:c]
            vn16 = v_new.astype(BF16)
            o = r1[c:2 * c] + _dot(qk_ref[rows, h * c:(h + 1) * c], vn16)
            st_ref[h] = st * gt_row[:, h:h + 1] + _dot_tn(kt_ref[rows, sl], vn16)
            o = _rms(o) * nw * _silu(z_ref[rows, sl].astype(F32))
            o_ref[rows, sl] = o.astype(BF16)
        return carry

    lax.fori_loop(0, rb // c, chunk, 0)


def _gdn_scan(u, w, qd, kt, qk, gcb, p, norm_w):
    s = u.shape[0]
    rb = min(512, s)
    row = lambda i: (i, 0)
    wide = pl.BlockSpec((rb, GDN_QK), row)
    return pl.pallas_call(
        functools.partial(_gdn_scan_kernel, rb=rb),
        out_shape=jax.ShapeDtypeStruct((s, GDN_V), BF16),
        grid=(s // rb,),
        in_specs=[wide, wide, wide, wide, pl.BlockSpec((rb, GDN_HEADS * GDN_CHUNK), row),
                  pl.BlockSpec((rb, LANES), row),
                  pl.BlockSpec((rb, GDN_V), lambda i: (i, P_Z // GDN_V)),
                  pl.BlockSpec((1, GDN_DV), lambda i: (0, 0))],
        out_specs=wide,
        scratch_shapes=[pltpu.VMEM((GDN_HEADS, GDN_DK, GDN_DV), F32)],
        compiler_params=_params(("arbitrary",)),
        name="gdn_scan",
    )(u, w, qd, kt, qk, gcb, p, norm_w.reshape(1, GDN_DV))


def _rope_table_kernel(pos_ref, inv_ref, cos_ref, sin_ref):
    ang = pos_ref[...].astype(F32) * inv_ref[...]
    lane = lax.broadcasted_iota(jnp.int32, ang.shape, 1)
    first_half = (lane % MLA_ROPE) < (MLA_ROPE // 2)
    cos_ref[...] = jnp.cos(ang)
    sin_ref[...] = jnp.where(first_half, -jnp.sin(ang), jnp.sin(ang))


def _rope_tables(positions):
    s = positions.shape[-1]
    tm = min(1024, s)
    half = MLA_ROPE // 2
    inv = (ROPE_THETA ** (-np.arange(half, dtype=np.float32) / half)).astype(np.float32)
    inv_row = jnp.asarray(np.tile(inv, LANES // half).reshape(1, LANES))
    return pl.pallas_call(
        _rope_table_kernel,
        out_shape=(jax.ShapeDtypeStruct((s, LANES), F32), jax.ShapeDtypeStruct((s, LANES), F32)),
        grid=(s // tm,),
        in_specs=[pl.BlockSpec((tm, 1), lambda i: (i, 0)), pl.BlockSpec((1, LANES), lambda i: (0, 0))],
        out_specs=(pl.BlockSpec((tm, LANES), lambda i: (i, 0)), pl.BlockSpec((tm, LANES), lambda i: (i, 0))),
        compiler_params=_params(("parallel",)),
        name="rope_tables",
    )(positions.reshape(s, 1), inv_row)


def _rope_apply(x, cos, sin_signed):
    width = x.shape[-1]
    half = MLA_ROPE // 2
    lane = lax.broadcasted_iota(jnp.int32, x.shape, 1)
    first_half = (lane % MLA_ROPE) < half
    swapped = jnp.where(first_half, pltpu.roll(x, width - half, 1), pltpu.roll(x, half, 1))
    return x * cos + swapped * sin_signed


def _mla_proj_kernel(cq_ref, ckv_ref, kr_ref, cos_ref, sin_ref, qn_ref, wuq_ref, kvn_ref, wukv_ref,
                     q_ref, k_ref, v_ref):
    cos = cos_ref[...]
    sin = sin_ref[...]
    scale = (MLA_NOPE + MLA_ROPE) ** -0.5
    cq = cq_ref[...].astype(F32)
    cqn = cq * lax.rsqrt(jnp.sum(cq * cq, axis=-1, keepdims=True) * (1.0 / MLA_Q_RANK) + EPS) * qn_ref[...]
    q = _dot(cqn.astype(BF16), wuq_ref[...])
    nope_w = MLA_HEADS * MLA_NOPE
    q_rope = _rope_apply(q[:, nope_w:], jnp.concatenate([cos, cos], -1), jnp.concatenate([sin, sin], -1))
    zeros = jnp.zeros((q.shape[0], MLA_QK_PAD - MLA_NOPE - MLA_ROPE), F32)
    ckv = ckv_ref[...].astype(F32)
    kv = _dot((_rms(ckv) * kvn_ref[...]).astype(BF16), wukv_ref[...])
    k_pe = _rope_apply(kr_ref[...].astype(F32), cos, sin)[:, :MLA_ROPE]
    for h in range(MLA_HEADS):
        qh = jnp.concatenate([q[:, h * MLA_NOPE:(h + 1) * MLA_NOPE],
                              q_rope[:, h * MLA_ROPE:(h + 1) * MLA_ROPE], zeros], axis=-1) * scale
        q_ref[h] = qh.astype(BF16)
        kh = jnp.concatenate([kv[:, h * MLA_NOPE:(h + 1) * MLA_NOPE], k_pe, zeros], axis=-1)
        k_ref[h] = kh.astype(BF16)
        v_ref[h] = kv[:, nope_w + h * MLA_DV:nope_w + (h + 1) * MLA_DV].astype(BF16)


def _mla_proj(p, cos, sin, q_norm, w_uq, kv_norm, w_ukv):
    s = p.shape[0]
    tm = min(512, s)
    const = lambda i: (0, 0)
    return pl.pallas_call(
        _mla_proj_kernel,
        out_shape=(jax.ShapeDtypeStruct((MLA_HEADS, s, MLA_QK_PAD), BF16),
                   jax.ShapeDtypeStruct((MLA_HEADS, s, MLA_QK_PAD), BF16),
                   jax.ShapeDtypeStruct((MLA_HEADS, s, MLA_DV), BF16)),
        grid=(s // tm,),
        in_specs=[pl.BlockSpec((tm, MLA_Q_RANK_PAD), lambda i: (i, P_CQ // MLA_Q_RANK_PAD)),
                  pl.BlockSpec((tm, LANES), lambda i: (i, P_CKV // LANES)),
                  pl.BlockSpec((tm, LANES), lambda i: (i, P_KROPE // LANES)),
                  pl.BlockSpec((tm, LANES), lambda i: (i, 0)),
                  pl.BlockSpec((tm, LANES), lambda i: (i, 0)),
                  pl.BlockSpec(q_norm.shape, const), pl.BlockSpec(w_uq.shape, const),
                  pl.BlockSpec(kv_norm.shape, const), pl.BlockSpec(w_ukv.shape, const)],
        out_specs=(pl.BlockSpec((MLA_HEADS, tm, MLA_QK_PAD), lambda i: (0, i, 0)),
                   pl.BlockSpec((MLA_HEADS, tm, MLA_QK_PAD), lambda i: (0, i, 0)),
                   pl.BlockSpec((MLA_HEADS, tm, MLA_DV), lambda i: (0, i, 0))),
        compiler_params=_params(("parallel",)),
        name="mla_proj",
    )(p, p, p, cos, sin, q_norm, w_uq, kv_norm, w_ukv)


def _mla_attn_kernel(q_ref, k_ref, v_ref, o_ref, *, bq, bk):
    qi = pl.program_id(1)
    q = q_ref[0]

    def step(kb, carry, masked):
        m, l, acc = carry
        r0 = pl.multiple_of(kb * bk, bk)
        s = _dot_nt(q, k_ref[0, pl.ds(r0, bk), :])
        if masked:
            qpos = qi * bq + lax.broadcasted_iota(jnp.int32, s.shape, 0)
            kpos = kb * bk + lax.broadcasted_iota(jnp.int32, s.shape, 1)
            s = jnp.where(kpos <= qpos, s, -jnp.inf)
        m_new = jnp.maximum(m, jnp.max(s, axis=-1, keepdims=True))
        alpha = jnp.exp(m - m_new)
        pexp = jnp.exp(s - m_new)
        l = alpha * l + jnp.sum(pexp, axis=-1, keepdims=True)
        acc = alpha * acc + _dot(pexp.astype(BF16), v_ref[0, pl.ds(r0, bk), :])
        return m_new, l, acc

    init = (jnp.full((bq, 1), -jnp.inf, F32), jnp.zeros((bq, 1), F32), jnp.zeros((bq, MLA_DV), F32))
    n_full = (qi * bq) // bk
    carry = lax.fori_loop(0, n_full, lambda kb, cr: step(kb, cr, False), init)
    for d in range(bq // bk):
        carry = step(n_full + d, carry, True)
    m, l, acc = carry
    o_ref[...] = (acc / l).astype(BF16)


def _mla_attn(q, k, v):
    nh, s, _ = q.shape
    bq = min(512, s)
    bk = min(512, s)
    return pl.pallas_call(
        functools.partial(_mla_attn_kernel, bq=bq, bk=bk),
        out_shape=jax.ShapeDtypeStruct((s, nh * MLA_DV), BF16),
        grid=(nh, s // bq),
        in_specs=[pl.BlockSpec((1, bq, MLA_QK_PAD), lambda h, i: (h, i, 0)),
                  pl.BlockSpec((1, s, MLA_QK_PAD), lambda h, i: (h, 0, 0)),
                  pl.BlockSpec((1, s, MLA_DV), lambda h, i: (h, 0, 0))],
        out_specs=pl.BlockSpec((bq, MLA_DV), lambda h, i: (i, h)),
        compiler_params=_params(("parallel", "arbitrary")),
        name="mla_attn",
    )(q, k, v)


def _swa_kernel(q_ref, k_ref, v_ref, kh_ref, vh_ref, sink_ref, slope_ref, o_ref, *, nblk):
    w = SWA_WINDOW
    g = SWA_GROUP
    i = pl.program_id(0)
    qi = lax.broadcasted_iota(jnp.int32, (w, 2 * w), 0)
    kj = lax.broadcasted_iota(jnp.int32, (w, 2 * w), 1)
    dist = qi + w - kj
    band = (dist >= 0) & (dist < w)
    dist_f = dist.astype(F32)
    for blk in range(nblk):
        rows = slice(blk * w, (blk + 1) * w)
        if blk == 0:
            k_prev, v_prev = kh_ref[...], vh_ref[...]
            valid = band & ((kj >= w) | (i > 0))
        else:
            k_prev, v_prev = k_ref[(blk - 1) * w:blk * w, :], v_ref[(blk - 1) * w:blk * w, :]
            valid = band
        kk = jnp.concatenate([k_prev, k_ref[rows, :]], axis=0)
        vv = jnp.concatenate([v_prev, v_ref[rows, :]], axis=0)
        outs = []
        for hk in range(SWA_KV_HEADS):
            kh = kk[:, hk * SWA_DH:(hk + 1) * SWA_DH]
            vh = vv[:, hk * SWA_DH:(hk + 1) * SWA_DH]
            for gi in range(g):
                hq = hk * g + gi
                qh = q_ref[rows, hq * SWA_DH:(hq + 1) * SWA_DH]
                s = _dot_nt(qh, kh) * (SWA_DH ** -0.5) - slope_ref[:, hq:hq + 1] * dist_f
                s = jnp.where(valid, s, -jnp.inf)
                sink = sink_ref[:, hq:hq + 1]
                m = jnp.maximum(jnp.max(s, axis=-1, keepdims=True), sink)
                pexp = jnp.exp(s - m)
                denom = jnp.sum(pexp, axis=-1, keepdims=True) + jnp.exp(sink - m)
                outs.append(_dot((pexp / denom).astype(BF16), vh))
        o_ref[rows, :] = jnp.concatenate(outs, axis=-1).astype(BF16)


def _swa(p, sinks):
    s = p.shape[0]
    w = SWA_WINDOW
    rb = min(512, s)
    nblk = rb // w
    slopes = (2.0 ** (-8.0 * (np.arange(SWA_HEADS, dtype=np.float32) + 1.0) / SWA_HEADS)).astype(np.float32)
    slope_row = jnp.asarray(np.pad(slopes, (0, LANES - SWA_HEADS)).reshape(1, LANES))
    sink_row = jnp.zeros((1, LANES), F32).at[0, :SWA_HEADS].set(sinks)
    prev = lambda col: (lambda i: (jnp.maximum(i * nblk - 1, 0), col))
    return pl.pallas_call(
        functools.partial(_swa_kernel, nblk=nblk),
        out_shape=jax.ShapeDtypeStruct((s, SWA_OUT), BF16),
        grid=(s // rb,),
        in_specs=[pl.BlockSpec((rb, SWA_OUT), lambda i: (i, P_SWQ // SWA_OUT)),
                  pl.BlockSpec((rb, SWA_KV), lambda i: (i, P_SWK // SWA_KV)),
                  pl.BlockSpec((rb, SWA_KV), lambda i: (i, P_SWV // SWA_KV)),
                  pl.BlockSpec((w, SWA_KV), prev(P_SWK // SWA_KV)),
                  pl.BlockSpec((w, SWA_KV), prev(P_SWV // SWA_KV)),
                  pl.BlockSpec((1, LANES), lambda i: (0, 0)),
                  pl.BlockSpec((1, LANES), lambda i: (0, 0))],
        out_specs=pl.BlockSpec((rb, SWA_OUT), lambda i: (i, 0)),
        compiler_params=_params(("parallel",)),
        name="swa",
    )(p, p, p, p, p, sink_row, slope_row)


def _outproj_kernel(oa_ref, ob_ref, oc_ref, w_ref, x_ref, pn_ref, g1_ref, fn_ref, sc_ref, sh_ref,
                    x1_ref, h2_ref):
    a_w, b_w = oa_ref.shape[1], ob_ref.shape[1]
    mix = (_dot(oa_ref[...], w_ref[0:a_w, :]) + _dot(ob_ref[...], w_ref[a_w:a_w + b_w, :])
           + _dot(oc_ref[...], w_ref[a_w + b_w:, :]))
    x1 = x_ref[...] + g1_ref[...] * (_rms(mix) * pn_ref[...])
    x1_ref[...] = x1
    h2_ref[...] = _norm_mod(x1, fn_ref[...], sc_ref[...], sh_ref[...]).astype(BF16)


def _out_proj(o_a, o_b, o_c, w_out, x, post_norm, gate1, ffn_norm, scale2, shift2):
    s, d = x.shape
    tm = min(256, s)
    row = lambda i: (i, 0)
    vec = pl.BlockSpec((1, d), lambda i: (0, 0))
    return pl.pallas_call(
        _outproj_kernel,
        out_shape=(jax.ShapeDtypeStruct((s, d), F32), jax.ShapeDtypeStruct((s, d), BF16)),
        grid=(s // tm,),
        in_specs=[pl.BlockSpec((tm, o_a.shape[1]), row), pl.BlockSpec((tm, o_b.shape[1]), row),
                  pl.BlockSpec((tm, o_c.shape[1]), row), pl.BlockSpec(w_out.shape, lambda i: (0, 0)),
                  pl.BlockSpec((tm, d), row), vec, vec, vec, vec, vec],
        out_specs=(pl.BlockSpec((tm, d), row), pl.BlockSpec((tm, d), row)),
        compiler_params=_params(("parallel",)),
        name="out_proj",
    )(o_a, o_b, o_c, w_out, x, post_norm, gate1, ffn_norm, scale2, shift2)


def _gelu_tanh(x):
    return 0.5 * x * (1.0 + jnp.tanh(math.sqrt(2.0 / math.pi) * (x + 0.044715 * (x * x * x))))


def _ffn_up_kernel(h_ref, wg_ref, wu_ref, cg_ref, cu_ref, bg_ref, bu_ref, o_ref, xg_ref, xu_ref, *, tm):
    @pl.when(pl.program_id(1) == 0)
    def _():
        xg_ref[0:8, :] = jnp.zeros((8, xg_ref.shape[1]), F32)
        xu_ref[0:8, :] = jnp.zeros((8, xu_ref.shape[1]), F32)

    h = h_ref[...]

    def conv(x_ref, w_ref, cw_ref, b_ref):
        x_ref[8:8 + tm, :] = _dot(h, w_ref[...])
        y = b_ref[...] + cw_ref[2:3, :] * x_ref[8:8 + tm, :]
        y = y + cw_ref[1:2, :] * x_ref[7:7 + tm, :]
        y = y + cw_ref[0:1, :] * x_ref[6:6 + tm, :]
        x_ref[0:8, :] = x_ref[tm:tm + 8, :]
        return y

    gate = conv(xg_ref, wg_ref, cg_ref, bg_ref)
    up = conv(xu_ref, wu_ref, cu_ref, bu_ref)
    o_ref[...] = (_gelu_tanh(gate) * up).astype(BF16)


def _ffn_up(h2, w_up, conv_w, conv_b):
    s, d = h2.shape
    d_ff = w_up.shape[1] // 2
    tm, tn = min(1024, s), 512
    nj = d_ff // tn
    lo = lambda j, i: (0, j)
    hi = lambda j, i: (0, j + nj)
    return pl.pallas_call(
        functools.partial(_ffn_up_kernel, tm=tm),
        out_shape=jax.ShapeDtypeStruct((s, d_ff), BF16),
        grid=(nj, s // tm),
        in_specs=[pl.BlockSpec((tm, d), lambda j, i: (i, 0)),
                  pl.BlockSpec((d, tn), lo), pl.BlockSpec((d, tn), hi),
                  pl.BlockSpec((FFN_CONV, tn), lo), pl.BlockSpec((FFN_CONV, tn), hi),
                  pl.BlockSpec((1, tn), lo), pl.BlockSpec((1, tn), hi)],
        out_specs=pl.BlockSpec((tm, tn), lambda j, i: (i, j)),
        scratch_shapes=[pltpu.VMEM((tm + 8, tn), F32), pltpu.VMEM((tm + 8, tn), F32)],
        compiler_params=_params(("parallel", "arbitrary")),
        name="ffn_up",
    )(h2, w_up, w_up, conv_w, conv_w, conv_b, conv_b)


def _ffn_down_kernel(g_ref, w_ref, x_ref, pn_ref, g2_ref, nn_ref, sc_ref, sh_ref, x2_ref, hn_ref, acc_ref):
    kk = pl.program_id(1)

    @pl.when(kk == 0)
    def _():
        acc_ref[...] = jnp.zeros_like(acc_ref)

    acc_ref[...] += _dot(g_ref[...], w_ref[...])

    @pl.when(kk == pl.num_programs(1) - 1)
    def _():
        x2 = x_ref[...] + g2_ref[...] * (_rms(acc_ref[...]) * pn_ref[...])
        x2_ref[...] = x2
        hn_ref[...] = _norm_mod(x2, nn_ref[...], sc_ref[...], sh_ref[...]).astype(BF16)


def _ffn_down(g, w_down, x1, post_norm, gate2, next_norm, next_scale, next_shift):
    s, d = x1.shape
    d_ff = g.shape[1]
    tm, tk = min(512, s), 512
    row = lambda i, k: (i, 0)
    vec = pl.BlockSpec((1, d), lambda i, k: (0, 0))
    return pl.pallas_call(
        _ffn_down_kernel,
        out_shape=(jax.ShapeDtypeStruct((s, d), F32), jax.ShapeDtypeStruct((s, d), BF16)),
        grid=(s // tm, d_ff // tk),
        in_specs=[pl.BlockSpec((tm, tk), lambda i, k: (i, k)), pl.BlockSpec((tk, d), lambda i, k: (k, 0)),
                  pl.BlockSpec((tm, d), row), vec, vec, vec, vec, vec],
        out_specs=(pl.BlockSpec((tm, d), row), pl.BlockSpec((tm, d), row)),
        scratch_shapes=[pltpu.VMEM((tm, d), F32)],
        compiler_params=_params(("parallel", "arbitrary")),
        name="ffn_down",
    )(g, w_down, x1, post_norm, gate2, next_norm, next_scale, next_shift)


def _layout_w_in(w_in):
    d = w_in.shape[0]
    sizes = (GDN_QK, GDN_QK, GDN_V, GDN_V, GDN_HEADS, GDN_HEADS, MLA_Q_RANK, MLA_KV_RANK, MLA_ROPE,
             SWA_OUT, SWA_KV, SWA_KV)
    offs = np.concatenate([[0], np.cumsum(sizes)])
    part = lambda n: w_in[:, offs[n]:offs[n + 1]]
    zeros = lambda n: jnp.zeros((d, n), w_in.dtype)
    w_p = jnp.concatenate([part(0), part(1), part(2), part(3),
                           part(6), zeros(MLA_Q_RANK_PAD - MLA_Q_RANK),
                           part(9), part(7), part(8), zeros(LANES - MLA_ROPE), part(10), part(11)],
                          axis=1).astype(BF16)
    w_ab = jnp.concatenate([part(4), part(5), zeros(LANES - 2 * GDN_HEADS)], axis=1).astype(BF16)
    return w_p, w_ab


def _layout_mla(q_norm, w_uq, kv_norm, w_ukv):
    dqk = MLA_NOPE + MLA_ROPE
    uq = w_uq.reshape(MLA_Q_RANK, MLA_HEADS, dqk)
    uq = jnp.concatenate([uq[:, :, :MLA_NOPE].reshape(MLA_Q_RANK, -1), uq[:, :, MLA_NOPE:].reshape(MLA_Q_RANK, -1)],
                         axis=1)
    uq = jnp.pad(uq, ((0, MLA_Q_RANK_PAD - MLA_Q_RANK), (0, 0))).astype(BF16)
    qn = jnp.pad(q_norm, (0, MLA_Q_RANK_PAD - MLA_Q_RANK)).reshape(1, MLA_Q_RANK_PAD)
    ukv = w_ukv.reshape(MLA_KV_RANK, MLA_HEADS, MLA_NOPE + MLA_DV)
    ukv = jnp.concatenate([ukv[:, :, :MLA_NOPE].reshape(MLA_KV_RANK, -1),
                           ukv[:, :, MLA_NOPE:].reshape(MLA_KV_RANK, -1)], axis=1).astype(BF16)
    return qn, uq, kv_norm.reshape(1, MLA_KV_RANK), ukv


def kernel(x, c, positions, ada_w, ada_b, mix_pre_norm, mix_post_norm, w_in, w_out, gdn_conv, gdn_a_log, gdn_dt_bias, gdn_norm, mla_q_norm, mla_w_uq, mla_kv_norm, mla_w_ukv, swa_sinks, ffn_pre_norm, ffn_post_norm, ffn_w_up, ffn_conv, ffn_conv_b, ffn_w_down):
    batch, s, d = x.shape
    assert batch == 1, "kernels are written for a single sequence"
    depth = ada_w.shape[0]
    xs = x.reshape(s, d)
    mod = _adaln_mod(c, ada_w, ada_b).reshape(depth, N_MOD, 1, d)
    cos, sin = _rope_tables(positions)
    vec = lambda a: a.reshape(1, d)

    h = _pre_norm(xs, vec(mix_pre_norm[0]), mod[0, 1], mod[0, 0])
    for l in range(depth):
        shift1, scale1, gate1, shift2, scale2, gate2 = (mod[l, n] for n in range(N_MOD))
        w_p, w_ab = _layout_w_in(w_in[l])
        p, ab = _in_proj(h, w_p, w_ab)

        q_a, k_a, v_a, gcb, gct = _gdn_prep(p, ab, gdn_conv[l], gdn_a_log[l], gdn_dt_bias[l])
        u, w, qd, kt, qk = _gdn_local(q_a, k_a, v_a, gcb, gct)
        o_a = _gdn_scan(u, w, qd, kt, qk, gcb, p, gdn_norm[l])

        q_b, k_b, v_b = _mla_proj(p, cos, sin, *_layout_mla(mla_q_norm[l], mla_w_uq[l], mla_kv_norm[l],
                                                            mla_w_ukv[l]))
        o_b = _mla_attn(q_b, k_b, v_b)

        o_c = _swa(p, swa_sinks[l])

        x1, h2 = _out_proj(o_a, o_b, o_c, w_out[l].astype(BF16), xs, vec(mix_post_norm[l]), gate1,
                           vec(ffn_pre_norm[l]), scale2, shift2)
        g = _ffn_up(h2, ffn_w_up[l].astype(BF16), ffn_conv[l], ffn_conv_b[l].reshape(1, -1))
        nl = min(l + 1, depth - 1)
        xs, h = _ffn_down(g, ffn_w_down[l].astype(BF16), x1, vec(ffn_post_norm[l]), gate2,
                          vec(mix_pre_norm[nl]), mod[nl, 1], mod[nl, 0])
    return xs.reshape(batch, s, d)
```

```python
import functools
import math

import numpy as np
import jax
import jax.numpy as jnp
from jax import lax
from jax.experimental import pallas as pl
from jax.experimental.pallas import tpu as pltpu

F32 = jnp.float32
BF16 = jnp.bfloat16

EPS = 1e-6
N_MOD = 6
GDN_HEADS = 8
GDN_DK = 128
GDN_DV = 128
GDN_CONV = 4
GDN_CHUNK = 64
GDN_QK = GDN_HEADS * GDN_DK
GDN_V = GDN_HEADS * GDN_DV
MLA_HEADS = 4
MLA_Q_RANK = 448
MLA_Q_RANK_PAD = 512
MLA_KV_RANK = 128
MLA_NOPE = 128
MLA_ROPE = 64
MLA_DV = 128
MLA_QK_PAD = 256
MLA_BLOCK = 512
ROPE_THETA = 10000.0
SWA_HEADS = 8
SWA_KV_HEADS = 2
SWA_GROUP = SWA_HEADS // SWA_KV_HEADS
SWA_DH = 64
SWA_WINDOW = 128
SWA_OUT = SWA_HEADS * SWA_DH
SWA_KV = SWA_KV_HEADS * SWA_DH
FFN_CONV = 3

LANES = 128
GDN_GROUP = 256
VMEM_LIMIT_MB = 56

P_QKV = 0
P_Z = 3072
P_CQ = 4096
P_SWQ = 4608
P_CKV = 5120
P_KROPE = 5248
P_SWK = 5376
P_SWV = 5504
P_WIDTH = 5632


def _params(semantics):
    return pltpu.CompilerParams(dimension_semantics=semantics, vmem_limit_bytes=VMEM_LIMIT_MB << 20)


def _sigmoid(x):
    return 1.0 / (1.0 + jnp.exp(-x))


def _silu(x):
    return x * _sigmoid(x)


def _softplus(x):
    return jnp.maximum(x, 0.0) + jnp.log(1.0 + jnp.exp(-jnp.abs(x)))


def _rms(x):
    return x * lax.rsqrt(jnp.mean(x * x, axis=-1, keepdims=True) + EPS)


def _dot(a, b):
    return jnp.dot(a, b, preferred_element_type=F32)


def _dot_nt(a, b):
    return lax.dot_general(a, b, (((1,), (1,)), ((), ())), preferred_element_type=F32)


def _dot_tn(a, b):
    return lax.dot_general(a, b, (((0,), (0,)), ((), ())), preferred_element_type=F32)


def _mod_kernel(c_ref, w_ref, b_ref, o_ref, *, d, kc):
    cact = _silu(c_ref[...])
    acc = b_ref[0]
    for k0 in range(0, d, kc):
        acc = acc + jnp.sum(w_ref[0, k0:k0 + kc, :] * cact[k0:k0 + kc], axis=0, keepdims=True)
    o_ref[0] = acc


def _adaln_mod(c, ada_w, ada_b):
    depth, d, n = ada_w.shape
    tn = 1024
    return pl.pallas_call(
        functools.partial(_mod_kernel, d=d, kc=256),
        out_shape=jax.ShapeDtypeStruct((depth, 1, n), F32),
        grid=(depth, n // tn),
        in_specs=[pl.BlockSpec((d, 1), lambda l, j: (0, 0)),
                  pl.BlockSpec((1, d, tn), lambda l, j: (l, 0, j)),
                  pl.BlockSpec((1, 1, tn), lambda l, j: (l, 0, j))],
        out_specs=pl.BlockSpec((1, 1, tn), lambda l, j: (l, 0, j)),
        compiler_params=_params(("parallel", "parallel")),
        name="adaln_mod",
    )(c.reshape(d, 1), ada_w, ada_b.reshape(depth, 1, n))


def _norm_mod(x, w, scale, shift):
    return _rms(x) * w * (1.0 + scale) + shift


def _norm_mod_kernel(x_ref, w_ref, sc_ref, sh_ref, h_ref):
    h_ref[...] = _norm_mod(x_ref[...], w_ref[...], sc_ref[...], sh_ref[...]).astype(BF16)


def _pre_norm(x, w, scale, shift):
    s, d = x.shape
    tm = min(512, s)
    row = pl.BlockSpec((1, d), lambda i: (0, 0))
    return pl.pallas_call(
        _norm_mod_kernel,
        out_shape=jax.ShapeDtypeStruct((s, d), BF16),
        grid=(s // tm,),
        in_specs=[pl.BlockSpec((tm, d), lambda i: (i, 0)), row, row, row],
        out_specs=pl.BlockSpec((tm, d), lambda i: (i, 0)),
        compiler_params=_params(("parallel",)),
        name="pre_norm",
    )(x, w, scale, shift)


def _inproj_kernel(h_ref, w_ref, wab_ref, p_ref, ab_ref):
    h = h_ref[...]
    p_ref[...] = _dot(h, w_ref[...]).astype(BF16)

    @pl.when(pl.program_id(1) == 0)
    def _():
        ab_ref[...] = _dot(h, wab_ref[...])


def _in_proj(h, w_p, w_ab):
    s, d = h.shape
    n = w_p.shape[1]
    tm, tn = min(1024, s), 512
    return pl.pallas_call(
        _inproj_kernel,
        out_shape=(jax.ShapeDtypeStruct((s, n), BF16), jax.ShapeDtypeStruct((s, LANES), F32)),
        grid=(s // tm, n // tn),
        in_specs=[pl.BlockSpec((tm, d), lambda i, j: (i, 0)),
                  pl.BlockSpec((d, tn), lambda i, j: (0, j)),
                  pl.BlockSpec((d, LANES), lambda i, j: (0, 0))],
        out_specs=(pl.BlockSpec((tm, tn), lambda i, j: (i, j)),
                   pl.BlockSpec((tm, LANES), lambda i, j: (i, 0))),
        compiler_params=_params(("parallel", "arbitrary")),
        name="in_proj",
    )(h, w_p, w_ab)


def _gdn_prep_kernel(x_ref, halo_ref, cw_ref, ab_ref, alog_ref, dtb_ref, tri_ref,
                     q_ref, k_ref, v_ref, gcb_ref, gct_ref, xp_ref, *, tm):
    i = pl.program_id(0)
    outs = (q_ref, k_ref, v_ref)
    for grp in range(3):
        c0 = grp * GDN_QK
        halo = halo_ref[:, c0:c0 + GDN_QK].astype(F32)
        xp_ref[0:8, :] = jnp.where(i > 0, halo, 0.0)
        xp_ref[8:8 + tm, :] = x_ref[:, c0:c0 + GDN_QK].astype(F32)
        y = cw_ref[3:4, c0:c0 + GDN_QK] * xp_ref[8:8 + tm, :]
        for j in range(GDN_CONV - 1):
            off = 8 - (GDN_CONV - 1) + j
            y = y + cw_ref[j:j + 1, c0:c0 + GDN_QK] * xp_ref[off:off + tm, :]
        y = _silu(y)
        for h in range(GDN_HEADS):
            yh = y[:, h * GDN_DK:(h + 1) * GDN_DK]
            if grp < 2:
                yh = yh * lax.rsqrt(jnp.sum(yh * yh, axis=-1, keepdims=True) + EPS)
            if grp == 0:
                yh = yh * (GDN_DK ** -0.5)
            outs[grp][:, h * GDN_DK:(h + 1) * GDN_DK] = yh.astype(BF16)

    ab = ab_ref[...]
    g = -jnp.exp(alog_ref[...]) * _softplus(ab + dtb_ref[...])
    gc = jnp.dot(tri_ref[...], g, preferred_element_type=F32, precision=lax.Precision.HIGHEST)
    lane = lax.broadcasted_iota(jnp.int32, (1, LANES), 1)
    gcb = jnp.where(lane < GDN_HEADS, gc, _sigmoid(ab))
    gcb_ref[...] = gcb
    gct_ref[...] = gcb.T[0:2 * GDN_HEADS, :]


def _gdn_prep(p, ab, conv_w, a_log, dt_bias):
    s = p.shape[0]
    tm = min(512, s)
    c3 = 3 * GDN_QK
    alog_row = jnp.zeros((1, LANES), F32).at[0, :GDN_HEADS].set(a_log)
    dtb_row = jnp.zeros((1, LANES), F32).at[0, :GDN_HEADS].set(dt_bias)
    r = np.arange(tm)
    tri = jnp.asarray(((r[:, None] >= r[None, :]) & (r[:, None] // GDN_CHUNK == r[None, :] // GDN_CHUNK))
                      .astype(np.float32))
    hb = tm // 8
    row = lambda i: (i, 0)
    return pl.pallas_call(
        functools.partial(_gdn_prep_kernel, tm=tm),
        out_shape=(jax.ShapeDtypeStruct((s, GDN_QK), BF16), jax.ShapeDtypeStruct((s, GDN_QK), BF16),
                   jax.ShapeDtypeStruct((s, GDN_V), BF16), jax.ShapeDtypeStruct((s, LANES), F32),
                   jax.ShapeDtypeStruct((2 * GDN_HEADS, s), F32)),
        grid=(s // tm,),
        in_specs=[pl.BlockSpec((tm, c3), row),
                  pl.BlockSpec((8, c3), lambda i: (jnp.maximum(i * hb - 1, 0), 0)),
                  pl.BlockSpec((GDN_CONV, c3), lambda i: (0, 0)),
                  pl.BlockSpec((tm, LANES), row),
                  pl.BlockSpec((1, LANES), lambda i: (0, 0)),
                  pl.BlockSpec((1, LANES), lambda i: (0, 0)),
                  pl.BlockSpec((tm, tm), lambda i: (0, 0))],
        out_specs=(pl.BlockSpec((tm, GDN_QK), row), pl.BlockSpec((tm, GDN_QK), row),
                   pl.BlockSpec((tm, GDN_V), row), pl.BlockSpec((tm, LANES), row),
                   pl.BlockSpec((2 * GDN_HEADS, tm), lambda i: (0, i))),
        scratch_shapes=[pltpu.VMEM((tm + 8, GDN_QK), F32)],
        compiler_params=_params(("parallel",)),
        name="gdn_prep",
    )(p, p, conv_w, ab, alog_row, dtb_row, tri)


def _gdn_local_kernel(q_ref, k_ref, v_ref, gcb_ref, gct_ref,
                      u_ref, w_ref, qd_ref, kt_ref, qk_ref):
    n = GDN_GROUP
    c = GDN_CHUNK
    ri = lax.broadcasted_iota(jnp.int32, (n, n), 0)
    ci = lax.broadcasted_iota(jnp.int32, (n, n), 1)
    same_chunk = (ri // c) == (ci // c)
    incl = same_chunk & (ri >= ci)
    strict = same_chunk & (ri > ci)
    eye = (ri == ci).astype(F32)
    last_sel = (ci == (ri // c) * c + (c - 1)).astype(F32)
    gcb = gcb_ref[...]
    gc_last_all = jnp.dot(last_sel, gcb, preferred_element_type=F32, precision=lax.Precision.HIGHEST)
    heads = range(GDN_HEADS)
    sls = [slice(h * GDN_DK, (h + 1) * GDN_DK) for h in heads]
    lmats, ts, rhss = [], [], []
    for h in heads:
        sl = sls[h]
        q = q_ref[:, sl].astype(F32)
        k16 = k_ref[:, sl]
        k = k16.astype(F32)
        v = v_ref[:, sl].astype(F32)
        gc_col = gcb[:, h:h + 1]
        beta = gcb[:, GDN_HEADS + h:GDN_HEADS + h + 1]
        gc_row = gct_ref[h:h + 1, :]
        gc_last = gc_last_all[:, h:h + 1]
        diff = gc_col - gc_row
        decay = jnp.where(incl, jnp.exp(jnp.where(incl, diff, 0.0)), 0.0)
        kb = k * beta
        kk = _dot_nt(kb.astype(BF16), k16)
        lmat = jnp.where(strict, kk * decay, 0.0)
        lmats.append(lmat)
        ts.append(eye - jnp.where((ri // 2) == (ci // 2), lmat, 0.0))
        eg = jnp.exp(gc_col)
        rhss.append(jnp.concatenate([v * beta, kb * eg], axis=-1).astype(BF16))
        qd_ref[:, sl] = (q * eg).astype(BF16)
        kt_ref[:, sl] = (k * jnp.exp(gc_last - gc_col)).astype(BF16)
        qk = _dot_nt(q_ref[:, sl], k16) * decay
        qk_c = jnp.concatenate([qk[j * c:(j + 1) * c, j * c:(j + 1) * c] for j in range(n // c)], axis=0)
        qk_ref[:, h * c:(h + 1) * c] = qk_c.astype(BF16)
    b = 2
    while b < c:
        off_diag = ((ri // (2 * b)) == (ci // (2 * b))) & ((ri // b) != (ci // b))
        t16s = [ts[h].astype(BF16) for h in heads]
        ys = [_dot(t16s[h], jnp.where(off_diag, lmats[h], 0.0).astype(BF16)).astype(BF16) for h in heads]
        ts = [ts[h] - _dot(ys[h], t16s[h]) for h in heads]
        b *= 2
    for h in heads:
        uw = _dot(ts[h].astype(BF16), rhss[h])
        u_ref[:, sls[h]] = uw[:, :GDN_DV].astype(BF16)
        w_ref[:, sls[h]] = uw[:, GDN_DV:].astype(BF16)


def _gdn_local(q, k, v, gcb, gct):
    s = q.shape[0]
    n = GDN_GROUP
    row = lambda i: (i, 0)
    wide = pl.BlockSpec((n, GDN_QK), row)
    return pl.pallas_call(
        _gdn_local_kernel,
        out_shape=(jax.ShapeDtypeStruct((s, GDN_V), BF16), jax.ShapeDtypeStruct((s, GDN_QK), BF16),
                   jax.ShapeDtypeStruct((s, GDN_QK), BF16), jax.ShapeDtypeStruct((s, GDN_QK), BF16),
                   jax.ShapeDtypeStruct((s, GDN_HEADS * GDN_CHUNK), BF16)),
        grid=(s // n,),
        in_specs=[wide, wide, wide, pl.BlockSpec((n, LANES), row),
                  pl.BlockSpec((2 * GDN_HEADS, n), lambda i: (0, i))],
        out_specs=(wide, wide, wide, wide, pl.BlockSpec((n, GDN_HEADS * GDN_CHUNK), row)),
        compiler_params=_params(("parallel",)),
        name="gdn_local",
    )(q, k, v, gcb, gct)


def _gdn_scan_kernel(u_ref, w_ref, qd_ref, kt_ref, qk_ref, gcb_ref, z_ref, nw_ref, o_ref, st_ref, *, rb):
    c = GDN_CHUNK

    @pl.when(pl.program_id(0) == 0)
    def _():
        st_ref[...] = jnp.zeros_like(st_ref)

    nw = nw_ref[...]

    def chunk(ci, carry):
        r0 = pl.multiple_of(ci * c, c)
        rows = pl.ds(r0, c)
        gt_row = jnp.exp(gcb_ref[pl.ds(r0 + c - 1, 1), :])
        for h in range(GDN_HEADS):
            sl = slice(h * GDN_DK, (h + 1) * GDN_DK)
            st = st_ref[h]
            st16 = st.astype(BF16)
            wq = jnp.concatenate([w_ref[rows, sl], qd_ref[rows, sl]], axis=0)
            r1 = _dot(wq, st16)
            v_new = u_ref[rows, sl].astype(F32) - r1[0:c]
            vn16 = v_new.astype(BF16)
            o = r1[c:2 * c] + _dot(qk_ref[rows, h * c:(h + 1) * c], vn16)
            st_ref[h] = st * gt_row[:, h:h + 1] + _dot_tn(kt_ref[rows, sl], vn16)
            o = _rms(o) * nw * _silu(z_ref[rows, sl].astype(F32))
            o_ref[rows, sl] = o.astype(BF16)
        return carry

    lax.fori_loop(0, rb // c, chunk, 0)


def _gdn_scan(u, w, qd, kt, qk, gcb, p, norm_w):
    s = u.shape[0]
    rb = min(512, s)
    row = lambda i: (i, 0)
    wide = pl.BlockSpec((rb, GDN_QK), row)
    return pl.pallas_call(
        functools.partial(_gdn_scan_kernel, rb=rb),
        out_shape=jax.ShapeDtypeStruct((s, GDN_V), BF16),
        grid=(s // rb,),
        in_specs=[wide, wide, wide, wide, pl.BlockSpec((rb, GDN_HEADS * GDN_CHUNK), row),
                  pl.BlockSpec((rb, LANES), row),
                  pl.BlockSpec((rb, GDN_V), lambda i: (i, P_Z // GDN_V)),
                  pl.BlockSpec((1, GDN_DV), lambda i: (0, 0))],
        out_specs=wide,
        scratch_shapes=[pltpu.VMEM((GDN_HEADS, GDN_DK, GDN_DV), F32)],
        compiler_params=_params(("arbitrary",)),
        name="gdn_scan",
    )(u, w, qd, kt, qk, gcb, p, norm_w.reshape(1, GDN_DV))


def _rope_table_kernel(pos_ref, inv_ref, cos_ref, sin_ref):
    ang = pos_ref[...].astype(F32) * inv_ref[...]
    lane = lax.broadcasted_iota(jnp.int32, ang.shape, 1)
    first_half = (lane % MLA_ROPE) < (MLA_ROPE // 2)
    cos_ref[...] = jnp.cos(ang)
    sin_ref[...] = jnp.where(first_half, -jnp.sin(ang), jnp.sin(ang))


def _rope_tables(positions):
    s = positions.shape[-1]
    tm = min(1024, s)
    half = MLA_ROPE // 2
    inv = (ROPE_THETA ** (-np.arange(half, dtype=np.float32) / half)).astype(np.float32)
    inv_row = jnp.asarray(np.tile(inv, LANES // half).reshape(1, LANES))
    return pl.pallas_call(
        _rope_table_kernel,
        out_shape=(jax.ShapeDtypeStruct((s, LANES), F32), jax.ShapeDtypeStruct((s, LANES), F32)),
        grid=(s // tm,),
        in_specs=[pl.BlockSpec((tm, 1), lambda i: (i, 0)), pl.BlockSpec((1, LANES), lambda i: (0, 0))],
        out_specs=(pl.BlockSpec((tm, LANES), lambda i: (i, 0)), pl.BlockSpec((tm, LANES), lambda i: (i, 0))),
        compiler_params=_params(("parallel",)),
        name="rope_tables",
    )(positions.reshape(s, 1), inv_row)


def _rope_apply(x, cos, sin_signed):
    width = x.shape[-1]
    half = MLA_ROPE // 2
    lane = lax.broadcasted_iota(jnp.int32, x.shape, 1)
    first_half = (lane % MLA_ROPE) < half
    swapped = jnp.where(first_half, pltpu.roll(x, width - half, 1), pltpu.roll(x, half, 1))
    return x * cos + swapped * sin_signed


def _mla_proj_kernel(cq_ref, ckv_ref, kr_ref, cos_ref, sin_ref, qn_ref, wuq_ref, kvn_ref, wuk_ref, wuvt_ref,
                     q_ref, k_ref, vt_ref):
    cos = cos_ref[...]
    sin = sin_ref[...]
    scale = (MLA_NOPE + MLA_ROPE) ** -0.5 * math.log2(math.e)
    cq = cq_ref[...].astype(F32)
    cqn = cq * lax.rsqrt(jnp.sum(cq * cq, axis=-1, keepdims=True) * (1.0 / MLA_Q_RANK) + EPS) * qn_ref[...]
    q = _dot(cqn.astype(BF16), wuq_ref[...])
    nope_w = MLA_HEADS * MLA_NOPE
    q_rope = _rope_apply(q[:, nope_w:], jnp.concatenate([cos, cos], -1), jnp.concatenate([sin, sin], -1))
    zeros = jnp.zeros((q.shape[0], MLA_QK_PAD - MLA_NOPE - MLA_ROPE), F32)
    ckvn = (_rms(ckv_ref[...].astype(F32)) * kvn_ref[...]).astype(BF16)
    k_nope = _dot(ckvn, wuk_ref[...])
    v_t = _dot_nt(wuvt_ref[...], ckvn)
    k_pe = _rope_apply(kr_ref[...].astype(F32), cos, sin)[:, :MLA_ROPE]
    for h in range(MLA_HEADS):
        qh = jnp.concatenate([q[:, h * MLA_NOPE:(h + 1) * MLA_NOPE],
                              q_rope[:, h * MLA_ROPE:(h + 1) * MLA_ROPE], zeros], axis=-1) * scale
        q_ref[h] = qh.astype(BF16)
        kh = jnp.concatenate([k_nope[:, h * MLA_NOPE:(h + 1) * MLA_NOPE], k_pe, zeros], axis=-1)
        k_ref[h] = kh.astype(BF16)
        vt_ref[h, 0] = v_t[h * MLA_DV:(h + 1) * MLA_DV, :].astype(BF16)


def _mla_proj(p, cos, sin, q_norm, w_uq, kv_norm, w_uk, w_uvt):
    s = p.shape[0]
    tm = min(MLA_BLOCK, s)
    const = lambda i: (0, 0)
    return pl.pallas_call(
        _mla_proj_kernel,
        out_shape=(jax.ShapeDtypeStruct((MLA_HEADS, s, MLA_QK_PAD), BF16),
                   jax.ShapeDtypeStruct((MLA_HEADS, s, MLA_QK_PAD), BF16),
                   jax.ShapeDtypeStruct((MLA_HEADS, s // tm, MLA_DV, tm), BF16)),
        grid=(s // tm,),
        in_specs=[pl.BlockSpec((tm, MLA_Q_RANK_PAD), lambda i: (i, P_CQ // MLA_Q_RANK_PAD)),
                  pl.BlockSpec((tm, LANES), lambda i: (i, P_CKV // LANES)),
                  pl.BlockSpec((tm, LANES), lambda i: (i, P_KROPE // LANES)),
                  pl.BlockSpec((tm, LANES), lambda i: (i, 0)),
                  pl.BlockSpec((tm, LANES), lambda i: (i, 0)),
                  pl.BlockSpec(q_norm.shape, const), pl.BlockSpec(w_uq.shape, const),
                  pl.BlockSpec(kv_norm.shape, const), pl.BlockSpec(w_uk.shape, const),
                  pl.BlockSpec(w_uvt.shape, const)],
        out_specs=(pl.BlockSpec((MLA_HEADS, tm, MLA_QK_PAD), lambda i: (0, i, 0)),
                   pl.BlockSpec((MLA_HEADS, tm, MLA_QK_PAD), lambda i: (0, i, 0)),
                   pl.BlockSpec((MLA_HEADS, 1, MLA_DV, tm), lambda i: (0, i, 0, 0))),
        compiler_params=_params(("parallel",)),
        name="mla_proj",
    )(p, p, p, cos, sin, q_norm, w_uq, kv_norm, w_uk, w_uvt)


def _mla_attn_kernel(q_ref, k_ref, vt_ref, o_ref, st0, st1, mb0, mb1, m_ref, l_ref, acc_ref, *, bk):
    qi = pl.program_id(1)
    slots = ((st0, mb0), (st1, mb1))
    key = lax.broadcasted_iota(jnp.int32, (bk, 2 * bk), 0)
    qry = lax.broadcasted_iota(jnp.int32, (bk, 2 * bk), 1)

    def scores(t, slot, mask=None):
        st_ref, mb_ref = slots[slot]
        r0 = pl.multiple_of(t * bk, bk)
        st = _dot_nt(k_ref[0, pl.ds(r0, bk), :], q_ref[0])
        if mask is not None:
            st = jnp.where(mask, st, -jnp.inf)
        st_ref[...] = st
        mb_ref[...] = jnp.max(st, axis=0, keepdims=True)

    def update(t, slot):
        st_ref, mb_ref = slots[slot]
        m = m_ref[...]
        m_new = jnp.maximum(m, mb_ref[...])
        alpha = jnp.exp2(m - m_new)
        pexp = jnp.exp2(st_ref[...] - m_new)
        l_ref[...] = alpha * l_ref[...] + jnp.sum(pexp, axis=0, keepdims=True)
        acc_ref[...] = alpha * acc_ref[...] + _dot(vt_ref[0, t], pexp.astype(BF16))
        m_ref[...] = m_new

    m_ref[...] = jnp.full(m_ref.shape, -jnp.inf, F32)
    l_ref[...] = jnp.zeros(l_ref.shape, F32)
    acc_ref[...] = jnp.zeros(acc_ref.shape, F32)
    diag0 = key <= qry
    diag1 = key + bk <= qry

    @pl.when(qi == 0)
    def _():
        scores(0, 0, diag0)

    @pl.when(qi > 0)
    def _():
        scores(0, 0)

    def pair(j, carry):
        update(2 * j, 0)
        scores(2 * j + 1, 1)
        update(2 * j + 1, 1)
        scores(2 * j + 2, 0)
        return carry

    lax.fori_loop(0, qi - 1, pair, 0)

    @pl.when(qi > 0)
    def _():
        update(2 * qi - 2, 0)
        scores(2 * qi - 1, 1)
        update(2 * qi - 1, 1)
        scores(2 * qi, 0, diag0)

    update(2 * qi, 0)
    scores(2 * qi + 1, 1, diag1)
    update(2 * qi + 1, 1)
    o_ref[...] = (acc_ref[...] / l_ref[...]).T.astype(BF16)


def _mla_attn(q, k, vt):
    nh, s, _ = q.shape
    bk = vt.shape[-1]
    bq = 2 * bk
    assert s % bq == 0
    return pl.pallas_call(
        functools.partial(_mla_attn_kernel, bk=bk),
        out_shape=jax.ShapeDtypeStruct((s, nh * MLA_DV), BF16),
        grid=(nh, s // bq),
        in_specs=[pl.BlockSpec((1, bq, MLA_QK_PAD), lambda h, i: (h, i, 0)),
                  pl.BlockSpec((1, s, MLA_QK_PAD), lambda h, i: (h, 0, 0)),
                  pl.BlockSpec((1, s // bk, MLA_DV, bk), lambda h, i: (h, 0, 0, 0))],
        out_specs=pl.BlockSpec((bq, MLA_DV), lambda h, i: (i, h)),
        scratch_shapes=[pltpu.VMEM((bk, bq), F32), pltpu.VMEM((bk, bq), F32),
                        pltpu.VMEM((1, bq), F32), pltpu.VMEM((1, bq), F32),
                        pltpu.VMEM((1, bq), F32), pltpu.VMEM((1, bq), F32), pltpu.VMEM((MLA_DV, bq), F32)],
        compiler_params=_params(("parallel", "arbitrary")),
        name="mla_attn",
    )(q, k, vt)


def _swa_kernel(q_ref, k_ref, v_ref, kh_ref, vh_ref, sink_ref, o_ref, *, nblk):
    w = SWA_WINDOW
    g = SWA_GROUP
    i = pl.program_id(0)
    row = lax.broadcasted_iota(jnp.int32, (g * w, 2 * w), 0)
    kj = lax.broadcasted_iota(jnp.int32, (g * w, 2 * w), 1)
    dist = row % w + w - kj
    band = (dist >= 0) & (dist < w)
    first_band = band & ((kj >= w) | (i > 0))
    row_g = lax.broadcasted_iota(jnp.int32, (g * w, 1), 0) // w
    dist_f = dist.astype(F32)
    biases, sink_cols = [], []
    for hk in range(SWA_KV_HEADS):
        slope = jnp.zeros((g * w, 1), F32)
        sink = jnp.zeros((g * w, 1), F32)
        for gi in range(g):
            hq = hk * g + gi
            slope = jnp.where(row_g == gi, 2.0 ** (-8.0 * (hq + 1.0) / SWA_HEADS), slope)
            sink = jnp.where(row_g == gi, sink_ref[:, hq:hq + 1], sink)
        biases.append(-slope * dist_f)
        sink_cols.append(sink)
    for blk in range(nblk):
        rows = slice(blk * w, (blk + 1) * w)
        if blk == 0:
            k_prev, v_prev, valid = kh_ref[...], vh_ref[...], first_band
        else:
            k_prev, v_prev, valid = k_ref[(blk - 1) * w:blk * w, :], v_ref[(blk - 1) * w:blk * w, :], band
        kk = jnp.concatenate([k_prev, k_ref[rows, :]], axis=0)
        vv = jnp.concatenate([v_prev, v_ref[rows, :]], axis=0)
        outs = []
        for hk in range(SWA_KV_HEADS):
            kh = kk[:, hk * SWA_DH:(hk + 1) * SWA_DH]
            vh = vv[:, hk * SWA_DH:(hk + 1) * SWA_DH]
            qg = jnp.concatenate([q_ref[rows, (hk * g + gi) * SWA_DH:(hk * g + gi + 1) * SWA_DH]
                                  for gi in range(g)], axis=0)
            qg = qg * (SWA_DH ** -0.5)
            s = jnp.where(valid, _dot_nt(qg, kh) + biases[hk], -jnp.inf)
            sink = sink_cols[hk]
            m = jnp.maximum(jnp.max(s, axis=-1, keepdims=True), sink)
            pexp = jnp.exp(s - m)
            denom = jnp.sum(pexp, axis=-1, keepdims=True) + jnp.exp(sink - m)
            og = _dot(pexp.astype(BF16), vh) / denom
            outs.extend(og[gi * w:(gi + 1) * w, :] for gi in range(g))
        o_ref[rows, :] = jnp.concatenate(outs, axis=-1).astype(BF16)


def _swa(p, sinks):
    s = p.shape[0]
    w = SWA_WINDOW
    rb = min(512, s)
    nblk = rb // w
    sink_row = jnp.zeros((1, LANES), F32).at[0, :SWA_HEADS].set(sinks)
    prev = lambda col: (lambda i: (jnp.maximum(i * nblk - 1, 0), col))
    return pl.pallas_call(
        functools.partial(_swa_kernel, nblk=nblk),
        out_shape=jax.ShapeDtypeStruct((s, SWA_OUT), BF16),
        grid=(s // rb,),
        in_specs=[pl.BlockSpec((rb, SWA_OUT), lambda i: (i, P_SWQ // SWA_OUT)),
                  pl.BlockSpec((rb, SWA_KV), lambda i: (i, P_SWK // SWA_KV)),
                  pl.BlockSpec((rb, SWA_KV), lambda i: (i, P_SWV // SWA_KV)),
                  pl.BlockSpec((w, SWA_KV), prev(P_SWK // SWA_KV)),
                  pl.BlockSpec((w, SWA_KV), prev(P_SWV // SWA_KV)),
                  pl.BlockSpec((1, LANES), lambda i: (0, 0))],
        out_specs=pl.BlockSpec((rb, SWA_OUT), lambda i: (i, 0)),
        compiler_params=_params(("parallel",)),
        name="swa",
    )(p, p, p, p, p, sink_row)


def _outproj_kernel(oa_ref, ob_ref, oc_ref, w_ref, x_ref, pn_ref, g1_ref, fn_ref, sc_ref, sh_ref,
                    x1_ref, h2_ref):
    a_w, b_w = oa_ref.shape[1], ob_ref.shape[1]
    mix = (_dot(oa_ref[...], w_ref[0:a_w, :]) + _dot(ob_ref[...], w_ref[a_w:a_w + b_w, :])
           + _dot(oc_ref[...], w_ref[a_w + b_w:, :]))
    x1 = x_ref[...] + g1_ref[...] * (_rms(mix) * pn_ref[...])
    x1_ref[...] = x1
    h2_ref[...] = _norm_mod(x1, fn_ref[...], sc_ref[...], sh_ref[...]).astype(BF16)


def _out_proj(o_a, o_b, o_c, w_out, x, post_norm, gate1, ffn_norm, scale2, shift2):
    s, d = x.shape
    tm = min(256, s)
    row = lambda i: (i, 0)
    vec = pl.BlockSpec((1, d), lambda i: (0, 0))
    return pl.pallas_call(
        _outproj_kernel,
        out_shape=(jax.ShapeDtypeStruct((s, d), F32), jax.ShapeDtypeStruct((s, d), BF16)),
        grid=(s // tm,),
        in_specs=[pl.BlockSpec((tm, o_a.shape[1]), row), pl.BlockSpec((tm, o_b.shape[1]), row),
                  pl.BlockSpec((tm, o_c.shape[1]), row), pl.BlockSpec(w_out.shape, lambda i: (0, 0)),
                  pl.BlockSpec((tm, d), row), vec, vec, vec, vec, vec],
        out_specs=(pl.BlockSpec((tm, d), row), pl.BlockSpec((tm, d), row)),
        compiler_params=_params(("parallel",)),
        name="out_proj",
    )(o_a, o_b, o_c, w_out, x, post_norm, gate1, ffn_norm, scale2, shift2)


def _gelu_tanh(x):
    return 0.5 * x * (1.0 + jnp.tanh(math.sqrt(2.0 / math.pi) * (x + 0.044715 * (x * x * x))))


def _ffn_up_kernel(h_ref, wg_ref, wu_ref, cg_ref, cu_ref, bg_ref, bu_ref, o_ref, xg_ref, xu_ref, *, tm):
    @pl.when(pl.program_id(1) == 0)
    def _():
        xg_ref[0:8, :] = jnp.zeros((8, xg_ref.shape[1]), F32)
        xu_ref[0:8, :] = jnp.zeros((8, xu_ref.shape[1]), F32)

    h = h_ref[...]

    def conv(x_ref, w_ref, cw_ref, b_ref):
        x_ref[8:8 + tm, :] = _dot(h, w_ref[...])
        y = b_ref[...] + cw_ref[2:3, :] * x_ref[8:8 + tm, :]
        y = y + cw_ref[1:2, :] * x_ref[7:7 + tm, :]
        y = y + cw_ref[0:1, :] * x_ref[6:6 + tm, :]
        x_ref[0:8, :] = x_ref[tm:tm + 8, :]
        return y

    gate = conv(xg_ref, wg_ref, cg_ref, bg_ref)
    up = conv(xu_ref, wu_ref, cu_ref, bu_ref)
    o_ref[...] = (_gelu_tanh(gate) * up).astype(BF16)


def _ffn_up(h2, w_up, conv_w, conv_b):
    s, d = h2.shape
    d_ff = w_up.shape[1] // 2
    tm, tn = min(1024, s), 512
    nj = d_ff // tn
    lo = lambda j, i: (0, j)
    hi = lambda j, i: (0, j + nj)
    return pl.pallas_call(
        functools.partial(_ffn_up_kernel, tm=tm),
        out_shape=jax.ShapeDtypeStruct((s, d_ff), BF16),
        grid=(nj, s // tm),
        in_specs=[pl.BlockSpec((tm, d), lambda j, i: (i, 0)),
                  pl.BlockSpec((d, tn), lo), pl.BlockSpec((d, tn), hi),
                  pl.BlockSpec((FFN_CONV, tn), lo), pl.BlockSpec((FFN_CONV, tn), hi),
                  pl.BlockSpec((1, tn), lo), pl.BlockSpec((1, tn), hi)],
        out_specs=pl.BlockSpec((tm, tn), lambda j, i: (i, j)),
        scratch_shapes=[pltpu.VMEM((tm + 8, tn), F32), pltpu.VMEM((tm + 8, tn), F32)],
        compiler_params=_params(("parallel", "arbitrary")),
        name="ffn_up",
    )(h2, w_up, w_up, conv_w, conv_w, conv_b, conv_b)


def _ffn_down_kernel(g_ref, w_ref, x_ref, pn_ref, g2_ref, nn_ref, sc_ref, sh_ref, x2_ref, hn_ref, acc_ref):
    kk = pl.program_id(1)

    @pl.when(kk == 0)
    def _():
        acc_ref[...] = jnp.zeros_like(acc_ref)

    acc_ref[...] += _dot(g_ref[...], w_ref[...])

    @pl.when(kk == pl.num_programs(1) - 1)
    def _():
        x2 = x_ref[...] + g2_ref[...] * (_rms(acc_ref[...]) * pn_ref[...])
        x2_ref[...] = x2
        hn_ref[...] = _norm_mod(x2, nn_ref[...], sc_ref[...], sh_ref[...]).astype(BF16)


def _ffn_down(g, w_down, x1, post_norm, gate2, next_norm, next_scale, next_shift):
    s, d = x1.shape
    d_ff = g.shape[1]
    tm, tk = min(512, s), 512
    row = lambda i, k: (i, 0)
    vec = pl.BlockSpec((1, d), lambda i, k: (0, 0))
    return pl.pallas_call(
        _ffn_down_kernel,
        out_shape=(jax.ShapeDtypeStruct((s, d), F32), jax.ShapeDtypeStruct((s, d), BF16)),
        grid=(s // tm, d_ff // tk),
        in_specs=[pl.BlockSpec((tm, tk), lambda i, k: (i, k)), pl.BlockSpec((tk, d), lambda i, k: (k, 0)),
                  pl.BlockSpec((tm, d), row), vec, vec, vec, vec, vec],
        out_specs=(pl.BlockSpec((tm, d), row), pl.BlockSpec((tm, d), row)),
        scratch_shapes=[pltpu.VMEM((tm, d), F32)],
        compiler_params=_params(("parallel", "arbitrary")),
        name="ffn_down",
    )(g, w_down, x1, post_norm, gate2, next_norm, next_scale, next_shift)


def _layout_w_in(w_in):
    d = w_in.shape[0]
    sizes = (GDN_QK, GDN_QK, GDN_V, GDN_V, GDN_HEADS, GDN_HEADS, MLA_Q_RANK, MLA_KV_RANK, MLA_ROPE,
             SWA_OUT, SWA_KV, SWA_KV)
    offs = np.concatenate([[0], np.cumsum(sizes)])
    part = lambda n: w_in[:, offs[n]:offs[n + 1]]
    zeros = lambda n: jnp.zeros((d, n), w_in.dtype)
    w_p = jnp.concatenate([part(0), part(1), part(2), part(3),
                           part(6), zeros(MLA_Q_RANK_PAD - MLA_Q_RANK),
                           part(9), part(7), part(8), zeros(LANES - MLA_ROPE), part(10), part(11)],
                          axis=1).astype(BF16)
    w_ab = jnp.concatenate([part(4), part(5), zeros(LANES - 2 * GDN_HEADS)], axis=1).astype(BF16)
    return w_p, w_ab


def _layout_mla(q_norm, w_uq, kv_norm, w_ukv):
    dqk = MLA_NOPE + MLA_ROPE
    uq = w_uq.reshape(MLA_Q_RANK, MLA_HEADS, dqk)
    uq = jnp.concatenate([uq[:, :, :MLA_NOPE].reshape(MLA_Q_RANK, -1), uq[:, :, MLA_NOPE:].reshape(MLA_Q_RANK, -1)],
                         axis=1)
    uq = jnp.pad(uq, ((0, MLA_Q_RANK_PAD - MLA_Q_RANK), (0, 0))).astype(BF16)
    qn = jnp.pad(q_norm, (0, MLA_Q_RANK_PAD - MLA_Q_RANK)).reshape(1, MLA_Q_RANK_PAD)
    ukv = w_ukv.reshape(MLA_KV_RANK, MLA_HEADS, MLA_NOPE + MLA_DV)
    uk = ukv[:, :, :MLA_NOPE].reshape(MLA_KV_RANK, -1).astype(BF16)
    uvt = ukv[:, :, MLA_NOPE:].reshape(MLA_KV_RANK, -1).T.astype(BF16)
    return qn, uq, kv_norm.reshape(1, MLA_KV_RANK), uk, uvt


def kernel(x, c, positions, ada_w, ada_b, mix_pre_norm, mix_post_norm, w_in, w_out, gdn_conv, gdn_a_log, gdn_dt_bias, gdn_norm, mla_q_norm, mla_w_uq, mla_kv_norm, mla_w_ukv, swa_sinks, ffn_pre_norm, ffn_post_norm, ffn_w_up, ffn_conv, ffn_conv_b, ffn_w_down):
    batch, s, d = x.shape
    assert batch == 1, "kernels are written for a single sequence"
    depth = ada_w.shape[0]
    xs = x.reshape(s, d)
    mod = _adaln_mod(c, ada_w, ada_b).reshape(depth, N_MOD, 1, d)
    cos, sin = _rope_tables(positions)
    vec = lambda a: a.reshape(1, d)

    h = _pre_norm(xs, vec(mix_pre_norm[0]), mod[0, 1], mod[0, 0])
    for l in range(depth):
        shift1, scale1, gate1, shift2, scale2, gate2 = (mod[l, n] for n in range(N_MOD))
        w_p, w_ab = _layout_w_in(w_in[l])
        p, ab = _in_proj(h, w_p, w_ab)

        q_a, k_a, v_a, gcb, gct = _gdn_prep(p, ab, gdn_conv[l], gdn_a_log[l], gdn_dt_bias[l])
        u, w, qd, kt, qk = _gdn_local(q_a, k_a, v_a, gcb, gct)
        o_a = _gdn_scan(u, w, qd, kt, qk, gcb, p, gdn_norm[l])

        q_b, k_b, v_b = _mla_proj(p, cos, sin, *_layout_mla(mla_q_norm[l], mla_w_uq[l], mla_kv_norm[l],
                                                            mla_w_ukv[l]))
        o_b = _mla_attn(q_b, k_b, v_b)

        o_c = _swa(p, swa_sinks[l])

        x1, h2 = _out_proj(o_a, o_b, o_c, w_out[l].astype(BF16), xs, vec(mix_post_norm[l]), gate1,
                           vec(ffn_pre_norm[l]), scale2, shift2)
        g = _ffn_up(h2, ffn_w_up[l].astype(BF16), ffn_conv[l], ffn_conv_b[l].reshape(1, -1))
        nl = min(l + 1, depth - 1)
        xs, h = _ffn_down(g, ffn_w_down[l].astype(BF16), x1, vec(ffn_post_norm[l]), gate2,
                          vec(mix_pre_norm[nl]), mod[nl, 1], mod[nl, 0])
    return xs.reshape(batch, s, d)
```

```python
import functools
import math

import numpy as np
import jax
import jax.numpy as jnp
from jax import lax
from jax.experimental import pallas as pl
from jax.experimental.pallas import tpu as pltpu

F32 = jnp.float32
BF16 = jnp.bfloat16

EPS = 1e-6
N_MOD = 6
GDN_HEADS = 8
GDN_DK = 128
GDN_DV = 128
GDN_CONV = 4
GDN_CHUNK = 64
GDN_QK = GDN_HEADS * GDN_DK
GDN_V = GDN_HEADS * GDN_DV
MLA_HEADS = 4
MLA_Q_RANK = 448
MLA_Q_RANK_PAD = 512
MLA_KV_RANK = 128
MLA_NOPE = 128
MLA_ROPE = 64
MLA_DV = 128
MLA_QK_PAD = 256
MLA_BLOCK = 512
ROPE_THETA = 10000.0
SWA_HEADS = 8
SWA_KV_HEADS = 2
SWA_GROUP = SWA_HEADS // SWA_KV_HEADS
SWA_DH = 64
SWA_WINDOW = 128
SWA_OUT = SWA_HEADS * SWA_DH
SWA_KV = SWA_KV_HEADS * SWA_DH
SWA_GORDER = (0, 2, 1, 3)
FFN_CONV = 3

LANES = 128
GDN_GROUP = 256
VMEM_LIMIT_MB = 56

P_QKV = 0
P_Z = 3072
P_CQ = 4096
P_SWQ = 4608
P_CKV = 5120
P_KROPE = 5248
P_SWK = 5376
P_SWV = 5504
P_WIDTH = 5632


def _params(semantics):
    return pltpu.CompilerParams(dimension_semantics=semantics, vmem_limit_bytes=VMEM_LIMIT_MB << 20)


def _sigmoid(x):
    return 1.0 / (1.0 + jnp.exp(-x))


def _silu(x):
    return x * _sigmoid(x)


def _softplus(x):
    return jnp.maximum(x, 0.0) + jnp.log(1.0 + jnp.exp(-jnp.abs(x)))


def _rms(x):
    return x * lax.rsqrt(jnp.mean(x * x, axis=-1, keepdims=True) + EPS)


def _dot(a, b):
    return jnp.dot(a, b, preferred_element_type=F32)


def _dot_nt(a, b):
    return lax.dot_general(a, b, (((1,), (1,)), ((), ())), preferred_element_type=F32)


def _dot_tn(a, b):
    return lax.dot_general(a, b, (((0,), (0,)), ((), ())), preferred_element_type=F32)


def _mod_kernel(c_ref, w_ref, b_ref, o_ref, *, d, kc):
    cact = _silu(c_ref[...])
    acc = b_ref[0]
    for k0 in range(0, d, kc):
        acc = acc + jnp.sum(w_ref[0, k0:k0 + kc, :] * cact[k0:k0 + kc], axis=0, keepdims=True)
    o_ref[0] = acc


def _adaln_mod(c, ada_w, ada_b):
    depth, d, n = ada_w.shape
    tn = 1024
    return pl.pallas_call(
        functools.partial(_mod_kernel, d=d, kc=256),
        out_shape=jax.ShapeDtypeStruct((depth, 1, n), F32),
        grid=(depth, n // tn),
        in_specs=[pl.BlockSpec((d, 1), lambda l, j: (0, 0)),
                  pl.BlockSpec((1, d, tn), lambda l, j: (l, 0, j)),
                  pl.BlockSpec((1, 1, tn), lambda l, j: (l, 0, j))],
        out_specs=pl.BlockSpec((1, 1, tn), lambda l, j: (l, 0, j)),
        compiler_params=_params(("parallel", "parallel")),
        name="adaln_mod",
    )(c.reshape(d, 1), ada_w, ada_b.reshape(depth, 1, n))


def _norm_mod(x, w, scale, shift):
    return _rms(x) * w * (1.0 + scale) + shift


def _norm_mod_kernel(x_ref, w_ref, sc_ref, sh_ref, h_ref):
    h_ref[...] = _norm_mod(x_ref[...], w_ref[...], sc_ref[...], sh_ref[...]).astype(BF16)


def _pre_norm(x, w, scale, shift):
    s, d = x.shape
    tm = min(512, s)
    row = pl.BlockSpec((1, d), lambda i: (0, 0))
    return pl.pallas_call(
        _norm_mod_kernel,
        out_shape=jax.ShapeDtypeStruct((s, d), BF16),
        grid=(s // tm,),
        in_specs=[pl.BlockSpec((tm, d), lambda i: (i, 0)), row, row, row],
        out_specs=pl.BlockSpec((tm, d), lambda i: (i, 0)),
        compiler_params=_params(("parallel",)),
        name="pre_norm",
    )(x, w, scale, shift)


def _inproj_kernel(h_ref, w_ref, wab_ref, p_ref, ab_ref):
    h = h_ref[...]
    p_ref[...] = _dot(h, w_ref[...]).astype(BF16)

    @pl.when(pl.program_id(1) == 0)
    def _():
        ab_ref[...] = _dot(h, wab_ref[...])


def _in_proj(h, w_p, w_ab):
    s, d = h.shape
    n = w_p.shape[1]
    tm, tn = min(2048, s), 512
    return pl.pallas_call(
        _inproj_kernel,
        out_shape=(jax.ShapeDtypeStruct((s, n), BF16), jax.ShapeDtypeStruct((s, LANES), F32)),
        grid=(s // tm, n // tn),
        in_specs=[pl.BlockSpec((tm, d), lambda i, j: (i, 0)),
                  pl.BlockSpec((d, tn), lambda i, j: (0, j)),
                  pl.BlockSpec((d, LANES), lambda i, j: (0, 0))],
        out_specs=(pl.BlockSpec((tm, tn), lambda i, j: (i, j)),
                   pl.BlockSpec((tm, LANES), lambda i, j: (i, 0))),
        compiler_params=_params(("parallel", "arbitrary")),
        name="in_proj",
    )(h, w_p, w_ab)


def _gdn_prep_kernel(x_ref, halo_ref, cw_ref, ab_ref, alog_ref, dtb_ref, tri_ref,
                     q_ref, k_ref, v_ref, gcb_ref, gct_ref, xp_ref, *, tm):
    i = pl.program_id(0)
    outs = (q_ref, k_ref, v_ref)
    for grp in range(3):
        c0 = grp * GDN_QK
        halo = halo_ref[:, c0:c0 + GDN_QK].astype(F32)
        xp_ref[0:8, :] = jnp.where(i > 0, halo, 0.0)
        xp_ref[8:8 + tm, :] = x_ref[:, c0:c0 + GDN_QK].astype(F32)
        y = cw_ref[3:4, c0:c0 + GDN_QK] * xp_ref[8:8 + tm, :]
        for j in range(GDN_CONV - 1):
            off = 8 - (GDN_CONV - 1) + j
            y = y + cw_ref[j:j + 1, c0:c0 + GDN_QK] * xp_ref[off:off + tm, :]
        y = _silu(y)
        for h in range(GDN_HEADS):
            yh = y[:, h * GDN_DK:(h + 1) * GDN_DK]
            if grp < 2:
                yh = yh * lax.rsqrt(jnp.sum(yh * yh, axis=-1, keepdims=True) + EPS)
            if grp == 0:
                yh = yh * (GDN_DK ** -0.5)
            outs[grp][:, h * GDN_DK:(h + 1) * GDN_DK] = yh.astype(BF16)

    ab = ab_ref[...]
    g = -jnp.exp(alog_ref[...]) * _softplus(ab + dtb_ref[...])
    gc = jnp.dot(tri_ref[...], g, preferred_element_type=F32, precision=lax.Precision.HIGHEST)
    lane = lax.broadcasted_iota(jnp.int32, (1, LANES), 1)
    gcb = jnp.where(lane < GDN_HEADS, gc, _sigmoid(ab))
    gcb_ref[...] = gcb
    gct_ref[...] = gcb.T[0:2 * GDN_HEADS, :]


def _gdn_prep(p, ab, conv_w, a_log, dt_bias):
    s = p.shape[0]
    tm = min(512, s)
    c3 = 3 * GDN_QK
    alog_row = jnp.zeros((1, LANES), F32).at[0, :GDN_HEADS].set(a_log)
    dtb_row = jnp.zeros((1, LANES), F32).at[0, :GDN_HEADS].set(dt_bias)
    r = np.arange(tm)
    tri = jnp.asarray(((r[:, None] >= r[None, :]) & (r[:, None] // GDN_CHUNK == r[None, :] // GDN_CHUNK))
                      .astype(np.float32))
    hb = tm // 8
    row = lambda i: (i, 0)
    return pl.pallas_call(
        functools.partial(_gdn_prep_kernel, tm=tm),
        out_shape=(jax.ShapeDtypeStruct((s, GDN_QK), BF16), jax.ShapeDtypeStruct((s, GDN_QK), BF16),
                   jax.ShapeDtypeStruct((s, GDN_V), BF16), jax.ShapeDtypeStruct((s, LANES), F32),
                   jax.ShapeDtypeStruct((2 * GDN_HEADS, s), F32)),
        grid=(s // tm,),
        in_specs=[pl.BlockSpec((tm, c3), row),
                  pl.BlockSpec((8, c3), lambda i: (jnp.maximum(i * hb - 1, 0), 0)),
                  pl.BlockSpec((GDN_CONV, c3), lambda i: (0, 0)),
                  pl.BlockSpec((tm, LANES), row),
                  pl.BlockSpec((1, LANES), lambda i: (0, 0)),
                  pl.BlockSpec((1, LANES), lambda i: (0, 0)),
                  pl.BlockSpec((tm, tm), lambda i: (0, 0))],
        out_specs=(pl.BlockSpec((tm, GDN_QK), row), pl.BlockSpec((tm, GDN_QK), row),
                   pl.BlockSpec((tm, GDN_V), row), pl.BlockSpec((tm, LANES), row),
                   pl.BlockSpec((2 * GDN_HEADS, tm), lambda i: (0, i))),
        scratch_shapes=[pltpu.VMEM((tm + 8, GDN_QK), F32)],
        compiler_params=_params(("parallel",)),
        name="gdn_prep",
    )(p, p, conv_w, ab, alog_row, dtb_row, tri)


def _gdn_local_kernel(q_ref, k_ref, v_ref, gcb_ref, gct_ref,
                      u_ref, w_ref, qd_ref, kt_ref, qk_ref):
    n = GDN_GROUP
    c = GDN_CHUNK
    ri = lax.broadcasted_iota(jnp.int32, (n, n), 0)
    ci = lax.broadcasted_iota(jnp.int32, (n, n), 1)
    same_chunk = (ri // c) == (ci // c)
    incl = same_chunk & (ri >= ci)
    strict = same_chunk & (ri > ci)
    eye = (ri == ci).astype(F32)
    last_sel = (ci == (ri // c) * c + (c - 1)).astype(F32)
    gcb = gcb_ref[...]
    gc_last_all = jnp.dot(last_sel, gcb, preferred_element_type=F32, precision=lax.Precision.HIGHEST)
    heads = range(GDN_HEADS)
    sls = [slice(h * GDN_DK, (h + 1) * GDN_DK) for h in heads]
    lmats, ts, rhss = [], [], []
    for h in heads:
        sl = sls[h]
        q = q_ref[:, sl].astype(F32)
        k16 = k_ref[:, sl]
        k = k16.astype(F32)
        v = v_ref[:, sl].astype(F32)
        gc_col = gcb[:, h:h + 1]
        beta = gcb[:, GDN_HEADS + h:GDN_HEADS + h + 1]
        gc_row = gct_ref[h:h + 1, :]
        gc_last = gc_last_all[:, h:h + 1]
        diff = gc_col - gc_row
        decay = jnp.where(incl, jnp.exp(jnp.where(incl, diff, 0.0)), 0.0)
        kb = k * beta
        kk = _dot_nt(kb.astype(BF16), k16)
        lmat = jnp.where(strict, kk * decay, 0.0)
        lmats.append(lmat)
        ts.append(eye - jnp.where((ri // 2) == (ci // 2), lmat, 0.0))
        eg = jnp.exp(gc_col)
        rhss.append(jnp.concatenate([v * beta, kb * eg], axis=-1).astype(BF16))
        qd_ref[:, sl] = (q * eg).astype(BF16)
        kt_ref[:, sl] = (k * jnp.exp(gc_last - gc_col)).astype(BF16)
        qk = _dot_nt(q_ref[:, sl], k16) * decay
        qk_c = jnp.concatenate([qk[j * c:(j + 1) * c, j * c:(j + 1) * c] for j in range(n // c)], axis=0)
        qk_ref[:, h * c:(h + 1) * c] = qk_c.astype(BF16)
    b = 2
    while b < c:
        off_diag = ((ri // (2 * b)) == (ci // (2 * b))) & ((ri // b) != (ci // b))
        t16s = [ts[h].astype(BF16) for h in heads]
        ys = [_dot(t16s[h], jnp.where(off_diag, lmats[h], 0.0).astype(BF16)).astype(BF16) for h in heads]
        ts = [ts[h] - _dot(ys[h], t16s[h]) for h in heads]
        b *= 2
    for h in heads:
        uw = _dot(ts[h].astype(BF16), rhss[h])
        u_ref[:, sls[h]] = uw[:, :GDN_DV].astype(BF16)
        w_ref[:, sls[h]] = uw[:, GDN_DV:].astype(BF16)


def _gdn_local(q, k, v, gcb, gct):
    s = q.shape[0]
    n = GDN_GROUP
    row = lambda i: (i, 0)
    wide = pl.BlockSpec((n, GDN_QK), row)
    return pl.pallas_call(
        _gdn_local_kernel,
        out_shape=(jax.ShapeDtypeStruct((s, GDN_V), BF16), jax.ShapeDtypeStruct((s, GDN_QK), BF16),
                   jax.ShapeDtypeStruct((s, GDN_QK), BF16), jax.ShapeDtypeStruct((s, GDN_QK), BF16),
                   jax.ShapeDtypeStruct((s, GDN_HEADS * GDN_CHUNK), BF16)),
        grid=(s // n,),
        in_specs=[wide, wide, wide, pl.BlockSpec((n, LANES), row),
                  pl.BlockSpec((2 * GDN_HEADS, n), lambda i: (0, i))],
        out_specs=(wide, wide, wide, wide, pl.BlockSpec((n, GDN_HEADS * GDN_CHUNK), row)),
        compiler_params=_params(("parallel",)),
        name="gdn_local",
    )(q, k, v, gcb, gct)


def _gdn_scan_kernel(u_ref, w_ref, qd_ref, kt_ref, qk_ref, gcb_ref, z_ref, nw_ref, o_ref, st_ref, *, rb):
    c = GDN_CHUNK

    @pl.when(pl.program_id(0) == 0)
    def _():
        st_ref[...] = jnp.zeros_like(st_ref)

    nw = nw_ref[...]

    def chunk(ci, carry):
        r0 = pl.multiple_of(ci * c, c)
        rows = pl.ds(r0, c)
        gt_row = jnp.exp(gcb_ref[pl.ds(r0 + c - 1, 1), :])
        heads = range(GDN_HEADS)
        sls = [slice(h * GDN_DK, (h + 1) * GDN_DK) for h in heads]
        sts = [st_ref[h] for h in heads]
        r1s = [_dot(jnp.concatenate([w_ref[rows, sls[h]], qd_ref[rows, sls[h]]], axis=0),
                    sts[h].astype(BF16)) for h in heads]
        vns = [(u_ref[rows, sls[h]].astype(F32) - r1s[h][0:c]).astype(BF16) for h in heads]
        for h in heads:
            st_ref[h] = sts[h] * gt_row[:, h:h + 1] + _dot_tn(kt_ref[rows, sls[h]], vns[h])
        os_ = [r1s[h][c:2 * c] + _dot(qk_ref[rows, h * c:(h + 1) * c], vns[h]) for h in heads]
        for h in heads:
            o = _rms(os_[h]) * nw * _silu(z_ref[rows, sls[h]].astype(F32))
            o_ref[rows, sls[h]] = o.astype(BF16)
        return carry

    lax.fori_loop(0, rb // c, chunk, 0)


def _gdn_scan(u, w, qd, kt, qk, gcb, p, norm_w):
    s = u.shape[0]
    rb = min(512, s)
    row = lambda i: (i, 0)
    wide = pl.BlockSpec((rb, GDN_QK), row)
    return pl.pallas_call(
        functools.partial(_gdn_scan_kernel, rb=rb),
        out_shape=jax.ShapeDtypeStruct((s, GDN_V), BF16),
        grid=(s // rb,),
        in_specs=[wide, wide, wide, wide, pl.BlockSpec((rb, GDN_HEADS * GDN_CHUNK), row),
                  pl.BlockSpec((rb, LANES), row),
                  pl.BlockSpec((rb, GDN_V), lambda i: (i, P_Z // GDN_V)),
                  pl.BlockSpec((1, GDN_DV), lambda i: (0, 0))],
        out_specs=wide,
        scratch_shapes=[pltpu.VMEM((GDN_HEADS, GDN_DK, GDN_DV), F32)],
        compiler_params=_params(("arbitrary",)),
        name="gdn_scan",
    )(u, w, qd, kt, qk, gcb, p, norm_w.reshape(1, GDN_DV))


def _rope_table_kernel(pos_ref, inv_ref, cos_ref, sin_ref):
    ang = pos_ref[...].astype(F32) * inv_ref[...]
    lane = lax.broadcasted_iota(jnp.int32, ang.shape, 1)
    first_half = (lane % MLA_ROPE) < (MLA_ROPE // 2)
    cos_ref[...] = jnp.cos(ang)
    sin_ref[...] = jnp.where(first_half, -jnp.sin(ang), jnp.sin(ang))


def _rope_tables(positions):
    s = positions.shape[-1]
    tm = min(1024, s)
    half = MLA_ROPE // 2
    inv = (ROPE_THETA ** (-np.arange(half, dtype=np.float32) / half)).astype(np.float32)
    inv_row = jnp.asarray(np.tile(inv, LANES // half).reshape(1, LANES))
    return pl.pallas_call(
        _rope_table_kernel,
        out_shape=(jax.ShapeDtypeStruct((s, LANES), F32), jax.ShapeDtypeStruct((s, LANES), F32)),
        grid=(s // tm,),
        in_specs=[pl.BlockSpec((tm, 1), lambda i: (i, 0)), pl.BlockSpec((1, LANES), lambda i: (0, 0))],
        out_specs=(pl.BlockSpec((tm, LANES), lambda i: (i, 0)), pl.BlockSpec((tm, LANES), lambda i: (i, 0))),
        compiler_params=_params(("parallel",)),
        name="rope_tables",
    )(positions.reshape(s, 1), inv_row)


def _rope_apply(x, cos, sin_signed):
    width = x.shape[-1]
    half = MLA_ROPE // 2
    lane = lax.broadcasted_iota(jnp.int32, x.shape, 1)
    first_half = (lane % MLA_ROPE) < half
    swapped = jnp.where(first_half, pltpu.roll(x, width - half, 1), pltpu.roll(x, half, 1))
    return x * cos + swapped * sin_signed


def _mla_proj_kernel(cq_ref, ckv_ref, kr_ref, cos_ref, sin_ref, qn_ref, wuq_ref, kvn_ref, wuk_ref, wuvt_ref,
                     q_ref, k_ref, vt_ref):
    cos = cos_ref[...]
    sin = sin_ref[...]
    scale = (MLA_NOPE + MLA_ROPE) ** -0.5 * math.log2(math.e)
    cq = cq_ref[...].astype(F32)
    cqn = cq * lax.rsqrt(jnp.sum(cq * cq, axis=-1, keepdims=True) * (1.0 / MLA_Q_RANK) + EPS) * qn_ref[...]
    q = _dot(cqn.astype(BF16), wuq_ref[...])
    nope_w = MLA_HEADS * MLA_NOPE
    q_rope = _rope_apply(q[:, nope_w:], jnp.concatenate([cos, cos], -1), jnp.concatenate([sin, sin], -1))
    zeros = jnp.zeros((q.shape[0], MLA_QK_PAD - MLA_NOPE - MLA_ROPE), F32)
    ckvn = (_rms(ckv_ref[...].astype(F32)) * kvn_ref[...]).astype(BF16)
    k_nope = _dot(ckvn, wuk_ref[...])
    v_t = _dot_nt(wuvt_ref[...], ckvn)
    k_pe = _rope_apply(kr_ref[...].astype(F32), cos, sin)[:, :MLA_ROPE]
    for h in range(MLA_HEADS):
        qh = jnp.concatenate([q[:, h * MLA_NOPE:(h + 1) * MLA_NOPE],
                              q_rope[:, h * MLA_ROPE:(h + 1) * MLA_ROPE], zeros], axis=-1) * scale
        q_ref[h] = qh.astype(BF16)
        kh = jnp.concatenate([k_nope[:, h * MLA_NOPE:(h + 1) * MLA_NOPE], k_pe, zeros], axis=-1)
        k_ref[h] = kh.astype(BF16)
        vt_ref[h, 0] = v_t[h * MLA_DV:(h + 1) * MLA_DV, :].astype(BF16)


def _mla_proj(p, cos, sin, q_norm, w_uq, kv_norm, w_uk, w_uvt):
    s = p.shape[0]
    tm = min(MLA_BLOCK, s)
    const = lambda i: (0, 0)
    return pl.pallas_call(
        _mla_proj_kernel,
        out_shape=(jax.ShapeDtypeStruct((MLA_HEADS, s, MLA_QK_PAD), BF16),
                   jax.ShapeDtypeStruct((MLA_HEADS, s, MLA_QK_PAD), BF16),
                   jax.ShapeDtypeStruct((MLA_HEADS, s // tm, MLA_DV, tm), BF16)),
        grid=(s // tm,),
        in_specs=[pl.BlockSpec((tm, MLA_Q_RANK_PAD), lambda i: (i, P_CQ // MLA_Q_RANK_PAD)),
                  pl.BlockSpec((tm, LANES), lambda i: (i, P_CKV // LANES)),
                  pl.BlockSpec((tm, LANES), lambda i: (i, P_KROPE // LANES)),
                  pl.BlockSpec((tm, LANES), lambda i: (i, 0)),
                  pl.BlockSpec((tm, LANES), lambda i: (i, 0)),
                  pl.BlockSpec(q_norm.shape, const), pl.BlockSpec(w_uq.shape, const),
                  pl.BlockSpec(kv_norm.shape, const), pl.BlockSpec(w_uk.shape, const),
                  pl.BlockSpec(w_uvt.shape, const)],
        out_specs=(pl.BlockSpec((MLA_HEADS, tm, MLA_QK_PAD), lambda i: (0, i, 0)),
                   pl.BlockSpec((MLA_HEADS, tm, MLA_QK_PAD), lambda i: (0, i, 0)),
                   pl.BlockSpec((MLA_HEADS, 1, MLA_DV, tm), lambda i: (0, i, 0, 0))),
        compiler_params=_params(("parallel",)),
        name="mla_proj",
    )(p, p, p, cos, sin, q_norm, w_uq, kv_norm, w_uk, w_uvt)


def _mla_attn_kernel(q_ref, k_ref, vt_ref, o_ref, st0, st1, mb0, mb1, m_ref, l_ref, acc_ref, *, bk):
    qi = pl.program_id(1)
    slots = ((st0, mb0), (st1, mb1))
    key = lax.broadcasted_iota(jnp.int32, (bk, 2 * bk), 0)
    qry = lax.broadcasted_iota(jnp.int32, (bk, 2 * bk), 1)

    def scores(t, slot, mask=None, q0=0):
        st_ref, mb_ref = slots[slot]
        r0 = pl.multiple_of(t * bk, bk)
        st = _dot_nt(k_ref[0, pl.ds(r0, bk), :], q_ref[0, q0:, :])
        if mask is not None:
            st = jnp.where(mask, st, -jnp.inf)
        st_ref[:, q0:] = st
        mb_ref[:, q0:] = jnp.max(st, axis=0, keepdims=True)

    def update(t, slot, q0=0):
        st_ref, mb_ref = slots[slot]
        m = m_ref[:, q0:]
        m_new = jnp.maximum(m, mb_ref[:, q0:])
        alpha = jnp.exp2(m - m_new)
        pexp = jnp.exp2(st_ref[:, q0:] - m_new)
        l_ref[:, q0:] = alpha * l_ref[:, q0:] + jnp.sum(pexp, axis=0, keepdims=True)
        acc_ref[:, q0:] = alpha * acc_ref[:, q0:] + _dot(vt_ref[0, t], pexp.astype(BF16))
        m_ref[:, q0:] = m_new

    m_ref[...] = jnp.full(m_ref.shape, -jnp.inf, F32)
    l_ref[...] = jnp.zeros(l_ref.shape, F32)
    acc_ref[...] = jnp.zeros(acc_ref.shape, F32)
    diag0 = key <= qry
    diag1 = diag0[:, :bk]

    @pl.when(qi == 0)
    def _():
        scores(0, 0, diag0)

    @pl.when(qi > 0)
    def _():
        scores(0, 0)

    def pair(j, carry):
        scores(2 * j + 1, 1)
        update(2 * j, 0)
        scores(2 * j + 2, 0)
        update(2 * j + 1, 1)
        return carry

    lax.fori_loop(0, qi - 1, pair, 0)

    @pl.when(qi > 0)
    def _():
        scores(2 * qi - 1, 1)
        update(2 * qi - 2, 0)
        scores(2 * qi, 0, diag0)
        update(2 * qi - 1, 1)

    scores(2 * qi + 1, 1, diag1, q0=bk)
    update(2 * qi, 0)
    update(2 * qi + 1, 1, q0=bk)
    o_ref[...] = (acc_ref[...] / l_ref[...]).T.astype(BF16)


def _mla_attn(q, k, vt):
    nh, s, _ = q.shape
    bk = vt.shape[-1]
    bq = 2 * bk
    assert s % bq == 0
    return pl.pallas_call(
        functools.partial(_mla_attn_kernel, bk=bk),
        out_shape=jax.ShapeDtypeStruct((s, nh * MLA_DV), BF16),
        grid=(nh, s // bq),
        in_specs=[pl.BlockSpec((1, bq, MLA_QK_PAD), lambda h, i: (h, i, 0)),
                  pl.BlockSpec((1, s, MLA_QK_PAD), lambda h, i: (h, 0, 0)),
                  pl.BlockSpec((1, s // bk, MLA_DV, bk), lambda h, i: (h, 0, 0, 0))],
        out_specs=pl.BlockSpec((bq, MLA_DV), lambda h, i: (i, h)),
        scratch_shapes=[pltpu.VMEM((bk, bq), F32), pltpu.VMEM((bk, bq), F32),
                        pltpu.VMEM((1, bq), F32), pltpu.VMEM((1, bq), F32),
                        pltpu.VMEM((1, bq), F32), pltpu.VMEM((1, bq), F32), pltpu.VMEM((MLA_DV, bq), F32)],
        compiler_params=_params(("parallel", "arbitrary")),
        name="mla_attn",
    )(q, k, vt)


def _swa_kernel(q_ref, k_ref, v_ref, kh_ref, vh_ref, sink_ref, o_ref, *, nblk):
    w = SWA_WINDOW
    g = SWA_GROUP
    i = pl.program_id(0)
    key = lax.broadcasted_iota(jnp.int32, (2 * w, g * w), 0)
    col = lax.broadcasted_iota(jnp.int32, (2 * w, g * w), 1)
    dist = col % w + w - key
    band = (dist >= 0) & (dist < w)
    first_band = band & ((key >= w) | (i > 0))
    dist_f = dist.astype(F32)
    biases = []
    for hk in range(SWA_KV_HEADS):
        slope = jnp.zeros((2 * w, g * w), F32)
        for gi, gq in enumerate(SWA_GORDER):
            slope = jnp.where(col // w == gi, 2.0 ** (-8.0 * (hk * g + gq + 1.0) / SWA_HEADS), slope)
        biases.append(-slope * dist_f)
    low = lax.broadcasted_iota(jnp.int32, (2 * w, 2 * SWA_DH), 1) < SWA_DH
    top = lax.broadcasted_iota(jnp.int32, (2 * SWA_DH, 1), 0) < SWA_DH
    for blk in range(nblk):
        rows = slice(blk * w, (blk + 1) * w)
        if blk == 0:
            k_prev, v_prev, valid = kh_ref[...], vh_ref[...], first_band
        else:
            k_prev, v_prev, valid = k_ref[(blk - 1) * w:blk * w, :], v_ref[(blk - 1) * w:blk * w, :], band
        kk = jnp.concatenate([k_prev, k_ref[rows, :]], axis=0).astype(F32)
        kk_sw = pltpu.roll(kk, SWA_DH, 1)
        vvt = jnp.concatenate([v_prev, v_ref[rows, :]], axis=0).astype(F32).T
        probs, inv_denoms = [], []
        for hk in range(SWA_KV_HEADS):
            c0 = hk * g * SWA_DH
            q2 = jnp.concatenate([q_ref[rows, c0:c0 + 2 * SWA_DH], q_ref[rows, c0 + 2 * SWA_DH:c0 + 4 * SWA_DH]],
                                 axis=0) * (SWA_DH ** -0.5)
            k_low = jnp.where(low, kk if hk == 0 else kk_sw, 0.0).astype(BF16)
            k_high = jnp.where(low, 0.0, kk_sw if hk == 0 else kk).astype(BF16)
            st = jnp.concatenate([_dot_nt(k_low, q2), _dot_nt(k_high, q2)], axis=1)
            st = jnp.where(valid, st + biases[hk], -jnp.inf)
            sink = sink_ref[hk]
            m = jnp.maximum(jnp.max(st, axis=0, keepdims=True), sink)
            pexp = jnp.exp(st - m)
            denom = jnp.sum(pexp, axis=0, keepdims=True) + jnp.exp(sink - m)
            probs.append(pexp.astype(BF16))
            inv_denoms.append(1.0 / denom)
        v0 = jnp.where(top, vvt, 0.0).astype(BF16)
        v1 = jnp.where(top, 0.0, vvt).astype(BF16)
        o_t = (_dot(v0, probs[0]) + _dot(v1, probs[1])) * jnp.where(top, inv_denoms[0], inv_denoms[1])
        for gi in range(g):
            o_ref[rows, gi * LANES:(gi + 1) * LANES] = o_t[:, gi * w:(gi + 1) * w].T.astype(BF16)


def _swa(p, sinks):
    s = p.shape[0]
    w = SWA_WINDOW
    rb = min(512, s)
    nblk = rb // w
    order = np.asarray([[hk * SWA_GROUP + gq for gq in SWA_GORDER] for hk in range(SWA_KV_HEADS)])
    sink_rows = jnp.repeat(sinks[order], w, axis=1).reshape(SWA_KV_HEADS, 1, SWA_GROUP * w)
    prev = lambda col: (lambda i: (jnp.maximum(i * nblk - 1, 0), col))
    return pl.pallas_call(
        functools.partial(_swa_kernel, nblk=nblk),
        out_shape=jax.ShapeDtypeStruct((s, SWA_OUT), BF16),
        grid=(s // rb,),
        in_specs=[pl.BlockSpec((rb, SWA_OUT), lambda i: (i, P_SWQ // SWA_OUT)),
                  pl.BlockSpec((rb, SWA_KV), lambda i: (i, P_SWK // SWA_KV)),
                  pl.BlockSpec((rb, SWA_KV), lambda i: (i, P_SWV // SWA_KV)),
                  pl.BlockSpec((w, SWA_KV), prev(P_SWK // SWA_KV)),
                  pl.BlockSpec((w, SWA_KV), prev(P_SWV // SWA_KV)),
                  pl.BlockSpec(sink_rows.shape, lambda i: (0, 0, 0))],
        out_specs=pl.BlockSpec((rb, SWA_OUT), lambda i: (i, 0)),
        compiler_params=_params(("parallel",)),
        name="swa",
    )(p, p, p, p, p, sink_rows)


def _outproj_kernel(oa_ref, ob_ref, oc_ref, w_ref, x_ref, pn_ref, g1_ref, fn_ref, sc_ref, sh_ref,
                    x1_ref, h2_ref):
    a_w, b_w = oa_ref.shape[1], ob_ref.shape[1]
    mix = (_dot(oa_ref[...], w_ref[0:a_w, :]) + _dot(ob_ref[...], w_ref[a_w:a_w + b_w, :])
           + _dot(oc_ref[...], w_ref[a_w + b_w:, :]))
    x1 = x_ref[...] + g1_ref[...] * (_rms(mix) * pn_ref[...])
    x1_ref[...] = x1
    h2_ref[...] = _norm_mod(x1, fn_ref[...], sc_ref[...], sh_ref[...]).astype(BF16)


def _out_proj(o_a, o_b, o_c, w_out, x, post_norm, gate1, ffn_norm, scale2, shift2):
    s, d = x.shape
    tm = min(256, s)
    row = lambda i: (i, 0)
    vec = pl.BlockSpec((1, d), lambda i: (0, 0))
    return pl.pallas_call(
        _outproj_kernel,
        out_shape=(jax.ShapeDtypeStruct((s, d), F32), jax.ShapeDtypeStruct((s, d), BF16)),
        grid=(s // tm,),
        in_specs=[pl.BlockSpec((tm, o_a.shape[1]), row), pl.BlockSpec((tm, o_b.shape[1]), row),
                  pl.BlockSpec((tm, o_c.shape[1]), row),
                  pl.BlockSpec(w_out.shape, lambda i: (0, 0), pipeline_mode=pl.Buffered(1)),
                  pl.BlockSpec((tm, d), row), vec, vec, vec, vec, vec],
        out_specs=(pl.BlockSpec((tm, d), row), pl.BlockSpec((tm, d), row)),
        compiler_params=_params(("parallel",)),
        name="out_proj",
    )(o_a, o_b, o_c, w_out, x, post_norm, gate1, ffn_norm, scale2, shift2)


def _gelu_tanh(x):
    return 0.5 * x * (1.0 + jnp.tanh(math.sqrt(2.0 / math.pi) * (x + 0.044715 * (x * x * x))))


def _ffn_up_kernel(h_ref, wg_ref, wu_ref, cg_ref, cu_ref, bg_ref, bu_ref, o_ref, xg_ref, xu_ref, *, tm):
    @pl.when(pl.program_id(1) == 0)
    def _():
        xg_ref[0:8, :] = jnp.zeros((8, xg_ref.shape[1]), F32)
        xu_ref[0:8, :] = jnp.zeros((8, xu_ref.shape[1]), F32)

    h = h_ref[...]

    def conv(x_ref, w_ref, cw_ref, b_ref):
        x_ref[8:8 + tm, :] = _dot(h, w_ref[...])
        y = b_ref[...] + cw_ref[2:3, :] * x_ref[8:8 + tm, :]
        y = y + cw_ref[1:2, :] * x_ref[7:7 + tm, :]
        y = y + cw_ref[0:1, :] * x_ref[6:6 + tm, :]
        x_ref[0:8, :] = x_ref[tm:tm + 8, :]
        return y

    gate = conv(xg_ref, wg_ref, cg_ref, bg_ref)
    up = conv(xu_ref, wu_ref, cu_ref, bu_ref)
    o_ref[...] = (_gelu_tanh(gate) * up).astype(BF16)


def _ffn_up(h2, w_up, conv_w, conv_b):
    s, d = h2.shape
    d_ff = w_up.shape[1] // 2
    tm, tn = min(1024, s), 512
    nj = d_ff // tn
    lo = lambda j, i: (0, j)
    hi = lambda j, i: (0, j + nj)
    return pl.pallas_call(
        functools.partial(_ffn_up_kernel, tm=tm),
        out_shape=jax.ShapeDtypeStruct((s, d_ff), BF16),
        grid=(nj, s // tm),
        in_specs=[pl.BlockSpec((tm, d), lambda j, i: (i, 0)),
                  pl.BlockSpec((d, tn), lo), pl.BlockSpec((d, tn), hi),
                  pl.BlockSpec((FFN_CONV, tn), lo), pl.BlockSpec((FFN_CONV, tn), hi),
                  pl.BlockSpec((1, tn), lo), pl.BlockSpec((1, tn), hi)],
        out_specs=pl.BlockSpec((tm, tn), lambda j, i: (i, j)),
        scratch_shapes=[pltpu.VMEM((tm + 8, tn), F32), pltpu.VMEM((tm + 8, tn), F32)],
        compiler_params=_params(("parallel", "arbitrary")),
        name="ffn_up",
    )(h2, w_up, w_up, conv_w, conv_w, conv_b, conv_b)


def _ffn_down_kernel(g_ref, w_ref, x_ref, pn_ref, g2_ref, nn_ref, sc_ref, sh_ref, x2_ref, hn_ref):
    y = _dot(g_ref[...], w_ref[...])
    x2 = x_ref[...] + g2_ref[...] * (_rms(y) * pn_ref[...])
    x2_ref[...] = x2
    hn_ref[...] = _norm_mod(x2, nn_ref[...], sc_ref[...], sh_ref[...]).astype(BF16)


def _ffn_down(g, w_down, x1, post_norm, gate2, next_norm, next_scale, next_shift):
    s, d = x1.shape
    d_ff = g.shape[1]
    tm = min(256, s)
    row = lambda i: (i, 0)
    vec = pl.BlockSpec((1, d), lambda i: (0, 0))
    return pl.pallas_call(
        _ffn_down_kernel,
        out_shape=(jax.ShapeDtypeStruct((s, d), F32), jax.ShapeDtypeStruct((s, d), BF16)),
        grid=(s // tm,),
        in_specs=[pl.BlockSpec((tm, d_ff), row),
                  pl.BlockSpec((d_ff, d), lambda i: (0, 0), pipeline_mode=pl.Buffered(1)),
                  pl.BlockSpec((tm, d), row), vec, vec, vec, vec, vec],
        out_specs=(pl.BlockSpec((tm, d), row), pl.BlockSpec((tm, d), row)),
        compiler_params=_params(("parallel",)),
        name="ffn_down",
    )(g, w_down, x1, post_norm, gate2, next_norm, next_scale, next_shift)


def _layout_w_in(w_in):
    d = w_in.shape[0]
    sizes = (GDN_QK, GDN_QK, GDN_V, GDN_V, GDN_HEADS, GDN_HEADS, MLA_Q_RANK, MLA_KV_RANK, MLA_ROPE,
             SWA_OUT, SWA_KV, SWA_KV)
    offs = np.concatenate([[0], np.cumsum(sizes)])
    part = lambda n: w_in[:, offs[n]:offs[n + 1]]
    zeros = lambda n: jnp.zeros((d, n), w_in.dtype)
    w_p = jnp.concatenate([part(0), part(1), part(2), part(3),
                           part(6), zeros(MLA_Q_RANK_PAD - MLA_Q_RANK),
                           part(9), part(7), part(8), zeros(LANES - MLA_ROPE), part(10), part(11)],
                          axis=1).astype(BF16)
    w_ab = jnp.concatenate([part(4), part(5), zeros(LANES - 2 * GDN_HEADS)], axis=1).astype(BF16)
    return w_p, w_ab


def _layout_w_out(w_out):
    a = w_out.shape[0] - SWA_OUT
    wc = w_out[a:].reshape(SWA_KV_HEADS, SWA_GROUP, SWA_DH, -1)[:, np.asarray(SWA_GORDER)]
    wc = wc.transpose(1, 0, 2, 3).reshape(SWA_OUT, -1)
    return jnp.concatenate([w_out[:a], wc], axis=0).astype(BF16)


def _layout_mla(q_norm, w_uq, kv_norm, w_ukv):
    dqk = MLA_NOPE + MLA_ROPE
    uq = w_uq.reshape(MLA_Q_RANK, MLA_HEADS, dqk)
    uq = jnp.concatenate([uq[:, :, :MLA_NOPE].reshape(MLA_Q_RANK, -1), uq[:, :, MLA_NOPE:].reshape(MLA_Q_RANK, -1)],
                         axis=1)
    uq = jnp.pad(uq, ((0, MLA_Q_RANK_PAD - MLA_Q_RANK), (0, 0))).astype(BF16)
    qn = jnp.pad(q_norm, (0, MLA_Q_RANK_PAD - MLA_Q_RANK)).reshape(1, MLA_Q_RANK_PAD)
    ukv = w_ukv.reshape(MLA_KV_RANK, MLA_HEADS, MLA_NOPE + MLA_DV)
    uk = ukv[:, :, :MLA_NOPE].reshape(MLA_KV_RANK, -1).astype(BF16)
    uvt = ukv[:, :, MLA_NOPE:].reshape(MLA_KV_RANK, -1).T.astype(BF16)
    return qn, uq, kv_norm.reshape(1, MLA_KV_RANK), uk, uvt


def kernel(x, c, positions, ada_w, ada_b, mix_pre_norm, mix_post_norm, w_in, w_out, gdn_conv, gdn_a_log, gdn_dt_bias, gdn_norm, mla_q_norm, mla_w_uq, mla_kv_norm, mla_w_ukv, swa_sinks, ffn_pre_norm, ffn_post_norm, ffn_w_up, ffn_conv, ffn_conv_b, ffn_w_down):
    batch, s, d = x.shape
    assert batch == 1, "kernels are written for a single sequence"
    depth = ada_w.shape[0]
    xs = x.reshape(s, d)
    mod = _adaln_mod(c, ada_w, ada_b).reshape(depth, N_MOD, 1, d)
    cos, sin = _rope_tables(positions)
    vec = lambda a: a.reshape(1, d)

    h = _pre_norm(xs, vec(mix_pre_norm[0]), mod[0, 1], mod[0, 0])
    for l in range(depth):
        shift1, scale1, gate1, shift2, scale2, gate2 = (mod[l, n] for n in range(N_MOD))
        w_p, w_ab = _layout_w_in(w_in[l])
        p, ab = _in_proj(h, w_p, w_ab)

        q_a, k_a, v_a, gcb, gct = _gdn_prep(p, ab, gdn_conv[l], gdn_a_log[l], gdn_dt_bias[l])
        u, w, qd, kt, qk = _gdn_local(q_a, k_a, v_a, gcb, gct)
        o_a = _gdn_scan(u, w, qd, kt, qk, gcb, p, gdn_norm[l])

        q_b, k_b, v_b = _mla_proj(p, cos, sin, *_layout_mla(mla_q_norm[l], mla_w_uq[l], mla_kv_norm[l],
                                                            mla_w_ukv[l]))
        o_b = _mla_attn(q_b, k_b, v_b)

        o_c = _swa(p, swa_sinks[l])

        x1, h2 = _out_proj(o_a, o_b, o_c, _layout_w_out(w_out[l]), xs, vec(mix_post_norm[l]), gate1,
                           vec(ffn_pre_norm[l]), scale2, shift2)
        g = _ffn_up(h2, ffn_w_up[l].astype(BF16), ffn_conv[l], ffn_conv_b[l].reshape(1, -1))
        nl = min(l + 1, depth - 1)
        xs, h = _ffn_down(g, ffn_w_down[l].astype(BF16), x1, vec(ffn_post_norm[l]), gate2,
                          vec(mix_pre_norm[nl]), mod[nl, 1], mod[nl, 0])
    return xs.reshape(batch, s, d)
```

```python
import functools
import math

import numpy as np
import jax
import jax.numpy as jnp
from jax import lax
from jax.experimental import pallas as pl
from jax.experimental.pallas import tpu as pltpu

F32 = jnp.float32
BF16 = jnp.bfloat16

EPS = 1e-6
N_MOD = 6
GDN_HEADS = 8
GDN_DK = 128
GDN_DV = 128
GDN_CONV = 4
GDN_CHUNK = 64
GDN_QK = GDN_HEADS * GDN_DK
GDN_V = GDN_HEADS * GDN_DV
MLA_HEADS = 4
MLA_Q_RANK = 448
MLA_Q_RANK_PAD = 512
MLA_KV_RANK = 128
MLA_NOPE = 128
MLA_ROPE = 64
MLA_DV = 128
MLA_QK_PAD = 256
MLA_BLOCK = 512
ROPE_THETA = 10000.0
SWA_HEADS = 8
SWA_KV_HEADS = 2
SWA_GROUP = SWA_HEADS // SWA_KV_HEADS
SWA_DH = 64
SWA_WINDOW = 128
SWA_OUT = SWA_HEADS * SWA_DH
SWA_KV = SWA_KV_HEADS * SWA_DH
SWA_GORDER = (0, 2, 1, 3)
FFN_CONV = 3

LANES = 128
GDN_GROUP = 256
VMEM_LIMIT_MB = 56

P_QKV = 0
P_Z = 3072
P_CQ = 4096
P_SWQ = 4608
P_CKV = 5120
P_KROPE = 5248
P_SWK = 5376
P_SWV = 5504
P_WIDTH = 5632


def _params(semantics):
    return pltpu.CompilerParams(dimension_semantics=semantics, vmem_limit_bytes=VMEM_LIMIT_MB << 20)


def _sigmoid(x):
    return 1.0 / (1.0 + jnp.exp(-x))


def _silu(x):
    return x * _sigmoid(x)


def _softplus(x):
    return jnp.maximum(x, 0.0) + jnp.log(1.0 + jnp.exp(-jnp.abs(x)))


def _rms(x):
    return x * lax.rsqrt(jnp.mean(x * x, axis=-1, keepdims=True) + EPS)


def _dot(a, b):
    return jnp.dot(a, b, preferred_element_type=F32)


def _dot_nt(a, b):
    return lax.dot_general(a, b, (((1,), (1,)), ((), ())), preferred_element_type=F32)


def _dot_tn(a, b):
    return lax.dot_general(a, b, (((0,), (0,)), ((), ())), preferred_element_type=F32)


def _mod_kernel(c_ref, w_ref, b_ref, o_ref, *, d, kc):
    cact = _silu(c_ref[...])
    acc = b_ref[0]
    for k0 in range(0, d, kc):
        acc = acc + jnp.sum(w_ref[0, k0:k0 + kc, :] * cact[k0:k0 + kc], axis=0, keepdims=True)
    o_ref[0] = acc


def _adaln_mod(c, ada_w, ada_b):
    depth, d, n = ada_w.shape
    tn = 1024
    return pl.pallas_call(
        functools.partial(_mod_kernel, d=d, kc=256),
        out_shape=jax.ShapeDtypeStruct((depth, 1, n), F32),
        grid=(depth, n // tn),
        in_specs=[pl.BlockSpec((d, 1), lambda l, j: (0, 0)),
                  pl.BlockSpec((1, d, tn), lambda l, j: (l, 0, j)),
                  pl.BlockSpec((1, 1, tn), lambda l, j: (l, 0, j))],
        out_specs=pl.BlockSpec((1, 1, tn), lambda l, j: (l, 0, j)),
        compiler_params=_params(("parallel", "parallel")),
        name="adaln_mod",
    )(c.reshape(d, 1), ada_w, ada_b.reshape(depth, 1, n))


def _norm_mod(x, w, scale, shift):
    return _rms(x) * w * (1.0 + scale) + shift


def _norm_mod_kernel(x_ref, w_ref, sc_ref, sh_ref, h_ref):
    h_ref[...] = _norm_mod(x_ref[...], w_ref[...], sc_ref[...], sh_ref[...]).astype(BF16)


def _pre_norm(x, w, scale, shift):
    s, d = x.shape
    tm = min(512, s)
    row = pl.BlockSpec((1, d), lambda i: (0, 0))
    return pl.pallas_call(
        _norm_mod_kernel,
        out_shape=jax.ShapeDtypeStruct((s, d), BF16),
        grid=(s // tm,),
        in_specs=[pl.BlockSpec((tm, d), lambda i: (i, 0)), row, row, row],
        out_specs=pl.BlockSpec((tm, d), lambda i: (i, 0)),
        compiler_params=_params(("parallel",)),
        name="pre_norm",
    )(x, w, scale, shift)


def _inproj_kernel(h_ref, w_ref, wt_ref, wab_ref, p_ref, ab_ref, *, n_head):
    j = pl.program_id(1)
    h = h_ref[...]

    @pl.when(j < n_head)
    def _():
        p_ref[...] = _dot(h, w_ref[...].astype(BF16)).astype(BF16)

    @pl.when(j >= n_head)
    def _():
        p_ref[...] = _dot(h, wt_ref[...].astype(BF16)).astype(BF16)

    @pl.when(j == 0)
    def _():
        ab_ref[...] = _dot(h, wab_ref[...])


def _in_proj(h, w_in, layer, w_tail, w_ab):
    s, d = h.shape
    tm, tn = min(2048, s), 512
    n_head = P_CQ // tn
    n = P_CQ + w_tail.shape[1]
    return pl.pallas_call(
        functools.partial(_inproj_kernel, n_head=n_head),
        out_shape=(jax.ShapeDtypeStruct((s, n), BF16), jax.ShapeDtypeStruct((s, LANES), F32)),
        grid=(s // tm, n // tn),
        in_specs=[pl.BlockSpec((tm, d), lambda i, j: (i, 0)),
                  pl.BlockSpec((None, d, tn), lambda i, j: (layer, 0, jnp.minimum(j, n_head - 1))),
                  pl.BlockSpec((d, tn), lambda i, j: (0, jnp.maximum(j - n_head, 0))),
                  pl.BlockSpec((d, LANES), lambda i, j: (0, 0))],
        out_specs=(pl.BlockSpec((tm, tn), lambda i, j: (i, j)),
                   pl.BlockSpec((tm, LANES), lambda i, j: (i, 0))),
        compiler_params=_params(("parallel", "arbitrary")),
        name="in_proj",
    )(h, w_in, w_tail, w_ab)


def _gdn_prep_kernel(x_ref, halo_ref, cw_ref, ab_ref, alog_ref, dtb_ref, tri_ref,
                     q_ref, k_ref, v_ref, gcb_ref, gct_ref, xp_ref, *, tm):
    i = pl.program_id(0)
    outs = (q_ref, k_ref, v_ref)
    for grp in range(3):
        c0 = grp * GDN_QK
        halo = halo_ref[:, c0:c0 + GDN_QK].astype(F32)
        xp_ref[0:8, :] = jnp.where(i > 0, halo, 0.0)
        xp_ref[8:8 + tm, :] = x_ref[:, c0:c0 + GDN_QK].astype(F32)
        y = cw_ref[3:4, c0:c0 + GDN_QK] * xp_ref[8:8 + tm, :]
        for j in range(GDN_CONV - 1):
            off = 8 - (GDN_CONV - 1) + j
            y = y + cw_ref[j:j + 1, c0:c0 + GDN_QK] * xp_ref[off:off + tm, :]
        y = _silu(y)
        for h in range(GDN_HEADS):
            yh = y[:, h * GDN_DK:(h + 1) * GDN_DK]
            if grp < 2:
                yh = yh * lax.rsqrt(jnp.sum(yh * yh, axis=-1, keepdims=True) + EPS)
            if grp == 0:
                yh = yh * (GDN_DK ** -0.5)
            outs[grp][:, h * GDN_DK:(h + 1) * GDN_DK] = yh.astype(BF16)

    ab = ab_ref[...]
    g = -jnp.exp(alog_ref[...]) * _softplus(ab + dtb_ref[...])
    gc = jnp.dot(tri_ref[...], g, preferred_element_type=F32, precision=lax.Precision.HIGHEST)
    lane = lax.broadcasted_iota(jnp.int32, (1, LANES), 1)
    gcb = jnp.where(lane < GDN_HEADS, gc, _sigmoid(ab))
    gcb_ref[...] = gcb
    gct_ref[...] = gcb.T[0:2 * GDN_HEADS, :]


def _gdn_prep(p, ab, conv_w, a_log, dt_bias):
    s = p.shape[0]
    tm = min(512, s)
    c3 = 3 * GDN_QK
    alog_row = jnp.zeros((1, LANES), F32).at[0, :GDN_HEADS].set(a_log)
    dtb_row = jnp.zeros((1, LANES), F32).at[0, :GDN_HEADS].set(dt_bias)
    r = np.arange(tm)
    tri = jnp.asarray(((r[:, None] >= r[None, :]) & (r[:, None] // GDN_CHUNK == r[None, :] // GDN_CHUNK))
                      .astype(np.float32))
    hb = tm // 8
    row = lambda i: (i, 0)
    return pl.pallas_call(
        functools.partial(_gdn_prep_kernel, tm=tm),
        out_shape=(jax.ShapeDtypeStruct((s, GDN_QK), BF16), jax.ShapeDtypeStruct((s, GDN_QK), BF16),
                   jax.ShapeDtypeStruct((s, GDN_V), BF16), jax.ShapeDtypeStruct((s, LANES), F32),
                   jax.ShapeDtypeStruct((2 * GDN_HEADS, s), F32)),
        grid=(s // tm,),
        in_specs=[pl.BlockSpec((tm, c3), row),
                  pl.BlockSpec((8, c3), lambda i: (jnp.maximum(i * hb - 1, 0), 0)),
                  pl.BlockSpec((GDN_CONV, c3), lambda i: (0, 0)),
                  pl.BlockSpec((tm, LANES), row),
                  pl.BlockSpec((1, LANES), lambda i: (0, 0)),
                  pl.BlockSpec((1, LANES), lambda i: (0, 0)),
                  pl.BlockSpec((tm, tm), lambda i: (0, 0))],
        out_specs=(pl.BlockSpec((tm, GDN_QK), row), pl.BlockSpec((tm, GDN_QK), row),
                   pl.BlockSpec((tm, GDN_V), row), pl.BlockSpec((tm, LANES), row),
                   pl.BlockSpec((2 * GDN_HEADS, tm), lambda i: (0, i))),
        scratch_shapes=[pltpu.VMEM((tm + 8, GDN_QK), F32)],
        compiler_params=_params(("parallel",)),
        name="gdn_prep",
    )(p, p, conv_w, ab, alog_row, dtb_row, tri)


def _gdn_local_kernel(q_ref, k_ref, v_ref, gcb_ref, gct_ref,
                      u_ref, w_ref, qd_ref, kt_ref, qk_ref):
    n = GDN_GROUP
    c = GDN_CHUNK
    ri = lax.broadcasted_iota(jnp.int32, (n, n), 0)
    ci = lax.broadcasted_iota(jnp.int32, (n, n), 1)
    same_chunk = (ri // c) == (ci // c)
    incl = same_chunk & (ri >= ci)
    strict = same_chunk & (ri > ci)
    eye = (ri == ci).astype(F32)
    last_sel = (ci == (ri // c) * c + (c - 1)).astype(F32)
    gcb = gcb_ref[...]
    gc_last_all = jnp.dot(last_sel, gcb, preferred_element_type=F32, precision=lax.Precision.HIGHEST)
    heads = range(GDN_HEADS)
    sls = [slice(h * GDN_DK, (h + 1) * GDN_DK) for h in heads]
    lmats, ts, rhss = [], [], []
    for h in heads:
        sl = sls[h]
        q = q_ref[:, sl].astype(F32)
        k16 = k_ref[:, sl]
        k = k16.astype(F32)
        v = v_ref[:, sl].astype(F32)
        gc_col = gcb[:, h:h + 1]
        beta = gcb[:, GDN_HEADS + h:GDN_HEADS + h + 1]
        gc_row = gct_ref[h:h + 1, :]
        gc_last = gc_last_all[:, h:h + 1]
        diff = gc_col - gc_row
        decay = jnp.where(incl, jnp.exp(jnp.where(incl, diff, 0.0)), 0.0)
        kb = k * beta
        kk = _dot_nt(kb.astype(BF16), k16)
        lmat = jnp.where(strict, kk * decay, 0.0)
        lmats.append(lmat)
        ts.append(eye - jnp.where((ri // 2) == (ci // 2), lmat, 0.0))
        eg = jnp.exp(gc_col)
        rhss.append(jnp.concatenate([v * beta, kb * eg], axis=-1).astype(BF16))
        qd_ref[:, sl] = (q * eg).astype(BF16)
        kt_ref[:, sl] = (k * jnp.exp(gc_last - gc_col)).astype(BF16)
        qk = _dot_nt(q_ref[:, sl], k16) * decay
        qk_c = jnp.concatenate([qk[j * c:(j + 1) * c, j * c:(j + 1) * c] for j in range(n // c)], axis=0)
        qk_ref[:, h * c:(h + 1) * c] = qk_c.astype(BF16)
    b = 2
    while b < c:
        off_diag = ((ri // (2 * b)) == (ci // (2 * b))) & ((ri // b) != (ci // b))
        t16s = [ts[h].astype(BF16) for h in heads]
        ys = [_dot(t16s[h], jnp.where(off_diag, lmats[h], 0.0).astype(BF16)).astype(BF16) for h in heads]
        ts = [ts[h] - _dot(ys[h], t16s[h]) for h in heads]
        b *= 2
    for h in heads:
        uw = _dot(ts[h].astype(BF16), rhss[h])
        u_ref[:, sls[h]] = uw[:, :GDN_DV].astype(BF16)
        w_ref[:, sls[h]] = uw[:, GDN_DV:].astype(BF16)


def _gdn_local(q, k, v, gcb, gct):
    s = q.shape[0]
    n = GDN_GROUP
    row = lambda i: (i, 0)
    wide = pl.BlockSpec((n, GDN_QK), row)
    return pl.pallas_call(
        _gdn_local_kernel,
        out_shape=(jax.ShapeDtypeStruct((s, GDN_V), BF16), jax.ShapeDtypeStruct((s, GDN_QK), BF16),
                   jax.ShapeDtypeStruct((s, GDN_QK), BF16), jax.ShapeDtypeStruct((s, GDN_QK), BF16),
                   jax.ShapeDtypeStruct((s, GDN_HEADS * GDN_CHUNK), BF16)),
        grid=(s // n,),
        in_specs=[wide, wide, wide, pl.BlockSpec((n, LANES), row),
                  pl.BlockSpec((2 * GDN_HEADS, n), lambda i: (0, i))],
        out_specs=(wide, wide, wide, wide, pl.BlockSpec((n, GDN_HEADS * GDN_CHUNK), row)),
        compiler_params=_params(("parallel",)),
        name="gdn_local",
    )(q, k, v, gcb, gct)


def _gdn_scan_kernel(u_ref, w_ref, qd_ref, kt_ref, qk_ref, gcb_ref, z_ref, nw_ref, o_ref, st_ref, *, rb):
    c = GDN_CHUNK

    @pl.when(pl.program_id(0) == 0)
    def _():
        st_ref[...] = jnp.zeros_like(st_ref)

    nw = nw_ref[...]

    def chunk(ci, carry):
        r0 = pl.multiple_of(ci * c, c)
        rows = pl.ds(r0, c)
        gt_row = jnp.exp(gcb_ref[pl.ds(r0 + c - 1, 1), :])
        heads = range(GDN_HEADS)
        sls = [slice(h * GDN_DK, (h + 1) * GDN_DK) for h in heads]
        sts = [st_ref[h] for h in heads]
        r1s = [_dot(jnp.concatenate([w_ref[rows, sls[h]], qd_ref[rows, sls[h]]], axis=0),
                    sts[h].astype(BF16)) for h in heads]
        vns = [(u_ref[rows, sls[h]].astype(F32) - r1s[h][0:c]).astype(BF16) for h in heads]
        for h in heads:
            st_ref[h] = sts[h] * gt_row[:, h:h + 1] + _dot_tn(kt_ref[rows, sls[h]], vns[h])
        os_ = [r1s[h][c:2 * c] + _dot(qk_ref[rows, h * c:(h + 1) * c], vns[h]) for h in heads]
        for h in heads:
            o = _rms(os_[h]) * nw * _silu(z_ref[rows, sls[h]].astype(F32))
            o_ref[rows, sls[h]] = o.astype(BF16)
        return carry

    lax.fori_loop(0, rb // c, chunk, 0)


def _gdn_scan(u, w, qd, kt, qk, gcb, p, norm_w):
    s = u.shape[0]
    rb = min(512, s)
    row = lambda i: (i, 0)
    wide = pl.BlockSpec((rb, GDN_QK), row)
    return pl.pallas_call(
        functools.partial(_gdn_scan_kernel, rb=rb),
        out_shape=jax.ShapeDtypeStruct((s, GDN_V), BF16),
        grid=(s // rb,),
        in_specs=[wide, wide, wide, wide, pl.BlockSpec((rb, GDN_HEADS * GDN_CHUNK), row),
                  pl.BlockSpec((rb, LANES), row),
                  pl.BlockSpec((rb, GDN_V), lambda i: (i, P_Z // GDN_V)),
                  pl.BlockSpec((1, GDN_DV), lambda i: (0, 0))],
        out_specs=wide,
        scratch_shapes=[pltpu.VMEM((GDN_HEADS, GDN_DK, GDN_DV), F32)],
        compiler_params=_params(("arbitrary",)),
        name="gdn_scan",
    )(u, w, qd, kt, qk, gcb, p, norm_w.reshape(1, GDN_DV))


def _rope_table_kernel(pos_ref, inv_ref, cos_ref, sin_ref):
    ang = pos_ref[...].astype(F32) * inv_ref[...]
    lane = lax.broadcasted_iota(jnp.int32, ang.shape, 1)
    first_half = (lane % MLA_ROPE) < (MLA_ROPE // 2)
    cos_ref[...] = jnp.cos(ang)
    sin_ref[...] = jnp.where(first_half, -jnp.sin(ang), jnp.sin(ang))


def _rope_tables(positions):
    s = positions.shape[-1]
    tm = min(1024, s)
    half = MLA_ROPE // 2
    inv = (ROPE_THETA ** (-np.arange(half, dtype=np.float32) / half)).astype(np.float32)
    inv_row = jnp.asarray(np.tile(inv, LANES // half).reshape(1, LANES))
    return pl.pallas_call(
        _rope_table_kernel,
        out_shape=(jax.ShapeDtypeStruct((s, LANES), F32), jax.ShapeDtypeStruct((s, LANES), F32)),
        grid=(s // tm,),
        in_specs=[pl.BlockSpec((tm, 1), lambda i: (i, 0)), pl.BlockSpec((1, LANES), lambda i: (0, 0))],
        out_specs=(pl.BlockSpec((tm, LANES), lambda i: (i, 0)), pl.BlockSpec((tm, LANES), lambda i: (i, 0))),
        compiler_params=_params(("parallel",)),
        name="rope_tables",
    )(positions.reshape(s, 1), inv_row)


def _rope_apply(x, cos, sin_signed):
    width = x.shape[-1]
    half = MLA_ROPE // 2
    lane = lax.broadcasted_iota(jnp.int32, x.shape, 1)
    first_half = (lane % MLA_ROPE) < half
    swapped = jnp.where(first_half, pltpu.roll(x, width - half, 1), pltpu.roll(x, half, 1))
    return x * cos + swapped * sin_signed


def _mla_proj_kernel(cq_ref, ckv_ref, kr_ref, cos_ref, sin_ref, qn_ref, wuq_ref, kvn_ref, wuk_ref, wuvt_ref,
                     q_ref, k_ref, vt_ref):
    cos = cos_ref[...]
    sin = sin_ref[...]
    scale = (MLA_NOPE + MLA_ROPE) ** -0.5 * math.log2(math.e)
    cq = cq_ref[...].astype(F32)
    cqn = cq * lax.rsqrt(jnp.sum(cq * cq, axis=-1, keepdims=True) * (1.0 / MLA_Q_RANK) + EPS) * qn_ref[...]
    q = _dot(cqn.astype(BF16), wuq_ref[...])
    nope_w = MLA_HEADS * MLA_NOPE
    q_rope = _rope_apply(q[:, nope_w:], jnp.concatenate([cos, cos], -1), jnp.concatenate([sin, sin], -1))
    zeros = jnp.zeros((q.shape[0], MLA_QK_PAD - MLA_NOPE - MLA_ROPE), F32)
    ckvn = (_rms(ckv_ref[...].astype(F32)) * kvn_ref[...]).astype(BF16)
    k_nope = _dot(ckvn, wuk_ref[...])
    v_t = _dot_nt(wuvt_ref[...], ckvn)
    k_pe = _rope_apply(kr_ref[...].astype(F32), cos, sin)[:, :MLA_ROPE]
    for h in range(MLA_HEADS):
        qh = jnp.concatenate([q[:, h * MLA_NOPE:(h + 1) * MLA_NOPE],
                              q_rope[:, h * MLA_ROPE:(h + 1) * MLA_ROPE], zeros], axis=-1) * scale
        q_ref[h] = qh.astype(BF16)
        kh = jnp.concatenate([k_nope[:, h * MLA_NOPE:(h + 1) * MLA_NOPE], k_pe, zeros], axis=-1)
        k_ref[h] = kh.astype(BF16)
        vt_ref[h, 0] = v_t[h * MLA_DV:(h + 1) * MLA_DV, :].astype(BF16)


def _mla_proj(p, cos, sin, q_norm, w_uq, kv_norm, w_uk, w_uvt):
    s = p.shape[0]
    tm = min(MLA_BLOCK, s)
    const = lambda i: (0, 0)
    return pl.pallas_call(
        _mla_proj_kernel,
        out_shape=(jax.ShapeDtypeStruct((MLA_HEADS, s, MLA_QK_PAD), BF16),
                   jax.ShapeDtypeStruct((MLA_HEADS, s, MLA_QK_PAD), BF16),
                   jax.ShapeDtypeStruct((MLA_HEADS, s // tm, MLA_DV, tm), BF16)),
        grid=(s // tm,),
        in_specs=[pl.BlockSpec((tm, MLA_Q_RANK_PAD), lambda i: (i, P_CQ // MLA_Q_RANK_PAD)),
                  pl.BlockSpec((tm, LANES), lambda i: (i, P_CKV // LANES)),
                  pl.BlockSpec((tm, LANES), lambda i: (i, P_KROPE // LANES)),
                  pl.BlockSpec((tm, LANES), lambda i: (i, 0)),
                  pl.BlockSpec((tm, LANES), lambda i: (i, 0)),
                  pl.BlockSpec(q_norm.shape, const), pl.BlockSpec(w_uq.shape, const),
                  pl.BlockSpec(kv_norm.shape, const), pl.BlockSpec(w_uk.shape, const),
                  pl.BlockSpec(w_uvt.shape, const)],
        out_specs=(pl.BlockSpec((MLA_HEADS, tm, MLA_QK_PAD), lambda i: (0, i, 0)),
                   pl.BlockSpec((MLA_HEADS, tm, MLA_QK_PAD), lambda i: (0, i, 0)),
                   pl.BlockSpec((MLA_HEADS, 1, MLA_DV, tm), lambda i: (0, i, 0, 0))),
        compiler_params=_params(("parallel",)),
        name="mla_proj",
    )(p, p, p, cos, sin, q_norm, w_uq, kv_norm, w_uk, w_uvt)


def _mla_attn_kernel(q_ref, k_ref, vt_ref, o_ref, st0, st1, mb0, mb1, m_ref, l_ref, acc_ref, *, bk, nd):
    qi = pl.program_id(1)
    slots = ((st0, mb0), (st1, mb1))
    bq = nd * bk
    n0 = nd * qi
    key = lax.broadcasted_iota(jnp.int32, (bk, bq), 0)
    qry = lax.broadcasted_iota(jnp.int32, (bk, bq), 1)

    def scores(t, slot, mask=None, q0=0):
        st_ref, mb_ref = slots[slot]
        r0 = pl.multiple_of(t * bk, bk)
        st = _dot_nt(k_ref[0, pl.ds(r0, bk), :], q_ref[0, q0:, :])
        if mask is not None:
            st = jnp.where(mask, st, -jnp.inf)
        st_ref[:, q0:] = st
        mb_ref[:, q0:] = jnp.max(st, axis=0, keepdims=True)

    def update(t, slot, q0=0):
        st_ref, mb_ref = slots[slot]
        m = m_ref[:, q0:]
        m_new = jnp.maximum(m, mb_ref[:, q0:])
        alpha = jnp.exp2(m - m_new)
        pexp = jnp.exp2(st_ref[:, q0:] - m_new)
        l_ref[:, q0:] = alpha * l_ref[:, q0:] + jnp.sum(pexp, axis=0, keepdims=True)
        acc_ref[:, q0:] = alpha * acc_ref[:, q0:] + _dot(vt_ref[0, t], pexp.astype(BF16))
        m_ref[:, q0:] = m_new

    m_ref[...] = jnp.full(m_ref.shape, -jnp.inf, F32)
    l_ref[...] = jnp.zeros(l_ref.shape, F32)
    acc_ref[...] = jnp.zeros(acc_ref.shape, F32)
    diag = key <= qry

    @pl.when(qi == 0)
    def _():
        scores(0, 0, diag)

    @pl.when(qi > 0)
    def _():
        scores(0, 0)

    def pair(j, carry):
        scores(2 * j + 1, 1)
        update(2 * j, 0)
        scores(2 * j + 2, 0)
        update(2 * j + 1, 1)
        return carry

    lax.fori_loop(0, n0 // 2 - 1, pair, 0)

    @pl.when(qi > 0)
    def _():
        scores(n0 - 1, 1)
        update(n0 - 2, 0)
        scores(n0, 0, diag)
        update(n0 - 1, 1)

    for d in range(1, nd):
        scores(n0 + d, d % 2, diag[:, :bq - d * bk], q0=d * bk)
        update(n0 + d - 1, (d - 1) % 2, q0=(d - 1) * bk)
    update(n0 + nd - 1, (nd - 1) % 2, q0=(nd - 1) * bk)
    o_ref[...] = (acc_ref[...] / l_ref[...]).T.astype(BF16)


def _mla_attn(q, k, vt):
    nh, s, _ = q.shape
    bk = vt.shape[-1]
    nd = 4 if s % (4 * bk) == 0 else 2
    bq = nd * bk
    assert s % bq == 0
    return pl.pallas_call(
        functools.partial(_mla_attn_kernel, bk=bk, nd=nd),
        out_shape=jax.ShapeDtypeStruct((s, nh * MLA_DV), BF16),
        grid=(nh, s // bq),
        in_specs=[pl.BlockSpec((1, bq, MLA_QK_PAD), lambda h, i: (h, i, 0)),
                  pl.BlockSpec((1, s, MLA_QK_PAD), lambda h, i: (h, 0, 0)),
                  pl.BlockSpec((1, s // bk, MLA_DV, bk), lambda h, i: (h, 0, 0, 0))],
        out_specs=pl.BlockSpec((bq, MLA_DV), lambda h, i: (i, h)),
        scratch_shapes=[pltpu.VMEM((bk, bq), F32), pltpu.VMEM((bk, bq), F32),
                        pltpu.VMEM((1, bq), F32), pltpu.VMEM((1, bq), F32),
                        pltpu.VMEM((1, bq), F32), pltpu.VMEM((1, bq), F32), pltpu.VMEM((MLA_DV, bq), F32)],
        compiler_params=_params(("parallel", "arbitrary")),
        name="mla_attn",
    )(q, k, vt)


def _swa_kernel(q_ref, k_ref, v_ref, kh_ref, vh_ref, sink_ref, o_ref, *, nblk):
    w = SWA_WINDOW
    g = SWA_GROUP
    i = pl.program_id(0)
    key = lax.broadcasted_iota(jnp.int32, (2 * w, g * w), 0)
    col = lax.broadcasted_iota(jnp.int32, (2 * w, g * w), 1)
    dist = col % w + w - key
    band = (dist >= 0) & (dist < w)
    first_band = band & ((key >= w) | (i > 0))
    dist_f = dist.astype(F32)
    biases = []
    for hk in range(SWA_KV_HEADS):
        slope = jnp.zeros((2 * w, g * w), F32)
        for gi, gq in enumerate(SWA_GORDER):
            slope = jnp.where(col // w == gi, 2.0 ** (-8.0 * (hk * g + gq + 1.0) / SWA_HEADS), slope)
        biases.append(-slope * dist_f)
    low = lax.broadcasted_iota(jnp.int32, (2 * w, 2 * SWA_DH), 1) < SWA_DH
    top = lax.broadcasted_iota(jnp.int32, (2 * SWA_DH, 1), 0) < SWA_DH
    for blk in range(nblk):
        rows = slice(blk * w, (blk + 1) * w)
        if blk == 0:
            k_prev, v_prev, valid = kh_ref[...], vh_ref[...], first_band
        else:
            k_prev, v_prev, valid = k_ref[(blk - 1) * w:blk * w, :], v_ref[(blk - 1) * w:blk * w, :], band
        kk = jnp.concatenate([k_prev, k_ref[rows, :]], axis=0).astype(F32)
        kk_sw = pltpu.roll(kk, SWA_DH, 1)
        vvt = jnp.concatenate([v_prev, v_ref[rows, :]], axis=0).astype(F32).T
        probs, inv_denoms = [], []
        for hk in range(SWA_KV_HEADS):
            c0 = hk * g * SWA_DH
            q2 = jnp.concatenate([q_ref[rows, c0:c0 + 2 * SWA_DH], q_ref[rows, c0 + 2 * SWA_DH:c0 + 4 * SWA_DH]],
                                 axis=0) * (SWA_DH ** -0.5)
            k_low = jnp.where(low, kk if hk == 0 else kk_sw, 0.0).astype(BF16)
            k_high = jnp.where(low, 0.0, kk_sw if hk == 0 else kk).astype(BF16)
            st = jnp.concatenate([_dot_nt(k_low, q2), _dot_nt(k_high, q2)], axis=1)
            st = jnp.where(valid, st + biases[hk], -jnp.inf)
            sink = sink_ref[hk]
            m = jnp.maximum(jnp.max(st, axis=0, keepdims=True), sink)
            pexp = jnp.exp(st - m)
            denom = jnp.sum(pexp, axis=0, keepdims=True) + jnp.exp(sink - m)
            probs.append(pexp.astype(BF16))
            inv_denoms.append(1.0 / denom)
        v0 = jnp.where(top, vvt, 0.0).astype(BF16)
        v1 = jnp.where(top, 0.0, vvt).astype(BF16)
        o_t = (_dot(v0, probs[0]) + _dot(v1, probs[1])) * jnp.where(top, inv_denoms[0], inv_denoms[1])
        for gi in range(g):
            o_ref[rows, gi * LANES:(gi + 1) * LANES] = o_t[:, gi * w:(gi + 1) * w].T.astype(BF16)


def _swa(p, sinks):
    s = p.shape[0]
    w = SWA_WINDOW
    rb = min(512, s)
    nblk = rb // w
    order = np.asarray([[hk * SWA_GROUP + gq for gq in SWA_GORDER] for hk in range(SWA_KV_HEADS)])
    sink_rows = jnp.repeat(sinks[order], w, axis=1).reshape(SWA_KV_HEADS, 1, SWA_GROUP * w)
    prev = lambda col: (lambda i: (jnp.maximum(i * nblk - 1, 0), col))
    return pl.pallas_call(
        functools.partial(_swa_kernel, nblk=nblk),
        out_shape=jax.ShapeDtypeStruct((s, SWA_OUT), BF16),
        grid=(s // rb,),
        in_specs=[pl.BlockSpec((rb, SWA_OUT), lambda i: (i, P_SWQ // SWA_OUT)),
                  pl.BlockSpec((rb, SWA_KV), lambda i: (i, P_SWK // SWA_KV)),
                  pl.BlockSpec((rb, SWA_KV), lambda i: (i, P_SWV // SWA_KV)),
                  pl.BlockSpec((w, SWA_KV), prev(P_SWK // SWA_KV)),
                  pl.BlockSpec((w, SWA_KV), prev(P_SWV // SWA_KV)),
                  pl.BlockSpec(sink_rows.shape, lambda i: (0, 0, 0))],
        out_specs=pl.BlockSpec((rb, SWA_OUT), lambda i: (i, 0)),
        compiler_params=_params(("parallel",)),
        name="swa",
    )(p, p, p, p, p, sink_rows)


def _outproj_kernel(oa_ref, ob_ref, oc_ref, w_ref, x_ref, pn_ref, g1_ref, fn_ref, sc_ref, sh_ref,
                    x1_ref, h2_ref):
    a_w, b_w = oa_ref.shape[1], ob_ref.shape[1]
    mix = (_dot(oa_ref[...], w_ref[0:a_w, :]) + _dot(ob_ref[...], w_ref[a_w:a_w + b_w, :])
           + _dot(oc_ref[...], w_ref[a_w + b_w:, :]))
    x1 = x_ref[...] + g1_ref[...] * (_rms(mix) * pn_ref[...])
    x1_ref[...] = x1
    h2_ref[...] = _norm_mod(x1, fn_ref[...], sc_ref[...], sh_ref[...]).astype(BF16)


def _out_proj(o_a, o_b, o_c, w_out, x, post_norm, gate1, ffn_norm, scale2, shift2):
    s, d = x.shape
    tm = min(256, s)
    row = lambda i: (i, 0)
    vec = pl.BlockSpec((1, d), lambda i: (0, 0))
    return pl.pallas_call(
        _outproj_kernel,
        out_shape=(jax.ShapeDtypeStruct((s, d), F32), jax.ShapeDtypeStruct((s, d), BF16)),
        grid=(s // tm,),
        in_specs=[pl.BlockSpec((tm, o_a.shape[1]), row), pl.BlockSpec((tm, o_b.shape[1]), row),
                  pl.BlockSpec((tm, o_c.shape[1]), row),
                  pl.BlockSpec(w_out.shape, lambda i: (0, 0), pipeline_mode=pl.Buffered(1)),
                  pl.BlockSpec((tm, d), row), vec, vec, vec, vec, vec],
        out_specs=(pl.BlockSpec((tm, d), row), pl.BlockSpec((tm, d), row)),
        compiler_params=_params(("parallel",)),
        name="out_proj",
    )(o_a, o_b, o_c, w_out, x, post_norm, gate1, ffn_norm, scale2, shift2)


def _gelu_tanh(x):
    return 0.5 * x * (1.0 + jnp.tanh(math.sqrt(2.0 / math.pi) * (x + 0.044715 * (x * x * x))))


def _ffn_up_kernel(h_ref, wg_ref, wu_ref, cg_ref, cu_ref, bg_ref, bu_ref, o_ref,
                   wg16_ref, wu16_ref, xg_ref, xu_ref, *, tm):
    @pl.when(pl.program_id(1) == 0)
    def _():
        wg16_ref[...] = wg_ref[...].astype(BF16)
        wu16_ref[...] = wu_ref[...].astype(BF16)
        xg_ref[0:8, :] = jnp.zeros((8, xg_ref.shape[1]), F32)
        xu_ref[0:8, :] = jnp.zeros((8, xu_ref.shape[1]), F32)

    h = h_ref[...]

    def conv(x_ref, w_ref, cw_ref, b_ref):
        x_ref[8:8 + tm, :] = _dot(h, w_ref[...])
        y = b_ref[...] + cw_ref[2:3, :] * x_ref[8:8 + tm, :]
        y = y + cw_ref[1:2, :] * x_ref[7:7 + tm, :]
        y = y + cw_ref[0:1, :] * x_ref[6:6 + tm, :]
        x_ref[0:8, :] = x_ref[tm:tm + 8, :]
        return y

    gate = conv(xg_ref, wg16_ref, cg_ref, bg_ref)
    up = conv(xu_ref, wu16_ref, cu_ref, bu_ref)
    o_ref[...] = (_gelu_tanh(gate) * up).astype(BF16)


def _ffn_up(h2, w_up, layer, conv_w, conv_b):
    s, d = h2.shape
    d_ff = w_up.shape[2] // 2
    tm, tn = min(1024, s), 512
    nj = d_ff // tn
    lo = lambda j, i: (0, j)
    hi = lambda j, i: (0, j + nj)
    w_lo = pl.BlockSpec((None, d, tn), lambda j, i: (layer, 0, j))
    w_hi = pl.BlockSpec((None, d, tn), lambda j, i: (layer, 0, j + nj))
    return pl.pallas_call(
        functools.partial(_ffn_up_kernel, tm=tm),
        out_shape=jax.ShapeDtypeStruct((s, d_ff), BF16),
        grid=(nj, s // tm),
        in_specs=[pl.BlockSpec((tm, d), lambda j, i: (i, 0)),
                  w_lo, w_hi,
                  pl.BlockSpec((FFN_CONV, tn), lo), pl.BlockSpec((FFN_CONV, tn), hi),
                  pl.BlockSpec((1, tn), lo), pl.BlockSpec((1, tn), hi)],
        out_specs=pl.BlockSpec((tm, tn), lambda j, i: (i, j)),
        scratch_shapes=[pltpu.VMEM((d, tn), BF16), pltpu.VMEM((d, tn), BF16),
                        pltpu.VMEM((tm + 8, tn), F32), pltpu.VMEM((tm + 8, tn), F32)],
        compiler_params=_params(("parallel", "arbitrary")),
        name="ffn_up",
    )(h2, w_up, w_up, conv_w, conv_w, conv_b, conv_b)


def _ffn_down_kernel(g_ref, w_ref, x_ref, pn_ref, g2_ref, nn_ref, sc_ref, sh_ref, x2_ref, hn_ref):
    y = _dot(g_ref[...], w_ref[...])
    x2 = x_ref[...] + g2_ref[...] * (_rms(y) * pn_ref[...])
    x2_ref[...] = x2
    hn_ref[...] = _norm_mod(x2, nn_ref[...], sc_ref[...], sh_ref[...]).astype(BF16)


def _ffn_down(g, w_down, x1, post_norm, gate2, next_norm, next_scale, next_shift):
    s, d = x1.shape
    d_ff = g.shape[1]
    tm = min(256, s)
    row = lambda i: (i, 0)
    vec = pl.BlockSpec((1, d), lambda i: (0, 0))
    return pl.pallas_call(
        _ffn_down_kernel,
        out_shape=(jax.ShapeDtypeStruct((s, d), F32), jax.ShapeDtypeStruct((s, d), BF16)),
        grid=(s // tm,),
        in_specs=[pl.BlockSpec((tm, d_ff), row),
                  pl.BlockSpec((d_ff, d), lambda i: (0, 0), pipeline_mode=pl.Buffered(1)),
                  pl.BlockSpec((tm, d), row), vec, vec, vec, vec, vec],
        out_specs=(pl.BlockSpec((tm, d), row), pl.BlockSpec((tm, d), row)),
        compiler_params=_params(("parallel",)),
        name="ffn_down",
    )(g, w_down, x1, post_norm, gate2, next_norm, next_scale, next_shift)


def _layout_w_in(w_in):
    d = w_in.shape[0]
    sizes = (GDN_QK, GDN_QK, GDN_V, GDN_V, GDN_HEADS, GDN_HEADS, MLA_Q_RANK, MLA_KV_RANK, MLA_ROPE,
             SWA_OUT, SWA_KV, SWA_KV)
    offs = np.concatenate([[0], np.cumsum(sizes)])
    part = lambda n: w_in[:, offs[n]:offs[n + 1]]
    zeros = lambda n: jnp.zeros((d, n), w_in.dtype)
    w_tail = jnp.concatenate([part(6), zeros(MLA_Q_RANK_PAD - MLA_Q_RANK),
                              part(9), part(7), part(8), zeros(LANES - MLA_ROPE), part(10), part(11)], axis=1)
    w_ab = jnp.concatenate([part(4), part(5), zeros(LANES - 2 * GDN_HEADS)], axis=1).astype(BF16)
    return w_tail, w_ab


def _layout_w_out(w_out):
    a = w_out.shape[0] - SWA_OUT
    wc = w_out[a:].reshape(SWA_KV_HEADS, SWA_GROUP, SWA_DH, -1)[:, np.asarray(SWA_GORDER)]
    wc = wc.transpose(1, 0, 2, 3).reshape(SWA_OUT, -1)
    return jnp.concatenate([w_out[:a], wc], axis=0).astype(BF16)


def _layout_mla(q_norm, w_uq, kv_norm, w_ukv):
    dqk = MLA_NOPE + MLA_ROPE
    uq = w_uq.reshape(MLA_Q_RANK, MLA_HEADS, dqk)
    uq = jnp.concatenate([uq[:, :, :MLA_NOPE].reshape(MLA_Q_RANK, -1), uq[:, :, MLA_NOPE:].reshape(MLA_Q_RANK, -1)],
                         axis=1)
    uq = jnp.pad(uq, ((0, MLA_Q_RANK_PAD - MLA_Q_RANK), (0, 0))).astype(BF16)
    qn = jnp.pad(q_norm, (0, MLA_Q_RANK_PAD - MLA_Q_RANK)).reshape(1, MLA_Q_RANK_PAD)
    ukv = w_ukv.reshape(MLA_KV_RANK, MLA_HEADS, MLA_NOPE + MLA_DV)
    uk = ukv[:, :, :MLA_NOPE].reshape(MLA_KV_RANK, -1).astype(BF16)
    uvt = ukv[:, :, MLA_NOPE:].reshape(MLA_KV_RANK, -1).T.astype(BF16)
    return qn, uq, kv_norm.reshape(1, MLA_KV_RANK), uk, uvt


def kernel(x, c, positions, ada_w, ada_b, mix_pre_norm, mix_post_norm, w_in, w_out, gdn_conv, gdn_a_log, gdn_dt_bias, gdn_norm, mla_q_norm, mla_w_uq, mla_kv_norm, mla_w_ukv, swa_sinks, ffn_pre_norm, ffn_post_norm, ffn_w_up, ffn_conv, ffn_conv_b, ffn_w_down):
    batch, s, d = x.shape
    assert batch == 1, "kernels are written for a single sequence"
    depth = ada_w.shape[0]
    xs = x.reshape(s, d)
    mod = _adaln_mod(c, ada_w, ada_b).reshape(depth, N_MOD, 1, d)
    cos, sin = _rope_tables(positions)
    vec = lambda a: a.reshape(1, d)

    h = _pre_norm(xs, vec(mix_pre_norm[0]), mod[0, 1], mod[0, 0])
    for l in range(depth):
        shift1, scale1, gate1, shift2, scale2, gate2 = (mod[l, n] for n in range(N_MOD))
        w_tail, w_ab = _layout_w_in(w_in[l])
        p, ab = _in_proj(h, w_in, l, w_tail, w_ab)

        q_a, k_a, v_a, gcb, gct = _gdn_prep(p, ab, gdn_conv[l], gdn_a_log[l], gdn_dt_bias[l])
        u, w, qd, kt, qk = _gdn_local(q_a, k_a, v_a, gcb, gct)
        o_a = _gdn_scan(u, w, qd, kt, qk, gcb, p, gdn_norm[l])

        q_b, k_b, v_b = _mla_proj(p, cos, sin, *_layout_mla(mla_q_norm[l], mla_w_uq[l], mla_kv_norm[l],
                                                            mla_w_ukv[l]))
        o_b = _mla_attn(q_b, k_b, v_b)

        o_c = _swa(p, swa_sinks[l])

        x1, h2 = _out_proj(o_a, o_b, o_c, _layout_w_out(w_out[l]), xs, vec(mix_post_norm[l]), gate1,
                           vec(ffn_pre_norm[l]), scale2, shift2)
        g = _ffn_up(h2, ffn_w_up, l, ffn_conv[l], ffn_conv_b[l].reshape(1, -1))
        nl = min(l + 1, depth - 1)
        xs, h = _ffn_down(g, ffn_w_down[l].astype(BF16), x1, vec(ffn_post_norm[l]), gate2,
                          vec(mix_pre_norm[nl]), mod[nl, 1], mod[nl, 0])
    return xs.reshape(batch, s, d)
```

```python
import functools
import math

import numpy as np
import jax
import jax.numpy as jnp
from jax import lax
from jax.experimental import pallas as pl
from jax.experimental.pallas import tpu as pltpu

F32 = jnp.float32
BF16 = jnp.bfloat16

EPS = 1e-6
N_MOD = 6
GDN_HEADS = 8
GDN_DK = 128
GDN_DV = 128
GDN_CONV = 4
GDN_CHUNK = 64
GDN_QK = GDN_HEADS * GDN_DK
GDN_V = GDN_HEADS * GDN_DV
MLA_HEADS = 4
MLA_Q_RANK = 448
MLA_Q_RANK_PAD = 512
MLA_KV_RANK = 128
MLA_NOPE = 128
MLA_ROPE = 64
MLA_DV = 128
MLA_QK_PAD = 256
MLA_BLOCK = 512
ROPE_THETA = 10000.0
SWA_HEADS = 8
SWA_KV_HEADS = 2
SWA_GROUP = SWA_HEADS // SWA_KV_HEADS
SWA_DH = 64
SWA_WINDOW = 128
SWA_OUT = SWA_HEADS * SWA_DH
SWA_KV = SWA_KV_HEADS * SWA_DH
SWA_GORDER = (0, 2, 1, 3)
FFN_CONV = 3

LANES = 128
GDN_GROUP = 256
VMEM_LIMIT_MB = 56

P_QKV = 0
P_Z = 3072
P_CQ = 4096
P_SWQ = 4608
P_CKV = 5120
P_KROPE = 5248
P_SWK = 5376
P_SWV = 5504
P_WIDTH = 5632


def _params(semantics):
    return pltpu.CompilerParams(dimension_semantics=semantics, vmem_limit_bytes=VMEM_LIMIT_MB << 20)


def _sigmoid(x):
    return 1.0 / (1.0 + jnp.exp(-x))


def _silu(x):
    return x * _sigmoid(x)


def _softplus(x):
    return jnp.maximum(x, 0.0) + jnp.log(1.0 + jnp.exp(-jnp.abs(x)))


def _rms(x):
    return x * lax.rsqrt(jnp.mean(x * x, axis=-1, keepdims=True) + EPS)


def _dot(a, b):
    return jnp.dot(a, b, preferred_element_type=F32)


def _dot_nt(a, b):
    return lax.dot_general(a, b, (((1,), (1,)), ((), ())), preferred_element_type=F32)


def _dot_tn(a, b):
    return lax.dot_general(a, b, (((0,), (0,)), ((), ())), preferred_element_type=F32)


def _mod_kernel(c_ref, w_ref, b_ref, o_ref, *, d, kc):
    cact = _silu(c_ref[...])
    acc = b_ref[0]
    for k0 in range(0, d, kc):
        acc = acc + jnp.sum(w_ref[0, k0:k0 + kc, :] * cact[k0:k0 + kc], axis=0, keepdims=True)
    o_ref[0] = acc


def _adaln_mod(c, ada_w, ada_b):
    depth, d, n = ada_w.shape
    tn = 1024
    return pl.pallas_call(
        functools.partial(_mod_kernel, d=d, kc=256),
        out_shape=jax.ShapeDtypeStruct((depth, 1, n), F32),
        grid=(depth, n // tn),
        in_specs=[pl.BlockSpec((d, 1), lambda l, j: (0, 0)),
                  pl.BlockSpec((1, d, tn), lambda l, j: (l, 0, j)),
                  pl.BlockSpec((1, 1, tn), lambda l, j: (l, 0, j))],
        out_specs=pl.BlockSpec((1, 1, tn), lambda l, j: (l, 0, j)),
        compiler_params=_params(("parallel", "parallel")),
        name="adaln_mod",
    )(c.reshape(d, 1), ada_w, ada_b.reshape(depth, 1, n))


def _norm_mod(x, w, scale, shift):
    return _rms(x) * w * (1.0 + scale) + shift


def _norm_mod_kernel(x_ref, w_ref, sc_ref, sh_ref, h_ref):
    h_ref[...] = _norm_mod(x_ref[...], w_ref[...], sc_ref[...], sh_ref[...]).astype(BF16)


def _pre_norm(x, w, scale, shift):
    s, d = x.shape
    tm = min(512, s)
    row = pl.BlockSpec((1, d), lambda i: (0, 0))
    return pl.pallas_call(
        _norm_mod_kernel,
        out_shape=jax.ShapeDtypeStruct((s, d), BF16),
        grid=(s // tm,),
        in_specs=[pl.BlockSpec((tm, d), lambda i: (i, 0)), row, row, row],
        out_specs=pl.BlockSpec((tm, d), lambda i: (i, 0)),
        compiler_params=_params(("parallel",)),
        name="pre_norm",
    )(x, w, scale, shift)


def _inproj_kernel(h_ref, w_ref, wab_ref, p_ref, ab_ref):
    h = h_ref[...]
    p_ref[...] = _dot(h, w_ref[...]).astype(BF16)

    @pl.when(pl.program_id(1) == 0)
    def _():
        ab_ref[...] = _dot(h, wab_ref[...])


def _in_proj(h, w_p, w_ab, layer):
    s, d = h.shape
    n = w_p.shape[2]
    tm, tn = min(2048, s), 512
    return pl.pallas_call(
        _inproj_kernel,
        out_shape=(jax.ShapeDtypeStruct((s, n), BF16), jax.ShapeDtypeStruct((s, LANES), F32)),
        grid=(s // tm, n // tn),
        in_specs=[pl.BlockSpec((tm, d), lambda i, j: (i, 0)),
                  pl.BlockSpec((None, d, tn), lambda i, j: (layer, 0, j)),
                  pl.BlockSpec((None, d, LANES), lambda i, j: (layer, 0, 0))],
        out_specs=(pl.BlockSpec((tm, tn), lambda i, j: (i, j)),
                   pl.BlockSpec((tm, LANES), lambda i, j: (i, 0))),
        compiler_params=_params(("parallel", "arbitrary")),
        name="in_proj",
    )(h, w_p, w_ab)


def _gdn_prep_kernel(x_ref, halo_ref, cw_ref, ab_ref, alog_ref, dtb_ref, tri_ref,
                     q_ref, k_ref, v_ref, gcb_ref, gct_ref, xp_ref, *, tm):
    i = pl.program_id(0)
    outs = (q_ref, k_ref, v_ref)
    for grp in range(3):
        c0 = grp * GDN_QK
        halo = halo_ref[:, c0:c0 + GDN_QK].astype(F32)
        xp_ref[0:8, :] = jnp.where(i > 0, halo, 0.0)
        xp_ref[8:8 + tm, :] = x_ref[:, c0:c0 + GDN_QK].astype(F32)
        y = cw_ref[3:4, c0:c0 + GDN_QK] * xp_ref[8:8 + tm, :]
        for j in range(GDN_CONV - 1):
            off = 8 - (GDN_CONV - 1) + j
            y = y + cw_ref[j:j + 1, c0:c0 + GDN_QK] * xp_ref[off:off + tm, :]
        y = _silu(y)
        for h in range(GDN_HEADS):
            yh = y[:, h * GDN_DK:(h + 1) * GDN_DK]
            if grp < 2:
                yh = yh * lax.rsqrt(jnp.sum(yh * yh, axis=-1, keepdims=True) + EPS)
            if grp == 0:
                yh = yh * (GDN_DK ** -0.5)
            outs[grp][:, h * GDN_DK:(h + 1) * GDN_DK] = yh.astype(BF16)

    ab = ab_ref[...]
    g = -jnp.exp(alog_ref[...]) * _softplus(ab + dtb_ref[...])
    gc = jnp.dot(tri_ref[...], g, preferred_element_type=F32, precision=lax.Precision.HIGHEST)
    lane = lax.broadcasted_iota(jnp.int32, (1, LANES), 1)
    gcb = jnp.where(lane < GDN_HEADS, gc, _sigmoid(ab))
    gcb_ref[...] = gcb
    gct_ref[...] = gcb.T[0:2 * GDN_HEADS, :]


def _gdn_prep(p, ab, conv_w, a_log, dt_bias):
    s = p.shape[0]
    tm = min(512, s)
    c3 = 3 * GDN_QK
    alog_row = jnp.zeros((1, LANES), F32).at[0, :GDN_HEADS].set(a_log)
    dtb_row = jnp.zeros((1, LANES), F32).at[0, :GDN_HEADS].set(dt_bias)
    r = np.arange(tm)
    tri = jnp.asarray(((r[:, None] >= r[None, :]) & (r[:, None] // GDN_CHUNK == r[None, :] // GDN_CHUNK))
                      .astype(np.float32))
    hb = tm // 8
    row = lambda i: (i, 0)
    return pl.pallas_call(
        functools.partial(_gdn_prep_kernel, tm=tm),
        out_shape=(jax.ShapeDtypeStruct((s, GDN_QK), BF16), jax.ShapeDtypeStruct((s, GDN_QK), BF16),
                   jax.ShapeDtypeStruct((s, GDN_V), BF16), jax.ShapeDtypeStruct((s, LANES), F32),
                   jax.ShapeDtypeStruct((2 * GDN_HEADS, s), F32)),
        grid=(s // tm,),
        in_specs=[pl.BlockSpec((tm, c3), row),
                  pl.BlockSpec((8, c3), lambda i: (jnp.maximum(i * hb - 1, 0), 0)),
                  pl.BlockSpec((GDN_CONV, c3), lambda i: (0, 0)),
                  pl.BlockSpec((tm, LANES), row),
                  pl.BlockSpec((1, LANES), lambda i: (0, 0)),
                  pl.BlockSpec((1, LANES), lambda i: (0, 0)),
                  pl.BlockSpec((tm, tm), lambda i: (0, 0))],
        out_specs=(pl.BlockSpec((tm, GDN_QK), row), pl.BlockSpec((tm, GDN_QK), row),
                   pl.BlockSpec((tm, GDN_V), row), pl.BlockSpec((tm, LANES), row),
                   pl.BlockSpec((2 * GDN_HEADS, tm), lambda i: (0, i))),
        scratch_shapes=[pltpu.VMEM((tm + 8, GDN_QK), F32)],
        compiler_params=_params(("parallel",)),
        name="gdn_prep",
    )(p, p, conv_w, ab, alog_row, dtb_row, tri)


def _gdn_local_kernel(q_ref, k_ref, v_ref, gcb_ref, gct_ref,
                      u_ref, w_ref, qd_ref, kt_ref, qk_ref):
    n = GDN_GROUP
    c = GDN_CHUNK
    ri = lax.broadcasted_iota(jnp.int32, (n, n), 0)
    ci = lax.broadcasted_iota(jnp.int32, (n, n), 1)
    same_chunk = (ri // c) == (ci // c)
    incl = same_chunk & (ri >= ci)
    strict = same_chunk & (ri > ci)
    eye = (ri == ci).astype(F32)
    last_sel = (ci == (ri // c) * c + (c - 1)).astype(F32)
    gcb = gcb_ref[...]
    gc_last_all = jnp.dot(last_sel, gcb, preferred_element_type=F32, precision=lax.Precision.HIGHEST)
    heads = range(GDN_HEADS)
    sls = [slice(h * GDN_DK, (h + 1) * GDN_DK) for h in heads]
    lmats, ts, rhss = [], [], []
    for h in heads:
        sl = sls[h]
        q = q_ref[:, sl].astype(F32)
        k16 = k_ref[:, sl]
        k = k16.astype(F32)
        v = v_ref[:, sl].astype(F32)
        gc_col = gcb[:, h:h + 1]
        beta = gcb[:, GDN_HEADS + h:GDN_HEADS + h + 1]
        gc_row = gct_ref[h:h + 1, :]
        gc_last = gc_last_all[:, h:h + 1]
        diff = gc_col - gc_row
        decay = jnp.where(incl, jnp.exp(jnp.where(incl, diff, 0.0)), 0.0)
        kb = k * beta
        kk = _dot_nt(kb.astype(BF16), k16)
        lmat = jnp.where(strict, kk * decay, 0.0)
        lmats.append(lmat)
        ts.append(eye - jnp.where((ri // 2) == (ci // 2), lmat, 0.0))
        eg = jnp.exp(gc_col)
        rhss.append(jnp.concatenate([v * beta, kb * eg], axis=-1).astype(BF16))
        qd_ref[:, sl] = (q * eg).astype(BF16)
        kt_ref[:, sl] = (k * jnp.exp(gc_last - gc_col)).astype(BF16)
        qk = _dot_nt(q_ref[:, sl], k16) * decay
        qk_c = jnp.concatenate([qk[j * c:(j + 1) * c, j * c:(j + 1) * c] for j in range(n // c)], axis=0)
        qk_ref[:, h * c:(h + 1) * c] = qk_c.astype(BF16)
    b = 2
    while b < c:
        off_diag = ((ri // (2 * b)) == (ci // (2 * b))) & ((ri // b) != (ci // b))
        t16s = [ts[h].astype(BF16) for h in heads]
        ys = [_dot(t16s[h], jnp.where(off_diag, lmats[h], 0.0).astype(BF16)).astype(BF16) for h in heads]
        ts = [ts[h] - _dot(ys[h], t16s[h]) for h in heads]
        b *= 2
    for h in heads:
        uw = _dot(ts[h].astype(BF16), rhss[h])
        u_ref[:, sls[h]] = uw[:, :GDN_DV].astype(BF16)
        w_ref[:, sls[h]] = uw[:, GDN_DV:].astype(BF16)


def _gdn_local(q, k, v, gcb, gct):
    s = q.shape[0]
    n = GDN_GROUP
    row = lambda i: (i, 0)
    wide = pl.BlockSpec((n, GDN_QK), row)
    return pl.pallas_call(
        _gdn_local_kernel,
        out_shape=(jax.ShapeDtypeStruct((s, GDN_V), BF16), jax.ShapeDtypeStruct((s, GDN_QK), BF16),
                   jax.ShapeDtypeStruct((s, GDN_QK), BF16), jax.ShapeDtypeStruct((s, GDN_QK), BF16),
                   jax.ShapeDtypeStruct((s, GDN_HEADS * GDN_CHUNK), BF16)),
        grid=(s // n,),
        in_specs=[wide, wide, wide, pl.BlockSpec((n, LANES), row),
                  pl.BlockSpec((2 * GDN_HEADS, n), lambda i: (0, i))],
        out_specs=(wide, wide, wide, wide, pl.BlockSpec((n, GDN_HEADS * GDN_CHUNK), row)),
        compiler_params=_params(("parallel",)),
        name="gdn_local",
    )(q, k, v, gcb, gct)


def _gdn_scan_kernel(u_ref, w_ref, qd_ref, kt_ref, qk_ref, gcb_ref, z_ref, nw_ref, o_ref, st_ref, *, rb):
    c = GDN_CHUNK

    @pl.when(pl.program_id(0) == 0)
    def _():
        st_ref[...] = jnp.zeros_like(st_ref)

    nw = nw_ref[...]

    def chunk(ci, carry):
        r0 = pl.multiple_of(ci * c, c)
        rows = pl.ds(r0, c)
        gt_row = jnp.exp(gcb_ref[pl.ds(r0 + c - 1, 1), :])
        heads = range(GDN_HEADS)
        sls = [slice(h * GDN_DK, (h + 1) * GDN_DK) for h in heads]
        sts = [st_ref[h] for h in heads]
        r1s = [_dot(jnp.concatenate([w_ref[rows, sls[h]], qd_ref[rows, sls[h]]], axis=0),
                    sts[h].astype(BF16)) for h in heads]
        vns = [(u_ref[rows, sls[h]].astype(F32) - r1s[h][0:c]).astype(BF16) for h in heads]
        for h in heads:
            st_ref[h] = sts[h] * gt_row[:, h:h + 1] + _dot_tn(kt_ref[rows, sls[h]], vns[h])
        os_ = [r1s[h][c:2 * c] + _dot(qk_ref[rows, h * c:(h + 1) * c], vns[h]) for h in heads]
        for h in heads:
            o = _rms(os_[h]) * nw * _silu(z_ref[rows, sls[h]].astype(F32))
            o_ref[rows, sls[h]] = o.astype(BF16)
        return carry

    lax.fori_loop(0, rb // c, chunk, 0)


def _gdn_scan(u, w, qd, kt, qk, gcb, p, norm_w):
    s = u.shape[0]
    rb = min(512, s)
    row = lambda i: (i, 0)
    wide = pl.BlockSpec((rb, GDN_QK), row)
    return pl.pallas_call(
        functools.partial(_gdn_scan_kernel, rb=rb),
        out_shape=jax.ShapeDtypeStruct((s, GDN_V), BF16),
        grid=(s // rb,),
        in_specs=[wide, wide, wide, wide, pl.BlockSpec((rb, GDN_HEADS * GDN_CHUNK), row),
                  pl.BlockSpec((rb, LANES), row),
                  pl.BlockSpec((rb, GDN_V), lambda i: (i, P_Z // GDN_V)),
                  pl.BlockSpec((1, GDN_DV), lambda i: (0, 0))],
        out_specs=wide,
        scratch_shapes=[pltpu.VMEM((GDN_HEADS, GDN_DK, GDN_DV), F32)],
        compiler_params=_params(("arbitrary",)),
        name="gdn_scan",
    )(u, w, qd, kt, qk, gcb, p, norm_w.reshape(1, GDN_DV))


def _rope_table_kernel(pos_ref, inv_ref, cos_ref, sin_ref):
    ang = pos_ref[...].astype(F32) * inv_ref[...]
    lane = lax.broadcasted_iota(jnp.int32, ang.shape, 1)
    first_half = (lane % MLA_ROPE) < (MLA_ROPE // 2)
    cos_ref[...] = jnp.cos(ang)
    sin_ref[...] = jnp.where(first_half, -jnp.sin(ang), jnp.sin(ang))


def _rope_tables(positions):
    s = positions.shape[-1]
    tm = min(1024, s)
    half = MLA_ROPE // 2
    inv = (ROPE_THETA ** (-np.arange(half, dtype=np.float32) / half)).astype(np.float32)
    inv_row = jnp.asarray(np.tile(inv, LANES // half).reshape(1, LANES))
    return pl.pallas_call(
        _rope_table_kernel,
        out_shape=(jax.ShapeDtypeStruct((s, LANES), F32), jax.ShapeDtypeStruct((s, LANES), F32)),
        grid=(s // tm,),
        in_specs=[pl.BlockSpec((tm, 1), lambda i: (i, 0)), pl.BlockSpec((1, LANES), lambda i: (0, 0))],
        out_specs=(pl.BlockSpec((tm, LANES), lambda i: (i, 0)), pl.BlockSpec((tm, LANES), lambda i: (i, 0))),
        compiler_params=_params(("parallel",)),
        name="rope_tables",
    )(positions.reshape(s, 1), inv_row)


def _rope_apply(x, cos, sin_signed):
    width = x.shape[-1]
    half = MLA_ROPE // 2
    lane = lax.broadcasted_iota(jnp.int32, x.shape, 1)
    first_half = (lane % MLA_ROPE) < half
    swapped = jnp.where(first_half, pltpu.roll(x, width - half, 1), pltpu.roll(x, half, 1))
    return x * cos + swapped * sin_signed


def _mla_proj_kernel(cq_ref, ckv_ref, kr_ref, cos_ref, sin_ref, qn_ref, wuq_ref, kvn_ref, wuk_ref, wuvt_ref,
                     q_ref, k_ref, vt_ref):
    cos = cos_ref[...]
    sin = sin_ref[...]
    scale = (MLA_NOPE + MLA_ROPE) ** -0.5 * math.log2(math.e)
    cq = cq_ref[...].astype(F32)
    cqn = cq * lax.rsqrt(jnp.sum(cq * cq, axis=-1, keepdims=True) * (1.0 / MLA_Q_RANK) + EPS) * qn_ref[...]
    q = _dot(cqn.astype(BF16), wuq_ref[...])
    nope_w = MLA_HEADS * MLA_NOPE
    q_rope = _rope_apply(q[:, nope_w:], jnp.concatenate([cos, cos], -1), jnp.concatenate([sin, sin], -1))
    zeros = jnp.zeros((q.shape[0], MLA_QK_PAD - MLA_NOPE - MLA_ROPE), F32)
    ckvn = (_rms(ckv_ref[...].astype(F32)) * kvn_ref[...]).astype(BF16)
    k_nope = _dot(ckvn, wuk_ref[...])
    v_t = _dot_nt(wuvt_ref[...], ckvn)
    k_pe = _rope_apply(kr_ref[...].astype(F32), cos, sin)[:, :MLA_ROPE]
    for h in range(MLA_HEADS):
        qh = jnp.concatenate([q[:, h * MLA_NOPE:(h + 1) * MLA_NOPE],
                              q_rope[:, h * MLA_ROPE:(h + 1) * MLA_ROPE], zeros], axis=-1) * scale
        q_ref[h] = qh.astype(BF16)
        kh = jnp.concatenate([k_nope[:, h * MLA_NOPE:(h + 1) * MLA_NOPE], k_pe, zeros], axis=-1)
        k_ref[h] = kh.astype(BF16)
        vt_ref[h, 0] = v_t[h * MLA_DV:(h + 1) * MLA_DV, :].astype(BF16)


def _mla_proj(p, cos, sin, q_norm, w_uq, kv_norm, w_uk, w_uvt):
    s = p.shape[0]
    tm = min(MLA_BLOCK, s)
    const = lambda i: (0, 0)
    return pl.pallas_call(
        _mla_proj_kernel,
        out_shape=(jax.ShapeDtypeStruct((MLA_HEADS, s, MLA_QK_PAD), BF16),
                   jax.ShapeDtypeStruct((MLA_HEADS, s, MLA_QK_PAD), BF16),
                   jax.ShapeDtypeStruct((MLA_HEADS, s // tm, MLA_DV, tm), BF16)),
        grid=(s // tm,),
        in_specs=[pl.BlockSpec((tm, MLA_Q_RANK_PAD), lambda i: (i, P_CQ // MLA_Q_RANK_PAD)),
                  pl.BlockSpec((tm, LANES), lambda i: (i, P_CKV // LANES)),
                  pl.BlockSpec((tm, LANES), lambda i: (i, P_KROPE // LANES)),
                  pl.BlockSpec((tm, LANES), lambda i: (i, 0)),
                  pl.BlockSpec((tm, LANES), lambda i: (i, 0)),
                  pl.BlockSpec(q_norm.shape, const), pl.BlockSpec(w_uq.shape, const),
                  pl.BlockSpec(kv_norm.shape, const), pl.BlockSpec(w_uk.shape, const),
                  pl.BlockSpec(w_uvt.shape, const)],
        out_specs=(pl.BlockSpec((MLA_HEADS, tm, MLA_QK_PAD), lambda i: (0, i, 0)),
                   pl.BlockSpec((MLA_HEADS, tm, MLA_QK_PAD), lambda i: (0, i, 0)),
                   pl.BlockSpec((MLA_HEADS, 1, MLA_DV, tm), lambda i: (0, i, 0, 0))),
        compiler_params=_params(("parallel",)),
        name="mla_proj",
    )(p, p, p, cos, sin, q_norm, w_uq, kv_norm, w_uk, w_uvt)


def _mla_attn_kernel(q_ref, k_ref, vt_ref, o_ref, st0, st1, mb0, mb1, m_ref, l_ref, acc_ref, *, bk, nd):
    qi = pl.program_id(1)
    slots = ((st0, mb0), (st1, mb1))
    bq = nd * bk
    n0 = nd * qi
    key = lax.broadcasted_iota(jnp.int32, (bk, bq), 0)
    qry = lax.broadcasted_iota(jnp.int32, (bk, bq), 1)

    def scores(t, slot, mask=None, q0=0):
        st_ref, mb_ref = slots[slot]
        r0 = pl.multiple_of(t * bk, bk)
        st = _dot_nt(k_ref[0, pl.ds(r0, bk), :], q_ref[0, q0:, :])
        if mask is not None:
            st = jnp.where(mask, st, -jnp.inf)
        st_ref[:, q0:] = st
        mb_ref[:, q0:] = jnp.max(st, axis=0, keepdims=True)

    def update(t, slot, q0=0):
        st_ref, mb_ref = slots[slot]
        m = m_ref[:, q0:]
        m_new = jnp.maximum(m, mb_ref[:, q0:])
        alpha = jnp.exp2(m - m_new)
        pexp = jnp.exp2(st_ref[:, q0:] - m_new)
        l_ref[:, q0:] = alpha * l_ref[:, q0:] + jnp.sum(pexp, axis=0, keepdims=True)
        acc_ref[:, q0:] = alpha * acc_ref[:, q0:] + _dot(vt_ref[0, t], pexp.astype(BF16))
        m_ref[:, q0:] = m_new

    m_ref[...] = jnp.full(m_ref.shape, -jnp.inf, F32)
    l_ref[...] = jnp.zeros(l_ref.shape, F32)
    acc_ref[...] = jnp.zeros(acc_ref.shape, F32)
    diag = key <= qry

    @pl.when(qi == 0)
    def _():
        scores(0, 0, diag)

    @pl.when(qi > 0)
    def _():
        scores(0, 0)

    def pair(j, carry):
        scores(2 * j + 1, 1)
        update(2 * j, 0)
        scores(2 * j + 2, 0)
        update(2 * j + 1, 1)
        return carry

    lax.fori_loop(0, n0 // 2 - 1, pair, 0)

    @pl.when(qi > 0)
    def _():
        scores(n0 - 1, 1)
        update(n0 - 2, 0)
        scores(n0, 0, diag)
        update(n0 - 1, 1)

    for d in range(1, nd):
        scores(n0 + d, d % 2, diag[:, :bq - d * bk], q0=d * bk)
        update(n0 + d - 1, (d - 1) % 2, q0=(d - 1) * bk)
    update(n0 + nd - 1, (nd - 1) % 2, q0=(nd - 1) * bk)
    o_ref[...] = (acc_ref[...] / l_ref[...]).T.astype(BF16)


def _mla_attn(q, k, vt):
    nh, s, _ = q.shape
    bk = vt.shape[-1]
    nd = 4 if s % (4 * bk) == 0 else 2
    bq = nd * bk
    assert s % bq == 0
    return pl.pallas_call(
        functools.partial(_mla_attn_kernel, bk=bk, nd=nd),
        out_shape=jax.ShapeDtypeStruct((s, nh * MLA_DV), BF16),
        grid=(nh, s // bq),
        in_specs=[pl.BlockSpec((1, bq, MLA_QK_PAD), lambda h, i: (h, i, 0)),
                  pl.BlockSpec((1, s, MLA_QK_PAD), lambda h, i: (h, 0, 0)),
                  pl.BlockSpec((1, s // bk, MLA_DV, bk), lambda h, i: (h, 0, 0, 0))],
        out_specs=pl.BlockSpec((bq, MLA_DV), lambda h, i: (i, h)),
        scratch_shapes=[pltpu.VMEM((bk, bq), F32), pltpu.VMEM((bk, bq), F32),
                        pltpu.VMEM((1, bq), F32), pltpu.VMEM((1, bq), F32),
                        pltpu.VMEM((1, bq), F32), pltpu.VMEM((1, bq), F32), pltpu.VMEM((MLA_DV, bq), F32)],
        compiler_params=_params(("parallel", "arbitrary")),
        name="mla_attn",
    )(q, k, vt)


def _swa_kernel(q_ref, k_ref, v_ref, kh_ref, vh_ref, sink_ref, o_ref, *, nblk):
    w = SWA_WINDOW
    g = SWA_GROUP
    i = pl.program_id(0)
    key = lax.broadcasted_iota(jnp.int32, (2 * w, g * w), 0)
    col = lax.broadcasted_iota(jnp.int32, (2 * w, g * w), 1)
    dist = col % w + w - key
    band = (dist >= 0) & (dist < w)
    first_band = band & ((key >= w) | (i > 0))
    dist_f = dist.astype(F32)
    biases = []
    for hk in range(SWA_KV_HEADS):
        slope = jnp.zeros((2 * w, g * w), F32)
        for gi, gq in enumerate(SWA_GORDER):
            slope = jnp.where(col // w == gi, 2.0 ** (-8.0 * (hk * g + gq + 1.0) / SWA_HEADS), slope)
        biases.append(-slope * dist_f)
    low = lax.broadcasted_iota(jnp.int32, (2 * w, 2 * SWA_DH), 1) < SWA_DH
    top = lax.broadcasted_iota(jnp.int32, (2 * SWA_DH, 1), 0) < SWA_DH
    for blk in range(nblk):
        rows = slice(blk * w, (blk + 1) * w)
        if blk == 0:
            k_prev, v_prev, valid = kh_ref[...], vh_ref[...], first_band
        else:
            k_prev, v_prev, valid = k_ref[(blk - 1) * w:blk * w, :], v_ref[(blk - 1) * w:blk * w, :], band
        kk = jnp.concatenate([k_prev, k_ref[rows, :]], axis=0).astype(F32)
        kk_sw = pltpu.roll(kk, SWA_DH, 1)
        vvt = jnp.concatenate([v_prev, v_ref[rows, :]], axis=0).astype(F32).T
        probs, inv_denoms = [], []
        for hk in range(SWA_KV_HEADS):
            c0 = hk * g * SWA_DH
            q2 = jnp.concatenate([q_ref[rows, c0:c0 + 2 * SWA_DH], q_ref[rows, c0 + 2 * SWA_DH:c0 + 4 * SWA_DH]],
                                 axis=0) * (SWA_DH ** -0.5)
            k_low = jnp.where(low, kk if hk == 0 else kk_sw, 0.0).astype(BF16)
            k_high = jnp.where(low, 0.0, kk_sw if hk == 0 else kk).astype(BF16)
            st = jnp.concatenate([_dot_nt(k_low, q2), _dot_nt(k_high, q2)], axis=1)
            st = jnp.where(valid, st + biases[hk], -jnp.inf)
            sink = sink_ref[hk]
            m = jnp.maximum(jnp.max(st, axis=0, keepdims=True), sink)
            pexp = jnp.exp(st - m)
            denom = jnp.sum(pexp, axis=0, keepdims=True) + jnp.exp(sink - m)
            probs.append(pexp.astype(BF16))
            inv_denoms.append(1.0 / denom)
        v0 = jnp.where(top, vvt, 0.0).astype(BF16)
        v1 = jnp.where(top, 0.0, vvt).astype(BF16)
        o_t = (_dot(v0, probs[0]) + _dot(v1, probs[1])) * jnp.where(top, inv_denoms[0], inv_denoms[1])
        for gi in range(g):
            o_ref[rows, gi * LANES:(gi + 1) * LANES] = o_t[:, gi * w:(gi + 1) * w].T.astype(BF16)


def _swa(p, sinks):
    s = p.shape[0]
    w = SWA_WINDOW
    rb = min(512, s)
    nblk = rb // w
    order = np.asarray([[hk * SWA_GROUP + gq for gq in SWA_GORDER] for hk in range(SWA_KV_HEADS)])
    sink_rows = jnp.repeat(sinks[order], w, axis=1).reshape(SWA_KV_HEADS, 1, SWA_GROUP * w)
    prev = lambda col: (lambda i: (jnp.maximum(i * nblk - 1, 0), col))
    return pl.pallas_call(
        functools.partial(_swa_kernel, nblk=nblk),
        out_shape=jax.ShapeDtypeStruct((s, SWA_OUT), BF16),
        grid=(s // rb,),
        in_specs=[pl.BlockSpec((rb, SWA_OUT), lambda i: (i, P_SWQ // SWA_OUT)),
                  pl.BlockSpec((rb, SWA_KV), lambda i: (i, P_SWK // SWA_KV)),
                  pl.BlockSpec((rb, SWA_KV), lambda i: (i, P_SWV // SWA_KV)),
                  pl.BlockSpec((w, SWA_KV), prev(P_SWK // SWA_KV)),
                  pl.BlockSpec((w, SWA_KV), prev(P_SWV // SWA_KV)),
                  pl.BlockSpec(sink_rows.shape, lambda i: (0, 0, 0))],
        out_specs=pl.BlockSpec((rb, SWA_OUT), lambda i: (i, 0)),
        compiler_params=_params(("parallel",)),
        name="swa",
    )(p, p, p, p, p, sink_rows)


def _outproj_kernel(oa_ref, ob_ref, oc_ref, w_ref, x_ref, pn_ref, g1_ref, fn_ref, sc_ref, sh_ref,
                    x1_ref, h2_ref):
    a_w, b_w = oa_ref.shape[1], ob_ref.shape[1]
    mix = (_dot(oa_ref[...], w_ref[0:a_w, :]) + _dot(ob_ref[...], w_ref[a_w:a_w + b_w, :])
           + _dot(oc_ref[...], w_ref[a_w + b_w:, :]))
    x1 = x_ref[...] + g1_ref[...] * (_rms(mix) * pn_ref[...])
    x1_ref[...] = x1
    h2_ref[...] = _norm_mod(x1, fn_ref[...], sc_ref[...], sh_ref[...]).astype(BF16)


def _out_proj(o_a, o_b, o_c, w_out, layer, x, post_norm, gate1, ffn_norm, scale2, shift2):
    s, d = x.shape
    tm = min(256, s)
    row = lambda i: (i, 0)
    vec = pl.BlockSpec((1, d), lambda i: (0, 0))
    return pl.pallas_call(
        _outproj_kernel,
        out_shape=(jax.ShapeDtypeStruct((s, d), F32), jax.ShapeDtypeStruct((s, d), BF16)),
        grid=(s // tm,),
        in_specs=[pl.BlockSpec((tm, o_a.shape[1]), row), pl.BlockSpec((tm, o_b.shape[1]), row),
                  pl.BlockSpec((tm, o_c.shape[1]), row),
                  pl.BlockSpec((None,) + w_out.shape[1:], lambda i: (layer, 0, 0), pipeline_mode=pl.Buffered(1)),
                  pl.BlockSpec((tm, d), row), vec, vec, vec, vec, vec],
        out_specs=(pl.BlockSpec((tm, d), row), pl.BlockSpec((tm, d), row)),
        compiler_params=_params(("parallel",)),
        name="out_proj",
    )(o_a, o_b, o_c, w_out, x, post_norm, gate1, ffn_norm, scale2, shift2)


def _gelu_tanh(x):
    return 0.5 * x * (1.0 + jnp.tanh(math.sqrt(2.0 / math.pi) * (x + 0.044715 * (x * x * x))))


def _ffn_up_kernel(h_ref, wg_ref, wu_ref, cg_ref, cu_ref, bg_ref, bu_ref, o_ref,
                   wg16_ref, wu16_ref, xg_ref, xu_ref, *, tm):
    @pl.when(pl.program_id(1) == 0)
    def _():
        wg16_ref[...] = wg_ref[...].astype(BF16)
        wu16_ref[...] = wu_ref[...].astype(BF16)
        xg_ref[0:8, :] = jnp.zeros((8, xg_ref.shape[1]), F32)
        xu_ref[0:8, :] = jnp.zeros((8, xu_ref.shape[1]), F32)

    h = h_ref[...]

    def conv(x_ref, w_ref, cw_ref, b_ref):
        x_ref[8:8 + tm, :] = _dot(h, w_ref[...])
        y = b_ref[...] + cw_ref[2:3, :] * x_ref[8:8 + tm, :]
        y = y + cw_ref[1:2, :] * x_ref[7:7 + tm, :]
        y = y + cw_ref[0:1, :] * x_ref[6:6 + tm, :]
        x_ref[0:8, :] = x_ref[tm:tm + 8, :]
        return y

    gate = conv(xg_ref, wg16_ref, cg_ref, bg_ref)
    up = conv(xu_ref, wu16_ref, cu_ref, bu_ref)
    o_ref[...] = (_gelu_tanh(gate) * up).astype(BF16)


def _ffn_up(h2, w_up, layer, conv_w, conv_b):
    s, d = h2.shape
    d_ff = w_up.shape[2] // 2
    tm, tn = min(1024, s), 512
    nj = d_ff // tn
    lo = lambda j, i: (0, j)
    hi = lambda j, i: (0, j + nj)
    w_lo = pl.BlockSpec((None, d, tn), lambda j, i: (layer, 0, j))
    w_hi = pl.BlockSpec((None, d, tn), lambda j, i: (layer, 0, j + nj))
    return pl.pallas_call(
        functools.partial(_ffn_up_kernel, tm=tm),
        out_shape=jax.ShapeDtypeStruct((s, d_ff), BF16),
        grid=(nj, s // tm),
        in_specs=[pl.BlockSpec((tm, d), lambda j, i: (i, 0)),
                  w_lo, w_hi,
                  pl.BlockSpec((FFN_CONV, tn), lo), pl.BlockSpec((FFN_CONV, tn), hi),
                  pl.BlockSpec((1, tn), lo), pl.BlockSpec((1, tn), hi)],
        out_specs=pl.BlockSpec((tm, tn), lambda j, i: (i, j)),
        scratch_shapes=[pltpu.VMEM((d, tn), BF16), pltpu.VMEM((d, tn), BF16),
                        pltpu.VMEM((tm + 8, tn), F32), pltpu.VMEM((tm + 8, tn), F32)],
        compiler_params=_params(("parallel", "arbitrary")),
        name="ffn_up",
    )(h2, w_up, w_up, conv_w, conv_w, conv_b, conv_b)


def _ffn_down_kernel(g_ref, w_ref, x_ref, pn_ref, g2_ref, *rest):
    y = _dot(g_ref[...], w_ref[...])
    x2 = x_ref[...] + g2_ref[...] * (_rms(y) * pn_ref[...])
    if len(rest) == 1:
        rest[0][...] = x2
    else:
        nn_ref, sc_ref, sh_ref, x2_ref, hn_ref = rest
        x2_ref[...] = x2
        hn_ref[...] = _norm_mod(x2, nn_ref[...], sc_ref[...], sh_ref[...]).astype(BF16)


def _ffn_down(g, w_down, layer, x1, post_norm, gate2, next_pre_norm=None):
    s, d = x1.shape
    d_ff = g.shape[1]
    tm = min(256, s)
    row = lambda i: (i, 0)
    vec = pl.BlockSpec((1, d), lambda i: (0, 0))
    tile_f32 = jax.ShapeDtypeStruct((s, d), F32)
    tile = pl.BlockSpec((tm, d), row)
    last = next_pre_norm is None
    return pl.pallas_call(
        _ffn_down_kernel,
        out_shape=tile_f32 if last else (tile_f32, jax.ShapeDtypeStruct((s, d), BF16)),
        grid=(s // tm,),
        in_specs=[pl.BlockSpec((tm, d_ff), row),
                  pl.BlockSpec((None, d_ff, d), lambda i: (layer, 0, 0), pipeline_mode=pl.Buffered(1)),
                  tile, vec, vec] + ([] if last else [vec, vec, vec]),
        out_specs=tile if last else (tile, tile),
        compiler_params=_params(("parallel",)),
        name="ffn_down",
    )(g, w_down, x1, post_norm, gate2, *(() if last else next_pre_norm))


def _layout_w_in(w_in):
    depth, d, _ = w_in.shape
    sizes = (GDN_QK, GDN_QK, GDN_V, GDN_V, GDN_HEADS, GDN_HEADS, MLA_Q_RANK, MLA_KV_RANK, MLA_ROPE,
             SWA_OUT, SWA_KV, SWA_KV)
    offs = np.concatenate([[0], np.cumsum(sizes)])
    part = lambda n: w_in[:, :, offs[n]:offs[n + 1]].astype(BF16)
    zeros = lambda n: jnp.zeros((depth, d, n), BF16)
    w_p = jnp.concatenate([w_in[:, :, :offs[4]].astype(BF16),
                           part(6), zeros(MLA_Q_RANK_PAD - MLA_Q_RANK),
                           part(9), part(7), part(8), zeros(LANES - MLA_ROPE), part(10), part(11)], axis=2)
    w_ab = jnp.concatenate([part(4), part(5), zeros(LANES - 2 * GDN_HEADS)], axis=2)
    return w_p, w_ab


def _layout_w_out(w_out):
    depth, d_mix, d = w_out.shape
    a = d_mix - SWA_OUT
    wc = w_out[:, a:].reshape(depth, SWA_KV_HEADS, SWA_GROUP, SWA_DH, d)[:, :, np.asarray(SWA_GORDER)]
    wc = wc.transpose(0, 2, 1, 3, 4).reshape(depth, SWA_OUT, d)
    return jnp.concatenate([w_out[:, :a].astype(BF16), wc.astype(BF16)], axis=1)


def _layout_mla(q_norm, w_uq, kv_norm, w_ukv):
    dqk = MLA_NOPE + MLA_ROPE
    uq = w_uq.reshape(MLA_Q_RANK, MLA_HEADS, dqk)
    uq = jnp.concatenate([uq[:, :, :MLA_NOPE].reshape(MLA_Q_RANK, -1), uq[:, :, MLA_NOPE:].reshape(MLA_Q_RANK, -1)],
                         axis=1)
    uq = jnp.pad(uq, ((0, MLA_Q_RANK_PAD - MLA_Q_RANK), (0, 0))).astype(BF16)
    qn = jnp.pad(q_norm, (0, MLA_Q_RANK_PAD - MLA_Q_RANK)).reshape(1, MLA_Q_RANK_PAD)
    ukv = w_ukv.reshape(MLA_KV_RANK, MLA_HEADS, MLA_NOPE + MLA_DV)
    uk = ukv[:, :, :MLA_NOPE].reshape(MLA_KV_RANK, -1).astype(BF16)
    uvt = ukv[:, :, MLA_NOPE:].reshape(MLA_KV_RANK, -1).T.astype(BF16)
    return qn, uq, kv_norm.reshape(1, MLA_KV_RANK), uk, uvt


def kernel(x, c, positions, ada_w, ada_b, mix_pre_norm, mix_post_norm, w_in, w_out, gdn_conv, gdn_a_log, gdn_dt_bias, gdn_norm, mla_q_norm, mla_w_uq, mla_kv_norm, mla_w_ukv, swa_sinks, ffn_pre_norm, ffn_post_norm, ffn_w_up, ffn_conv, ffn_conv_b, ffn_w_down):
    batch, s, d = x.shape
    assert batch == 1, "kernels are written for a single sequence"
    depth = ada_w.shape[0]
    xs = x.reshape(s, d)
    mod = _adaln_mod(c, ada_w, ada_b).reshape(depth, N_MOD, 1, d)
    cos, sin = _rope_tables(positions)
    vec = lambda a: a.reshape(1, d)

    w_p, w_ab = _layout_w_in(w_in)
    w_o = _layout_w_out(w_out)
    w_down = ffn_w_down.astype(BF16)

    h = _pre_norm(xs, vec(mix_pre_norm[0]), mod[0, 1], mod[0, 0])
    for l in range(depth):
        shift1, scale1, gate1, shift2, scale2, gate2 = (mod[l, n] for n in range(N_MOD))
        p, ab = _in_proj(h, w_p, w_ab, l)

        q_a, k_a, v_a, gcb, gct = _gdn_prep(p, ab, gdn_conv[l], gdn_a_log[l], gdn_dt_bias[l])
        u, w, qd, kt, qk = _gdn_local(q_a, k_a, v_a, gcb, gct)
        o_a = _gdn_scan(u, w, qd, kt, qk, gcb, p, gdn_norm[l])

        q_b, k_b, v_b = _mla_proj(p, cos, sin, *_layout_mla(mla_q_norm[l], mla_w_uq[l], mla_kv_norm[l],
                                                            mla_w_ukv[l]))
        o_b = _mla_attn(q_b, k_b, v_b)

        o_c = _swa(p, swa_sinks[l])

        x1, h2 = _out_proj(o_a, o_b, o_c, w_o, l, xs, vec(mix_post_norm[l]), gate1,
                           vec(ffn_pre_norm[l]), scale2, shift2)
        g = _ffn_up(h2, ffn_w_up, l, ffn_conv[l], ffn_conv_b[l].reshape(1, -1))
        if l + 1 < depth:
            xs, h = _ffn_down(g, w_down, l, x1, vec(ffn_post_norm[l]), gate2,
                              (vec(mix_pre_norm[l + 1]), mod[l + 1, 1], mod[l + 1, 0]))
        else:
            xs = _ffn_down(g, w_down, l, x1, vec(ffn_post_norm[l]), gate2)
    return xs.reshape(batch, s, d)
```

```python
import functools
import math

import numpy as np
import jax
import jax.numpy as jnp
from jax import lax
from jax.experimental import pallas as pl
from jax.experimental.pallas import tpu as pltpu

F32 = jnp.float32
BF16 = jnp.bfloat16

EPS = 1e-6
N_MOD = 6
GDN_HEADS = 8
GDN_DK = 128
GDN_DV = 128
GDN_CONV = 4
GDN_CHUNK = 64
GDN_QK = GDN_HEADS * GDN_DK
GDN_V = GDN_HEADS * GDN_DV
MLA_HEADS = 4
MLA_Q_RANK = 448
MLA_Q_RANK_PAD = 512
MLA_KV_RANK = 128
MLA_NOPE = 128
MLA_ROPE = 64
MLA_DV = 128
MLA_QK_PAD = 256
MLA_BLOCK = 512
ROPE_THETA = 10000.0
SWA_HEADS = 8
SWA_KV_HEADS = 2
SWA_GROUP = SWA_HEADS // SWA_KV_HEADS
SWA_DH = 64
SWA_WINDOW = 128
SWA_OUT = SWA_HEADS * SWA_DH
SWA_KV = SWA_KV_HEADS * SWA_DH
SWA_GORDER = (0, 2, 1, 3)
FFN_CONV = 3

LANES = 128
GDN_GROUP = 256
VMEM_LIMIT_MB = 56

P_QKV = 0
P_Z = 3072
P_CQ = 4096
P_SWQ = 4608
P_CKV = 5120
P_KROPE = 5248
P_SWK = 5376
P_SWV = 5504
P_WIDTH = 5632


def _params(semantics):
    return pltpu.CompilerParams(dimension_semantics=semantics, vmem_limit_bytes=VMEM_LIMIT_MB << 20)


def _sigmoid(x):
    return 1.0 / (1.0 + jnp.exp(-x))


def _silu(x):
    return x * _sigmoid(x)


def _softplus(x):
    return jnp.maximum(x, 0.0) + jnp.log(1.0 + jnp.exp(-jnp.abs(x)))


def _rms(x):
    return x * lax.rsqrt(jnp.mean(x * x, axis=-1, keepdims=True) + EPS)


def _dot(a, b):
    return jnp.dot(a, b, preferred_element_type=F32)


def _dot_nt(a, b):
    return lax.dot_general(a, b, (((1,), (1,)), ((), ())), preferred_element_type=F32)


def _dot_tn(a, b):
    return lax.dot_general(a, b, (((0,), (0,)), ((), ())), preferred_element_type=F32)


def _mod_kernel(c_ref, w_ref, b_ref, o_ref, *, d, kc):
    cact = _silu(c_ref[...])
    acc = b_ref[0]
    for k0 in range(0, d, kc):
        acc = acc + jnp.sum(w_ref[0, k0:k0 + kc, :] * cact[k0:k0 + kc], axis=0, keepdims=True)
    o_ref[0] = acc


def _adaln_mod(c, ada_w, ada_b):
    depth, d, n = ada_w.shape
    tn = 1024
    return pl.pallas_call(
        functools.partial(_mod_kernel, d=d, kc=256),
        out_shape=jax.ShapeDtypeStruct((depth, 1, n), F32),
        grid=(depth, n // tn),
        in_specs=[pl.BlockSpec((d, 1), lambda l, j: (0, 0)),
                  pl.BlockSpec((1, d, tn), lambda l, j: (l, 0, j)),
                  pl.BlockSpec((1, 1, tn), lambda l, j: (l, 0, j))],
        out_specs=pl.BlockSpec((1, 1, tn), lambda l, j: (l, 0, j)),
        compiler_params=_params(("parallel", "parallel")),
        name="adaln_mod",
    )(c.reshape(d, 1), ada_w, ada_b.reshape(depth, 1, n))


def _norm_mod(x, w, scale, shift):
    return _rms(x) * w * (1.0 + scale) + shift


def _inproj_kernel(h_ref, w_ref, wab_ref, p_ref, ab_ref):
    h = h_ref[...]
    p_ref[...] = _dot(h, w_ref[...]).astype(BF16)

    @pl.when(pl.program_id(1) == 0)
    def _():
        ab_ref[...] = _dot(h, wab_ref[...])


def _inproj_norm_kernel(x_ref, nw_ref, sc_ref, sh_ref, w_ref, wab_ref, p_ref, ab_ref, h_ref):
    @pl.when(pl.program_id(1) == 0)
    def _():
        h_ref[...] = _norm_mod(x_ref[...], nw_ref[...], sc_ref[...], sh_ref[...]).astype(BF16)

    _inproj_kernel(h_ref, w_ref, wab_ref, p_ref, ab_ref)


def _in_proj(h, w_p, w_ab, layer, pre_norm=None):
    s, d = h.shape
    n = w_p.shape[2]
    tm, tn = min(2048 if pre_norm is None else 1024, s), 512
    vec = pl.BlockSpec((1, d), lambda i, j: (0, 0))
    fused = pre_norm is not None
    return pl.pallas_call(
        _inproj_norm_kernel if fused else _inproj_kernel,
        out_shape=(jax.ShapeDtypeStruct((s, n), BF16), jax.ShapeDtypeStruct((s, LANES), F32)),
        grid=(s // tm, n // tn),
        in_specs=[pl.BlockSpec((tm, d), lambda i, j: (i, 0))] + ([vec, vec, vec] if fused else [])
                 + [pl.BlockSpec((None, d, tn), lambda i, j: (layer, 0, j)),
                    pl.BlockSpec((None, d, LANES), lambda i, j: (layer, 0, 0))],
        out_specs=(pl.BlockSpec((tm, tn), lambda i, j: (i, j)),
                   pl.BlockSpec((tm, LANES), lambda i, j: (i, 0))),
        scratch_shapes=[pltpu.VMEM((tm, d), BF16)] if fused else [],
        compiler_params=_params(("parallel", "arbitrary")),
        name="in_proj",
    )(h, *(pre_norm or ()), w_p, w_ab)


def _gdn_prep_kernel(x_ref, halo_ref, cw_ref, ab_ref, alog_ref, dtb_ref, tri_ref,
                     q_ref, k_ref, v_ref, gcb_ref, gct_ref, xp_ref, *, tm):
    i = pl.program_id(0)
    outs = (q_ref, k_ref, v_ref)
    for grp in range(3):
        c0 = grp * GDN_QK
        halo = halo_ref[:, c0:c0 + GDN_QK].astype(F32)
        xp_ref[0:8, :] = jnp.where(i > 0, halo, 0.0)
        xp_ref[8:8 + tm, :] = x_ref[:, c0:c0 + GDN_QK].astype(F32)
        y = cw_ref[3:4, c0:c0 + GDN_QK] * xp_ref[8:8 + tm, :]
        for j in range(GDN_CONV - 1):
            off = 8 - (GDN_CONV - 1) + j
            y = y + cw_ref[j:j + 1, c0:c0 + GDN_QK] * xp_ref[off:off + tm, :]
        y = _silu(y)
        for h in range(GDN_HEADS):
            yh = y[:, h * GDN_DK:(h + 1) * GDN_DK]
            if grp < 2:
                yh = yh * lax.rsqrt(jnp.sum(yh * yh, axis=-1, keepdims=True) + EPS)
            if grp == 0:
                yh = yh * (GDN_DK ** -0.5)
            outs[grp][:, h * GDN_DK:(h + 1) * GDN_DK] = yh.astype(BF16)

    ab = ab_ref[...]
    g = -jnp.exp(alog_ref[...]) * _softplus(ab + dtb_ref[...])
    gc = jnp.dot(tri_ref[...], g, preferred_element_type=F32, precision=lax.Precision.HIGHEST)
    lane = lax.broadcasted_iota(jnp.int32, (1, LANES), 1)
    gcb = jnp.where(lane < GDN_HEADS, gc, _sigmoid(ab))
    gcb_ref[...] = gcb
    gct_ref[...] = gcb.T[0:2 * GDN_HEADS, :]


def _gdn_prep(p, ab, conv_w, a_log, dt_bias):
    s = p.shape[0]
    tm = min(512, s)
    c3 = 3 * GDN_QK
    alog_row = jnp.zeros((1, LANES), F32).at[0, :GDN_HEADS].set(a_log)
    dtb_row = jnp.zeros((1, LANES), F32).at[0, :GDN_HEADS].set(dt_bias)
    r = np.arange(tm)
    tri = jnp.asarray(((r[:, None] >= r[None, :]) & (r[:, None] // GDN_CHUNK == r[None, :] // GDN_CHUNK))
                      .astype(np.float32))
    hb = tm // 8
    row = lambda i: (i, 0)
    return pl.pallas_call(
        functools.partial(_gdn_prep_kernel, tm=tm),
        out_shape=(jax.ShapeDtypeStruct((s, GDN_QK), BF16), jax.ShapeDtypeStruct((s, GDN_QK), BF16),
                   jax.ShapeDtypeStruct((s, GDN_V), BF16), jax.ShapeDtypeStruct((s, LANES), F32),
                   jax.ShapeDtypeStruct((2 * GDN_HEADS, s), F32)),
        grid=(s // tm,),
        in_specs=[pl.BlockSpec((tm, c3), row),
                  pl.BlockSpec((8, c3), lambda i: (jnp.maximum(i * hb - 1, 0), 0)),
                  pl.BlockSpec((GDN_CONV, c3), lambda i: (0, 0)),
                  pl.BlockSpec((tm, LANES), row),
                  pl.BlockSpec((1, LANES), lambda i: (0, 0)),
                  pl.BlockSpec((1, LANES), lambda i: (0, 0)),
                  pl.BlockSpec((tm, tm), lambda i: (0, 0))],
        out_specs=(pl.BlockSpec((tm, GDN_QK), row), pl.BlockSpec((tm, GDN_QK), row),
                   pl.BlockSpec((tm, GDN_V), row), pl.BlockSpec((tm, LANES), row),
                   pl.BlockSpec((2 * GDN_HEADS, tm), lambda i: (0, i))),
        scratch_shapes=[pltpu.VMEM((tm + 8, GDN_QK), F32)],
        compiler_params=_params(("parallel",)),
        name="gdn_prep",
    )(p, p, conv_w, ab, alog_row, dtb_row, tri)


def _gdn_local_kernel(q_ref, k_ref, v_ref, gcb_ref, gct_ref,
                      u_ref, w_ref, qd_ref, kt_ref, qk_ref):
    n = GDN_GROUP
    c = GDN_CHUNK
    ri = lax.broadcasted_iota(jnp.int32, (n, n), 0)
    ci = lax.broadcasted_iota(jnp.int32, (n, n), 1)
    same_chunk = (ri // c) == (ci // c)
    incl = same_chunk & (ri >= ci)
    strict = same_chunk & (ri > ci)
    eye = (ri == ci).astype(F32)
    last_sel = (ci == (ri // c) * c + (c - 1)).astype(F32)
    gcb = gcb_ref[...]
    gc_last_all = jnp.dot(last_sel, gcb, preferred_element_type=F32, precision=lax.Precision.HIGHEST)
    heads = range(GDN_HEADS)
    sls = [slice(h * GDN_DK, (h + 1) * GDN_DK) for h in heads]
    lmats, ts, rhss = [], [], []
    for h in heads:
        sl = sls[h]
        q = q_ref[:, sl].astype(F32)
        k16 = k_ref[:, sl]
        k = k16.astype(F32)
        v = v_ref[:, sl].astype(F32)
        gc_col = gcb[:, h:h + 1]
        beta = gcb[:, GDN_HEADS + h:GDN_HEADS + h + 1]
        gc_row = gct_ref[h:h + 1, :]
        gc_last = gc_last_all[:, h:h + 1]
        diff = gc_col - gc_row
        decay = jnp.where(incl, jnp.exp(jnp.where(incl, diff, 0.0)), 0.0)
        kb = k * beta
        kk = _dot_nt(kb.astype(BF16), k16)
        lmat = jnp.where(strict, kk * decay, 0.0)
        lmats.append(lmat)
        ts.append(eye - jnp.where((ri // 2) == (ci // 2), lmat, 0.0))
        eg = jnp.exp(gc_col)
        rhss.append(jnp.concatenate([v * beta, kb * eg], axis=-1).astype(BF16))
        qd_ref[:, sl] = (q * eg).astype(BF16)
        kt_ref[:, sl] = (k * jnp.exp(gc_last - gc_col)).astype(BF16)
        qk = _dot_nt(q_ref[:, sl], k16) * decay
        qk_c = jnp.concatenate([qk[j * c:(j + 1) * c, j * c:(j + 1) * c] for j in range(n // c)], axis=0)
        qk_ref[:, h * c:(h + 1) * c] = qk_c.astype(BF16)
    b = 2
    while b < c:
        off_diag = ((ri // (2 * b)) == (ci // (2 * b))) & ((ri // b) != (ci // b))
        t16s = [ts[h].astype(BF16) for h in heads]
        ys = [_dot(t16s[h], jnp.where(off_diag, lmats[h], 0.0).astype(BF16)).astype(BF16) for h in heads]
        ts = [ts[h] - _dot(ys[h], t16s[h]) for h in heads]
        b *= 2
    for h in heads:
        uw = _dot(ts[h].astype(BF16), rhss[h])
        u_ref[:, sls[h]] = uw[:, :GDN_DV].astype(BF16)
        w_ref[:, sls[h]] = uw[:, GDN_DV:].astype(BF16)


def _gdn_local(q, k, v, gcb, gct):
    s = q.shape[0]
    n = GDN_GROUP
    row = lambda i: (i, 0)
    wide = pl.BlockSpec((n, GDN_QK), row)
    return pl.pallas_call(
        _gdn_local_kernel,
        out_shape=(jax.ShapeDtypeStruct((s, GDN_V), BF16), jax.ShapeDtypeStruct((s, GDN_QK), BF16),
                   jax.ShapeDtypeStruct((s, GDN_QK), BF16), jax.ShapeDtypeStruct((s, GDN_QK), BF16),
                   jax.ShapeDtypeStruct((s, GDN_HEADS * GDN_CHUNK), BF16)),
        grid=(s // n,),
        in_specs=[wide, wide, wide, pl.BlockSpec((n, LANES), row),
                  pl.BlockSpec((2 * GDN_HEADS, n), lambda i: (0, i))],
        out_specs=(wide, wide, wide, wide, pl.BlockSpec((n, GDN_HEADS * GDN_CHUNK), row)),
        compiler_params=_params(("parallel",)),
        name="gdn_local",
    )(q, k, v, gcb, gct)


def _gdn_scan_kernel(u_ref, w_ref, qd_ref, kt_ref, qk_ref, gcb_ref, z_ref, nw_ref, o_ref, st_ref, *, rb):
    c = GDN_CHUNK

    @pl.when(pl.program_id(0) == 0)
    def _():
        st_ref[...] = jnp.zeros_like(st_ref)

    nw = nw_ref[...]

    def chunk(ci, carry):
        r0 = pl.multiple_of(ci * c, c)
        rows = pl.ds(r0, c)
        gt_row = jnp.exp(gcb_ref[pl.ds(r0 + c - 1, 1), :])
        heads = range(GDN_HEADS)
        sls = [slice(h * GDN_DK, (h + 1) * GDN_DK) for h in heads]
        sts = [st_ref[h] for h in heads]
        r1s = [_dot(jnp.concatenate([w_ref[rows, sls[h]], qd_ref[rows, sls[h]]], axis=0),
                    sts[h].astype(BF16)) for h in heads]
        vns = [(u_ref[rows, sls[h]].astype(F32) - r1s[h][0:c]).astype(BF16) for h in heads]
        for h in heads:
            st_ref[h] = sts[h] * gt_row[:, h:h + 1] + _dot_tn(kt_ref[rows, sls[h]], vns[h])
        os_ = [r1s[h][c:2 * c] + _dot(qk_ref[rows, h * c:(h + 1) * c], vns[h]) for h in heads]
        for h in heads:
            o = _rms(os_[h]) * nw * _silu(z_ref[rows, sls[h]].astype(F32))
            o_ref[rows, sls[h]] = o.astype(BF16)
        return carry

    lax.fori_loop(0, rb // c, chunk, 0)


def _gdn_scan(u, w, qd, kt, qk, gcb, p, norm_w):
    s = u.shape[0]
    rb = min(512, s)
    row = lambda i: (i, 0)
    wide = pl.BlockSpec((rb, GDN_QK), row)
    return pl.pallas_call(
        functools.partial(_gdn_scan_kernel, rb=rb),
        out_shape=jax.ShapeDtypeStruct((s, GDN_V), BF16),
        grid=(s // rb,),
        in_specs=[wide, wide, wide, wide, pl.BlockSpec((rb, GDN_HEADS * GDN_CHUNK), row),
                  pl.BlockSpec((rb, LANES), row),
                  pl.BlockSpec((rb, GDN_V), lambda i: (i, P_Z // GDN_V)),
                  pl.BlockSpec((1, GDN_DV), lambda i: (0, 0))],
        out_specs=wide,
        scratch_shapes=[pltpu.VMEM((GDN_HEADS, GDN_DK, GDN_DV), F32)],
        compiler_params=_params(("arbitrary",)),
        name="gdn_scan",
    )(u, w, qd, kt, qk, gcb, p, norm_w.reshape(1, GDN_DV))


def _rope_table_kernel(pos_ref, inv_ref, cos_ref, sin_ref):
    ang = pos_ref[...].astype(F32) * inv_ref[...]
    lane = lax.broadcasted_iota(jnp.int32, ang.shape, 1)
    first_half = (lane % MLA_ROPE) < (MLA_ROPE // 2)
    cos_ref[...] = jnp.cos(ang)
    sin_ref[...] = jnp.where(first_half, -jnp.sin(ang), jnp.sin(ang))


def _rope_tables(positions):
    s = positions.shape[-1]
    tm = min(1024, s)
    half = MLA_ROPE // 2
    inv = (ROPE_THETA ** (-np.arange(half, dtype=np.float32) / half)).astype(np.float32)
    inv_row = jnp.asarray(np.tile(inv, LANES // half).reshape(1, LANES))
    return pl.pallas_call(
        _rope_table_kernel,
        out_shape=(jax.ShapeDtypeStruct((s, LANES), F32), jax.ShapeDtypeStruct((s, LANES), F32)),
        grid=(s // tm,),
        in_specs=[pl.BlockSpec((tm, 1), lambda i: (i, 0)), pl.BlockSpec((1, LANES), lambda i: (0, 0))],
        out_specs=(pl.BlockSpec((tm, LANES), lambda i: (i, 0)), pl.BlockSpec((tm, LANES), lambda i: (i, 0))),
        compiler_params=_params(("parallel",)),
        name="rope_tables",
    )(positions.reshape(s, 1), inv_row)


def _rope_apply(x, cos, sin_signed):
    width = x.shape[-1]
    half = MLA_ROPE // 2
    lane = lax.broadcasted_iota(jnp.int32, x.shape, 1)
    first_half = (lane % MLA_ROPE) < half
    swapped = jnp.where(first_half, pltpu.roll(x, width - half, 1), pltpu.roll(x, half, 1))
    return x * cos + swapped * sin_signed


def _mla_proj_kernel(cq_ref, ckv_ref, kr_ref, cos_ref, sin_ref, qn_ref, wuq_ref, kvn_ref, wuk_ref, wuvt_ref,
                     q_ref, k_ref, vt_ref):
    cos = cos_ref[...]
    sin = sin_ref[...]
    scale = (MLA_NOPE + MLA_ROPE) ** -0.5 * math.log2(math.e)
    cq = cq_ref[...].astype(F32)
    cqn = cq * lax.rsqrt(jnp.sum(cq * cq, axis=-1, keepdims=True) * (1.0 / MLA_Q_RANK) + EPS) * qn_ref[...]
    q = _dot(cqn.astype(BF16), wuq_ref[...])
    nope_w = MLA_HEADS * MLA_NOPE
    q_rope = _rope_apply(q[:, nope_w:], jnp.concatenate([cos, cos], -1), jnp.concatenate([sin, sin], -1))
    zeros = jnp.zeros((q.shape[0], MLA_QK_PAD - MLA_NOPE - MLA_ROPE), F32)
    ckvn = (_rms(ckv_ref[...].astype(F32)) * kvn_ref[...]).astype(BF16)
    k_nope = _dot(ckvn, wuk_ref[...])
    v_t = _dot_nt(wuvt_ref[...], ckvn)
    k_pe = _rope_apply(kr_ref[...].astype(F32), cos, sin)[:, :MLA_ROPE]
    for h in range(MLA_HEADS):
        qh = jnp.concatenate([q[:, h * MLA_NOPE:(h + 1) * MLA_NOPE],
                              q_rope[:, h * MLA_ROPE:(h + 1) * MLA_ROPE], zeros], axis=-1) * scale
        q_ref[h] = qh.astype(BF16)
        kh = jnp.concatenate([k_nope[:, h * MLA_NOPE:(h + 1) * MLA_NOPE], k_pe, zeros], axis=-1)
        k_ref[h] = kh.astype(BF16)
        vt_ref[h, 0] = v_t[h * MLA_DV:(h + 1) * MLA_DV, :].astype(BF16)


def _mla_proj(p, cos, sin, q_norm, w_uq, kv_norm, w_uk, w_uvt):
    s = p.shape[0]
    tm = min(MLA_BLOCK, s)
    const = lambda i: (0, 0)
    return pl.pallas_call(
        _mla_proj_kernel,
        out_shape=(jax.ShapeDtypeStruct((MLA_HEADS, s, MLA_QK_PAD), BF16),
                   jax.ShapeDtypeStruct((MLA_HEADS, s, MLA_QK_PAD), BF16),
                   jax.ShapeDtypeStruct((MLA_HEADS, s // tm, MLA_DV, tm), BF16)),
        grid=(s // tm,),
        in_specs=[pl.BlockSpec((tm, MLA_Q_RANK_PAD), lambda i: (i, P_CQ // MLA_Q_RANK_PAD)),
                  pl.BlockSpec((tm, LANES), lambda i: (i, P_CKV // LANES)),
                  pl.BlockSpec((tm, LANES), lambda i: (i, P_KROPE // LANES)),
                  pl.BlockSpec((tm, LANES), lambda i: (i, 0)),
                  pl.BlockSpec((tm, LANES), lambda i: (i, 0)),
                  pl.BlockSpec(q_norm.shape, const), pl.BlockSpec(w_uq.shape, const),
                  pl.BlockSpec(kv_norm.shape, const), pl.BlockSpec(w_uk.shape, const),
                  pl.BlockSpec(w_uvt.shape, const)],
        out_specs=(pl.BlockSpec((MLA_HEADS, tm, MLA_QK_PAD), lambda i: (0, i, 0)),
                   pl.BlockSpec((MLA_HEADS, tm, MLA_QK_PAD), lambda i: (0, i, 0)),
                   pl.BlockSpec((MLA_HEADS, 1, MLA_DV, tm), lambda i: (0, i, 0, 0))),
        compiler_params=_params(("parallel",)),
        name="mla_proj",
    )(p, p, p, cos, sin, q_norm, w_uq, kv_norm, w_uk, w_uvt)


def _mla_attn_kernel(q_ref, k_ref, vt_ref, o_ref, st0, st1, mb0, mb1, m_ref, l_ref, acc_ref, *, bk, nd):
    qi = pl.program_id(1)
    slots = ((st0, mb0), (st1, mb1))
    bq = nd * bk
    n0 = nd * qi
    tri = lax.broadcasted_iota(jnp.int32, (bk, bk), 0) <= lax.broadcasted_iota(jnp.int32, (bk, bk), 1)

    def scores(t, slot, c0, masked):
        st_ref, mb_ref = slots[slot]
        r0 = pl.multiple_of(t * bk, bk)
        st = _dot_nt(k_ref[0, pl.ds(r0, bk), :], q_ref[0, c0:c0 + bk, :])
        if masked:
            st = jnp.where(tri, st, -jnp.inf)
        st_ref[:, c0:c0 + bk] = st
        mb_ref[:, c0:c0 + bk] = jnp.max(st, axis=0, keepdims=True)

    def update(t, slot, c0):
        st_ref, mb_ref = slots[slot]
        cols = slice(c0, c0 + bk)
        m = m_ref[:, cols]
        m_new = jnp.maximum(m, mb_ref[:, cols])
        alpha = jnp.exp2(m - m_new)
        pexp = jnp.exp2(st_ref[:, cols] - m_new)
        l_ref[:, cols] = alpha * l_ref[:, cols] + jnp.sum(pexp, axis=0, keepdims=True)
        acc_ref[:, cols] = alpha * acc_ref[:, cols] + _dot(vt_ref[0, t], pexp.astype(BF16))
        m_ref[:, cols] = m_new

    def sweep(ts=None, ss=None, ds=None, tu=None, su=None, du=None):
        for g in range(nd):
            if ts is not None and (ds is None or g >= ds):
                scores(ts, ss, g * bk, masked=(g == ds))
            if tu is not None and (du is None or g >= du):
                update(tu, su, g * bk)

    m_ref[...] = jnp.full(m_ref.shape, -jnp.inf, F32)
    l_ref[...] = jnp.zeros(l_ref.shape, F32)
    acc_ref[...] = jnp.zeros(acc_ref.shape, F32)

    @pl.when(qi == 0)
    def _():
        sweep(ts=0, ss=0, ds=0)

    @pl.when(qi > 0)
    def _():
        sweep(ts=0, ss=0)

    def pair(j, carry):
        sweep(ts=2 * j + 1, ss=1, tu=2 * j, su=0)
        sweep(ts=2 * j + 2, ss=0, tu=2 * j + 1, su=1)
        return carry

    lax.fori_loop(0, n0 // 2 - 1, pair, 0)

    @pl.when(qi > 0)
    def _():
        sweep(ts=n0 - 1, ss=1, tu=n0 - 2, su=0)
        sweep(ts=n0, ss=0, ds=0, tu=n0 - 1, su=1)

    for d in range(1, nd):
        sweep(ts=n0 + d, ss=d % 2, ds=d, tu=n0 + d - 1, su=(d - 1) % 2, du=d - 1)
    sweep(tu=n0 + nd - 1, su=(nd - 1) % 2, du=nd - 1)
    o_ref[...] = (acc_ref[...] / l_ref[...]).T.astype(BF16)


def _mla_attn(q, k, vt):
    nh, s, _ = q.shape
    bk = vt.shape[-1]
    nd = 4 if s % (4 * bk) == 0 else 2
    bq = nd * bk
    assert s % bq == 0
    return pl.pallas_call(
        functools.partial(_mla_attn_kernel, bk=bk, nd=nd),
        out_shape=jax.ShapeDtypeStruct((s, nh * MLA_DV), BF16),
        grid=(nh, s // bq),
        in_specs=[pl.BlockSpec((1, bq, MLA_QK_PAD), lambda h, i: (h, i, 0)),
                  pl.BlockSpec((1, s, MLA_QK_PAD), lambda h, i: (h, 0, 0)),
                  pl.BlockSpec((1, s // bk, MLA_DV, bk), lambda h, i: (h, 0, 0, 0))],
        out_specs=pl.BlockSpec((bq, MLA_DV), lambda h, i: (i, h)),
        scratch_shapes=[pltpu.VMEM((bk, bq), F32), pltpu.VMEM((bk, bq), F32),
                        pltpu.VMEM((1, bq), F32), pltpu.VMEM((1, bq), F32),
                        pltpu.VMEM((1, bq), F32), pltpu.VMEM((1, bq), F32), pltpu.VMEM((MLA_DV, bq), F32)],
        compiler_params=_params(("parallel", "arbitrary")),
        name="mla_attn",
    )(q, k, vt)


def _swa_kernel(q_ref, k_ref, v_ref, kh_ref, vh_ref, sink_ref, o_ref, *, nblk):
    w = SWA_WINDOW
    g = SWA_GROUP
    i = pl.program_id(0)
    key = lax.broadcasted_iota(jnp.int32, (2 * w, g * w), 0)
    col = lax.broadcasted_iota(jnp.int32, (2 * w, g * w), 1)
    dist = col % w + w - key
    band = (dist >= 0) & (dist < w)
    first_band = band & ((key >= w) | (i > 0))
    dist_f = dist.astype(F32)
    biases = []
    for hk in range(SWA_KV_HEADS):
        slope = jnp.zeros((2 * w, g * w), F32)
        for gi, gq in enumerate(SWA_GORDER):
            slope = jnp.where(col // w == gi, 2.0 ** (-8.0 * (hk * g + gq + 1.0) / SWA_HEADS), slope)
        biases.append(-slope * dist_f)
    low = lax.broadcasted_iota(jnp.int32, (2 * w, 2 * SWA_DH), 1) < SWA_DH
    top = lax.broadcasted_iota(jnp.int32, (2 * SWA_DH, 1), 0) < SWA_DH
    for blk in range(nblk):
        rows = slice(blk * w, (blk + 1) * w)
        if blk == 0:
            k_prev, v_prev, valid = kh_ref[...], vh_ref[...], first_band
        else:
            k_prev, v_prev, valid = k_ref[(blk - 1) * w:blk * w, :], v_ref[(blk - 1) * w:blk * w, :], band
        kk = jnp.concatenate([k_prev, k_ref[rows, :]], axis=0).astype(F32)
        kk_sw = pltpu.roll(kk, SWA_DH, 1)
        vvt = jnp.concatenate([v_prev, v_ref[rows, :]], axis=0).astype(F32).T
        probs, inv_denoms = [], []
        for hk in range(SWA_KV_HEADS):
            c0 = hk * g * SWA_DH
            q2 = jnp.concatenate([q_ref[rows, c0:c0 + 2 * SWA_DH], q_ref[rows, c0 + 2 * SWA_DH:c0 + 4 * SWA_DH]],
                                 axis=0) * (SWA_DH ** -0.5)
            k_low = jnp.where(low, kk if hk == 0 else kk_sw, 0.0).astype(BF16)
            k_high = jnp.where(low, 0.0, kk_sw if hk == 0 else kk).astype(BF16)
            st = jnp.concatenate([_dot_nt(k_low, q2), _dot_nt(k_high, q2)], axis=1)
            st = jnp.where(valid, st + biases[hk], -jnp.inf)
            sink = sink_ref[hk]
            m = jnp.maximum(jnp.max(st, axis=0, keepdims=True), sink)
            pexp = jnp.exp(st - m)
            denom = jnp.sum(pexp, axis=0, keepdims=True) + jnp.exp(sink - m)
            probs.append(pexp.astype(BF16))
            inv_denoms.append(1.0 / denom)
        v0 = jnp.where(top, vvt, 0.0).astype(BF16)
        v1 = jnp.where(top, 0.0, vvt).astype(BF16)
        o_t = (_dot(v0, probs[0]) + _dot(v1, probs[1])) * jnp.where(top, inv_denoms[0], inv_denoms[1])
        for gi in range(g):
            o_ref[rows, gi * LANES:(gi + 1) * LANES] = o_t[:, gi * w:(gi + 1) * w].T.astype(BF16)


def _swa(p, sinks):
    s = p.shape[0]
    w = SWA_WINDOW
    rb = min(512, s)
    nblk = rb // w
    order = np.asarray([[hk * SWA_GROUP + gq for gq in SWA_GORDER] for hk in range(SWA_KV_HEADS)])
    sink_rows = jnp.repeat(sinks[order], w, axis=1).reshape(SWA_KV_HEADS, 1, SWA_GROUP * w)
    prev = lambda col: (lambda i: (jnp.maximum(i * nblk - 1, 0), col))
    return pl.pallas_call(
        functools.partial(_swa_kernel, nblk=nblk),
        out_shape=jax.ShapeDtypeStruct((s, SWA_OUT), BF16),
        grid=(s // rb,),
        in_specs=[pl.BlockSpec((rb, SWA_OUT), lambda i: (i, P_SWQ // SWA_OUT)),
                  pl.BlockSpec((rb, SWA_KV), lambda i: (i, P_SWK // SWA_KV)),
                  pl.BlockSpec((rb, SWA_KV), lambda i: (i, P_SWV // SWA_KV)),
                  pl.BlockSpec((w, SWA_KV), prev(P_SWK // SWA_KV)),
                  pl.BlockSpec((w, SWA_KV), prev(P_SWV // SWA_KV)),
                  pl.BlockSpec(sink_rows.shape, lambda i: (0, 0, 0))],
        out_specs=pl.BlockSpec((rb, SWA_OUT), lambda i: (i, 0)),
        compiler_params=_params(("parallel",)),
        name="swa",
    )(p, p, p, p, p, sink_rows)


def _outproj_kernel(oa_ref, ob_ref, oc_ref, w_ref, x_ref, pn_ref, g1_ref, fn_ref, sc_ref, sh_ref,
                    x1_ref, h2_ref, *, sub):
    a_w, b_w = oa_ref.shape[1], ob_ref.shape[1]
    for r0 in range(0, x_ref.shape[0], sub):
        rows = slice(r0, r0 + sub)
        mix = (_dot(oa_ref[rows, :], w_ref[0:a_w, :]) + _dot(ob_ref[rows, :], w_ref[a_w:a_w + b_w, :])
               + _dot(oc_ref[rows, :], w_ref[a_w + b_w:, :]))
        x1 = x_ref[rows, :] + g1_ref[...] * (_rms(mix) * pn_ref[...])
        x1_ref[rows, :] = x1
        h2_ref[rows, :] = _norm_mod(x1, fn_ref[...], sc_ref[...], sh_ref[...]).astype(BF16)


def _out_proj(o_a, o_b, o_c, w_out, layer, x, post_norm, gate1, ffn_norm, scale2, shift2):
    s, d = x.shape
    tm = min(512, s)
    row = lambda i: (i, 0)
    vec = pl.BlockSpec((1, d), lambda i: (0, 0))
    return pl.pallas_call(
        functools.partial(_outproj_kernel, sub=min(256, tm)),
        out_shape=(jax.ShapeDtypeStruct((s, d), F32), jax.ShapeDtypeStruct((s, d), BF16)),
        grid=(s // tm,),
        in_specs=[pl.BlockSpec((tm, o_a.shape[1]), row), pl.BlockSpec((tm, o_b.shape[1]), row),
                  pl.BlockSpec((tm, o_c.shape[1]), row),
                  pl.BlockSpec((None,) + w_out.shape[1:], lambda i: (layer, 0, 0), pipeline_mode=pl.Buffered(1)),
                  pl.BlockSpec((tm, d), row), vec, vec, vec, vec, vec],
        out_specs=(pl.BlockSpec((tm, d), row), pl.BlockSpec((tm, d), row)),
        compiler_params=_params(("parallel",)),
        name="out_proj",
    )(o_a, o_b, o_c, w_out, x, post_norm, gate1, ffn_norm, scale2, shift2)


def _gelu_tanh(x):
    return 0.5 * x * (1.0 + jnp.tanh(math.sqrt(2.0 / math.pi) * (x + 0.044715 * (x * x * x))))


def _ffn_up_kernel(h_ref, wg_ref, wu_ref, cg_ref, cu_ref, bg_ref, bu_ref, o_ref,
                   wg16_ref, wu16_ref, xg_ref, xu_ref, *, tm):
    @pl.when(pl.program_id(1) == 0)
    def _():
        wg16_ref[...] = wg_ref[...].astype(BF16)
        wu16_ref[...] = wu_ref[...].astype(BF16)
        xg_ref[0:8, :] = jnp.zeros((8, xg_ref.shape[1]), F32)
        xu_ref[0:8, :] = jnp.zeros((8, xu_ref.shape[1]), F32)

    h = h_ref[...]

    def conv(x_ref, w_ref, cw_ref, b_ref):
        x_ref[8:8 + tm, :] = _dot(h, w_ref[...])
        y = b_ref[...] + cw_ref[2:3, :] * x_ref[8:8 + tm, :]
        y = y + cw_ref[1:2, :] * x_ref[7:7 + tm, :]
        y = y + cw_ref[0:1, :] * x_ref[6:6 + tm, :]
        x_ref[0:8, :] = x_ref[tm:tm + 8, :]
        return y

    gate = conv(xg_ref, wg16_ref, cg_ref, bg_ref)
    up = conv(xu_ref, wu16_ref, cu_ref, bu_ref)
    o_ref[...] = (_gelu_tanh(gate) * up).astype(BF16)


def _ffn_up(h2, w_up, layer, conv_w, conv_b):
    s, d = h2.shape
    d_ff = w_up.shape[2] // 2
    tm, tn = min(1024, s), 512
    nj = d_ff // tn
    lo = lambda j, i: (0, j)
    hi = lambda j, i: (0, j + nj)
    w_lo = pl.BlockSpec((None, d, tn), lambda j, i: (layer, 0, j))
    w_hi = pl.BlockSpec((None, d, tn), lambda j, i: (layer, 0, j + nj))
    return pl.pallas_call(
        functools.partial(_ffn_up_kernel, tm=tm),
        out_shape=jax.ShapeDtypeStruct((s, d_ff), BF16),
        grid=(nj, s // tm),
        in_specs=[pl.BlockSpec((tm, d), lambda j, i: (i, 0)),
                  w_lo, w_hi,
                  pl.BlockSpec((FFN_CONV, tn), lo), pl.BlockSpec((FFN_CONV, tn), hi),
                  pl.BlockSpec((1, tn), lo), pl.BlockSpec((1, tn), hi)],
        out_specs=pl.BlockSpec((tm, tn), lambda j, i: (i, j)),
        scratch_shapes=[pltpu.VMEM((d, tn), BF16), pltpu.VMEM((d, tn), BF16),
                        pltpu.VMEM((tm + 8, tn), F32), pltpu.VMEM((tm + 8, tn), F32)],
        compiler_params=_params(("parallel", "arbitrary")),
        name="ffn_up",
    )(h2, w_up, w_up, conv_w, conv_w, conv_b, conv_b)


def _ffn_down_kernel(g_ref, w_ref, x_ref, pn_ref, g2_ref, *rest):
    y = _dot(g_ref[...], w_ref[...])
    x2 = x_ref[...] + g2_ref[...] * (_rms(y) * pn_ref[...])
    if len(rest) == 1:
        rest[0][...] = x2
    else:
        nn_ref, sc_ref, sh_ref, x2_ref, hn_ref = rest
        x2_ref[...] = x2
        hn_ref[...] = _norm_mod(x2, nn_ref[...], sc_ref[...], sh_ref[...]).astype(BF16)


def _ffn_down(g, w_down, layer, x1, post_norm, gate2, next_pre_norm=None):
    s, d = x1.shape
    d_ff = g.shape[1]
    tm = min(256, s)
    row = lambda i: (i, 0)
    vec = pl.BlockSpec((1, d), lambda i: (0, 0))
    tile_f32 = jax.ShapeDtypeStruct((s, d), F32)
    tile = pl.BlockSpec((tm, d), row)
    last = next_pre_norm is None
    return pl.pallas_call(
        _ffn_down_kernel,
        out_shape=tile_f32 if last else (tile_f32, jax.ShapeDtypeStruct((s, d), BF16)),
        grid=(s // tm,),
        in_specs=[pl.BlockSpec((tm, d_ff), row),
                  pl.BlockSpec((None, d_ff, d), lambda i: (layer, 0, 0), pipeline_mode=pl.Buffered(1)),
                  tile, vec, vec] + ([] if last else [vec, vec, vec]),
        out_specs=tile if last else (tile, tile),
        compiler_params=_params(("parallel",)),
        name="ffn_down",
    )(g, w_down, x1, post_norm, gate2, *(() if last else next_pre_norm))


def _layout_w_in(w_in):
    depth, d, _ = w_in.shape
    sizes = (GDN_QK, GDN_QK, GDN_V, GDN_V, GDN_HEADS, GDN_HEADS, MLA_Q_RANK, MLA_KV_RANK, MLA_ROPE,
             SWA_OUT, SWA_KV, SWA_KV)
    offs = np.concatenate([[0], np.cumsum(sizes)])
    part = lambda n: w_in[:, :, offs[n]:offs[n + 1]].astype(BF16)
    zeros = lambda n: jnp.zeros((depth, d, n), BF16)
    w_p = jnp.concatenate([w_in[:, :, :offs[4]].astype(BF16),
                           part(6), zeros(MLA_Q_RANK_PAD - MLA_Q_RANK),
                           part(9), part(7), part(8), zeros(LANES - MLA_ROPE), part(10), part(11)], axis=2)
    w_ab = jnp.concatenate([part(4), part(5), zeros(LANES - 2 * GDN_HEADS)], axis=2)
    return w_p, w_ab


def _layout_w_out(w_out):
    depth, d_mix, d = w_out.shape
    a = d_mix - SWA_OUT
    wc = w_out[:, a:].reshape(depth, SWA_KV_HEADS, SWA_GROUP, SWA_DH, d)[:, :, np.asarray(SWA_GORDER)]
    wc = wc.transpose(0, 2, 1, 3, 4).reshape(depth, SWA_OUT, d)
    return jnp.concatenate([w_out[:, :a].astype(BF16), wc.astype(BF16)], axis=1)


def _layout_mla(q_norm, w_uq, kv_norm, w_ukv):
    dqk = MLA_NOPE + MLA_ROPE
    uq = w_uq.reshape(MLA_Q_RANK, MLA_HEADS, dqk)
    uq = jnp.concatenate([uq[:, :, :MLA_NOPE].reshape(MLA_Q_RANK, -1), uq[:, :, MLA_NOPE:].reshape(MLA_Q_RANK, -1)],
                         axis=1)
    uq = jnp.pad(uq, ((0, MLA_Q_RANK_PAD - MLA_Q_RANK), (0, 0))).astype(BF16)
    qn = jnp.pad(q_norm, (0, MLA_Q_RANK_PAD - MLA_Q_RANK)).reshape(1, MLA_Q_RANK_PAD)
    ukv = w_ukv.reshape(MLA_KV_RANK, MLA_HEADS, MLA_NOPE + MLA_DV)
    uk = ukv[:, :, :MLA_NOPE].reshape(MLA_KV_RANK, -1).astype(BF16)
    uvt = ukv[:, :, MLA_NOPE:].reshape(MLA_KV_RANK, -1).T.astype(BF16)
    return qn, uq, kv_norm.reshape(1, MLA_KV_RANK), uk, uvt


def kernel(x, c, positions, ada_w, ada_b, mix_pre_norm, mix_post_norm, w_in, w_out, gdn_conv, gdn_a_log, gdn_dt_bias, gdn_norm, mla_q_norm, mla_w_uq, mla_kv_norm, mla_w_ukv, swa_sinks, ffn_pre_norm, ffn_post_norm, ffn_w_up, ffn_conv, ffn_conv_b, ffn_w_down):
    batch, s, d = x.shape
    assert batch == 1, "kernels are written for a single sequence"
    depth = ada_w.shape[0]
    xs = x.reshape(s, d)
    mod = _adaln_mod(c, ada_w, ada_b).reshape(depth, N_MOD, 1, d)
    cos, sin = _rope_tables(positions)
    vec = lambda a: a.reshape(1, d)

    w_p, w_ab = _layout_w_in(w_in)
    w_o = _layout_w_out(w_out)
    w_down = ffn_w_down.astype(BF16)

    h = None
    for l in range(depth):
        shift1, scale1, gate1, shift2, scale2, gate2 = (mod[l, n] for n in range(N_MOD))
        if l == 0:
            p, ab = _in_proj(xs, w_p, w_ab, l, pre_norm=(vec(mix_pre_norm[l]), scale1, shift1))
        else:
            p, ab = _in_proj(h, w_p, w_ab, l)

        q_a, k_a, v_a, gcb, gct = _gdn_prep(p, ab, gdn_conv[l], gdn_a_log[l], gdn_dt_bias[l])
        u, w, qd, kt, qk = _gdn_local(q_a, k_a, v_a, gcb, gct)
        o_a = _gdn_scan(u, w, qd, kt, qk, gcb, p, gdn_norm[l])

        q_b, k_b, v_b = _mla_proj(p, cos, sin, *_layout_mla(mla_q_norm[l], mla_w_uq[l], mla_kv_norm[l],
                                                            mla_w_ukv[l]))
        o_b = _mla_attn(q_b, k_b, v_b)

        o_c = _swa(p, swa_sinks[l])

        x1, h2 = _out_proj(o_a, o_b, o_c, w_o, l, xs, vec(mix_post_norm[l]), gate1,
                           vec(ffn_pre_norm[l]), scale2, shift2)
        g = _ffn_up(h2, ffn_w_up, l, ffn_conv[l], ffn_conv_b[l].reshape(1, -1))
        if l + 1 < depth:
            xs, h = _ffn_down(g, w_down, l, x1, vec(ffn_post_norm[l]), gate2,
                              (vec(mix_pre_norm[l + 1]), mod[l + 1, 1], mod[l + 1, 0]))
        else:
            xs = _ffn_down(g, w_down, l, x1, vec(ffn_post_norm[l]), gate2)
    return xs.reshape(batch, s, d)
```

```python
import functools
import math

import numpy as np
import jax
import jax.numpy as jnp
from jax import lax
from jax.experimental import pallas as pl
from jax.experimental.pallas import tpu as pltpu

F32 = jnp.float32
BF16 = jnp.bfloat16

EPS = 1e-6
N_MOD = 6
GDN_HEADS = 8
GDN_DK = 128
GDN_DV = 128
GDN_CONV = 4
GDN_CHUNK = 64
GDN_QK = GDN_HEADS * GDN_DK
GDN_V = GDN_HEADS * GDN_DV
MLA_HEADS = 4
MLA_Q_RANK = 448
MLA_Q_RANK_PAD = 512
MLA_KV_RANK = 128
MLA_NOPE = 128
MLA_ROPE = 64
MLA_DV = 128
MLA_QK_PAD = 256
MLA_BLOCK = 512
ROPE_THETA = 10000.0
SWA_HEADS = 8
SWA_KV_HEADS = 2
SWA_GROUP = SWA_HEADS // SWA_KV_HEADS
SWA_DH = 64
SWA_WINDOW = 128
SWA_OUT = SWA_HEADS * SWA_DH
SWA_KV = SWA_KV_HEADS * SWA_DH
SWA_GORDER = (0, 2, 1, 3)
FFN_CONV = 3

LANES = 128
GDN_GROUP = 256
VMEM_LIMIT_MB = 56

P_QKV = 0
P_Z = 3072
P_CQ = 4096
P_SWQ = 4608
P_CKV = 5120
P_KROPE = 5248
P_SWK = 5376
P_SWV = 5504
P_WIDTH = 5632


def _params(semantics):
    return pltpu.CompilerParams(dimension_semantics=semantics, vmem_limit_bytes=VMEM_LIMIT_MB << 20)


def _sigmoid(x):
    return 0.5 + 0.5 * jnp.tanh(0.5 * x)


def _silu(x):
    h = 0.5 * x
    return h + h * jnp.tanh(h)


def _softplus(x):
    return jnp.maximum(x, 0.0) + jnp.log(1.0 + jnp.exp(-jnp.abs(x)))


def _rms(x):
    return x * lax.rsqrt(jnp.mean(x * x, axis=-1, keepdims=True) + EPS)


def _dot(a, b):
    return jnp.dot(a, b, preferred_element_type=F32)


def _dot_nt(a, b):
    return lax.dot_general(a, b, (((1,), (1,)), ((), ())), preferred_element_type=F32)


def _dot_tn(a, b):
    return lax.dot_general(a, b, (((0,), (0,)), ((), ())), preferred_element_type=F32)


def _mod_kernel(c_ref, w_ref, b_ref, o_ref, *, d, kc):
    cact = _silu(c_ref[...])
    acc = b_ref[0]
    for k0 in range(0, d, kc):
        acc = acc + jnp.sum(w_ref[0, k0:k0 + kc, :] * cact[k0:k0 + kc], axis=0, keepdims=True)
    o_ref[0] = acc


def _adaln_mod(c, ada_w, ada_b):
    depth, d, n = ada_w.shape
    tn = 1024
    return pl.pallas_call(
        functools.partial(_mod_kernel, d=d, kc=256),
        out_shape=jax.ShapeDtypeStruct((depth, 1, n), F32),
        grid=(depth, n // tn),
        in_specs=[pl.BlockSpec((d, 1), lambda l, j: (0, 0)),
                  pl.BlockSpec((1, d, tn), lambda l, j: (l, 0, j)),
                  pl.BlockSpec((1, 1, tn), lambda l, j: (l, 0, j))],
        out_specs=pl.BlockSpec((1, 1, tn), lambda l, j: (l, 0, j)),
        compiler_params=_params(("parallel", "parallel")),
        name="adaln_mod",
    )(c.reshape(d, 1), ada_w, ada_b.reshape(depth, 1, n))


def _norm_mod(x, w, scale, shift):
    return _rms(x) * w * (1.0 + scale) + shift


def _norm_mod_kernel(x_ref, w_ref, sc_ref, sh_ref, h_ref):
    h_ref[...] = _norm_mod(x_ref[...], w_ref[...], sc_ref[...], sh_ref[...]).astype(BF16)


def _pre_norm(x, w, scale, shift):
    s, d = x.shape
    tm = min(512, s)
    row = pl.BlockSpec((1, d), lambda i: (0, 0))
    return pl.pallas_call(
        _norm_mod_kernel,
        out_shape=jax.ShapeDtypeStruct((s, d), BF16),
        grid=(s // tm,),
        in_specs=[pl.BlockSpec((tm, d), lambda i: (i, 0)), row, row, row],
        out_specs=pl.BlockSpec((tm, d), lambda i: (i, 0)),
        compiler_params=_params(("parallel",)),
        name="pre_norm",
    )(x, w, scale, shift)


def _inproj_kernel(h_ref, w_ref, wab_ref, p_ref, ab_ref):
    h = h_ref[...]
    p_ref[...] = _dot(h, w_ref[...]).astype(BF16)

    @pl.when(pl.program_id(1) == 0)
    def _():
        ab_ref[...] = _dot(h, wab_ref[...])


def _in_proj(h, w_p, w_ab, layer):
    s, d = h.shape
    n = w_p.shape[2]
    tm, tn = min(2048, s), 512
    return pl.pallas_call(
        _inproj_kernel,
        out_shape=(jax.ShapeDtypeStruct((s, n), BF16), jax.ShapeDtypeStruct((s, LANES), F32)),
        grid=(s // tm, n // tn),
        in_specs=[pl.BlockSpec((tm, d), lambda i, j: (i, 0)),
                  pl.BlockSpec((None, d, tn), lambda i, j: (layer, 0, j)),
                  pl.BlockSpec((None, d, LANES), lambda i, j: (layer, 0, 0))],
        out_specs=(pl.BlockSpec((tm, tn), lambda i, j: (i, j)),
                   pl.BlockSpec((tm, LANES), lambda i, j: (i, 0))),
        compiler_params=_params(("parallel", "arbitrary")),
        name="in_proj",
    )(h, w_p, w_ab)


def _gdn_prep_kernel(x_ref, halo_ref, cw_ref, ab_ref, alog_ref, dtb_ref, tri_ref,
                     q_ref, k_ref, v_ref, gcb_ref, gct_ref, xp_ref, *, tm):
    i = pl.program_id(0)
    outs = (q_ref, k_ref, v_ref)
    for grp in range(3):
        c0 = grp * GDN_QK
        halo = halo_ref[:, c0:c0 + GDN_QK].astype(F32)
        xp_ref[0:8, :] = jnp.where(i > 0, halo, 0.0)
        xp_ref[8:8 + tm, :] = x_ref[:, c0:c0 + GDN_QK].astype(F32)
        y = cw_ref[3:4, c0:c0 + GDN_QK] * xp_ref[8:8 + tm, :]
        for j in range(GDN_CONV - 1):
            off = 8 - (GDN_CONV - 1) + j
            y = y + cw_ref[j:j + 1, c0:c0 + GDN_QK] * xp_ref[off:off + tm, :]
        y = _silu(y)
        for h in range(GDN_HEADS):
            yh = y[:, h * GDN_DK:(h + 1) * GDN_DK]
            if grp < 2:
                yh = yh * lax.rsqrt(jnp.sum(yh * yh, axis=-1, keepdims=True) + EPS)
            if grp == 0:
                yh = yh * (GDN_DK ** -0.5)
            outs[grp][:, h * GDN_DK:(h + 1) * GDN_DK] = yh.astype(BF16)

    ab = ab_ref[...]
    g = -jnp.exp(alog_ref[...]) * _softplus(ab + dtb_ref[...])
    gc = jnp.dot(tri_ref[...], g, preferred_element_type=F32, precision=lax.Precision.HIGHEST)
    lane = lax.broadcasted_iota(jnp.int32, (1, LANES), 1)
    gcb = jnp.where(lane < GDN_HEADS, gc, _sigmoid(ab))
    gcb_ref[...] = gcb
    gct_ref[...] = gcb.T[0:2 * GDN_HEADS, :]


def _gdn_prep(p, ab, conv_w, a_log, dt_bias):
    s = p.shape[0]
    tm = min(512, s)
    c3 = 3 * GDN_QK
    alog_row = jnp.zeros((1, LANES), F32).at[0, :GDN_HEADS].set(a_log)
    dtb_row = jnp.zeros((1, LANES), F32).at[0, :GDN_HEADS].set(dt_bias)
    r = np.arange(tm)
    tri = jnp.asarray(((r[:, None] >= r[None, :]) & (r[:, None] // GDN_CHUNK == r[None, :] // GDN_CHUNK))
                      .astype(np.float32))
    hb = tm // 8
    row = lambda i: (i, 0)
    return pl.pallas_call(
        functools.partial(_gdn_prep_kernel, tm=tm),
        out_shape=(jax.ShapeDtypeStruct((s, GDN_QK), BF16), jax.ShapeDtypeStruct((s, GDN_QK), BF16),
                   jax.ShapeDtypeStruct((s, GDN_V), BF16), jax.ShapeDtypeStruct((s, LANES), F32),
                   jax.ShapeDtypeStruct((2 * GDN_HEADS, s), F32)),
        grid=(s // tm,),
        in_specs=[pl.BlockSpec((tm, c3), row),
                  pl.BlockSpec((8, c3), lambda i: (jnp.maximum(i * hb - 1, 0), 0)),
                  pl.BlockSpec((GDN_CONV, c3), lambda i: (0, 0)),
                  pl.BlockSpec((tm, LANES), row),
                  pl.BlockSpec((1, LANES), lambda i: (0, 0)),
                  pl.BlockSpec((1, LANES), lambda i: (0, 0)),
                  pl.BlockSpec((tm, tm), lambda i: (0, 0))],
        out_specs=(pl.BlockSpec((tm, GDN_QK), row), pl.BlockSpec((tm, GDN_QK), row),
                   pl.BlockSpec((tm, GDN_V), row), pl.BlockSpec((tm, LANES), row),
                   pl.BlockSpec((2 * GDN_HEADS, tm), lambda i: (0, i))),
        scratch_shapes=[pltpu.VMEM((tm + 8, GDN_QK), F32)],
        compiler_params=_params(("parallel",)),
        name="gdn_prep",
    )(p, p, conv_w, ab, alog_row, dtb_row, tri)


def _gdn_local_kernel(q_ref, k_ref, v_ref, gcb_ref, gct_ref,
                      u_ref, w_ref, qd_ref, kt_ref, qk_ref):
    n = GDN_GROUP
    c = GDN_CHUNK
    ri = lax.broadcasted_iota(jnp.int32, (n, n), 0)
    ci = lax.broadcasted_iota(jnp.int32, (n, n), 1)
    same_chunk = (ri // c) == (ci // c)
    incl = same_chunk & (ri >= ci)
    strict = same_chunk & (ri > ci)
    eye = (ri == ci).astype(F32)
    last_sel = (ci == (ri // c) * c + (c - 1)).astype(F32)
    gcb = gcb_ref[...]
    gc_last_all = jnp.dot(last_sel, gcb, preferred_element_type=F32, precision=lax.Precision.HIGHEST)
    heads = range(GDN_HEADS)
    sls = [slice(h * GDN_DK, (h + 1) * GDN_DK) for h in heads]
    lmats, ts, rhss = [], [], []
    for h in heads:
        sl = sls[h]
        q = q_ref[:, sl].astype(F32)
        k16 = k_ref[:, sl]
        k = k16.astype(F32)
        v = v_ref[:, sl].astype(F32)
        gc_col = gcb[:, h:h + 1]
        beta = gcb[:, GDN_HEADS + h:GDN_HEADS + h + 1]
        gc_row = gct_ref[h:h + 1, :]
        gc_last = gc_last_all[:, h:h + 1]
        diff = gc_col - gc_row
        decay = jnp.where(incl, jnp.exp(jnp.where(incl, diff, 0.0)), 0.0)
        kb = k * beta
        kk = _dot_nt(kb.astype(BF16), k16)
        lmat = jnp.where(strict, kk * decay, 0.0)
        lmats.append(lmat)
        ts.append(eye - jnp.where((ri // 2) == (ci // 2), lmat, 0.0))
        eg = jnp.exp(gc_col)
        rhss.append(jnp.concatenate([v * beta, kb * eg], axis=-1).astype(BF16))
        qd_ref[:, sl] = (q * eg).astype(BF16)
        kt_ref[:, sl] = (k * jnp.exp(gc_last - gc_col)).astype(BF16)
        qk = _dot_nt(q_ref[:, sl], k16) * decay
        qk_c = jnp.concatenate([qk[j * c:(j + 1) * c, j * c:(j + 1) * c] for j in range(n // c)], axis=0)
        qk_ref[:, h * c:(h + 1) * c] = qk_c.astype(BF16)
    b = 2
    while b < c:
        off_diag = ((ri // (2 * b)) == (ci // (2 * b))) & ((ri // b) != (ci // b))
        t16s = [ts[h].astype(BF16) for h in heads]
        ys = [_dot(t16s[h], jnp.where(off_diag, lmats[h], 0.0).astype(BF16)).astype(BF16) for h in heads]
        ts = [ts[h] - _dot(ys[h], t16s[h]) for h in heads]
        b *= 2
    for h in heads:
        uw = _dot(ts[h].astype(BF16), rhss[h])
        u_ref[:, sls[h]] = uw[:, :GDN_DV].astype(BF16)
        w_ref[:, sls[h]] = uw[:, GDN_DV:].astype(BF16)


def _gdn_local(q, k, v, gcb, gct):
    s = q.shape[0]
    n = GDN_GROUP
    row = lambda i: (i, 0)
    wide = pl.BlockSpec((n, GDN_QK), row)
    return pl.pallas_call(
        _gdn_local_kernel,
        out_shape=(jax.ShapeDtypeStruct((s, GDN_V), BF16), jax.ShapeDtypeStruct((s, GDN_QK), BF16),
                   jax.ShapeDtypeStruct((s, GDN_QK), BF16), jax.ShapeDtypeStruct((s, GDN_QK), BF16),
                   jax.ShapeDtypeStruct((s, GDN_HEADS * GDN_CHUNK), BF16)),
        grid=(s // n,),
        in_specs=[wide, wide, wide, pl.BlockSpec((n, LANES), row),
                  pl.BlockSpec((2 * GDN_HEADS, n), lambda i: (0, i))],
        out_specs=(wide, wide, wide, wide, pl.BlockSpec((n, GDN_HEADS * GDN_CHUNK), row)),
        compiler_params=_params(("parallel",)),
        name="gdn_local",
    )(q, k, v, gcb, gct)


def _gdn_scan_kernel(u_ref, w_ref, qd_ref, kt_ref, qk_ref, gcb_ref, z_ref, nw_ref, o_ref, st_ref, *, rb):
    c = GDN_CHUNK

    @pl.when(pl.program_id(0) == 0)
    def _():
        st_ref[...] = jnp.zeros_like(st_ref)

    nw = nw_ref[...]

    def chunk(ci, carry):
        r0 = pl.multiple_of(ci * c, c)
        rows = pl.ds(r0, c)
        gt_row = jnp.exp(gcb_ref[pl.ds(r0 + c - 1, 1), :])
        heads = range(GDN_HEADS)
        sls = [slice(h * GDN_DK, (h + 1) * GDN_DK) for h in heads]
        sts = [st_ref[h] for h in heads]
        r1s = [_dot(jnp.concatenate([w_ref[rows, sls[h]], qd_ref[rows, sls[h]]], axis=0),
                    sts[h].astype(BF16)) for h in heads]
        vns = [(u_ref[rows, sls[h]].astype(F32) - r1s[h][0:c]).astype(BF16) for h in heads]
        for h in heads:
            st_ref[h] = sts[h] * gt_row[:, h:h + 1] + _dot_tn(kt_ref[rows, sls[h]], vns[h])
        os_ = [r1s[h][c:2 * c] + _dot(qk_ref[rows, h * c:(h + 1) * c], vns[h]) for h in heads]
        for h in heads:
            o = _rms(os_[h]) * nw * _silu(z_ref[rows, sls[h]].astype(F32))
            o_ref[rows, sls[h]] = o.astype(BF16)
        return carry

    lax.fori_loop(0, rb // c, chunk, 0)


def _gdn_scan(u, w, qd, kt, qk, gcb, p, norm_w):
    s = u.shape[0]
    rb = min(512, s)
    row = lambda i: (i, 0)
    wide = pl.BlockSpec((rb, GDN_QK), row)
    return pl.pallas_call(
        functools.partial(_gdn_scan_kernel, rb=rb),
        out_shape=jax.ShapeDtypeStruct((s, GDN_V), BF16),
        grid=(s // rb,),
        in_specs=[wide, wide, wide, wide, pl.BlockSpec((rb, GDN_HEADS * GDN_CHUNK), row),
                  pl.BlockSpec((rb, LANES), row),
                  pl.BlockSpec((rb, GDN_V), lambda i: (i, P_Z // GDN_V)),
                  pl.BlockSpec((1, GDN_DV), lambda i: (0, 0))],
        out_specs=wide,
        scratch_shapes=[pltpu.VMEM((GDN_HEADS, GDN_DK, GDN_DV), F32)],
        compiler_params=_params(("arbitrary",)),
        name="gdn_scan",
    )(u, w, qd, kt, qk, gcb, p, norm_w.reshape(1, GDN_DV))


def _rope_table_kernel(pos_ref, inv_ref, cos_ref, sin_ref):
    ang = pos_ref[...].astype(F32) * inv_ref[...]
    lane = lax.broadcasted_iota(jnp.int32, ang.shape, 1)
    first_half = (lane % MLA_ROPE) < (MLA_ROPE // 2)
    cos_ref[...] = jnp.cos(ang)
    sin_ref[...] = jnp.where(first_half, -jnp.sin(ang), jnp.sin(ang))


def _rope_tables(positions):
    s = positions.shape[-1]
    tm = min(1024, s)
    half = MLA_ROPE // 2
    inv = (ROPE_THETA ** (-np.arange(half, dtype=np.float32) / half)).astype(np.float32)
    inv_row = jnp.asarray(np.tile(inv, LANES // half).reshape(1, LANES))
    return pl.pallas_call(
        _rope_table_kernel,
        out_shape=(jax.ShapeDtypeStruct((s, LANES), F32), jax.ShapeDtypeStruct((s, LANES), F32)),
        grid=(s // tm,),
        in_specs=[pl.BlockSpec((tm, 1), lambda i: (i, 0)), pl.BlockSpec((1, LANES), lambda i: (0, 0))],
        out_specs=(pl.BlockSpec((tm, LANES), lambda i: (i, 0)), pl.BlockSpec((tm, LANES), lambda i: (i, 0))),
        compiler_params=_params(("parallel",)),
        name="rope_tables",
    )(positions.reshape(s, 1), inv_row)


def _rope_apply(x, cos, sin_signed):
    width = x.shape[-1]
    half = MLA_ROPE // 2
    lane = lax.broadcasted_iota(jnp.int32, x.shape, 1)
    first_half = (lane % MLA_ROPE) < half
    swapped = jnp.where(first_half, pltpu.roll(x, width - half, 1), pltpu.roll(x, half, 1))
    return x * cos + swapped * sin_signed


def _mla_proj_kernel(cq_ref, ckv_ref, kr_ref, cos_ref, sin_ref, qn_ref, wuq_ref, kvn_ref, wuk_ref, wuvt_ref,
                     q_ref, k_ref, vt_ref):
    cos = cos_ref[...]
    sin = sin_ref[...]
    scale = (MLA_NOPE + MLA_ROPE) ** -0.5 * math.log2(math.e)
    cq = cq_ref[...].astype(F32)
    cqn = cq * lax.rsqrt(jnp.sum(cq * cq, axis=-1, keepdims=True) * (1.0 / MLA_Q_RANK) + EPS) * qn_ref[...]
    q = _dot(cqn.astype(BF16), wuq_ref[...])
    nope_w = MLA_HEADS * MLA_NOPE
    q_rope = _rope_apply(q[:, nope_w:], jnp.concatenate([cos, cos], -1), jnp.concatenate([sin, sin], -1))
    zeros = jnp.zeros((q.shape[0], MLA_QK_PAD - MLA_NOPE - MLA_ROPE), F32)
    ckvn = (_rms(ckv_ref[...].astype(F32)) * kvn_ref[...]).astype(BF16)
    k_nope = _dot(ckvn, wuk_ref[...])
    v_t = _dot_nt(wuvt_ref[...], ckvn)
    k_pe = _rope_apply(kr_ref[...].astype(F32), cos, sin)[:, :MLA_ROPE]
    for h in range(MLA_HEADS):
        qh = jnp.concatenate([q[:, h * MLA_NOPE:(h + 1) * MLA_NOPE],
                              q_rope[:, h * MLA_ROPE:(h + 1) * MLA_ROPE], zeros], axis=-1) * scale
        q_ref[h] = qh.astype(BF16)
        kh = jnp.concatenate([k_nope[:, h * MLA_NOPE:(h + 1) * MLA_NOPE], k_pe, zeros], axis=-1)
        k_ref[h] = kh.astype(BF16)
        vt_ref[h, 0] = v_t[h * MLA_DV:(h + 1) * MLA_DV, :].astype(BF16)


def _mla_proj(p, cos, sin, q_norm, w_uq, kv_norm, w_uk, w_uvt):
    s = p.shape[0]
    tm = min(MLA_BLOCK, s)
    const = lambda i: (0, 0)
    return pl.pallas_call(
        _mla_proj_kernel,
        out_shape=(jax.ShapeDtypeStruct((MLA_HEADS, s, MLA_QK_PAD), BF16),
                   jax.ShapeDtypeStruct((MLA_HEADS, s, MLA_QK_PAD), BF16),
                   jax.ShapeDtypeStruct((MLA_HEADS, s // tm, MLA_DV, tm), BF16)),
        grid=(s // tm,),
        in_specs=[pl.BlockSpec((tm, MLA_Q_RANK_PAD), lambda i: (i, P_CQ // MLA_Q_RANK_PAD)),
                  pl.BlockSpec((tm, LANES), lambda i: (i, P_CKV // LANES)),
                  pl.BlockSpec((tm, LANES), lambda i: (i, P_KROPE // LANES)),
                  pl.BlockSpec((tm, LANES), lambda i: (i, 0)),
                  pl.BlockSpec((tm, LANES), lambda i: (i, 0)),
                  pl.BlockSpec(q_norm.shape, const), pl.BlockSpec(w_uq.shape, const),
                  pl.BlockSpec(kv_norm.shape, const), pl.BlockSpec(w_uk.shape, const),
                  pl.BlockSpec(w_uvt.shape, const)],
        out_specs=(pl.BlockSpec((MLA_HEADS, tm, MLA_QK_PAD), lambda i: (0, i, 0)),
                   pl.BlockSpec((MLA_HEADS, tm, MLA_QK_PAD), lambda i: (0, i, 0)),
                   pl.BlockSpec((MLA_HEADS, 1, MLA_DV, tm), lambda i: (0, i, 0, 0))),
        compiler_params=_params(("parallel",)),
        name="mla_proj",
    )(p, p, p, cos, sin, q_norm, w_uq, kv_norm, w_uk, w_uvt)


def _mla_attn_kernel(q_ref, k_ref, vt_ref, o_ref, st0, st1, mb0, mb1, m_ref, l_ref, acc_ref, *, bk, nd):
    qi = pl.program_id(1)
    slots = ((st0, mb0), (st1, mb1))
    bq = nd * bk
    n0 = nd * qi
    tri = lax.broadcasted_iota(jnp.int32, (bk, bk), 0) <= lax.broadcasted_iota(jnp.int32, (bk, bk), 1)

    gq = bk // 2

    def scores(t, slot, c0, tri_mask):
        st_ref, mb_ref = slots[slot]
        r0 = pl.multiple_of(t * bk, bk)
        st = _dot_nt(k_ref[0, pl.ds(r0, bk), :], q_ref[0, c0:c0 + gq, :])
        if tri_mask is not None:
            st = jnp.where(tri_mask, st, -jnp.inf)
        st_ref[:, c0:c0 + gq] = st
        mb_ref[:, c0:c0 + gq] = jnp.max(st, axis=0, keepdims=True)

    def update(t, slot, c0):
        st_ref, mb_ref = slots[slot]
        cols = slice(c0, c0 + gq)
        m = m_ref[:, cols]
        m_new = jnp.maximum(m, mb_ref[:, cols])
        alpha = jnp.exp2(m - m_new)
        pexp = jnp.exp2(st_ref[:, cols] - m_new)
        l_ref[:, cols] = alpha * l_ref[:, cols] + jnp.sum(pexp, axis=0, keepdims=True)
        acc_ref[:, cols] = alpha * acc_ref[:, cols] + _dot(vt_ref[0, t], pexp.astype(BF16))
        m_ref[:, cols] = m_new

    def sweep(ts=None, ss=None, ds=None, tu=None, su=None, du=None):
        for c0 in range(0, bq, gq):
            g = c0 // bk
            if ts is not None and (ds is None or g >= ds):
                scores(ts, ss, c0, tri[:, c0 - g * bk:c0 - g * bk + gq] if g == ds else None)
            if tu is not None and (du is None or g >= du):
                update(tu, su, c0)

    m_ref[...] = jnp.full(m_ref.shape, -jnp.inf, F32)
    l_ref[...] = jnp.zeros(l_ref.shape, F32)
    acc_ref[...] = jnp.zeros(acc_ref.shape, F32)

    @pl.when(qi == 0)
    def _():
        sweep(ts=0, ss=0, ds=0)

    @pl.when(qi > 0)
    def _():
        sweep(ts=0, ss=0)

    def pair(j, carry):
        sweep(ts=2 * j + 1, ss=1, tu=2 * j, su=0)
        sweep(ts=2 * j + 2, ss=0, tu=2 * j + 1, su=1)
        return carry

    lax.fori_loop(0, n0 // 2 - 1, pair, 0)

    @pl.when(qi > 0)
    def _():
        sweep(ts=n0 - 1, ss=1, tu=n0 - 2, su=0)
        sweep(ts=n0, ss=0, ds=0, tu=n0 - 1, su=1)

    for d in range(1, nd):
        sweep(ts=n0 + d, ss=d % 2, ds=d, tu=n0 + d - 1, su=(d - 1) % 2, du=d - 1)
    sweep(tu=n0 + nd - 1, su=(nd - 1) % 2, du=nd - 1)
    o_ref[...] = (acc_ref[...] / l_ref[...]).T.astype(BF16)


def _mla_attn(q, k, vt):
    nh, s, _ = q.shape
    bk = vt.shape[-1]
    nd = 4 if s % (4 * bk) == 0 else 2
    bq = nd * bk
    assert s % bq == 0
    return pl.pallas_call(
        functools.partial(_mla_attn_kernel, bk=bk, nd=nd),
        out_shape=jax.ShapeDtypeStruct((s, nh * MLA_DV), BF16),
        grid=(nh, s // bq),
        in_specs=[pl.BlockSpec((1, bq, MLA_QK_PAD), lambda h, i: (h, i, 0)),
                  pl.BlockSpec((1, s, MLA_QK_PAD), lambda h, i: (h, 0, 0)),
                  pl.BlockSpec((1, s // bk, MLA_DV, bk), lambda h, i: (h, 0, 0, 0))],
        out_specs=pl.BlockSpec((bq, MLA_DV), lambda h, i: (i, h)),
        scratch_shapes=[pltpu.VMEM((bk, bq), F32), pltpu.VMEM((bk, bq), F32),
                        pltpu.VMEM((1, bq), F32), pltpu.VMEM((1, bq), F32),
                        pltpu.VMEM((1, bq), F32), pltpu.VMEM((1, bq), F32), pltpu.VMEM((MLA_DV, bq), F32)],
        compiler_params=_params(("parallel", "arbitrary")),
        name="mla_attn",
    )(q, k, vt)


def _swa_kernel(q_ref, k_ref, v_ref, kh_ref, vh_ref, sink_ref, o_ref, *, nblk):
    w = SWA_WINDOW
    g = SWA_GROUP
    i = pl.program_id(0)
    key = lax.broadcasted_iota(jnp.int32, (2 * w, g * w), 0)
    col = lax.broadcasted_iota(jnp.int32, (2 * w, g * w), 1)
    dist = col % w + w - key
    band = (dist >= 0) & (dist < w)
    first_band = band & ((key >= w) | (i > 0))
    dist_f = dist.astype(F32)
    biases = []
    for hk in range(SWA_KV_HEADS):
        slope = jnp.zeros((2 * w, g * w), F32)
        for gi, gq in enumerate(SWA_GORDER):
            slope = jnp.where(col // w == gi, 2.0 ** (-8.0 * (hk * g + gq + 1.0) / SWA_HEADS), slope)
        biases.append(-slope * dist_f)
    low = lax.broadcasted_iota(jnp.int32, (2 * w, 2 * SWA_DH), 1) < SWA_DH
    top = lax.broadcasted_iota(jnp.int32, (2 * SWA_DH, 1), 0) < SWA_DH
    for blk in range(nblk):
        rows = slice(blk * w, (blk + 1) * w)
        if blk == 0:
            k_prev, v_prev, valid = kh_ref[...], vh_ref[...], first_band
        else:
            k_prev, v_prev, valid = k_ref[(blk - 1) * w:blk * w, :], v_ref[(blk - 1) * w:blk * w, :], band
        kk = jnp.concatenate([k_prev, k_ref[rows, :]], axis=0).astype(F32)
        kk_sw = pltpu.roll(kk, SWA_DH, 1)
        vvt = jnp.concatenate([v_prev, v_ref[rows, :]], axis=0).astype(F32).T
        probs, inv_denoms = [], []
        for hk in range(SWA_KV_HEADS):
            c0 = hk * g * SWA_DH
            q2 = jnp.concatenate([q_ref[rows, c0:c0 + 2 * SWA_DH], q_ref[rows, c0 + 2 * SWA_DH:c0 + 4 * SWA_DH]],
                                 axis=0) * (SWA_DH ** -0.5)
            k_low = jnp.where(low, kk if hk == 0 else kk_sw, 0.0).astype(BF16)
            k_high = jnp.where(low, 0.0, kk_sw if hk == 0 else kk).astype(BF16)
            st = jnp.concatenate([_dot_nt(k_low, q2), _dot_nt(k_high, q2)], axis=1)
            st = jnp.where(valid, st + biases[hk], -jnp.inf)
            sink = sink_ref[hk]
            m = jnp.maximum(jnp.max(st, axis=0, keepdims=True), sink)
            pexp = jnp.exp(st - m)
            denom = jnp.sum(pexp, axis=0, keepdims=True) + jnp.exp(sink - m)
            probs.append(pexp.astype(BF16))
            inv_denoms.append(1.0 / denom)
        v0 = jnp.where(top, vvt, 0.0).astype(BF16)
        v1 = jnp.where(top, 0.0, vvt).astype(BF16)
        o_t = (_dot(v0, probs[0]) + _dot(v1, probs[1])) * jnp.where(top, inv_denoms[0], inv_denoms[1])
        for gi in range(g):
            o_ref[rows, gi * LANES:(gi + 1) * LANES] = o_t[:, gi * w:(gi + 1) * w].T.astype(BF16)


def _swa(p, sinks):
    s = p.shape[0]
    w = SWA_WINDOW
    rb = min(512, s)
    nblk = rb // w
    order = np.asarray([[hk * SWA_GROUP + gq for gq in SWA_GORDER] for hk in range(SWA_KV_HEADS)])
    sink_rows = jnp.repeat(sinks[order], w, axis=1).reshape(SWA_KV_HEADS, 1, SWA_GROUP * w)
    prev = lambda col: (lambda i: (jnp.maximum(i * nblk - 1, 0), col))
    return pl.pallas_call(
        functools.partial(_swa_kernel, nblk=nblk),
        out_shape=jax.ShapeDtypeStruct((s, SWA_OUT), BF16),
        grid=(s // rb,),
        in_specs=[pl.BlockSpec((rb, SWA_OUT), lambda i: (i, P_SWQ // SWA_OUT)),
                  pl.BlockSpec((rb, SWA_KV), lambda i: (i, P_SWK // SWA_KV)),
                  pl.BlockSpec((rb, SWA_KV), lambda i: (i, P_SWV // SWA_KV)),
                  pl.BlockSpec((w, SWA_KV), prev(P_SWK // SWA_KV)),
                  pl.BlockSpec((w, SWA_KV), prev(P_SWV // SWA_KV)),
                  pl.BlockSpec(sink_rows.shape, lambda i: (0, 0, 0))],
        out_specs=pl.BlockSpec((rb, SWA_OUT), lambda i: (i, 0)),
        compiler_params=_params(("parallel",)),
        name="swa",
    )(p, p, p, p, p, sink_rows)


def _outproj_kernel(oa_ref, ob_ref, oc_ref, w_ref, x_ref, pn_ref, g1_ref, fn_ref, sc_ref, sh_ref,
                    x1_ref, h2_ref, *, sub):
    a_w, b_w = oa_ref.shape[1], ob_ref.shape[1]
    for r0 in range(0, x_ref.shape[0], sub):
        rows = slice(r0, r0 + sub)
        mix = (_dot(oa_ref[rows, :], w_ref[0:a_w, :]) + _dot(ob_ref[rows, :], w_ref[a_w:a_w + b_w, :])
               + _dot(oc_ref[rows, :], w_ref[a_w + b_w:, :]))
        x1 = x_ref[rows, :] + g1_ref[...] * (_rms(mix) * pn_ref[...])
        x1_ref[rows, :] = x1
        h2_ref[rows, :] = _norm_mod(x1, fn_ref[...], sc_ref[...], sh_ref[...]).astype(BF16)


def _out_proj(o_a, o_b, o_c, w_out, layer, x, post_norm, gate1, ffn_norm, scale2, shift2):
    s, d = x.shape
    tm = min(512, s)
    row = lambda i: (i, 0)
    vec = pl.BlockSpec((1, d), lambda i: (0, 0))
    return pl.pallas_call(
        functools.partial(_outproj_kernel, sub=min(256, tm)),
        out_shape=(jax.ShapeDtypeStruct((s, d), F32), jax.ShapeDtypeStruct((s, d), BF16)),
        grid=(s // tm,),
        in_specs=[pl.BlockSpec((tm, o_a.shape[1]), row), pl.BlockSpec((tm, o_b.shape[1]), row),
                  pl.BlockSpec((tm, o_c.shape[1]), row),
                  pl.BlockSpec((None,) + w_out.shape[1:], lambda i: (layer, 0, 0), pipeline_mode=pl.Buffered(1)),
                  pl.BlockSpec((tm, d), row), vec, vec, vec, vec, vec],
        out_specs=(pl.BlockSpec((tm, d), row), pl.BlockSpec((tm, d), row)),
        compiler_params=_params(("parallel",)),
        name="out_proj",
    )(o_a, o_b, o_c, w_out, x, post_norm, gate1, ffn_norm, scale2, shift2)


def _gelu_tanh(x):
    return 0.5 * x * (1.0 + jnp.tanh(math.sqrt(2.0 / math.pi) * (x + 0.044715 * (x * x * x))))


def _ffn_up_kernel(h_ref, wg_ref, wu_ref, cg_ref, cu_ref, bg_ref, bu_ref, o_ref,
                   wg16_ref, wu16_ref, xg_ref, xu_ref, *, tm):
    @pl.when(pl.program_id(1) == 0)
    def _():
        wg16_ref[...] = wg_ref[...].astype(BF16)
        wu16_ref[...] = wu_ref[...].astype(BF16)
        xg_ref[0:8, :] = jnp.zeros((8, xg_ref.shape[1]), F32)
        xu_ref[0:8, :] = jnp.zeros((8, xu_ref.shape[1]), F32)

    h = h_ref[...]

    def conv(x_ref, w_ref, cw_ref, b_ref):
        x_ref[8:8 + tm, :] = _dot(h, w_ref[...])
        y = b_ref[...] + cw_ref[2:3, :] * x_ref[8:8 + tm, :]
        y = y + cw_ref[1:2, :] * x_ref[7:7 + tm, :]
        y = y + cw_ref[0:1, :] * x_ref[6:6 + tm, :]
        x_ref[0:8, :] = x_ref[tm:tm + 8, :]
        return y

    gate = conv(xg_ref, wg16_ref, cg_ref, bg_ref)
    up = conv(xu_ref, wu16_ref, cu_ref, bu_ref)
    o_ref[...] = (_gelu_tanh(gate) * up).astype(BF16)


def _ffn_up(h2, w_up, layer, conv_w, conv_b):
    s, d = h2.shape
    d_ff = w_up.shape[2] // 2
    tm, tn = min(1024, s), 512
    nj = d_ff // tn
    lo = lambda j, i: (0, j)
    hi = lambda j, i: (0, j + nj)
    w_lo = pl.BlockSpec((None, d, tn), lambda j, i: (layer, 0, j))
    w_hi = pl.BlockSpec((None, d, tn), lambda j, i: (layer, 0, j + nj))
    return pl.pallas_call(
        functools.partial(_ffn_up_kernel, tm=tm),
        out_shape=jax.ShapeDtypeStruct((s, d_ff), BF16),
        grid=(nj, s // tm),
        in_specs=[pl.BlockSpec((tm, d), lambda j, i: (i, 0)),
                  w_lo, w_hi,
                  pl.BlockSpec((FFN_CONV, tn), lo), pl.BlockSpec((FFN_CONV, tn), hi),
                  pl.BlockSpec((1, tn), lo), pl.BlockSpec((1, tn), hi)],
        out_specs=pl.BlockSpec((tm, tn), lambda j, i: (i, j)),
        scratch_shapes=[pltpu.VMEM((d, tn), BF16), pltpu.VMEM((d, tn), BF16),
                        pltpu.VMEM((tm + 8, tn), F32), pltpu.VMEM((tm + 8, tn), F32)],
        compiler_params=_params(("parallel", "arbitrary")),
        name="ffn_up",
    )(h2, w_up, w_up, conv_w, conv_w, conv_b, conv_b)


def _ffn_down_kernel(g_ref, w_ref, x_ref, pn_ref, g2_ref, *rest):
    y = _dot(g_ref[...], w_ref[...])
    x2 = x_ref[...] + g2_ref[...] * (_rms(y) * pn_ref[...])
    if len(rest) == 1:
        rest[0][...] = x2
    else:
        nn_ref, sc_ref, sh_ref, x2_ref, hn_ref = rest
        x2_ref[...] = x2
        hn_ref[...] = _norm_mod(x2, nn_ref[...], sc_ref[...], sh_ref[...]).astype(BF16)


def _ffn_down(g, w_down, layer, x1, post_norm, gate2, next_pre_norm=None):
    s, d = x1.shape
    d_ff = g.shape[1]
    tm = min(256, s)
    row = lambda i: (i, 0)
    vec = pl.BlockSpec((1, d), lambda i: (0, 0))
    tile_f32 = jax.ShapeDtypeStruct((s, d), F32)
    tile = pl.BlockSpec((tm, d), row)
    last = next_pre_norm is None
    return pl.pallas_call(
        _ffn_down_kernel,
        out_shape=tile_f32 if last else (tile_f32, jax.ShapeDtypeStruct((s, d), BF16)),
        grid=(s // tm,),
        in_specs=[pl.BlockSpec((tm, d_ff), row),
                  pl.BlockSpec((None, d_ff, d), lambda i: (layer, 0, 0), pipeline_mode=pl.Buffered(1)),
                  tile, vec, vec] + ([] if last else [vec, vec, vec]),
        out_specs=tile if last else (tile, tile),
        compiler_params=_params(("parallel",)),
        name="ffn_down",
    )(g, w_down, x1, post_norm, gate2, *(() if last else next_pre_norm))


def _layout_w_in(w_in):
    depth, d, _ = w_in.shape
    sizes = (GDN_QK, GDN_QK, GDN_V, GDN_V, GDN_HEADS, GDN_HEADS, MLA_Q_RANK, MLA_KV_RANK, MLA_ROPE,
             SWA_OUT, SWA_KV, SWA_KV)
    offs = np.concatenate([[0], np.cumsum(sizes)])
    part = lambda n: w_in[:, :, offs[n]:offs[n + 1]].astype(BF16)
    zeros = lambda n: jnp.zeros((depth, d, n), BF16)
    w_p = jnp.concatenate([w_in[:, :, :offs[4]].astype(BF16),
                           part(6), zeros(MLA_Q_RANK_PAD - MLA_Q_RANK),
                           part(9), part(7), part(8), zeros(LANES - MLA_ROPE), part(10), part(11)], axis=2)
    w_ab = jnp.concatenate([part(4), part(5), zeros(LANES - 2 * GDN_HEADS)], axis=2)
    return w_p, w_ab


def _layout_w_out(w_out):
    depth, d_mix, d = w_out.shape
    a = d_mix - SWA_OUT
    wc = w_out[:, a:].reshape(depth, SWA_KV_HEADS, SWA_GROUP, SWA_DH, d)[:, :, np.asarray(SWA_GORDER)]
    wc = wc.transpose(0, 2, 1, 3, 4).reshape(depth, SWA_OUT, d)
    return jnp.concatenate([w_out[:, :a].astype(BF16), wc.astype(BF16)], axis=1)


def _layout_mla(q_norm, w_uq, kv_norm, w_ukv):
    dqk = MLA_NOPE + MLA_ROPE
    uq = w_uq.reshape(MLA_Q_RANK, MLA_HEADS, dqk)
    uq = jnp.concatenate([uq[:, :, :MLA_NOPE].reshape(MLA_Q_RANK, -1), uq[:, :, MLA_NOPE:].reshape(MLA_Q_RANK, -1)],
                         axis=1)
    uq = jnp.pad(uq, ((0, MLA_Q_RANK_PAD - MLA_Q_RANK), (0, 0))).astype(BF16)
    qn = jnp.pad(q_norm, (0, MLA_Q_RANK_PAD - MLA_Q_RANK)).reshape(1, MLA_Q_RANK_PAD)
    ukv = w_ukv.reshape(MLA_KV_RANK, MLA_HEADS, MLA_NOPE + MLA_DV)
    uk = ukv[:, :, :MLA_NOPE].reshape(MLA_KV_RANK, -1).astype(BF16)
    uvt = ukv[:, :, MLA_NOPE:].reshape(MLA_KV_RANK, -1).T.astype(BF16)
    return qn, uq, kv_norm.reshape(1, MLA_KV_RANK), uk, uvt


def kernel(x, c, positions, ada_w, ada_b, mix_pre_norm, mix_post_norm, w_in, w_out, gdn_conv, gdn_a_log, gdn_dt_bias, gdn_norm, mla_q_norm, mla_w_uq, mla_kv_norm, mla_w_ukv, swa_sinks, ffn_pre_norm, ffn_post_norm, ffn_w_up, ffn_conv, ffn_conv_b, ffn_w_down):
    batch, s, d = x.shape
    assert batch == 1, "kernels are written for a single sequence"
    depth = ada_w.shape[0]
    xs = x.reshape(s, d)
    mod = _adaln_mod(c, ada_w, ada_b).reshape(depth, N_MOD, 1, d)
    cos, sin = _rope_tables(positions)
    vec = lambda a: a.reshape(1, d)

    w_p, w_ab = _layout_w_in(w_in)
    w_o = _layout_w_out(w_out)
    w_down = ffn_w_down.astype(BF16)

    h = _pre_norm(xs, vec(mix_pre_norm[0]), mod[0, 1], mod[0, 0])
    for l in range(depth):
        shift1, scale1, gate1, shift2, scale2, gate2 = (mod[l, n] for n in range(N_MOD))
        p, ab = _in_proj(h, w_p, w_ab, l)

        q_a, k_a, v_a, gcb, gct = _gdn_prep(p, ab, gdn_conv[l], gdn_a_log[l], gdn_dt_bias[l])
        u, w, qd, kt, qk = _gdn_local(q_a, k_a, v_a, gcb, gct)
        o_a = _gdn_scan(u, w, qd, kt, qk, gcb, p, gdn_norm[l])

        q_b, k_b, v_b = _mla_proj(p, cos, sin, *_layout_mla(mla_q_norm[l], mla_w_uq[l], mla_kv_norm[l],
                                                            mla_w_ukv[l]))
        o_b = _mla_attn(q_b, k_b, v_b)

        o_c = _swa(p, swa_sinks[l])

        x1, h2 = _out_proj(o_a, o_b, o_c, w_o, l, xs, vec(mix_post_norm[l]), gate1,
                           vec(ffn_pre_norm[l]), scale2, shift2)
        g = _ffn_up(h2, ffn_w_up, l, ffn_conv[l], ffn_conv_b[l].reshape(1, -1))
        if l + 1 < depth:
            xs, h = _ffn_down(g, w_down, l, x1, vec(ffn_post_norm[l]), gate2,
                              (vec(mix_pre_norm[l + 1]), mod[l + 1, 1], mod[l + 1, 0]))
        else:
            xs = _ffn_down(g, w_down, l, x1, vec(ffn_post_norm[l]), gate2)
    return xs.reshape(batch, s, d)
```

```python
import functools
import math

import numpy as np
import jax
import jax.numpy as jnp
from jax import lax
from jax.experimental import pallas as pl
from jax.experimental.pallas import tpu as pltpu

F32 = jnp.float32
BF16 = jnp.bfloat16

EPS = 1e-6
N_MOD = 6
GDN_HEADS = 8
GDN_DK = 128
GDN_DV = 128
GDN_CONV = 4
GDN_CHUNK = 64
GDN_QK = GDN_HEADS * GDN_DK
GDN_V = GDN_HEADS * GDN_DV
MLA_HEADS = 4
MLA_Q_RANK = 448
MLA_Q_RANK_PAD = 512
MLA_KV_RANK = 128
MLA_NOPE = 128
MLA_ROPE = 64
MLA_DV = 128
MLA_QK_PAD = 256
MLA_BLOCK = 512
ROPE_THETA = 10000.0
SWA_HEADS = 8
SWA_KV_HEADS = 2
SWA_GROUP = SWA_HEADS // SWA_KV_HEADS
SWA_DH = 64
SWA_WINDOW = 128
SWA_OUT = SWA_HEADS * SWA_DH
SWA_KV = SWA_KV_HEADS * SWA_DH
SWA_GORDER = (0, 2, 1, 3)
FFN_CONV = 3

LANES = 128
GDN_GROUP = 256
VMEM_LIMIT_MB = 56

P_QKV = 0
P_Z = 3072
P_CQ = 4096
P_SWQ = 4608
P_CKV = 5120
P_KROPE = 5248
P_SWK = 5376
P_SWV = 5504
P_WIDTH = 5632


def _params(semantics):
    return pltpu.CompilerParams(dimension_semantics=semantics, vmem_limit_bytes=VMEM_LIMIT_MB << 20)


def _sigmoid(x):
    return 0.5 + 0.5 * jnp.tanh(0.5 * x)


def _silu(x):
    h = 0.5 * x
    return h + h * jnp.tanh(h)


def _softplus(x):
    return jnp.maximum(x, 0.0) + jnp.log(1.0 + jnp.exp(-jnp.abs(x)))


def _rms(x):
    return x * lax.rsqrt(jnp.mean(x * x, axis=-1, keepdims=True) + EPS)


def _dot(a, b):
    return jnp.dot(a, b, preferred_element_type=F32)


def _dot_nt(a, b):
    return lax.dot_general(a, b, (((1,), (1,)), ((), ())), preferred_element_type=F32)


def _dot_tn(a, b):
    return lax.dot_general(a, b, (((0,), (0,)), ((), ())), preferred_element_type=F32)


def _mod_kernel(c_ref, w_ref, b_ref, o_ref, *, d, kc):
    cact = _silu(c_ref[...])
    acc = b_ref[0]
    for k0 in range(0, d, kc):
        acc = acc + jnp.sum(w_ref[0, k0:k0 + kc, :] * cact[k0:k0 + kc], axis=0, keepdims=True)
    o_ref[0] = acc


def _adaln_mod(c, ada_w, ada_b):
    depth, d, n = ada_w.shape
    tn = 1024
    return pl.pallas_call(
        functools.partial(_mod_kernel, d=d, kc=256),
        out_shape=jax.ShapeDtypeStruct((depth, 1, n), F32),
        grid=(depth, n // tn),
        in_specs=[pl.BlockSpec((d, 1), lambda l, j: (0, 0)),
                  pl.BlockSpec((1, d, tn), lambda l, j: (l, 0, j)),
                  pl.BlockSpec((1, 1, tn), lambda l, j: (l, 0, j))],
        out_specs=pl.BlockSpec((1, 1, tn), lambda l, j: (l, 0, j)),
        compiler_params=_params(("parallel", "parallel")),
        name="adaln_mod",
    )(c.reshape(d, 1), ada_w, ada_b.reshape(depth, 1, n))


def _norm_mod(x, w, scale, shift):
    return _rms(x) * w * (1.0 + scale) + shift


def _norm_mod_kernel(x_ref, w_ref, sc_ref, sh_ref, h_ref):
    h_ref[...] = _norm_mod(x_ref[...], w_ref[...], sc_ref[...], sh_ref[...]).astype(BF16)


def _pre_norm(x, w, scale, shift):
    s, d = x.shape
    tm = min(512, s)
    row = pl.BlockSpec((1, d), lambda i: (0, 0))
    return pl.pallas_call(
        _norm_mod_kernel,
        out_shape=jax.ShapeDtypeStruct((s, d), BF16),
        grid=(s // tm,),
        in_specs=[pl.BlockSpec((tm, d), lambda i: (i, 0)), row, row, row],
        out_specs=pl.BlockSpec((tm, d), lambda i: (i, 0)),
        compiler_params=_params(("parallel",)),
        name="pre_norm",
    )(x, w, scale, shift)


def _inproj_kernel(h_ref, w_ref, wt_ref, wab_ref, p_ref, ab_ref, *, n_head):
    j = pl.program_id(1)
    h = h_ref[...]

    @pl.when(j < n_head)
    def _():
        p_ref[...] = _dot(h, w_ref[...]).astype(BF16)

    @pl.when(j >= n_head)
    def _():
        p_ref[...] = _dot(h, wt_ref[...]).astype(BF16)

    @pl.when(j == 0)
    def _():
        ab_ref[...] = _dot(h, wab_ref[...])


def _in_proj(h, w_in16, w_tail, w_ab, layer):
    s, d = h.shape
    tm, tn = min(2048, s), 512
    n_head = P_CQ // tn
    n = P_CQ + w_tail.shape[2]
    return pl.pallas_call(
        functools.partial(_inproj_kernel, n_head=n_head),
        out_shape=(jax.ShapeDtypeStruct((s, n), BF16), jax.ShapeDtypeStruct((s, LANES), F32)),
        grid=(s // tm, n // tn),
        in_specs=[pl.BlockSpec((tm, d), lambda i, j: (i, 0)),
                  pl.BlockSpec((None, d, tn), lambda i, j: (layer, 0, jnp.minimum(j, n_head - 1))),
                  pl.BlockSpec((None, d, tn), lambda i, j: (layer, 0, jnp.maximum(j - n_head, 0))),
                  pl.BlockSpec((None, d, LANES), lambda i, j: (layer, 0, 0))],
        out_specs=(pl.BlockSpec((tm, tn), lambda i, j: (i, j)),
                   pl.BlockSpec((tm, LANES), lambda i, j: (i, 0))),
        compiler_params=_params(("parallel", "arbitrary")),
        name="in_proj",
    )(h, w_in16, w_tail, w_ab)


def _gdn_prep_kernel(x_ref, halo_ref, cw_ref, ab_ref, alog_ref, dtb_ref, tri_ref,
                     q_ref, k_ref, v_ref, gcb_ref, gct_ref, xp_ref, *, tm):
    i = pl.program_id(0)
    outs = (q_ref, k_ref, v_ref)
    for grp in range(3):
        c0 = grp * GDN_QK
        halo = halo_ref[:, c0:c0 + GDN_QK].astype(F32)
        xp_ref[0:8, :] = jnp.where(i > 0, halo, 0.0)
        xp_ref[8:8 + tm, :] = x_ref[:, c0:c0 + GDN_QK].astype(F32)
        y = cw_ref[3:4, c0:c0 + GDN_QK] * xp_ref[8:8 + tm, :]
        for j in range(GDN_CONV - 1):
            off = 8 - (GDN_CONV - 1) + j
            y = y + cw_ref[j:j + 1, c0:c0 + GDN_QK] * xp_ref[off:off + tm, :]
        y = _silu(y)
        for h in range(GDN_HEADS):
            yh = y[:, h * GDN_DK:(h + 1) * GDN_DK]
            if grp < 2:
                yh = yh * lax.rsqrt(jnp.sum(yh * yh, axis=-1, keepdims=True) + EPS)
            if grp == 0:
                yh = yh * (GDN_DK ** -0.5)
            outs[grp][:, h * GDN_DK:(h + 1) * GDN_DK] = yh.astype(BF16)

    ab = ab_ref[...]
    g = -jnp.exp(alog_ref[...]) * _softplus(ab + dtb_ref[...])
    gc = jnp.dot(tri_ref[...], g, preferred_element_type=F32, precision=lax.Precision.HIGHEST)
    lane = lax.broadcasted_iota(jnp.int32, (1, LANES), 1)
    gcb = jnp.where(lane < GDN_HEADS, gc, _sigmoid(ab))
    gcb_ref[...] = gcb
    gct_ref[...] = gcb.T[0:2 * GDN_HEADS, :]


def _gdn_prep(p, ab, conv_w, a_log, dt_bias):
    s = p.shape[0]
    tm = min(512, s)
    c3 = 3 * GDN_QK
    alog_row = jnp.zeros((1, LANES), F32).at[0, :GDN_HEADS].set(a_log)
    dtb_row = jnp.zeros((1, LANES), F32).at[0, :GDN_HEADS].set(dt_bias)
    r = np.arange(tm)
    tri = jnp.asarray(((r[:, None] >= r[None, :]) & (r[:, None] // GDN_CHUNK == r[None, :] // GDN_CHUNK))
                      .astype(np.float32))
    hb = tm // 8
    row = lambda i: (i, 0)
    return pl.pallas_call(
        functools.partial(_gdn_prep_kernel, tm=tm),
        out_shape=(jax.ShapeDtypeStruct((s, GDN_QK), BF16), jax.ShapeDtypeStruct((s, GDN_QK), BF16),
                   jax.ShapeDtypeStruct((s, GDN_V), BF16), jax.ShapeDtypeStruct((s, LANES), F32),
                   jax.ShapeDtypeStruct((2 * GDN_HEADS, s), F32)),
        grid=(s // tm,),
        in_specs=[pl.BlockSpec((tm, c3), row),
                  pl.BlockSpec((8, c3), lambda i: (jnp.maximum(i * hb - 1, 0), 0)),
                  pl.BlockSpec((GDN_CONV, c3), lambda i: (0, 0)),
                  pl.BlockSpec((tm, LANES), row),
                  pl.BlockSpec((1, LANES), lambda i: (0, 0)),
                  pl.BlockSpec((1, LANES), lambda i: (0, 0)),
                  pl.BlockSpec((tm, tm), lambda i: (0, 0))],
        out_specs=(pl.BlockSpec((tm, GDN_QK), row), pl.BlockSpec((tm, GDN_QK), row),
                   pl.BlockSpec((tm, GDN_V), row), pl.BlockSpec((tm, LANES), row),
                   pl.BlockSpec((2 * GDN_HEADS, tm), lambda i: (0, i))),
        scratch_shapes=[pltpu.VMEM((tm + 8, GDN_QK), F32)],
        compiler_params=_params(("parallel",)),
        name="gdn_prep",
    )(p, p, conv_w, ab, alog_row, dtb_row, tri)


def _gdn_local_kernel(q_ref, k_ref, v_ref, gcb_ref, gct_ref,
                      u_ref, w_ref, qd_ref, kt_ref, qk_ref):
    n = GDN_GROUP
    c = GDN_CHUNK
    ri = lax.broadcasted_iota(jnp.int32, (n, n), 0)
    ci = lax.broadcasted_iota(jnp.int32, (n, n), 1)
    same_chunk = (ri // c) == (ci // c)
    incl = same_chunk & (ri >= ci)
    strict = same_chunk & (ri > ci)
    eye = (ri == ci).astype(F32)
    last_sel = (ci == (ri // c) * c + (c - 1)).astype(F32)
    gcb = gcb_ref[...]
    gc_last_all = jnp.dot(last_sel, gcb, preferred_element_type=F32, precision=lax.Precision.HIGHEST)
    heads = range(GDN_HEADS)
    sls = [slice(h * GDN_DK, (h + 1) * GDN_DK) for h in heads]
    lmats, ts, rhss = [], [], []
    for h in heads:
        sl = sls[h]
        q = q_ref[:, sl].astype(F32)
        k16 = k_ref[:, sl]
        k = k16.astype(F32)
        v = v_ref[:, sl].astype(F32)
        gc_col = gcb[:, h:h + 1]
        beta = gcb[:, GDN_HEADS + h:GDN_HEADS + h + 1]
        gc_row = gct_ref[h:h + 1, :]
        gc_last = gc_last_all[:, h:h + 1]
        diff = gc_col - gc_row
        decay = jnp.where(incl, jnp.exp(jnp.where(incl, diff, 0.0)), 0.0)
        kb = k * beta
        kk = _dot_nt(kb.astype(BF16), k16)
        lmat = jnp.where(strict, kk * decay, 0.0)
        lmats.append(lmat)
        ts.append(eye - jnp.where((ri // 2) == (ci // 2), lmat, 0.0))
        eg = jnp.exp(gc_col)
        rhss.append(jnp.concatenate([v * beta, kb * eg], axis=-1).astype(BF16))
        qd_ref[:, sl] = (q * eg).astype(BF16)
        kt_ref[:, sl] = (k * jnp.exp(gc_last - gc_col)).astype(BF16)
        qk = _dot_nt(q_ref[:, sl], k16) * decay
        qk_c = jnp.concatenate([qk[j * c:(j + 1) * c, j * c:(j + 1) * c] for j in range(n // c)], axis=0)
        qk_ref[:, h * c:(h + 1) * c] = qk_c.astype(BF16)
    b = 2
    while b < c:
        off_diag = ((ri // (2 * b)) == (ci // (2 * b))) & ((ri // b) != (ci // b))
        t16s = [ts[h].astype(BF16) for h in heads]
        ys = [_dot(t16s[h], jnp.where(off_diag, lmats[h], 0.0).astype(BF16)).astype(BF16) for h in heads]
        ts = [ts[h] - _dot(ys[h], t16s[h]) for h in heads]
        b *= 2
    for h in heads:
        uw = _dot(ts[h].astype(BF16), rhss[h])
        u_ref[:, sls[h]] = uw[:, :GDN_DV].astype(BF16)
        w_ref[:, sls[h]] = uw[:, GDN_DV:].astype(BF16)


def _gdn_local(q, k, v, gcb, gct):
    s = q.shape[0]
    n = GDN_GROUP
    row = lambda i: (i, 0)
    wide = pl.BlockSpec((n, GDN_QK), row)
    return pl.pallas_call(
        _gdn_local_kernel,
        out_shape=(jax.ShapeDtypeStruct((s, GDN_V), BF16), jax.ShapeDtypeStruct((s, GDN_QK), BF16),
                   jax.ShapeDtypeStruct((s, GDN_QK), BF16), jax.ShapeDtypeStruct((s, GDN_QK), BF16),
                   jax.ShapeDtypeStruct((s, GDN_HEADS * GDN_CHUNK), BF16)),
        grid=(s // n,),
        in_specs=[wide, wide, wide, pl.BlockSpec((n, LANES), row),
                  pl.BlockSpec((2 * GDN_HEADS, n), lambda i: (0, i))],
        out_specs=(wide, wide, wide, wide, pl.BlockSpec((n, GDN_HEADS * GDN_CHUNK), row)),
        compiler_params=_params(("parallel",)),
        name="gdn_local",
    )(q, k, v, gcb, gct)


def _gdn_scan_kernel(u_ref, w_ref, qd_ref, kt_ref, qk_ref, gcb_ref, z_ref, nw_ref, o_ref, st_ref, *, rb):
    c = GDN_CHUNK

    @pl.when(pl.program_id(0) == 0)
    def _():
        st_ref[...] = jnp.zeros_like(st_ref)

    nw = nw_ref[...]

    def chunk(ci, carry):
        r0 = pl.multiple_of(ci * c, c)
        rows = pl.ds(r0, c)
        gt_row = jnp.exp(gcb_ref[pl.ds(r0 + c - 1, 1), :])
        heads = range(GDN_HEADS)
        sls = [slice(h * GDN_DK, (h + 1) * GDN_DK) for h in heads]
        sts = [st_ref[h] for h in heads]
        r1s = [_dot(jnp.concatenate([w_ref[rows, sls[h]], qd_ref[rows, sls[h]]], axis=0),
                    sts[h].astype(BF16)) for h in heads]
        vns = [(u_ref[rows, sls[h]].astype(F32) - r1s[h][0:c]).astype(BF16) for h in heads]
        for h in heads:
            st_ref[h] = sts[h] * gt_row[:, h:h + 1] + _dot_tn(kt_ref[rows, sls[h]], vns[h])
        os_ = [r1s[h][c:2 * c] + _dot(qk_ref[rows, h * c:(h + 1) * c], vns[h]) for h in heads]
        for h in heads:
            o = _rms(os_[h]) * nw * _silu(z_ref[rows, sls[h]].astype(F32))
            o_ref[rows, sls[h]] = o.astype(BF16)
        return carry

    lax.fori_loop(0, rb // c, chunk, 0)


def _gdn_scan(u, w, qd, kt, qk, gcb, p, norm_w):
    s = u.shape[0]
    rb = min(512, s)
    row = lambda i: (i, 0)
    wide = pl.BlockSpec((rb, GDN_QK), row)
    return pl.pallas_call(
        functools.partial(_gdn_scan_kernel, rb=rb),
        out_shape=jax.ShapeDtypeStruct((s, GDN_V), BF16),
        grid=(s // rb,),
        in_specs=[wide, wide, wide, wide, pl.BlockSpec((rb, GDN_HEADS * GDN_CHUNK), row),
                  pl.BlockSpec((rb, LANES), row),
                  pl.BlockSpec((rb, GDN_V), lambda i: (i, P_Z // GDN_V)),
                  pl.BlockSpec((1, GDN_DV), lambda i: (0, 0))],
        out_specs=wide,
        scratch_shapes=[pltpu.VMEM((GDN_HEADS, GDN_DK, GDN_DV), F32)],
        compiler_params=_params(("arbitrary",)),
        name="gdn_scan",
    )(u, w, qd, kt, qk, gcb, p, norm_w.reshape(1, GDN_DV))


def _rope_table_kernel(pos_ref, inv_ref, cos_ref, sin_ref):
    ang = pos_ref[...].astype(F32) * inv_ref[...]
    lane = lax.broadcasted_iota(jnp.int32, ang.shape, 1)
    first_half = (lane % MLA_ROPE) < (MLA_ROPE // 2)
    cos_ref[...] = jnp.cos(ang)
    sin_ref[...] = jnp.where(first_half, -jnp.sin(ang), jnp.sin(ang))


def _rope_tables(positions):
    s = positions.shape[-1]
    tm = min(1024, s)
    half = MLA_ROPE // 2
    inv = (ROPE_THETA ** (-np.arange(half, dtype=np.float32) / half)).astype(np.float32)
    inv_row = jnp.asarray(np.tile(inv, LANES // half).reshape(1, LANES))
    return pl.pallas_call(
        _rope_table_kernel,
        out_shape=(jax.ShapeDtypeStruct((s, LANES), F32), jax.ShapeDtypeStruct((s, LANES), F32)),
        grid=(s // tm,),
        in_specs=[pl.BlockSpec((tm, 1), lambda i: (i, 0)), pl.BlockSpec((1, LANES), lambda i: (0, 0))],
        out_specs=(pl.BlockSpec((tm, LANES), lambda i: (i, 0)), pl.BlockSpec((tm, LANES), lambda i: (i, 0))),
        compiler_params=_params(("parallel",)),
        name="rope_tables",
    )(positions.reshape(s, 1), inv_row)


def _rope_apply(x, cos, sin_signed):
    width = x.shape[-1]
    half = MLA_ROPE // 2
    lane = lax.broadcasted_iota(jnp.int32, x.shape, 1)
    first_half = (lane % MLA_ROPE) < half
    swapped = jnp.where(first_half, pltpu.roll(x, width - half, 1), pltpu.roll(x, half, 1))
    return x * cos + swapped * sin_signed


def _mla_proj_kernel(cq_ref, ckv_ref, kr_ref, cos_ref, sin_ref, qn_ref, wuq_ref, kvn_ref, wuk_ref, wuvt_ref,
                     q_ref, k_ref, vt_ref):
    cos = cos_ref[...]
    sin = sin_ref[...]
    scale = (MLA_NOPE + MLA_ROPE) ** -0.5 * math.log2(math.e)
    cq = cq_ref[...].astype(F32)
    cqn = cq * lax.rsqrt(jnp.sum(cq * cq, axis=-1, keepdims=True) * (1.0 / MLA_Q_RANK) + EPS) * qn_ref[...]
    q = _dot(cqn.astype(BF16), wuq_ref[...])
    nope_w = MLA_HEADS * MLA_NOPE
    q_rope = _rope_apply(q[:, nope_w:], jnp.concatenate([cos, cos], -1), jnp.concatenate([sin, sin], -1))
    zeros = jnp.zeros((q.shape[0], MLA_QK_PAD - MLA_NOPE - MLA_ROPE), F32)
    ckvn = (_rms(ckv_ref[...].astype(F32)) * kvn_ref[...]).astype(BF16)
    k_nope = _dot(ckvn, wuk_ref[...])
    v_t = _dot_nt(wuvt_ref[...], ckvn)
    k_pe = _rope_apply(kr_ref[...].astype(F32), cos, sin)[:, :MLA_ROPE]
    for h in range(MLA_HEADS):
        qh = jnp.concatenate([q[:, h * MLA_NOPE:(h + 1) * MLA_NOPE],
                              q_rope[:, h * MLA_ROPE:(h + 1) * MLA_ROPE], zeros], axis=-1) * scale
        q_ref[h] = qh.astype(BF16)
        kh = jnp.concatenate([k_nope[:, h * MLA_NOPE:(h + 1) * MLA_NOPE], k_pe, zeros], axis=-1)
        k_ref[h] = kh.astype(BF16)
        vt_ref[h, 0] = v_t[h * MLA_DV:(h + 1) * MLA_DV, :].astype(BF16)


def _mla_proj(p, cos, sin, q_norm, w_uq, kv_norm, w_uk, w_uvt):
    s = p.shape[0]
    tm = min(MLA_BLOCK, s)
    const = lambda i: (0, 0)
    return pl.pallas_call(
        _mla_proj_kernel,
        out_shape=(jax.ShapeDtypeStruct((MLA_HEADS, s, MLA_QK_PAD), BF16),
                   jax.ShapeDtypeStruct((MLA_HEADS, s, MLA_QK_PAD), BF16),
                   jax.ShapeDtypeStruct((MLA_HEADS, s // tm, MLA_DV, tm), BF16)),
        grid=(s // tm,),
        in_specs=[pl.BlockSpec((tm, MLA_Q_RANK_PAD), lambda i: (i, P_CQ // MLA_Q_RANK_PAD)),
                  pl.BlockSpec((tm, LANES), lambda i: (i, P_CKV // LANES)),
                  pl.BlockSpec((tm, LANES), lambda i: (i, P_KROPE // LANES)),
                  pl.BlockSpec((tm, LANES), lambda i: (i, 0)),
                  pl.BlockSpec((tm, LANES), lambda i: (i, 0)),
                  pl.BlockSpec(q_norm.shape, const), pl.BlockSpec(w_uq.shape, const),
                  pl.BlockSpec(kv_norm.shape, const), pl.BlockSpec(w_uk.shape, const),
                  pl.BlockSpec(w_uvt.shape, const)],
        out_specs=(pl.BlockSpec((MLA_HEADS, tm, MLA_QK_PAD), lambda i: (0, i, 0)),
                   pl.BlockSpec((MLA_HEADS, tm, MLA_QK_PAD), lambda i: (0, i, 0)),
                   pl.BlockSpec((MLA_HEADS, 1, MLA_DV, tm), lambda i: (0, i, 0, 0))),
        compiler_params=_params(("parallel",)),
        name="mla_proj",
    )(p, p, p, cos, sin, q_norm, w_uq, kv_norm, w_uk, w_uvt)


def _mla_attn_kernel(q_ref, k_ref, vt_ref, o_ref, st0, st1, mb0, mb1, m_ref, l_ref, acc_ref, *, bk, nd):
    qi = pl.program_id(1)
    slots = ((st0, mb0), (st1, mb1))
    bq = nd * bk
    n0 = nd * qi
    tri = lax.broadcasted_iota(jnp.int32, (bk, bk), 0) <= lax.broadcasted_iota(jnp.int32, (bk, bk), 1)

    gq = bk // 2

    def scores(t, slot, c0, tri_mask):
        st_ref, mb_ref = slots[slot]
        r0 = pl.multiple_of(t * bk, bk)
        st = _dot_nt(k_ref[0, pl.ds(r0, bk), :], q_ref[0, c0:c0 + gq, :])
        if tri_mask is not None:
            st = jnp.where(tri_mask, st, -jnp.inf)
        st_ref[:, c0:c0 + gq] = st
        mb_ref[:, c0:c0 + gq] = jnp.max(st, axis=0, keepdims=True)

    def update(t, slot, c0):
        st_ref, mb_ref = slots[slot]
        cols = slice(c0, c0 + gq)
        m = m_ref[:, cols]
        m_new = jnp.maximum(m, mb_ref[:, cols])
        alpha = jnp.exp2(m - m_new)
        pexp = jnp.exp2(st_ref[:, cols] - m_new)
        l_ref[:, cols] = alpha * l_ref[:, cols] + jnp.sum(pexp, axis=0, keepdims=True)
        acc_ref[:, cols] = alpha * acc_ref[:, cols] + _dot(vt_ref[0, t], pexp.astype(BF16))
        m_ref[:, cols] = m_new

    def sweep(ts=None, ss=None, ds=None, tu=None, su=None, du=None):
        for c0 in range(0, bq, gq):
            g = c0 // bk
            if ts is not None and (ds is None or g >= ds):
                scores(ts, ss, c0, tri[:, c0 - g * bk:c0 - g * bk + gq] if g == ds else None)
            if tu is not None and (du is None or g >= du):
                update(tu, su, c0)

    m_ref[...] = jnp.full(m_ref.shape, -jnp.inf, F32)
    l_ref[...] = jnp.zeros(l_ref.shape, F32)
    acc_ref[...] = jnp.zeros(acc_ref.shape, F32)

    @pl.when(qi == 0)
    def _():
        sweep(ts=0, ss=0, ds=0)

    @pl.when(qi > 0)
    def _():
        sweep(ts=0, ss=0)

    def pair(j, carry):
        sweep(ts=2 * j + 1, ss=1, tu=2 * j, su=0)
        sweep(ts=2 * j + 2, ss=0, tu=2 * j + 1, su=1)
        return carry

    lax.fori_loop(0, n0 // 2 - 1, pair, 0)

    @pl.when(qi > 0)
    def _():
        sweep(ts=n0 - 1, ss=1, tu=n0 - 2, su=0)
        sweep(ts=n0, ss=0, ds=0, tu=n0 - 1, su=1)

    for d in range(1, nd):
        sweep(ts=n0 + d, ss=d % 2, ds=d, tu=n0 + d - 1, su=(d - 1) % 2, du=d - 1)
    sweep(tu=n0 + nd - 1, su=(nd - 1) % 2, du=nd - 1)
    o_ref[...] = (acc_ref[...] / l_ref[...]).T.astype(BF16)


def _mla_attn(q, k, vt):
    nh, s, _ = q.shape
    bk = vt.shape[-1]
    nd = 4 if s % (4 * bk) == 0 else 2
    bq = nd * bk
    assert s % bq == 0
    return pl.pallas_call(
        functools.partial(_mla_attn_kernel, bk=bk, nd=nd),
        out_shape=jax.ShapeDtypeStruct((s, nh * MLA_DV), BF16),
        grid=(nh, s // bq),
        in_specs=[pl.BlockSpec((1, bq, MLA_QK_PAD), lambda h, i: (h, i, 0)),
                  pl.BlockSpec((1, s, MLA_QK_PAD), lambda h, i: (h, 0, 0)),
                  pl.BlockSpec((1, s // bk, MLA_DV, bk), lambda h, i: (h, 0, 0, 0))],
        out_specs=pl.BlockSpec((bq, MLA_DV), lambda h, i: (i, h)),
        scratch_shapes=[pltpu.VMEM((bk, bq), F32), pltpu.VMEM((bk, bq), F32),
                        pltpu.VMEM((1, bq), F32), pltpu.VMEM((1, bq), F32),
                        pltpu.VMEM((1, bq), F32), pltpu.VMEM((1, bq), F32), pltpu.VMEM((MLA_DV, bq), F32)],
        compiler_params=_params(("parallel", "arbitrary")),
        name="mla_attn",
    )(q, k, vt)


def _swa_kernel(q_ref, k_ref, v_ref, kh_ref, vh_ref, sink_ref, o_ref, *, nblk):
    w = SWA_WINDOW
    g = SWA_GROUP
    i = pl.program_id(0)
    key = lax.broadcasted_iota(jnp.int32, (2 * w, g * w), 0)
    col = lax.broadcasted_iota(jnp.int32, (2 * w, g * w), 1)
    dist = col % w + w - key
    band = (dist >= 0) & (dist < w)
    first_band = band & ((key >= w) | (i > 0))
    dist_f = dist.astype(F32)
    biases = []
    for hk in range(SWA_KV_HEADS):
        slope = jnp.zeros((2 * w, g * w), F32)
        for gi, gq in enumerate(SWA_GORDER):
            slope = jnp.where(col // w == gi, 2.0 ** (-8.0 * (hk * g + gq + 1.0) / SWA_HEADS), slope)
        biases.append(-slope * dist_f)
    low = lax.broadcasted_iota(jnp.int32, (2 * w, 2 * SWA_DH), 1) < SWA_DH
    top = lax.broadcasted_iota(jnp.int32, (2 * SWA_DH, 1), 0) < SWA_DH
    for blk in range(nblk):
        rows = slice(blk * w, (blk + 1) * w)
        if blk == 0:
            k_prev, v_prev, valid = kh_ref[...], vh_ref[...], first_band
        else:
            k_prev, v_prev, valid = k_ref[(blk - 1) * w:blk * w, :], v_ref[(blk - 1) * w:blk * w, :], band
        kk = jnp.concatenate([k_prev, k_ref[rows, :]], axis=0).astype(F32)
        kk_sw = pltpu.roll(kk, SWA_DH, 1)
        vvt = jnp.concatenate([v_prev, v_ref[rows, :]], axis=0).astype(F32).T
        probs, inv_denoms = [], []
        for hk in range(SWA_KV_HEADS):
            c0 = hk * g * SWA_DH
            q2 = jnp.concatenate([q_ref[rows, c0:c0 + 2 * SWA_DH], q_ref[rows, c0 + 2 * SWA_DH:c0 + 4 * SWA_DH]],
                                 axis=0) * (SWA_DH ** -0.5)
            k_low = jnp.where(low, kk if hk == 0 else kk_sw, 0.0).astype(BF16)
            k_high = jnp.where(low, 0.0, kk_sw if hk == 0 else kk).astype(BF16)
            st = jnp.concatenate([_dot_nt(k_low, q2), _dot_nt(k_high, q2)], axis=1)
            st = jnp.where(valid, st + biases[hk], -jnp.inf)
            sink = sink_ref[hk]
            m = jnp.maximum(jnp.max(st, axis=0, keepdims=True), sink)
            pexp = jnp.exp(st - m)
            denom = jnp.sum(pexp, axis=0, keepdims=True) + jnp.exp(sink - m)
            probs.append(pexp.astype(BF16))
            inv_denoms.append(1.0 / denom)
        v0 = jnp.where(top, vvt, 0.0).astype(BF16)
        v1 = jnp.where(top, 0.0, vvt).astype(BF16)
        o_t = (_dot(v0, probs[0]) + _dot(v1, probs[1])) * jnp.where(top, inv_denoms[0], inv_denoms[1])
        for gi in range(g):
            o_ref[rows, gi * LANES:(gi + 1) * LANES] = o_t[:, gi * w:(gi + 1) * w].T.astype(BF16)


def _swa(p, sinks):
    s = p.shape[0]
    w = SWA_WINDOW
    rb = min(512, s)
    nblk = rb // w
    order = np.asarray([[hk * SWA_GROUP + gq for gq in SWA_GORDER] for hk in range(SWA_KV_HEADS)])
    sink_rows = jnp.repeat(sinks[order], w, axis=1).reshape(SWA_KV_HEADS, 1, SWA_GROUP * w)
    prev = lambda col: (lambda i: (jnp.maximum(i * nblk - 1, 0), col))
    return pl.pallas_call(
        functools.partial(_swa_kernel, nblk=nblk),
        out_shape=jax.ShapeDtypeStruct((s, SWA_OUT), BF16),
        grid=(s // rb,),
        in_specs=[pl.BlockSpec((rb, SWA_OUT), lambda i: (i, P_SWQ // SWA_OUT)),
                  pl.BlockSpec((rb, SWA_KV), lambda i: (i, P_SWK // SWA_KV)),
                  pl.BlockSpec((rb, SWA_KV), lambda i: (i, P_SWV // SWA_KV)),
                  pl.BlockSpec((w, SWA_KV), prev(P_SWK // SWA_KV)),
                  pl.BlockSpec((w, SWA_KV), prev(P_SWV // SWA_KV)),
                  pl.BlockSpec(sink_rows.shape, lambda i: (0, 0, 0))],
        out_specs=pl.BlockSpec((rb, SWA_OUT), lambda i: (i, 0)),
        compiler_params=_params(("parallel",)),
        name="swa",
    )(p, p, p, p, p, sink_rows)


def _outproj_kernel(oa_ref, ob_ref, oc_ref, w_ref, x_ref, pn_ref, g1_ref, fn_ref, sc_ref, sh_ref,
                    x1_ref, h2_ref, *, sub):
    a_w, b_w = oa_ref.shape[1], ob_ref.shape[1]
    for r0 in range(0, x_ref.shape[0], sub):
        rows = slice(r0, r0 + sub)
        mix = (_dot(oa_ref[rows, :], w_ref[0:a_w, :]) + _dot(ob_ref[rows, :], w_ref[a_w:a_w + b_w, :])
               + _dot(oc_ref[rows, :], w_ref[a_w + b_w:, :]))
        x1 = x_ref[rows, :] + g1_ref[...] * (_rms(mix) * pn_ref[...])
        x1_ref[rows, :] = x1
        h2_ref[rows, :] = _norm_mod(x1, fn_ref[...], sc_ref[...], sh_ref[...]).astype(BF16)


def _out_proj(o_a, o_b, o_c, w_out, layer, x, post_norm, gate1, ffn_norm, scale2, shift2):
    s, d = x.shape
    tm = min(512, s)
    row = lambda i: (i, 0)
    vec = pl.BlockSpec((1, d), lambda i: (0, 0))
    return pl.pallas_call(
        functools.partial(_outproj_kernel, sub=min(256, tm)),
        out_shape=(jax.ShapeDtypeStruct((s, d), F32), jax.ShapeDtypeStruct((s, d), BF16)),
        grid=(s // tm,),
        in_specs=[pl.BlockSpec((tm, o_a.shape[1]), row), pl.BlockSpec((tm, o_b.shape[1]), row),
                  pl.BlockSpec((tm, o_c.shape[1]), row),
                  pl.BlockSpec((None,) + w_out.shape[1:], lambda i: (layer, 0, 0), pipeline_mode=pl.Buffered(1)),
                  pl.BlockSpec((tm, d), row), vec, vec, vec, vec, vec],
        out_specs=(pl.BlockSpec((tm, d), row), pl.BlockSpec((tm, d), row)),
        compiler_params=_params(("parallel",)),
        name="out_proj",
    )(o_a, o_b, o_c, w_out, x, post_norm, gate1, ffn_norm, scale2, shift2)


def _gelu_tanh(x):
    return 0.5 * x * (1.0 + jnp.tanh(math.sqrt(2.0 / math.pi) * (x + 0.044715 * (x * x * x))))


def _ffn_up_kernel(h_ref, wg_ref, wu_ref, cg_ref, cu_ref, bg_ref, bu_ref, o_ref,
                   wg16_ref, wu16_ref, xg_ref, xu_ref, *, tm):
    @pl.when(pl.program_id(1) == 0)
    def _():
        wg16_ref[...] = wg_ref[...].astype(BF16)
        wu16_ref[...] = wu_ref[...].astype(BF16)
        xg_ref[0:8, :] = jnp.zeros((8, xg_ref.shape[1]), F32)
        xu_ref[0:8, :] = jnp.zeros((8, xu_ref.shape[1]), F32)

    h = h_ref[...]

    def conv(x_ref, w_ref, cw_ref, b_ref):
        x_ref[8:8 + tm, :] = _dot(h, w_ref[...])
        y = b_ref[...] + cw_ref[2:3, :] * x_ref[8:8 + tm, :]
        y = y + cw_ref[1:2, :] * x_ref[7:7 + tm, :]
        y = y + cw_ref[0:1, :] * x_ref[6:6 + tm, :]
        x_ref[0:8, :] = x_ref[tm:tm + 8, :]
        return y

    gate = conv(xg_ref, wg16_ref, cg_ref, bg_ref)
    up = conv(xu_ref, wu16_ref, cu_ref, bu_ref)
    o_ref[...] = (_gelu_tanh(gate) * up).astype(BF16)


def _ffn_up(h2, w_up, layer, conv_w, conv_b):
    s, d = h2.shape
    d_ff = w_up.shape[2] // 2
    tm, tn = min(1024, s), 512
    nj = d_ff // tn
    lo = lambda j, i: (0, j)
    hi = lambda j, i: (0, j + nj)
    w_lo = pl.BlockSpec((None, d, tn), lambda j, i: (layer, 0, j))
    w_hi = pl.BlockSpec((None, d, tn), lambda j, i: (layer, 0, j + nj))
    return pl.pallas_call(
        functools.partial(_ffn_up_kernel, tm=tm),
        out_shape=jax.ShapeDtypeStruct((s, d_ff), BF16),
        grid=(nj, s // tm),
        in_specs=[pl.BlockSpec((tm, d), lambda j, i: (i, 0)),
                  w_lo, w_hi,
                  pl.BlockSpec((FFN_CONV, tn), lo), pl.BlockSpec((FFN_CONV, tn), hi),
                  pl.BlockSpec((1, tn), lo), pl.BlockSpec((1, tn), hi)],
        out_specs=pl.BlockSpec((tm, tn), lambda j, i: (i, j)),
        scratch_shapes=[pltpu.VMEM((d, tn), BF16), pltpu.VMEM((d, tn), BF16),
                        pltpu.VMEM((tm + 8, tn), F32), pltpu.VMEM((tm + 8, tn), F32)],
        compiler_params=_params(("parallel", "arbitrary")),
        name="ffn_up",
    )(h2, w_up, w_up, conv_w, conv_w, conv_b, conv_b)


def _ffn_down_kernel(g_ref, w_ref, x_ref, pn_ref, g2_ref, *rest):
    y = _dot(g_ref[...], w_ref[...])
    x2 = x_ref[...] + g2_ref[...] * (_rms(y) * pn_ref[...])
    if len(rest) == 1:
        rest[0][...] = x2
    else:
        nn_ref, sc_ref, sh_ref, x2_ref, hn_ref = rest
        x2_ref[...] = x2
        hn_ref[...] = _norm_mod(x2, nn_ref[...], sc_ref[...], sh_ref[...]).astype(BF16)


def _ffn_down(g, w_down, layer, x1, post_norm, gate2, next_pre_norm=None):
    s, d = x1.shape
    d_ff = g.shape[1]
    tm = min(256, s)
    row = lambda i: (i, 0)
    vec = pl.BlockSpec((1, d), lambda i: (0, 0))
    tile_f32 = jax.ShapeDtypeStruct((s, d), F32)
    tile = pl.BlockSpec((tm, d), row)
    last = next_pre_norm is None
    return pl.pallas_call(
        _ffn_down_kernel,
        out_shape=tile_f32 if last else (tile_f32, jax.ShapeDtypeStruct((s, d), BF16)),
        grid=(s // tm,),
        in_specs=[pl.BlockSpec((tm, d_ff), row),
                  pl.BlockSpec((None, d_ff, d), lambda i: (layer, 0, 0), pipeline_mode=pl.Buffered(1)),
                  tile, vec, vec] + ([] if last else [vec, vec, vec]),
        out_specs=tile if last else (tile, tile),
        compiler_params=_params(("parallel",)),
        name="ffn_down",
    )(g, w_down, x1, post_norm, gate2, *(() if last else next_pre_norm))


def _layout_w_in(w_in):
    depth, d, _ = w_in.shape
    sizes = (GDN_QK, GDN_QK, GDN_V, GDN_V, GDN_HEADS, GDN_HEADS, MLA_Q_RANK, MLA_KV_RANK, MLA_ROPE,
             SWA_OUT, SWA_KV, SWA_KV)
    offs = np.concatenate([[0], np.cumsum(sizes)])
    w16 = w_in.astype(BF16)
    part = lambda n: w16[:, :, offs[n]:offs[n + 1]]
    zeros = lambda n: jnp.zeros((depth, d, n), BF16)
    w_tail = jnp.concatenate([part(6), zeros(MLA_Q_RANK_PAD - MLA_Q_RANK),
                              part(9), part(7), part(8), zeros(LANES - MLA_ROPE), part(10), part(11)], axis=2)
    w_ab = jnp.concatenate([part(4), part(5), zeros(LANES - 2 * GDN_HEADS)], axis=2)
    return w16, w_tail, w_ab


def _layout_w_out(w_out):
    depth, d_mix, d = w_out.shape
    a = d_mix - SWA_OUT
    wc = w_out[:, a:].reshape(depth, SWA_KV_HEADS, SWA_GROUP, SWA_DH, d)[:, :, np.asarray(SWA_GORDER)]
    wc = wc.transpose(0, 2, 1, 3, 4).reshape(depth, SWA_OUT, d)
    return jnp.concatenate([w_out[:, :a].astype(BF16), wc.astype(BF16)], axis=1)


def _layout_mla(q_norm, w_uq, kv_norm, w_ukv):
    dqk = MLA_NOPE + MLA_ROPE
    uq = w_uq.reshape(MLA_Q_RANK, MLA_HEADS, dqk)
    uq = jnp.concatenate([uq[:, :, :MLA_NOPE].reshape(MLA_Q_RANK, -1), uq[:, :, MLA_NOPE:].reshape(MLA_Q_RANK, -1)],
                         axis=1)
    uq = jnp.pad(uq, ((0, MLA_Q_RANK_PAD - MLA_Q_RANK), (0, 0))).astype(BF16)
    qn = jnp.pad(q_norm, (0, MLA_Q_RANK_PAD - MLA_Q_RANK)).reshape(1, MLA_Q_RANK_PAD)
    ukv = w_ukv.reshape(MLA_KV_RANK, MLA_HEADS, MLA_NOPE + MLA_DV)
    uk = ukv[:, :, :MLA_NOPE].reshape(MLA_KV_RANK, -1).astype(BF16)
    uvt = ukv[:, :, MLA_NOPE:].reshape(MLA_KV_RANK, -1).T.astype(BF16)
    return qn, uq, kv_norm.reshape(1, MLA_KV_RANK), uk, uvt


def kernel(x, c, positions, ada_w, ada_b, mix_pre_norm, mix_post_norm, w_in, w_out, gdn_conv, gdn_a_log, gdn_dt_bias, gdn_norm, mla_q_norm, mla_w_uq, mla_kv_norm, mla_w_ukv, swa_sinks, ffn_pre_norm, ffn_post_norm, ffn_w_up, ffn_conv, ffn_conv_b, ffn_w_down):
    batch, s, d = x.shape
    assert batch == 1, "kernels are written for a single sequence"
    depth = ada_w.shape[0]
    xs = x.reshape(s, d)
    mod = _adaln_mod(c, ada_w, ada_b).reshape(depth, N_MOD, 1, d)
    cos, sin = _rope_tables(positions)
    vec = lambda a: a.reshape(1, d)

    w_in16, w_tail, w_ab = _layout_w_in(w_in)
    w_o = _layout_w_out(w_out)
    w_down = ffn_w_down.astype(BF16)

    h = _pre_norm(xs, vec(mix_pre_norm[0]), mod[0, 1], mod[0, 0])
    for l in range(depth):
        shift1, scale1, gate1, shift2, scale2, gate2 = (mod[l, n] for n in range(N_MOD))
        p, ab = _in_proj(h, w_in16, w_tail, w_ab, l)

        q_a, k_a, v_a, gcb, gct = _gdn_prep(p, ab, gdn_conv[l], gdn_a_log[l], gdn_dt_bias[l])
        u, w, qd, kt, qk = _gdn_local(q_a, k_a, v_a, gcb, gct)
        o_a = _gdn_scan(u, w, qd, kt, qk, gcb, p, gdn_norm[l])

        q_b, k_b, v_b = _mla_proj(p, cos, sin, *_layout_mla(mla_q_norm[l], mla_w_uq[l], mla_kv_norm[l],
                                                            mla_w_ukv[l]))
        o_b = _mla_attn(q_b, k_b, v_b)

        o_c = _swa(p, swa_sinks[l])

        x1, h2 = _out_proj(o_a, o_b, o_c, w_o, l, xs, vec(mix_post_norm[l]), gate1,
                           vec(ffn_pre_norm[l]), scale2, shift2)
        g = _ffn_up(h2, ffn_w_up, l, ffn_conv[l], ffn_conv_b[l].reshape(1, -1))
        if l + 1 < depth:
            xs, h = _ffn_down(g, w_down, l, x1, vec(ffn_post_norm[l]), gate2,
                              (vec(mix_pre_norm[l + 1]), mod[l + 1, 1], mod[l + 1, 0]))
        else:
            xs = _ffn_down(g, w_down, l, x1, vec(ffn_post_norm[l]), gate2)
    return xs.reshape(batch, s, d)
```

```python
import functools
import math

import numpy as np
import jax
import jax.numpy as jnp
from jax import lax
from jax.experimental import pallas as pl
from jax.experimental.pallas import tpu as pltpu

F32 = jnp.float32
BF16 = jnp.bfloat16

EPS = 1e-6
N_MOD = 6
GDN_HEADS = 8
GDN_DK = 128
GDN_DV = 128
GDN_CONV = 4
GDN_CHUNK = 64
GDN_QK = GDN_HEADS * GDN_DK
GDN_V = GDN_HEADS * GDN_DV
MLA_HEADS = 4
MLA_Q_RANK = 448
MLA_Q_RANK_PAD = 512
MLA_KV_RANK = 128
MLA_NOPE = 128
MLA_ROPE = 64
MLA_DV = 128
MLA_QK_PAD = 256
MLA_BLOCK = 512
ROPE_THETA = 10000.0
SWA_HEADS = 8
SWA_KV_HEADS = 2
SWA_GROUP = SWA_HEADS // SWA_KV_HEADS
SWA_DH = 64
SWA_WINDOW = 128
SWA_OUT = SWA_HEADS * SWA_DH
SWA_KV = SWA_KV_HEADS * SWA_DH
SWA_GORDER = (0, 2, 1, 3)
FFN_CONV = 3

LANES = 128
GDN_GROUP = 256
VMEM_LIMIT_MB = 56

P_QKV = 0
P_Z = 3072
P_CQ = 4096
P_SWQ = 4608
P_CKV = 5120
P_KROPE = 5248
P_SWK = 5376
P_SWV = 5504
P_WIDTH = 5632


def _params(semantics):
    return pltpu.CompilerParams(dimension_semantics=semantics, vmem_limit_bytes=VMEM_LIMIT_MB << 20)


def _sigmoid(x):
    return 0.5 + 0.5 * jnp.tanh(0.5 * x)


def _silu(x):
    h = 0.5 * x
    return h + h * jnp.tanh(h)


def _softplus(x):
    return jnp.maximum(x, 0.0) + jnp.log(1.0 + jnp.exp(-jnp.abs(x)))


def _rms(x):
    return x * lax.rsqrt(jnp.mean(x * x, axis=-1, keepdims=True) + EPS)


def _dot(a, b):
    return jnp.dot(a, b, preferred_element_type=F32)


def _dot_nt(a, b):
    return lax.dot_general(a, b, (((1,), (1,)), ((), ())), preferred_element_type=F32)


def _dot_tn(a, b):
    return lax.dot_general(a, b, (((0,), (0,)), ((), ())), preferred_element_type=F32)


def _mod_kernel(c_ref, w_ref, b_ref, o_ref, *, d, kc):
    cact = _silu(c_ref[...])
    acc = b_ref[0]
    for k0 in range(0, d, kc):
        acc = acc + jnp.sum(w_ref[0, k0:k0 + kc, :] * cact[k0:k0 + kc], axis=0, keepdims=True)
    o_ref[0] = acc


def _adaln_mod(c, ada_w, ada_b):
    depth, d, n = ada_w.shape
    tn = 1024
    return pl.pallas_call(
        functools.partial(_mod_kernel, d=d, kc=256),
        out_shape=jax.ShapeDtypeStruct((depth, 1, n), F32),
        grid=(depth, n // tn),
        in_specs=[pl.BlockSpec((d, 1), lambda l, j: (0, 0)),
                  pl.BlockSpec((1, d, tn), lambda l, j: (l, 0, j)),
                  pl.BlockSpec((1, 1, tn), lambda l, j: (l, 0, j))],
        out_specs=pl.BlockSpec((1, 1, tn), lambda l, j: (l, 0, j)),
        compiler_params=_params(("parallel", "parallel")),
        name="adaln_mod",
    )(c.reshape(d, 1), ada_w, ada_b.reshape(depth, 1, n))


def _norm_mod(x, w, scale, shift):
    return _rms(x) * w * (1.0 + scale) + shift


def _norm_mod_kernel(x_ref, w_ref, sc_ref, sh_ref, h_ref):
    h_ref[...] = _norm_mod(x_ref[...], w_ref[...], sc_ref[...], sh_ref[...]).astype(BF16)


def _pre_norm(x, w, scale, shift):
    s, d = x.shape
    tm = min(512, s)
    row = pl.BlockSpec((1, d), lambda i: (0, 0))
    return pl.pallas_call(
        _norm_mod_kernel,
        out_shape=jax.ShapeDtypeStruct((s, d), BF16),
        grid=(s // tm,),
        in_specs=[pl.BlockSpec((tm, d), lambda i: (i, 0)), row, row, row],
        out_specs=pl.BlockSpec((tm, d), lambda i: (i, 0)),
        compiler_params=_params(("parallel",)),
        name="pre_norm",
    )(x, w, scale, shift)


def _inproj_kernel(h_ref, w_ref, wab_ref, cw_ref, p_ref, ab_ref, x_ref, carry_ref, *, tm, tiles_per_part):
    i = pl.program_id(0)
    j = pl.program_id(1)
    h = h_ref[...]
    tn = p_ref.shape[1]
    rc = tm // 8

    def gdn_tile(part):
        @pl.when(i == 0)
        def _():
            x_ref[0:8, :] = jnp.zeros((8, tn), F32)

        @pl.when(i > 0)
        def _():
            x_ref[0:8, :] = carry_ref[j]

        def finish(r0):
            y = cw_ref[GDN_CONV - 1:GDN_CONV, :] * x_ref[8 + r0:8 + r0 + rc, :]
            for t in range(GDN_CONV - 1):
                off = 8 - (GDN_CONV - 1) + t + r0
                y = y + cw_ref[t:t + 1, :] * x_ref[off:off + rc, :]
            y = _silu(y)
            for c0 in range(0, tn, GDN_DK):
                yh = y[:, c0:c0 + GDN_DK]
                if part < 2:
                    yh = yh * lax.rsqrt(jnp.sum(yh * yh, axis=-1, keepdims=True) + EPS)
                if part == 0:
                    yh = yh * (GDN_DK ** -0.5)
                p_ref[r0:r0 + rc, c0:c0 + GDN_DK] = yh.astype(BF16)

        for r0 in range(0, tm, rc):
            x_ref[8 + r0:8 + r0 + rc, :] = _dot(h_ref[r0:r0 + rc, :], w_ref[...])
            if r0 > 0:
                finish(r0 - rc)
        finish(tm - rc)
        carry_ref[j] = x_ref[tm:tm + 8, :]

    for part in range(3):
        @pl.when((j >= part * tiles_per_part) & (j < (part + 1) * tiles_per_part))
        def _():
            gdn_tile(part)

    @pl.when(j >= 3 * tiles_per_part)
    def _():
        p_ref[...] = _dot(h, w_ref[...]).astype(BF16)

    @pl.when(j == 0)
    def _():
        ab_ref[...] = _dot(h, wab_ref[...])


def _in_proj(h, w_p, w_ab, layer, conv_w):
    s, d = h.shape
    n = w_p.shape[2]
    tm, tn = min(2048, s), 512
    tiles_per_part = GDN_QK // tn
    n_gdn = 3 * tiles_per_part
    return pl.pallas_call(
        functools.partial(_inproj_kernel, tm=tm, tiles_per_part=tiles_per_part),
        out_shape=(jax.ShapeDtypeStruct((s, n), BF16), jax.ShapeDtypeStruct((s, LANES), F32)),
        grid=(s // tm, n // tn),
        in_specs=[pl.BlockSpec((tm, d), lambda i, j: (i, 0)),
                  pl.BlockSpec((None, d, tn), lambda i, j: (layer, 0, j)),
                  pl.BlockSpec((None, d, LANES), lambda i, j: (layer, 0, 0)),
                  pl.BlockSpec((GDN_CONV, tn), lambda i, j: (0, jnp.minimum(j, n_gdn - 1)))],
        out_specs=(pl.BlockSpec((tm, tn), lambda i, j: (i, j)),
                   pl.BlockSpec((tm, LANES), lambda i, j: (i, 0))),
        scratch_shapes=[pltpu.VMEM((tm + 8, tn), F32), pltpu.VMEM((n_gdn, 8, tn), F32)],
        compiler_params=_params(("arbitrary", "arbitrary")),
        name="in_proj",
    )(h, w_p, w_ab, conv_w)


def _gdn_gates_kernel(ab_ref, alog_ref, dtb_ref, tri_ref, gcb_ref, gct_ref):
    ab = ab_ref[...]
    g = -jnp.exp(alog_ref[...]) * _softplus(ab + dtb_ref[...])
    gc = jnp.dot(tri_ref[...], g, preferred_element_type=F32, precision=lax.Precision.HIGHEST)
    lane = lax.broadcasted_iota(jnp.int32, (1, LANES), 1)
    gcb = jnp.where(lane < GDN_HEADS, gc, _sigmoid(ab))
    gcb_ref[...] = gcb
    gct_ref[...] = gcb.T[0:2 * GDN_HEADS, :]


def _gdn_gates(ab, a_log, dt_bias):
    s = ab.shape[0]
    tm = min(512, s)
    alog_row = jnp.zeros((1, LANES), F32).at[0, :GDN_HEADS].set(a_log)
    dtb_row = jnp.zeros((1, LANES), F32).at[0, :GDN_HEADS].set(dt_bias)
    r = np.arange(tm)
    tri = jnp.asarray(((r[:, None] >= r[None, :]) & (r[:, None] // GDN_CHUNK == r[None, :] // GDN_CHUNK))
                      .astype(np.float32))
    row = lambda i: (i, 0)
    return pl.pallas_call(
        _gdn_gates_kernel,
        out_shape=(jax.ShapeDtypeStruct((s, LANES), F32), jax.ShapeDtypeStruct((2 * GDN_HEADS, s), F32)),
        grid=(s // tm,),
        in_specs=[pl.BlockSpec((tm, LANES), row),
                  pl.BlockSpec((1, LANES), lambda i: (0, 0)),
                  pl.BlockSpec((1, LANES), lambda i: (0, 0)),
                  pl.BlockSpec((tm, tm), lambda i: (0, 0))],
        out_specs=(pl.BlockSpec((tm, LANES), row), pl.BlockSpec((2 * GDN_HEADS, tm), lambda i: (0, i))),
        compiler_params=_params(("parallel",)),
        name="gdn_gates",
    )(ab, alog_row, dtb_row, tri)


def _gdn_local_kernel(q_ref, k_ref, v_ref, gcb_ref, gct_ref,
                      u_ref, w_ref, qd_ref, kt_ref, qk_ref):
    n = GDN_GROUP
    c = GDN_CHUNK
    ri = lax.broadcasted_iota(jnp.int32, (n, n), 0)
    ci = lax.broadcasted_iota(jnp.int32, (n, n), 1)
    same_chunk = (ri // c) == (ci // c)
    incl = same_chunk & (ri >= ci)
    strict = same_chunk & (ri > ci)
    eye = (ri == ci).astype(F32)
    last_sel = (ci == (ri // c) * c + (c - 1)).astype(F32)
    gcb = gcb_ref[...]
    gc_last_all = jnp.dot(last_sel, gcb, preferred_element_type=F32, precision=lax.Precision.HIGHEST)
    heads = range(GDN_HEADS)
    sls = [slice(h * GDN_DK, (h + 1) * GDN_DK) for h in heads]
    lmats, ts, rhss = [], [], []
    for h in heads:
        sl = sls[h]
        q = q_ref[:, sl].astype(F32)
        k16 = k_ref[:, sl]
        k = k16.astype(F32)
        v = v_ref[:, sl].astype(F32)
        gc_col = gcb[:, h:h + 1]
        beta = gcb[:, GDN_HEADS + h:GDN_HEADS + h + 1]
        gc_row = gct_ref[h:h + 1, :]
        gc_last = gc_last_all[:, h:h + 1]
        diff = gc_col - gc_row
        decay = jnp.where(incl, jnp.exp(jnp.where(incl, diff, 0.0)), 0.0)
        kb = k * beta
        kk = _dot_nt(kb.astype(BF16), k16)
        lmat = jnp.where(strict, kk * decay, 0.0)
        lmats.append(lmat)
        ts.append(eye - jnp.where((ri // 2) == (ci // 2), lmat, 0.0))
        eg = jnp.exp(gc_col)
        rhss.append(jnp.concatenate([v * beta, kb * eg], axis=-1).astype(BF16))
        qd_ref[:, sl] = (q * eg).astype(BF16)
        kt_ref[:, sl] = (k * jnp.exp(gc_last - gc_col)).astype(BF16)
        qk = _dot_nt(q_ref[:, sl], k16) * decay
        qk_c = jnp.concatenate([qk[j * c:(j + 1) * c, j * c:(j + 1) * c] for j in range(n // c)], axis=0)
        qk_ref[:, h * c:(h + 1) * c] = qk_c.astype(BF16)
    b = 2
    while b < c:
        off_diag = ((ri // (2 * b)) == (ci // (2 * b))) & ((ri // b) != (ci // b))
        t16s = [ts[h].astype(BF16) for h in heads]
        ys = [_dot(t16s[h], jnp.where(off_diag, lmats[h], 0.0).astype(BF16)).astype(BF16) for h in heads]
        ts = [ts[h] - _dot(ys[h], t16s[h]) for h in heads]
        b *= 2
    for h in heads:
        uw = _dot(ts[h].astype(BF16), rhss[h])
        u_ref[:, sls[h]] = uw[:, :GDN_DV].astype(BF16)
        w_ref[:, sls[h]] = uw[:, GDN_DV:].astype(BF16)


def _gdn_local(p, gcb, gct):
    s = p.shape[0]
    n = GDN_GROUP
    row = lambda i: (i, 0)
    wide = pl.BlockSpec((n, GDN_QK), row)
    part = lambda c: pl.BlockSpec((n, GDN_QK), lambda i: (i, P_QKV // GDN_QK + c))
    return pl.pallas_call(
        _gdn_local_kernel,
        out_shape=(jax.ShapeDtypeStruct((s, GDN_V), BF16), jax.ShapeDtypeStruct((s, GDN_QK), BF16),
                   jax.ShapeDtypeStruct((s, GDN_QK), BF16), jax.ShapeDtypeStruct((s, GDN_QK), BF16),
                   jax.ShapeDtypeStruct((s, GDN_HEADS * GDN_CHUNK), BF16)),
        grid=(s // n,),
        in_specs=[part(0), part(1), part(2), pl.BlockSpec((n, LANES), row),
                  pl.BlockSpec((2 * GDN_HEADS, n), lambda i: (0, i))],
        out_specs=(wide, wide, wide, wide, pl.BlockSpec((n, GDN_HEADS * GDN_CHUNK), row)),
        compiler_params=_params(("parallel",)),
        name="gdn_local",
    )(p, p, p, gcb, gct)


def _gdn_scan_kernel(u_ref, w_ref, qd_ref, kt_ref, qk_ref, gcb_ref, z_ref, nw_ref, o_ref, st_ref, *, rb):
    c = GDN_CHUNK

    @pl.when(pl.program_id(0) == 0)
    def _():
        st_ref[...] = jnp.zeros_like(st_ref)

    nw = nw_ref[...]

    def chunk(ci, carry):
        r0 = pl.multiple_of(ci * c, c)
        rows = pl.ds(r0, c)
        gt_row = jnp.exp(gcb_ref[pl.ds(r0 + c - 1, 1), :])
        heads = range(GDN_HEADS)
        sls = [slice(h * GDN_DK, (h + 1) * GDN_DK) for h in heads]
        sts = [st_ref[h] for h in heads]
        r1s = [_dot(jnp.concatenate([w_ref[rows, sls[h]], qd_ref[rows, sls[h]]], axis=0),
                    sts[h].astype(BF16)) for h in heads]
        vns = [(u_ref[rows, sls[h]].astype(F32) - r1s[h][0:c]).astype(BF16) for h in heads]
        for h in heads:
            st_ref[h] = sts[h] * gt_row[:, h:h + 1] + _dot_tn(kt_ref[rows, sls[h]], vns[h])
        os_ = [r1s[h][c:2 * c] + _dot(qk_ref[rows, h * c:(h + 1) * c], vns[h]) for h in heads]
        for h in heads:
            o = _rms(os_[h]) * nw * _silu(z_ref[rows, sls[h]].astype(F32))
            o_ref[rows, sls[h]] = o.astype(BF16)
        return carry

    lax.fori_loop(0, rb // c, chunk, 0)


def _gdn_scan(u, w, qd, kt, qk, gcb, p, norm_w):
    s = u.shape[0]
    rb = min(512, s)
    row = lambda i: (i, 0)
    wide = pl.BlockSpec((rb, GDN_QK), row)
    return pl.pallas_call(
        functools.partial(_gdn_scan_kernel, rb=rb),
        out_shape=jax.ShapeDtypeStruct((s, GDN_V), BF16),
        grid=(s // rb,),
        in_specs=[wide, wide, wide, wide, pl.BlockSpec((rb, GDN_HEADS * GDN_CHUNK), row),
                  pl.BlockSpec((rb, LANES), row),
                  pl.BlockSpec((rb, GDN_V), lambda i: (i, P_Z // GDN_V)),
                  pl.BlockSpec((1, GDN_DV), lambda i: (0, 0))],
        out_specs=wide,
        scratch_shapes=[pltpu.VMEM((GDN_HEADS, GDN_DK, GDN_DV), F32)],
        compiler_params=_params(("arbitrary",)),
        name="gdn_scan",
    )(u, w, qd, kt, qk, gcb, p, norm_w.reshape(1, GDN_DV))


def _rope_table_kernel(pos_ref, inv_ref, cos_ref, sin_ref):
    ang = pos_ref[...].astype(F32) * inv_ref[...]
    lane = lax.broadcasted_iota(jnp.int32, ang.shape, 1)
    first_half = (lane % MLA_ROPE) < (MLA_ROPE // 2)
    cos_ref[...] = jnp.cos(ang)
    sin_ref[...] = jnp.where(first_half, -jnp.sin(ang), jnp.sin(ang))


def _rope_tables(positions):
    s = positions.shape[-1]
    tm = min(1024, s)
    half = MLA_ROPE // 2
    inv = (ROPE_THETA ** (-np.arange(half, dtype=np.float32) / half)).astype(np.float32)
    inv_row = jnp.asarray(np.tile(inv, LANES // half).reshape(1, LANES))
    return pl.pallas_call(
        _rope_table_kernel,
        out_shape=(jax.ShapeDtypeStruct((s, LANES), F32), jax.ShapeDtypeStruct((s, LANES), F32)),
        grid=(s // tm,),
        in_specs=[pl.BlockSpec((tm, 1), lambda i: (i, 0)), pl.BlockSpec((1, LANES), lambda i: (0, 0))],
        out_specs=(pl.BlockSpec((tm, LANES), lambda i: (i, 0)), pl.BlockSpec((tm, LANES), lambda i: (i, 0))),
        compiler_params=_params(("parallel",)),
        name="rope_tables",
    )(positions.reshape(s, 1), inv_row)


def _rope_apply(x, cos, sin_signed):
    width = x.shape[-1]
    half = MLA_ROPE // 2
    lane = lax.broadcasted_iota(jnp.int32, x.shape, 1)
    first_half = (lane % MLA_ROPE) < half
    swapped = jnp.where(first_half, pltpu.roll(x, width - half, 1), pltpu.roll(x, half, 1))
    return x * cos + swapped * sin_signed


def _mla_proj_kernel(cq_ref, ckv_ref, kr_ref, cos_ref, sin_ref, qn_ref, wuq_ref, kvn_ref, wuk_ref, wuvt_ref,
                     q_ref, k_ref, vt_ref):
    cos = cos_ref[...]
    sin = sin_ref[...]
    scale = (MLA_NOPE + MLA_ROPE) ** -0.5 * math.log2(math.e)
    cq = cq_ref[...].astype(F32)
    cqn = cq * lax.rsqrt(jnp.sum(cq * cq, axis=-1, keepdims=True) * (1.0 / MLA_Q_RANK) + EPS) * qn_ref[...]
    q = _dot(cqn.astype(BF16), wuq_ref[...])
    nope_w = MLA_HEADS * MLA_NOPE
    q_rope = _rope_apply(q[:, nope_w:], jnp.concatenate([cos, cos], -1), jnp.concatenate([sin, sin], -1))
    zeros = jnp.zeros((q.shape[0], MLA_QK_PAD - MLA_NOPE - MLA_ROPE), F32)
    ckvn = (_rms(ckv_ref[...].astype(F32)) * kvn_ref[...]).astype(BF16)
    k_nope = _dot(ckvn, wuk_ref[...])
    v_t = _dot_nt(wuvt_ref[...], ckvn)
    k_pe = _rope_apply(kr_ref[...].astype(F32), cos, sin)[:, :MLA_ROPE]
    for h in range(MLA_HEADS):
        qh = jnp.concatenate([q[:, h * MLA_NOPE:(h + 1) * MLA_NOPE],
                              q_rope[:, h * MLA_ROPE:(h + 1) * MLA_ROPE], zeros], axis=-1) * scale
        q_ref[h] = qh.astype(BF16)
        kh = jnp.concatenate([k_nope[:, h * MLA_NOPE:(h + 1) * MLA_NOPE], k_pe, zeros], axis=-1)
        k_ref[h] = kh.astype(BF16)
        vt_ref[h, 0] = v_t[h * MLA_DV:(h + 1) * MLA_DV, :].astype(BF16)


def _mla_proj(p, cos, sin, q_norm, w_uq, kv_norm, w_uk, w_uvt):
    s = p.shape[0]
    tm = min(MLA_BLOCK, s)
    const = lambda i: (0, 0)
    return pl.pallas_call(
        _mla_proj_kernel,
        out_shape=(jax.ShapeDtypeStruct((MLA_HEADS, s, MLA_QK_PAD), BF16),
                   jax.ShapeDtypeStruct((MLA_HEADS, s, MLA_QK_PAD), BF16),
                   jax.ShapeDtypeStruct((MLA_HEADS, s // tm, MLA_DV, tm), BF16)),
        grid=(s // tm,),
        in_specs=[pl.BlockSpec((tm, MLA_Q_RANK_PAD), lambda i: (i, P_CQ // MLA_Q_RANK_PAD)),
                  pl.BlockSpec((tm, LANES), lambda i: (i, P_CKV // LANES)),
                  pl.BlockSpec((tm, LANES), lambda i: (i, P_KROPE // LANES)),
                  pl.BlockSpec((tm, LANES), lambda i: (i, 0)),
                  pl.BlockSpec((tm, LANES), lambda i: (i, 0)),
                  pl.BlockSpec(q_norm.shape, const), pl.BlockSpec(w_uq.shape, const),
                  pl.BlockSpec(kv_norm.shape, const), pl.BlockSpec(w_uk.shape, const),
                  pl.BlockSpec(w_uvt.shape, const)],
        out_specs=(pl.BlockSpec((MLA_HEADS, tm, MLA_QK_PAD), lambda i: (0, i, 0)),
                   pl.BlockSpec((MLA_HEADS, tm, MLA_QK_PAD), lambda i: (0, i, 0)),
                   pl.BlockSpec((MLA_HEADS, 1, MLA_DV, tm), lambda i: (0, i, 0, 0))),
        compiler_params=_params(("parallel",)),
        name="mla_proj",
    )(p, p, p, cos, sin, q_norm, w_uq, kv_norm, w_uk, w_uvt)


def _mla_attn_kernel(q_ref, k_ref, vt_ref, o_ref, st0, st1, mb0, mb1, m_ref, l_ref, acc_ref, *, bk, nd):
    qi = pl.program_id(1)
    slots = ((st0, mb0), (st1, mb1))
    bq = nd * bk
    n0 = nd * qi
    tri = lax.broadcasted_iota(jnp.int32, (bk, bk), 0) <= lax.broadcasted_iota(jnp.int32, (bk, bk), 1)

    gq = bk // 2

    def scores(t, slot, c0, tri_mask):
        st_ref, mb_ref = slots[slot]
        r0 = pl.multiple_of(t * bk, bk)
        st = _dot_nt(k_ref[0, pl.ds(r0, bk), :], q_ref[0, c0:c0 + gq, :])
        if tri_mask is not None:
            st = jnp.where(tri_mask, st, -jnp.inf)
        st_ref[:, c0:c0 + gq] = st
        mb_ref[:, c0:c0 + gq] = jnp.max(st, axis=0, keepdims=True)

    def update(t, slot, c0):
        st_ref, mb_ref = slots[slot]
        cols = slice(c0, c0 + gq)
        m = m_ref[:, cols]
        m_new = jnp.maximum(m, mb_ref[:, cols])
        alpha = jnp.exp2(m - m_new)
        pexp = jnp.exp2(st_ref[:, cols] - m_new)
        l_ref[:, cols] = alpha * l_ref[:, cols] + jnp.sum(pexp, axis=0, keepdims=True)
        acc_ref[:, cols] = alpha * acc_ref[:, cols] + _dot(vt_ref[0, t], pexp.astype(BF16))
        m_ref[:, cols] = m_new

    def sweep(ts=None, ss=None, ds=None, tu=None, su=None, du=None):
        for c0 in range(0, bq, gq):
            g = c0 // bk
            if ts is not None and (ds is None or g >= ds):
                scores(ts, ss, c0, tri[:, c0 - g * bk:c0 - g * bk + gq] if g == ds else None)
            if tu is not None and (du is None or g >= du):
                update(tu, su, c0)

    m_ref[...] = jnp.full(m_ref.shape, -jnp.inf, F32)
    l_ref[...] = jnp.zeros(l_ref.shape, F32)
    acc_ref[...] = jnp.zeros(acc_ref.shape, F32)

    @pl.when(qi == 0)
    def _():
        sweep(ts=0, ss=0, ds=0)

    @pl.when(qi > 0)
    def _():
        sweep(ts=0, ss=0)

    def pair(j, carry):
        sweep(ts=2 * j + 1, ss=1, tu=2 * j, su=0)
        sweep(ts=2 * j + 2, ss=0, tu=2 * j + 1, su=1)
        return carry

    lax.fori_loop(0, n0 // 2 - 1, pair, 0)

    @pl.when(qi > 0)
    def _():
        sweep(ts=n0 - 1, ss=1, tu=n0 - 2, su=0)
        sweep(ts=n0, ss=0, ds=0, tu=n0 - 1, su=1)

    for d in range(1, nd):
        sweep(ts=n0 + d, ss=d % 2, ds=d, tu=n0 + d - 1, su=(d - 1) % 2, du=d - 1)
    sweep(tu=n0 + nd - 1, su=(nd - 1) % 2, du=nd - 1)
    o_ref[...] = (acc_ref[...] / l_ref[...]).T.astype(BF16)


def _mla_attn(q, k, vt):
    nh, s, _ = q.shape
    bk = vt.shape[-1]
    nd = 4 if s % (4 * bk) == 0 else 2
    bq = nd * bk
    assert s % bq == 0
    return pl.pallas_call(
        functools.partial(_mla_attn_kernel, bk=bk, nd=nd),
        out_shape=jax.ShapeDtypeStruct((s, nh * MLA_DV), BF16),
        grid=(nh, s // bq),
        in_specs=[pl.BlockSpec((1, bq, MLA_QK_PAD), lambda h, i: (h, i, 0)),
                  pl.BlockSpec((1, s, MLA_QK_PAD), lambda h, i: (h, 0, 0)),
                  pl.BlockSpec((1, s // bk, MLA_DV, bk), lambda h, i: (h, 0, 0, 0))],
        out_specs=pl.BlockSpec((bq, MLA_DV), lambda h, i: (i, h)),
        scratch_shapes=[pltpu.VMEM((bk, bq), F32), pltpu.VMEM((bk, bq), F32),
                        pltpu.VMEM((1, bq), F32), pltpu.VMEM((1, bq), F32),
                        pltpu.VMEM((1, bq), F32), pltpu.VMEM((1, bq), F32), pltpu.VMEM((MLA_DV, bq), F32)],
        compiler_params=_params(("parallel", "arbitrary")),
        name="mla_attn",
    )(q, k, vt)


def _swa_kernel(q_ref, k_ref, v_ref, kh_ref, vh_ref, sink_ref, o_ref, *, nblk):
    w = SWA_WINDOW
    g = SWA_GROUP
    i = pl.program_id(0)
    key = lax.broadcasted_iota(jnp.int32, (2 * w, g * w), 0)
    col = lax.broadcasted_iota(jnp.int32, (2 * w, g * w), 1)
    dist = col % w + w - key
    band = (dist >= 0) & (dist < w)
    first_band = band & ((key >= w) | (i > 0))
    dist_f = dist.astype(F32)
    biases = []
    for hk in range(SWA_KV_HEADS):
        slope = jnp.zeros((2 * w, g * w), F32)
        for gi, gq in enumerate(SWA_GORDER):
            slope = jnp.where(col // w == gi, 2.0 ** (-8.0 * (hk * g + gq + 1.0) / SWA_HEADS), slope)
        biases.append(-slope * dist_f)
    low = lax.broadcasted_iota(jnp.int32, (2 * w, 2 * SWA_DH), 1) < SWA_DH
    top = lax.broadcasted_iota(jnp.int32, (2 * SWA_DH, 1), 0) < SWA_DH
    for blk in range(nblk):
        rows = slice(blk * w, (blk + 1) * w)
        if blk == 0:
            k_prev, v_prev, valid = kh_ref[...], vh_ref[...], first_band
        else:
            k_prev, v_prev, valid = k_ref[(blk - 1) * w:blk * w, :], v_ref[(blk - 1) * w:blk * w, :], band
        kk = jnp.concatenate([k_prev, k_ref[rows, :]], axis=0).astype(F32)
        kk_sw = pltpu.roll(kk, SWA_DH, 1)
        vvt = jnp.concatenate([v_prev, v_ref[rows, :]], axis=0).astype(F32).T
        probs, inv_denoms = [], []
        for hk in range(SWA_KV_HEADS):
            c0 = hk * g * SWA_DH
            q2 = jnp.concatenate([q_ref[rows, c0:c0 + 2 * SWA_DH], q_ref[rows, c0 + 2 * SWA_DH:c0 + 4 * SWA_DH]],
                                 axis=0) * (SWA_DH ** -0.5)
            k_low = jnp.where(low, kk if hk == 0 else kk_sw, 0.0).astype(BF16)
            k_high = jnp.where(low, 0.0, kk_sw if hk == 0 else kk).astype(BF16)
            st = jnp.concatenate([_dot_nt(k_low, q2), _dot_nt(k_high, q2)], axis=1)
            st = jnp.where(valid, st + biases[hk], -jnp.inf)
            sink = sink_ref[hk]
            m = jnp.maximum(jnp.max(st, axis=0, keepdims=True), sink)
            pexp = jnp.exp(st - m)
            denom = jnp.sum(pexp, axis=0, keepdims=True) + jnp.exp(sink - m)
            probs.append(pexp.astype(BF16))
            inv_denoms.append(1.0 / denom)
        v0 = jnp.where(top, vvt, 0.0).astype(BF16)
        v1 = jnp.where(top, 0.0, vvt).astype(BF16)
        o_t = (_dot(v0, probs[0]) + _dot(v1, probs[1])) * jnp.where(top, inv_denoms[0], inv_denoms[1])
        for gi in range(g):
            o_ref[rows, gi * LANES:(gi + 1) * LANES] = o_t[:, gi * w:(gi + 1) * w].T.astype(BF16)


def _swa(p, sinks):
    s = p.shape[0]
    w = SWA_WINDOW
    rb = min(512, s)
    nblk = rb // w
    order = np.asarray([[hk * SWA_GROUP + gq for gq in SWA_GORDER] for hk in range(SWA_KV_HEADS)])
    sink_rows = jnp.repeat(sinks[order], w, axis=1).reshape(SWA_KV_HEADS, 1, SWA_GROUP * w)
    prev = lambda col: (lambda i: (jnp.maximum(i * nblk - 1, 0), col))
    return pl.pallas_call(
        functools.partial(_swa_kernel, nblk=nblk),
        out_shape=jax.ShapeDtypeStruct((s, SWA_OUT), BF16),
        grid=(s // rb,),
        in_specs=[pl.BlockSpec((rb, SWA_OUT), lambda i: (i, P_SWQ // SWA_OUT)),
                  pl.BlockSpec((rb, SWA_KV), lambda i: (i, P_SWK // SWA_KV)),
                  pl.BlockSpec((rb, SWA_KV), lambda i: (i, P_SWV // SWA_KV)),
                  pl.BlockSpec((w, SWA_KV), prev(P_SWK // SWA_KV)),
                  pl.BlockSpec((w, SWA_KV), prev(P_SWV // SWA_KV)),
                  pl.BlockSpec(sink_rows.shape, lambda i: (0, 0, 0))],
        out_specs=pl.BlockSpec((rb, SWA_OUT), lambda i: (i, 0)),
        compiler_params=_params(("parallel",)),
        name="swa",
    )(p, p, p, p, p, sink_rows)


def _outproj_kernel(oa_ref, ob_ref, oc_ref, w_ref, x_ref, pn_ref, g1_ref, fn_ref, sc_ref, sh_ref,
                    x1_ref, h2_ref, *, sub):
    a_w, b_w = oa_ref.shape[1], ob_ref.shape[1]
    for r0 in range(0, x_ref.shape[0], sub):
        rows = slice(r0, r0 + sub)
        mix = (_dot(oa_ref[rows, :], w_ref[0:a_w, :]) + _dot(ob_ref[rows, :], w_ref[a_w:a_w + b_w, :])
               + _dot(oc_ref[rows, :], w_ref[a_w + b_w:, :]))
        x1 = x_ref[rows, :] + g1_ref[...] * (_rms(mix) * pn_ref[...])
        x1_ref[rows, :] = x1
        h2_ref[rows, :] = _norm_mod(x1, fn_ref[...], sc_ref[...], sh_ref[...]).astype(BF16)


def _out_proj(o_a, o_b, o_c, w_out, layer, x, post_norm, gate1, ffn_norm, scale2, shift2):
    s, d = x.shape
    tm = min(512, s)
    row = lambda i: (i, 0)
    vec = pl.BlockSpec((1, d), lambda i: (0, 0))
    return pl.pallas_call(
        functools.partial(_outproj_kernel, sub=min(256, tm)),
        out_shape=(jax.ShapeDtypeStruct((s, d), F32), jax.ShapeDtypeStruct((s, d), BF16)),
        grid=(s // tm,),
        in_specs=[pl.BlockSpec((tm, o_a.shape[1]), row), pl.BlockSpec((tm, o_b.shape[1]), row),
                  pl.BlockSpec((tm, o_c.shape[1]), row),
                  pl.BlockSpec((None,) + w_out.shape[1:], lambda i: (layer, 0, 0), pipeline_mode=pl.Buffered(1)),
                  pl.BlockSpec((tm, d), row), vec, vec, vec, vec, vec],
        out_specs=(pl.BlockSpec((tm, d), row), pl.BlockSpec((tm, d), row)),
        compiler_params=_params(("parallel",)),
        name="out_proj",
    )(o_a, o_b, o_c, w_out, x, post_norm, gate1, ffn_norm, scale2, shift2)


def _gelu_tanh(x):
    return 0.5 * x * (1.0 + jnp.tanh(math.sqrt(2.0 / math.pi) * (x + 0.044715 * (x * x * x))))


def _ffn_up_kernel(h_ref, wg_ref, wu_ref, cg_ref, cu_ref, bg_ref, bu_ref, o_ref,
                   wg16_ref, wu16_ref, xg_ref, xu_ref, *, tm):
    @pl.when(pl.program_id(1) == 0)
    def _():
        wg16_ref[...] = wg_ref[...].astype(BF16)
        wu16_ref[...] = wu_ref[...].astype(BF16)
        xg_ref[0:8, :] = jnp.zeros((8, xg_ref.shape[1]), F32)
        xu_ref[0:8, :] = jnp.zeros((8, xu_ref.shape[1]), F32)

    h = h_ref[...]

    def conv(x_ref, w_ref, cw_ref, b_ref):
        x_ref[8:8 + tm, :] = _dot(h, w_ref[...])
        y = b_ref[...] + cw_ref[2:3, :] * x_ref[8:8 + tm, :]
        y = y + cw_ref[1:2, :] * x_ref[7:7 + tm, :]
        y = y + cw_ref[0:1, :] * x_ref[6:6 + tm, :]
        x_ref[0:8, :] = x_ref[tm:tm + 8, :]
        return y

    gate = conv(xg_ref, wg16_ref, cg_ref, bg_ref)
    up = conv(xu_ref, wu16_ref, cu_ref, bu_ref)
    o_ref[...] = (_gelu_tanh(gate) * up).astype(BF16)


def _ffn_up(h2, w_up, layer, conv_w, conv_b):
    s, d = h2.shape
    d_ff = w_up.shape[2] // 2
    tm, tn = min(1024, s), 512
    nj = d_ff // tn
    lo = lambda j, i: (0, j)
    hi = lambda j, i: (0, j + nj)
    w_lo = pl.BlockSpec((None, d, tn), lambda j, i: (layer, 0, j))
    w_hi = pl.BlockSpec((None, d, tn), lambda j, i: (layer, 0, j + nj))
    return pl.pallas_call(
        functools.partial(_ffn_up_kernel, tm=tm),
        out_shape=jax.ShapeDtypeStruct((s, d_ff), BF16),
        grid=(nj, s // tm),
        in_specs=[pl.BlockSpec((tm, d), lambda j, i: (i, 0)),
                  w_lo, w_hi,
                  pl.BlockSpec((FFN_CONV, tn), lo), pl.BlockSpec((FFN_CONV, tn), hi),
                  pl.BlockSpec((1, tn), lo), pl.BlockSpec((1, tn), hi)],
        out_specs=pl.BlockSpec((tm, tn), lambda j, i: (i, j)),
        scratch_shapes=[pltpu.VMEM((d, tn), BF16), pltpu.VMEM((d, tn), BF16),
                        pltpu.VMEM((tm + 8, tn), F32), pltpu.VMEM((tm + 8, tn), F32)],
        compiler_params=_params(("parallel", "arbitrary")),
        name="ffn_up",
    )(h2, w_up, w_up, conv_w, conv_w, conv_b, conv_b)


def _ffn_down_kernel(g_ref, w_ref, x_ref, pn_ref, g2_ref, *rest):
    y = _dot(g_ref[...], w_ref[...])
    x2 = x_ref[...] + g2_ref[...] * (_rms(y) * pn_ref[...])
    if len(rest) == 1:
        rest[0][...] = x2
    else:
        nn_ref, sc_ref, sh_ref, x2_ref, hn_ref = rest
        x2_ref[...] = x2
        hn_ref[...] = _norm_mod(x2, nn_ref[...], sc_ref[...], sh_ref[...]).astype(BF16)


def _ffn_down(g, w_down, layer, x1, post_norm, gate2, next_pre_norm=None):
    s, d = x1.shape
    d_ff = g.shape[1]
    tm = min(256, s)
    row = lambda i: (i, 0)
    vec = pl.BlockSpec((1, d), lambda i: (0, 0))
    tile_f32 = jax.ShapeDtypeStruct((s, d), F32)
    tile = pl.BlockSpec((tm, d), row)
    last = next_pre_norm is None
    return pl.pallas_call(
        _ffn_down_kernel,
        out_shape=tile_f32 if last else (tile_f32, jax.ShapeDtypeStruct((s, d), BF16)),
        grid=(s // tm,),
        in_specs=[pl.BlockSpec((tm, d_ff), row),
                  pl.BlockSpec((None, d_ff, d), lambda i: (layer, 0, 0), pipeline_mode=pl.Buffered(1)),
                  tile, vec, vec] + ([] if last else [vec, vec, vec]),
        out_specs=tile if last else (tile, tile),
        compiler_params=_params(("parallel",)),
        name="ffn_down",
    )(g, w_down, x1, post_norm, gate2, *(() if last else next_pre_norm))


def _layout_w_in(w_in):
    depth, d, _ = w_in.shape
    sizes = (GDN_QK, GDN_QK, GDN_V, GDN_V, GDN_HEADS, GDN_HEADS, MLA_Q_RANK, MLA_KV_RANK, MLA_ROPE,
             SWA_OUT, SWA_KV, SWA_KV)
    offs = np.concatenate([[0], np.cumsum(sizes)])
    part = lambda n: w_in[:, :, offs[n]:offs[n + 1]].astype(BF16)
    zeros = lambda n: jnp.zeros((depth, d, n), BF16)
    w_p = jnp.concatenate([w_in[:, :, :offs[4]].astype(BF16),
                           part(6), zeros(MLA_Q_RANK_PAD - MLA_Q_RANK),
                           part(9), part(7), part(8), zeros(LANES - MLA_ROPE), part(10), part(11)], axis=2)
    w_ab = jnp.concatenate([part(4), part(5), zeros(LANES - 2 * GDN_HEADS)], axis=2)
    return w_p, w_ab


def _layout_w_out(w_out):
    depth, d_mix, d = w_out.shape
    a = d_mix - SWA_OUT
    wc = w_out[:, a:].reshape(depth, SWA_KV_HEADS, SWA_GROUP, SWA_DH, d)[:, :, np.asarray(SWA_GORDER)]
    wc = wc.transpose(0, 2, 1, 3, 4).reshape(depth, SWA_OUT, d)
    return jnp.concatenate([w_out[:, :a].astype(BF16), wc.astype(BF16)], axis=1)


def _layout_mla(q_norm, w_uq, kv_norm, w_ukv):
    dqk = MLA_NOPE + MLA_ROPE
    uq = w_uq.reshape(MLA_Q_RANK, MLA_HEADS, dqk)
    uq = jnp.concatenate([uq[:, :, :MLA_NOPE].reshape(MLA_Q_RANK, -1), uq[:, :, MLA_NOPE:].reshape(MLA_Q_RANK, -1)],
                         axis=1)
    uq = jnp.pad(uq, ((0, MLA_Q_RANK_PAD - MLA_Q_RANK), (0, 0))).astype(BF16)
    qn = jnp.pad(q_norm, (0, MLA_Q_RANK_PAD - MLA_Q_RANK)).reshape(1, MLA_Q_RANK_PAD)
    ukv = w_ukv.reshape(MLA_KV_RANK, MLA_HEADS, MLA_NOPE + MLA_DV)
    uk = ukv[:, :, :MLA_NOPE].reshape(MLA_KV_RANK, -1).astype(BF16)
    uvt = ukv[:, :, MLA_NOPE:].reshape(MLA_KV_RANK, -1).T.astype(BF16)
    return qn, uq, kv_norm.reshape(1, MLA_KV_RANK), uk, uvt


def kernel(x, c, positions, ada_w, ada_b, mix_pre_norm, mix_post_norm, w_in, w_out, gdn_conv, gdn_a_log, gdn_dt_bias, gdn_norm, mla_q_norm, mla_w_uq, mla_kv_norm, mla_w_ukv, swa_sinks, ffn_pre_norm, ffn_post_norm, ffn_w_up, ffn_conv, ffn_conv_b, ffn_w_down):
    batch, s, d = x.shape
    assert batch == 1, "kernels are written for a single sequence"
    depth = ada_w.shape[0]
    xs = x.reshape(s, d)
    mod = _adaln_mod(c, ada_w, ada_b).reshape(depth, N_MOD, 1, d)
    cos, sin = _rope_tables(positions)
    vec = lambda a: a.reshape(1, d)

    w_p, w_ab = _layout_w_in(w_in)
    w_o = _layout_w_out(w_out)
    w_down = ffn_w_down.astype(BF16)

    h = _pre_norm(xs, vec(mix_pre_norm[0]), mod[0, 1], mod[0, 0])
    for l in range(depth):
        shift1, scale1, gate1, shift2, scale2, gate2 = (mod[l, n] for n in range(N_MOD))
        p, ab = _in_proj(h, w_p, w_ab, l, gdn_conv[l])

        gcb, gct = _gdn_gates(ab, gdn_a_log[l], gdn_dt_bias[l])
        u, w, qd, kt, qk = _gdn_local(p, gcb, gct)
        o_a = _gdn_scan(u, w, qd, kt, qk, gcb, p, gdn_norm[l])

        q_b, k_b, v_b = _mla_proj(p, cos, sin, *_layout_mla(mla_q_norm[l], mla_w_uq[l], mla_kv_norm[l],
                                                            mla_w_ukv[l]))
        o_b = _mla_attn(q_b, k_b, v_b)

        o_c = _swa(p, swa_sinks[l])

        x1, h2 = _out_proj(o_a, o_b, o_c, w_o, l, xs, vec(mix_post_norm[l]), gate1,
                           vec(ffn_pre_norm[l]), scale2, shift2)
        g = _ffn_up(h2, ffn_w_up, l, ffn_conv[l], ffn_conv_b[l].reshape(1, -1))
        if l + 1 < depth:
            xs, h = _ffn_down(g, w_down, l, x1, vec(ffn_post_norm[l]), gate2,
                              (vec(mix_pre_norm[l + 1]), mod[l + 1, 1], mod[l + 1, 0]))
        else:
            xs = _ffn_down(g, w_down, l, x1, vec(ffn_post_norm[l]), gate2)
    return xs.reshape(batch, s, d)
```

```python
import functools
import math

import numpy as np
import jax
import jax.numpy as jnp
from jax import lax
from jax.experimental import pallas as pl
from jax.experimental.pallas import tpu as pltpu

F32 = jnp.float32
BF16 = jnp.bfloat16

EPS = 1e-6
N_MOD = 6
GDN_HEADS = 8
GDN_DK = 128
GDN_DV = 128
GDN_CONV = 4
GDN_CHUNK = 64
GDN_QK = GDN_HEADS * GDN_DK
GDN_V = GDN_HEADS * GDN_DV
MLA_HEADS = 4
MLA_Q_RANK = 448
MLA_Q_RANK_PAD = 512
MLA_KV_RANK = 128
MLA_NOPE = 128
MLA_ROPE = 64
MLA_DV = 128
MLA_QK_PAD = 256
MLA_BLOCK = 512
ROPE_THETA = 10000.0
SWA_HEADS = 8
SWA_KV_HEADS = 2
SWA_GROUP = SWA_HEADS // SWA_KV_HEADS
SWA_DH = 64
SWA_WINDOW = 128
SWA_OUT = SWA_HEADS * SWA_DH
SWA_KV = SWA_KV_HEADS * SWA_DH
SWA_GORDER = (0, 2, 1, 3)
FFN_CONV = 3

LANES = 128
GDN_GROUP = 256
VMEM_LIMIT_MB = 56

P_QKV = 0
P_Z = 3072
P_CQ = 4096
P_SWQ = 4608
P_CKV = 5120
P_KROPE = 5248
P_SWK = 5376
P_SWV = 5504
P_WIDTH = 5632


def _params(semantics):
    return pltpu.CompilerParams(dimension_semantics=semantics, vmem_limit_bytes=VMEM_LIMIT_MB << 20)


def _sigmoid(x):
    return 0.5 + 0.5 * jnp.tanh(0.5 * x)


def _silu(x):
    h = 0.5 * x
    return h + h * jnp.tanh(h)


def _softplus(x):
    return jnp.maximum(x, 0.0) + jnp.log(1.0 + jnp.exp(-jnp.abs(x)))


def _rms(x):
    return x * lax.rsqrt(jnp.mean(x * x, axis=-1, keepdims=True) + EPS)


def _dot(a, b):
    return jnp.dot(a, b, preferred_element_type=F32)


def _dot_nt(a, b):
    return lax.dot_general(a, b, (((1,), (1,)), ((), ())), preferred_element_type=F32)


def _dot_tn(a, b):
    return lax.dot_general(a, b, (((0,), (0,)), ((), ())), preferred_element_type=F32)


def _mod_kernel(c_ref, w_ref, b_ref, o_ref, *, d, kc):
    cact = _silu(c_ref[...])
    acc = b_ref[0]
    for k0 in range(0, d, kc):
        acc = acc + jnp.sum(w_ref[0, k0:k0 + kc, :] * cact[k0:k0 + kc], axis=0, keepdims=True)
    o_ref[0] = acc


def _adaln_mod(c, ada_w, ada_b):
    depth, d, n = ada_w.shape
    tn = 1024
    return pl.pallas_call(
        functools.partial(_mod_kernel, d=d, kc=256),
        out_shape=jax.ShapeDtypeStruct((depth, 1, n), F32),
        grid=(depth, n // tn),
        in_specs=[pl.BlockSpec((d, 1), lambda l, j: (0, 0)),
                  pl.BlockSpec((1, d, tn), lambda l, j: (l, 0, j)),
                  pl.BlockSpec((1, 1, tn), lambda l, j: (l, 0, j))],
        out_specs=pl.BlockSpec((1, 1, tn), lambda l, j: (l, 0, j)),
        compiler_params=_params(("parallel", "parallel")),
        name="adaln_mod",
    )(c.reshape(d, 1), ada_w, ada_b.reshape(depth, 1, n))


def _norm_mod(x, w, scale, shift):
    return _rms(x) * w * (1.0 + scale) + shift


def _norm_mod_kernel(x_ref, w_ref, sc_ref, sh_ref, h_ref):
    h_ref[...] = _norm_mod(x_ref[...], w_ref[...], sc_ref[...], sh_ref[...]).astype(BF16)


def _pre_norm(x, w, scale, shift):
    s, d = x.shape
    tm = min(512, s)
    row = pl.BlockSpec((1, d), lambda i: (0, 0))
    return pl.pallas_call(
        _norm_mod_kernel,
        out_shape=jax.ShapeDtypeStruct((s, d), BF16),
        grid=(s // tm,),
        in_specs=[pl.BlockSpec((tm, d), lambda i: (i, 0)), row, row, row],
        out_specs=pl.BlockSpec((tm, d), lambda i: (i, 0)),
        compiler_params=_params(("parallel",)),
        name="pre_norm",
    )(x, w, scale, shift)


def _inproj_kernel(h_ref, w_ref, wab_ref, p_ref, ab_ref):
    h = h_ref[...]
    p_ref[...] = _dot(h, w_ref[...]).astype(BF16)

    @pl.when(pl.program_id(1) == 0)
    def _():
        ab_ref[...] = _dot(h, wab_ref[...])


def _in_proj(h, w_p, w_ab, layer):
    s, d = h.shape
    n = w_p.shape[2]
    tm, tn = min(2048, s), 512
    return pl.pallas_call(
        _inproj_kernel,
        out_shape=(jax.ShapeDtypeStruct((s, n), BF16), jax.ShapeDtypeStruct((s, LANES), F32)),
        grid=(s // tm, n // tn),
        in_specs=[pl.BlockSpec((tm, d), lambda i, j: (i, 0)),
                  pl.BlockSpec((None, d, tn), lambda i, j: (layer, 0, j)),
                  pl.BlockSpec((None, d, LANES), lambda i, j: (layer, 0, 0))],
        out_specs=(pl.BlockSpec((tm, tn), lambda i, j: (i, j)),
                   pl.BlockSpec((tm, LANES), lambda i, j: (i, 0))),
        compiler_params=_params(("parallel", "arbitrary")),
        name="in_proj",
    )(h, w_p, w_ab)


def _gdn_prep_kernel(x_ref, halo_ref, cw_ref, ab_ref, alog_ref, dtb_ref, tri_ref,
                     q_ref, k_ref, v_ref, gcb_ref, gct_ref, xp_ref, *, tm):
    i = pl.program_id(0)
    outs = (q_ref, k_ref, v_ref)
    for grp in range(3):
        c0 = grp * GDN_QK
        halo = halo_ref[:, c0:c0 + GDN_QK].astype(F32)
        xp_ref[0:8, :] = jnp.where(i > 0, halo, 0.0)
        xp_ref[8:8 + tm, :] = x_ref[:, c0:c0 + GDN_QK].astype(F32)
        y = cw_ref[3:4, c0:c0 + GDN_QK] * xp_ref[8:8 + tm, :]
        for j in range(GDN_CONV - 1):
            off = 8 - (GDN_CONV - 1) + j
            y = y + cw_ref[j:j + 1, c0:c0 + GDN_QK] * xp_ref[off:off + tm, :]
        y = _silu(y)
        for h in range(GDN_HEADS):
            yh = y[:, h * GDN_DK:(h + 1) * GDN_DK]
            if grp < 2:
                yh = yh * lax.rsqrt(jnp.sum(yh * yh, axis=-1, keepdims=True) + EPS)
            if grp == 0:
                yh = yh * (GDN_DK ** -0.5)
            outs[grp][:, h * GDN_DK:(h + 1) * GDN_DK] = yh.astype(BF16)

    ab = ab_ref[...]
    g = -jnp.exp(alog_ref[...]) * _softplus(ab + dtb_ref[...])
    gc = jnp.dot(tri_ref[...], g, preferred_element_type=F32, precision=lax.Precision.HIGHEST)
    lane = lax.broadcasted_iota(jnp.int32, (1, LANES), 1)
    gcb = jnp.where(lane < GDN_HEADS, gc, _sigmoid(ab))
    gcb_ref[...] = gcb
    gct_ref[...] = gcb.T[0:2 * GDN_HEADS, :]


def _gdn_prep(p, ab, conv_w, a_log, dt_bias):
    s = p.shape[0]
    tm = min(512, s)
    c3 = 3 * GDN_QK
    alog_row = jnp.zeros((1, LANES), F32).at[0, :GDN_HEADS].set(a_log)
    dtb_row = jnp.zeros((1, LANES), F32).at[0, :GDN_HEADS].set(dt_bias)
    r = np.arange(tm)
    tri = jnp.asarray(((r[:, None] >= r[None, :]) & (r[:, None] // GDN_CHUNK == r[None, :] // GDN_CHUNK))
                      .astype(np.float32))
    hb = tm // 8
    row = lambda i: (i, 0)
    return pl.pallas_call(
        functools.partial(_gdn_prep_kernel, tm=tm),
        out_shape=(jax.ShapeDtypeStruct((s, GDN_QK), BF16), jax.ShapeDtypeStruct((s, GDN_QK), BF16),
                   jax.ShapeDtypeStruct((s, GDN_V), BF16), jax.ShapeDtypeStruct((s, LANES), F32),
                   jax.ShapeDtypeStruct((2 * GDN_HEADS, s), F32)),
        grid=(s // tm,),
        in_specs=[pl.BlockSpec((tm, c3), row),
                  pl.BlockSpec((8, c3), lambda i: (jnp.maximum(i * hb - 1, 0), 0)),
                  pl.BlockSpec((GDN_CONV, c3), lambda i: (0, 0)),
                  pl.BlockSpec((tm, LANES), row),
                  pl.BlockSpec((1, LANES), lambda i: (0, 0)),
                  pl.BlockSpec((1, LANES), lambda i: (0, 0)),
                  pl.BlockSpec((tm, tm), lambda i: (0, 0))],
        out_specs=(pl.BlockSpec((tm, GDN_QK), row), pl.BlockSpec((tm, GDN_QK), row),
                   pl.BlockSpec((tm, GDN_V), row), pl.BlockSpec((tm, LANES), row),
                   pl.BlockSpec((2 * GDN_HEADS, tm), lambda i: (0, i))),
        scratch_shapes=[pltpu.VMEM((tm + 8, GDN_QK), F32)],
        compiler_params=_params(("parallel",)),
        name="gdn_prep",
    )(p, p, conv_w, ab, alog_row, dtb_row, tri)


def _gdn_local_kernel(q_ref, k_ref, v_ref, gcb_ref, gct_ref,
                      u_ref, w_ref, qd_ref, kt_ref, qk_ref):
    n = GDN_GROUP
    c = GDN_CHUNK
    ri = lax.broadcasted_iota(jnp.int32, (n, n), 0)
    ci = lax.broadcasted_iota(jnp.int32, (n, n), 1)
    same_chunk = (ri // c) == (ci // c)
    incl = same_chunk & (ri >= ci)
    strict = same_chunk & (ri > ci)
    eye = (ri == ci).astype(F32)
    last_sel = (ci == (ri // c) * c + (c - 1)).astype(F32)
    gcb = gcb_ref[...]
    gc_last_all = jnp.dot(last_sel, gcb, preferred_element_type=F32, precision=lax.Precision.HIGHEST)
    heads = range(GDN_HEADS)
    sls = [slice(h * GDN_DK, (h + 1) * GDN_DK) for h in heads]
    lane_chunk = lax.broadcasted_iota(jnp.int32, (c, n), 1) // c

    def to_cat(m):
        out = m[0:c]
        for j in range(1, n // c):
            out = out + m[j * c:(j + 1) * c]
        return out

    def to_bd(m_cat):
        zero = jnp.zeros_like(m_cat)
        return jnp.concatenate([jnp.where(lane_chunk == j, m_cat, zero) for j in range(n // c)], axis=0)

    lmats, tcs, rhss = [], [], []
    for h in heads:
        sl = sls[h]
        q = q_ref[:, sl].astype(F32)
        k16 = k_ref[:, sl]
        k = k16.astype(F32)
        v = v_ref[:, sl].astype(F32)
        gc_col = gcb[:, h:h + 1]
        beta = gcb[:, GDN_HEADS + h:GDN_HEADS + h + 1]
        gc_row = gct_ref[h:h + 1, :]
        gc_last = gc_last_all[:, h:h + 1]
        diff = gc_col - gc_row
        decay = jnp.where(incl, jnp.exp(jnp.where(incl, diff, 0.0)), 0.0)
        kb = k * beta
        kk = _dot_nt(kb.astype(BF16), k16)
        lmat = jnp.where(strict, kk * decay, 0.0)
        lmats.append(lmat)
        tcs.append(to_cat(eye - jnp.where((ri // 2) == (ci // 2), lmat, 0.0)))
        eg = jnp.exp(gc_col)
        rhss.append(jnp.concatenate([v * beta, kb * eg], axis=-1).astype(BF16))
        qd_ref[:, sl] = (q * eg).astype(BF16)
        kt_ref[:, sl] = (k * jnp.exp(gc_last - gc_col)).astype(BF16)
        qk = _dot_nt(q_ref[:, sl], k16) * decay
        qk_c = jnp.concatenate([qk[j * c:(j + 1) * c, j * c:(j + 1) * c] for j in range(n // c)], axis=0)
        qk_ref[:, h * c:(h + 1) * c] = qk_c.astype(BF16)
    b = 2
    while b < c:
        off_diag = ((ri // (2 * b)) == (ci // (2 * b))) & ((ri // b) != (ci // b))
        t16s = [tcs[h].astype(BF16) for h in heads]
        ys = [_dot(t16s[h], jnp.where(off_diag, lmats[h], 0.0).astype(BF16)).astype(BF16) for h in heads]
        tcs = [tcs[h] - _dot(ys[h], to_bd(t16s[h])) for h in heads]
        b *= 2
    for h in heads:
        uw = _dot(to_bd(tcs[h].astype(BF16)), rhss[h])
        u_ref[:, sls[h]] = uw[:, :GDN_DV].astype(BF16)
        w_ref[:, sls[h]] = uw[:, GDN_DV:].astype(BF16)


def _gdn_local(q, k, v, gcb, gct):
    s = q.shape[0]
    n = GDN_GROUP
    row = lambda i: (i, 0)
    wide = pl.BlockSpec((n, GDN_QK), row)
    return pl.pallas_call(
        _gdn_local_kernel,
        out_shape=(jax.ShapeDtypeStruct((s, GDN_V), BF16), jax.ShapeDtypeStruct((s, GDN_QK), BF16),
                   jax.ShapeDtypeStruct((s, GDN_QK), BF16), jax.ShapeDtypeStruct((s, GDN_QK), BF16),
                   jax.ShapeDtypeStruct((s, GDN_HEADS * GDN_CHUNK), BF16)),
        grid=(s // n,),
        in_specs=[wide, wide, wide, pl.BlockSpec((n, LANES), row),
                  pl.BlockSpec((2 * GDN_HEADS, n), lambda i: (0, i))],
        out_specs=(wide, wide, wide, wide, pl.BlockSpec((n, GDN_HEADS * GDN_CHUNK), row)),
        compiler_params=_params(("parallel",)),
        name="gdn_local",
    )(q, k, v, gcb, gct)


def _gdn_scan_kernel(u_ref, w_ref, qd_ref, kt_ref, qk_ref, gcb_ref, z_ref, nw_ref, o_ref, st_ref, *, rb):
    c = GDN_CHUNK

    @pl.when(pl.program_id(0) == 0)
    def _():
        st_ref[...] = jnp.zeros_like(st_ref)

    nw = nw_ref[...]

    def chunk(ci, carry):
        r0 = pl.multiple_of(ci * c, c)
        rows = pl.ds(r0, c)
        gt_row = jnp.exp(gcb_ref[pl.ds(r0 + c - 1, 1), :])
        heads = range(GDN_HEADS)
        sls = [slice(h * GDN_DK, (h + 1) * GDN_DK) for h in heads]
        sts = [st_ref[h] for h in heads]
        r1s = [_dot(jnp.concatenate([w_ref[rows, sls[h]], qd_ref[rows, sls[h]]], axis=0),
                    sts[h].astype(BF16)) for h in heads]
        vns = [(u_ref[rows, sls[h]].astype(F32) - r1s[h][0:c]).astype(BF16) for h in heads]
        for h in heads:
            st_ref[h] = sts[h] * gt_row[:, h:h + 1] + _dot_tn(kt_ref[rows, sls[h]], vns[h])
        os_ = [r1s[h][c:2 * c] + _dot(qk_ref[rows, h * c:(h + 1) * c], vns[h]) for h in heads]
        for h in heads:
            o = _rms(os_[h]) * nw * _silu(z_ref[rows, sls[h]].astype(F32))
            o_ref[rows, sls[h]] = o.astype(BF16)
        return carry

    lax.fori_loop(0, rb // c, chunk, 0)


def _gdn_scan(u, w, qd, kt, qk, gcb, p, norm_w):
    s = u.shape[0]
    rb = min(512, s)
    row = lambda i: (i, 0)
    wide = pl.BlockSpec((rb, GDN_QK), row)
    return pl.pallas_call(
        functools.partial(_gdn_scan_kernel, rb=rb),
        out_shape=jax.ShapeDtypeStruct((s, GDN_V), BF16),
        grid=(s // rb,),
        in_specs=[wide, wide, wide, wide, pl.BlockSpec((rb, GDN_HEADS * GDN_CHUNK), row),
                  pl.BlockSpec((rb, LANES), row),
                  pl.BlockSpec((rb, GDN_V), lambda i: (i, P_Z // GDN_V)),
                  pl.BlockSpec((1, GDN_DV), lambda i: (0, 0))],
        out_specs=wide,
        scratch_shapes=[pltpu.VMEM((GDN_HEADS, GDN_DK, GDN_DV), F32)],
        compiler_params=_params(("arbitrary",)),
        name="gdn_scan",
    )(u, w, qd, kt, qk, gcb, p, norm_w.reshape(1, GDN_DV))


def _rope_table_kernel(pos_ref, inv_ref, cos_ref, sin_ref):
    ang = pos_ref[...].astype(F32) * inv_ref[...]
    lane = lax.broadcasted_iota(jnp.int32, ang.shape, 1)
    first_half = (lane % MLA_ROPE) < (MLA_ROPE // 2)
    cos_ref[...] = jnp.cos(ang)
    sin_ref[...] = jnp.where(first_half, -jnp.sin(ang), jnp.sin(ang))


def _rope_tables(positions):
    s = positions.shape[-1]
    tm = min(1024, s)
    half = MLA_ROPE // 2
    inv = (ROPE_THETA ** (-np.arange(half, dtype=np.float32) / half)).astype(np.float32)
    inv_row = jnp.asarray(np.tile(inv, LANES // half).reshape(1, LANES))
    return pl.pallas_call(
        _rope_table_kernel,
        out_shape=(jax.ShapeDtypeStruct((s, LANES), F32), jax.ShapeDtypeStruct((s, LANES), F32)),
        grid=(s // tm,),
        in_specs=[pl.BlockSpec((tm, 1), lambda i: (i, 0)), pl.BlockSpec((1, LANES), lambda i: (0, 0))],
        out_specs=(pl.BlockSpec((tm, LANES), lambda i: (i, 0)), pl.BlockSpec((tm, LANES), lambda i: (i, 0))),
        compiler_params=_params(("parallel",)),
        name="rope_tables",
    )(positions.reshape(s, 1), inv_row)


def _rope_apply(x, cos, sin_signed):
    width = x.shape[-1]
    half = MLA_ROPE // 2
    lane = lax.broadcasted_iota(jnp.int32, x.shape, 1)
    first_half = (lane % MLA_ROPE) < half
    swapped = jnp.where(first_half, pltpu.roll(x, width - half, 1), pltpu.roll(x, half, 1))
    return x * cos + swapped * sin_signed


def _mla_proj_kernel(cq_ref, ckv_ref, kr_ref, cos_ref, sin_ref, qn_ref, wuq_ref, kvn_ref, wuk_ref, wuvt_ref,
                     q_ref, k_ref, vt_ref):
    cos = cos_ref[...]
    sin = sin_ref[...]
    scale = (MLA_NOPE + MLA_ROPE) ** -0.5 * math.log2(math.e)
    cq = cq_ref[...].astype(F32)
    cqn = cq * lax.rsqrt(jnp.sum(cq * cq, axis=-1, keepdims=True) * (1.0 / MLA_Q_RANK) + EPS) * qn_ref[...]
    q = _dot(cqn.astype(BF16), wuq_ref[...])
    nope_w = MLA_HEADS * MLA_NOPE
    q_rope = _rope_apply(q[:, nope_w:], jnp.concatenate([cos, cos], -1), jnp.concatenate([sin, sin], -1))
    zeros = jnp.zeros((q.shape[0], MLA_QK_PAD - MLA_NOPE - MLA_ROPE), F32)
    ckvn = (_rms(ckv_ref[...].astype(F32)) * kvn_ref[...]).astype(BF16)
    k_nope = _dot(ckvn, wuk_ref[...])
    v_t = _dot_nt(wuvt_ref[...], ckvn)
    k_pe = _rope_apply(kr_ref[...].astype(F32), cos, sin)[:, :MLA_ROPE]
    for h in range(MLA_HEADS):
        qh = jnp.concatenate([q[:, h * MLA_NOPE:(h + 1) * MLA_NOPE],
                              q_rope[:, h * MLA_ROPE:(h + 1) * MLA_ROPE], zeros], axis=-1) * scale
        q_ref[h] = qh.astype(BF16)
        kh = jnp.concatenate([k_nope[:, h * MLA_NOPE:(h + 1) * MLA_NOPE], k_pe, zeros], axis=-1)
        k_ref[h] = kh.astype(BF16)
        vt_ref[h, 0] = v_t[h * MLA_DV:(h + 1) * MLA_DV, :].astype(BF16)


def _mla_proj(p, cos, sin, q_norm, w_uq, kv_norm, w_uk, w_uvt):
    s = p.shape[0]
    tm = min(MLA_BLOCK, s)
    const = lambda i: (0, 0)
    return pl.pallas_call(
        _mla_proj_kernel,
        out_shape=(jax.ShapeDtypeStruct((MLA_HEADS, s, MLA_QK_PAD), BF16),
                   jax.ShapeDtypeStruct((MLA_HEADS, s, MLA_QK_PAD), BF16),
                   jax.ShapeDtypeStruct((MLA_HEADS, s // tm, MLA_DV, tm), BF16)),
        grid=(s // tm,),
        in_specs=[pl.BlockSpec((tm, MLA_Q_RANK_PAD), lambda i: (i, P_CQ // MLA_Q_RANK_PAD)),
                  pl.BlockSpec((tm, LANES), lambda i: (i, P_CKV // LANES)),
                  pl.BlockSpec((tm, LANES), lambda i: (i, P_KROPE // LANES)),
                  pl.BlockSpec((tm, LANES), lambda i: (i, 0)),
                  pl.BlockSpec((tm, LANES), lambda i: (i, 0)),
                  pl.BlockSpec(q_norm.shape, const), pl.BlockSpec(w_uq.shape, const),
                  pl.BlockSpec(kv_norm.shape, const), pl.BlockSpec(w_uk.shape, const),
                  pl.BlockSpec(w_uvt.shape, const)],
        out_specs=(pl.BlockSpec((MLA_HEADS, tm, MLA_QK_PAD), lambda i: (0, i, 0)),
                   pl.BlockSpec((MLA_HEADS, tm, MLA_QK_PAD), lambda i: (0, i, 0)),
                   pl.BlockSpec((MLA_HEADS, 1, MLA_DV, tm), lambda i: (0, i, 0, 0))),
        compiler_params=_params(("parallel",)),
        name="mla_proj",
    )(p, p, p, cos, sin, q_norm, w_uq, kv_norm, w_uk, w_uvt)


def _mla_attn_kernel(q_ref, k_ref, vt_ref, o_ref, st0, st1, mb0, mb1, m_ref, l_ref, acc_ref, *, bk, nd):
    qi = pl.program_id(1)
    slots = ((st0, mb0), (st1, mb1))
    bq = nd * bk
    n0 = nd * qi
    tri = lax.broadcasted_iota(jnp.int32, (bk, bk), 0) <= lax.broadcasted_iota(jnp.int32, (bk, bk), 1)

    gq = bk // 2

    def scores(t, slot, c0, tri_mask):
        st_ref, mb_ref = slots[slot]
        r0 = pl.multiple_of(t * bk, bk)
        st = _dot_nt(k_ref[0, pl.ds(r0, bk), :], q_ref[0, c0:c0 + gq, :])
        if tri_mask is not None:
            st = jnp.where(tri_mask, st, -jnp.inf)
        st_ref[:, c0:c0 + gq] = st
        mb_ref[:, c0:c0 + gq] = jnp.max(st, axis=0, keepdims=True)

    def update(t, slot, c0):
        st_ref, mb_ref = slots[slot]
        cols = slice(c0, c0 + gq)
        m = m_ref[:, cols]
        m_new = jnp.maximum(m, mb_ref[:, cols])
        alpha = jnp.exp2(m - m_new)
        pexp = jnp.exp2(st_ref[:, cols] - m_new)
        l_ref[:, cols] = alpha * l_ref[:, cols] + jnp.sum(pexp, axis=0, keepdims=True)
        acc_ref[:, cols] = alpha * acc_ref[:, cols] + _dot(vt_ref[0, t], pexp.astype(BF16))
        m_ref[:, cols] = m_new

    def sweep(ts=None, ss=None, ds=None, tu=None, su=None, du=None):
        for c0 in range(0, bq, gq):
            g = c0 // bk
            if ts is not None and (ds is None or g >= ds):
                scores(ts, ss, c0, tri[:, c0 - g * bk:c0 - g * bk + gq] if g == ds else None)
            if tu is not None and (du is None or g >= du):
                update(tu, su, c0)

    m_ref[...] = jnp.full(m_ref.shape, -jnp.inf, F32)
    l_ref[...] = jnp.zeros(l_ref.shape, F32)
    acc_ref[...] = jnp.zeros(acc_ref.shape, F32)

    @pl.when(qi == 0)
    def _():
        sweep(ts=0, ss=0, ds=0)

    @pl.when(qi > 0)
    def _():
        sweep(ts=0, ss=0)

    def pair(j, carry):
        sweep(ts=2 * j + 1, ss=1, tu=2 * j, su=0)
        sweep(ts=2 * j + 2, ss=0, tu=2 * j + 1, su=1)
        return carry

    lax.fori_loop(0, n0 // 2 - 1, pair, 0)

    @pl.when(qi > 0)
    def _():
        sweep(ts=n0 - 1, ss=1, tu=n0 - 2, su=0)
        sweep(ts=n0, ss=0, ds=0, tu=n0 - 1, su=1)

    for d in range(1, nd):
        sweep(ts=n0 + d, ss=d % 2, ds=d, tu=n0 + d - 1, su=(d - 1) % 2, du=d - 1)
    sweep(tu=n0 + nd - 1, su=(nd - 1) % 2, du=nd - 1)
    o_ref[...] = (acc_ref[...] / l_ref[...]).T.astype(BF16)


def _mla_attn(q, k, vt):
    nh, s, _ = q.shape
    bk = vt.shape[-1]
    nd = 4 if s % (4 * bk) == 0 else 2
    bq = nd * bk
    assert s % bq == 0
    return pl.pallas_call(
        functools.partial(_mla_attn_kernel, bk=bk, nd=nd),
        out_shape=jax.ShapeDtypeStruct((s, nh * MLA_DV), BF16),
        grid=(nh, s // bq),
        in_specs=[pl.BlockSpec((1, bq, MLA_QK_PAD), lambda h, i: (h, i, 0)),
                  pl.BlockSpec((1, s, MLA_QK_PAD), lambda h, i: (h, 0, 0)),
                  pl.BlockSpec((1, s // bk, MLA_DV, bk), lambda h, i: (h, 0, 0, 0))],
        out_specs=pl.BlockSpec((bq, MLA_DV), lambda h, i: (i, h)),
        scratch_shapes=[pltpu.VMEM((bk, bq), F32), pltpu.VMEM((bk, bq), F32),
                        pltpu.VMEM((1, bq), F32), pltpu.VMEM((1, bq), F32),
                        pltpu.VMEM((1, bq), F32), pltpu.VMEM((1, bq), F32), pltpu.VMEM((MLA_DV, bq), F32)],
        compiler_params=_params(("parallel", "arbitrary")),
        name="mla_attn",
    )(q, k, vt)


def _swa_kernel(q_ref, k_ref, v_ref, kh_ref, vh_ref, sink_ref, o_ref, *, nblk):
    w = SWA_WINDOW
    g = SWA_GROUP
    i = pl.program_id(0)
    key = lax.broadcasted_iota(jnp.int32, (2 * w, g * w), 0)
    col = lax.broadcasted_iota(jnp.int32, (2 * w, g * w), 1)
    dist = col % w + w - key
    band = (dist >= 0) & (dist < w)
    first_band = band & ((key >= w) | (i > 0))
    dist_f = dist.astype(F32)
    biases = []
    for hk in range(SWA_KV_HEADS):
        slope = jnp.zeros((2 * w, g * w), F32)
        for gi, gq in enumerate(SWA_GORDER):
            slope = jnp.where(col // w == gi, 2.0 ** (-8.0 * (hk * g + gq + 1.0) / SWA_HEADS), slope)
        biases.append(-slope * dist_f)
    low = lax.broadcasted_iota(jnp.int32, (2 * w, 2 * SWA_DH), 1) < SWA_DH
    top = lax.broadcasted_iota(jnp.int32, (2 * SWA_DH, 1), 0) < SWA_DH
    for blk in range(nblk):
        rows = slice(blk * w, (blk + 1) * w)
        if blk == 0:
            k_prev, v_prev, valid = kh_ref[...], vh_ref[...], first_band
        else:
            k_prev, v_prev, valid = k_ref[(blk - 1) * w:blk * w, :], v_ref[(blk - 1) * w:blk * w, :], band
        kk = jnp.concatenate([k_prev, k_ref[rows, :]], axis=0).astype(F32)
        kk_sw = pltpu.roll(kk, SWA_DH, 1)
        vvt = jnp.concatenate([v_prev, v_ref[rows, :]], axis=0).astype(F32).T
        probs, inv_denoms = [], []
        for hk in range(SWA_KV_HEADS):
            c0 = hk * g * SWA_DH
            q2 = jnp.concatenate([q_ref[rows, c0:c0 + 2 * SWA_DH], q_ref[rows, c0 + 2 * SWA_DH:c0 + 4 * SWA_DH]],
                                 axis=0) * (SWA_DH ** -0.5)
            k_low = jnp.where(low, kk if hk == 0 else kk_sw, 0.0).astype(BF16)
            k_high = jnp.where(low, 0.0, kk_sw if hk == 0 else kk).astype(BF16)
            st = jnp.concatenate([_dot_nt(k_low, q2), _dot_nt(k_high, q2)], axis=1)
            st = jnp.where(valid, st + biases[hk], -jnp.inf)
            sink = sink_ref[hk]
            m = jnp.maximum(jnp.max(st, axis=0, keepdims=True), sink)
            pexp = jnp.exp(st - m)
            denom = jnp.sum(pexp, axis=0, keepdims=True) + jnp.exp(sink - m)
            probs.append(pexp.astype(BF16))
            inv_denoms.append(1.0 / denom)
        v0 = jnp.where(top, vvt, 0.0).astype(BF16)
        v1 = jnp.where(top, 0.0, vvt).astype(BF16)
        o_t = (_dot(v0, probs[0]) + _dot(v1, probs[1])) * jnp.where(top, inv_denoms[0], inv_denoms[1])
        for gi in range(g):
            o_ref[rows, gi * LANES:(gi + 1) * LANES] = o_t[:, gi * w:(gi + 1) * w].T.astype(BF16)


def _swa(p, sinks):
    s = p.shape[0]
    w = SWA_WINDOW
    rb = min(512, s)
    nblk = rb // w
    order = np.asarray([[hk * SWA_GROUP + gq for gq in SWA_GORDER] for hk in range(SWA_KV_HEADS)])
    sink_rows = jnp.repeat(sinks[order], w, axis=1).reshape(SWA_KV_HEADS, 1, SWA_GROUP * w)
    prev = lambda col: (lambda i: (jnp.maximum(i * nblk - 1, 0), col))
    return pl.pallas_call(
        functools.partial(_swa_kernel, nblk=nblk),
        out_shape=jax.ShapeDtypeStruct((s, SWA_OUT), BF16),
        grid=(s // rb,),
        in_specs=[pl.BlockSpec((rb, SWA_OUT), lambda i: (i, P_SWQ // SWA_OUT)),
                  pl.BlockSpec((rb, SWA_KV), lambda i: (i, P_SWK // SWA_KV)),
                  pl.BlockSpec((rb, SWA_KV), lambda i: (i, P_SWV // SWA_KV)),
                  pl.BlockSpec((w, SWA_KV), prev(P_SWK // SWA_KV)),
                  pl.BlockSpec((w, SWA_KV), prev(P_SWV // SWA_KV)),
                  pl.BlockSpec(sink_rows.shape, lambda i: (0, 0, 0))],
        out_specs=pl.BlockSpec((rb, SWA_OUT), lambda i: (i, 0)),
        compiler_params=_params(("parallel",)),
        name="swa",
    )(p, p, p, p, p, sink_rows)


def _outproj_kernel(oa_ref, ob_ref, oc_ref, w_ref, x_ref, pn_ref, g1_ref, fn_ref, sc_ref, sh_ref,
                    x1_ref, h2_ref, *, sub):
    a_w, b_w = oa_ref.shape[1], ob_ref.shape[1]
    for r0 in range(0, x_ref.shape[0], sub):
        rows = slice(r0, r0 + sub)
        mix = (_dot(oa_ref[rows, :], w_ref[0:a_w, :]) + _dot(ob_ref[rows, :], w_ref[a_w:a_w + b_w, :])
               + _dot(oc_ref[rows, :], w_ref[a_w + b_w:, :]))
        x1 = x_ref[rows, :] + g1_ref[...] * (_rms(mix) * pn_ref[...])
        x1_ref[rows, :] = x1
        h2_ref[rows, :] = _norm_mod(x1, fn_ref[...], sc_ref[...], sh_ref[...]).astype(BF16)


def _out_proj(o_a, o_b, o_c, w_out, layer, x, post_norm, gate1, ffn_norm, scale2, shift2):
    s, d = x.shape
    tm = min(512, s)
    row = lambda i: (i, 0)
    vec = pl.BlockSpec((1, d), lambda i: (0, 0))
    return pl.pallas_call(
        functools.partial(_outproj_kernel, sub=min(256, tm)),
        out_shape=(jax.ShapeDtypeStruct((s, d), F32), jax.ShapeDtypeStruct((s, d), BF16)),
        grid=(s // tm,),
        in_specs=[pl.BlockSpec((tm, o_a.shape[1]), row), pl.BlockSpec((tm, o_b.shape[1]), row),
                  pl.BlockSpec((tm, o_c.shape[1]), row),
                  pl.BlockSpec((None,) + w_out.shape[1:], lambda i: (layer, 0, 0), pipeline_mode=pl.Buffered(1)),
                  pl.BlockSpec((tm, d), row), vec, vec, vec, vec, vec],
        out_specs=(pl.BlockSpec((tm, d), row), pl.BlockSpec((tm, d), row)),
        compiler_params=_params(("parallel",)),
        name="out_proj",
    )(o_a, o_b, o_c, w_out, x, post_norm, gate1, ffn_norm, scale2, shift2)


def _gelu_tanh(x):
    return 0.5 * x * (1.0 + jnp.tanh(math.sqrt(2.0 / math.pi) * (x + 0.044715 * (x * x * x))))


def _ffn_up_kernel(h_ref, wg_ref, wu_ref, cg_ref, cu_ref, bg_ref, bu_ref, o_ref,
                   wg16_ref, wu16_ref, xg_ref, xu_ref, *, tm):
    @pl.when(pl.program_id(1) == 0)
    def _():
        wg16_ref[...] = wg_ref[...].astype(BF16)
        wu16_ref[...] = wu_ref[...].astype(BF16)
        xg_ref[0:8, :] = jnp.zeros((8, xg_ref.shape[1]), F32)
        xu_ref[0:8, :] = jnp.zeros((8, xu_ref.shape[1]), F32)

    h = h_ref[...]

    def conv(x_ref, w_ref, cw_ref, b_ref):
        x_ref[8:8 + tm, :] = _dot(h, w_ref[...])
        y = b_ref[...] + cw_ref[2:3, :] * x_ref[8:8 + tm, :]
        y = y + cw_ref[1:2, :] * x_ref[7:7 + tm, :]
        y = y + cw_ref[0:1, :] * x_ref[6:6 + tm, :]
        x_ref[0:8, :] = x_ref[tm:tm + 8, :]
        return y

    gate = conv(xg_ref, wg16_ref, cg_ref, bg_ref)
    up = conv(xu_ref, wu16_ref, cu_ref, bu_ref)
    o_ref[...] = (_gelu_tanh(gate) * up).astype(BF16)


def _ffn_up(h2, w_up, layer, conv_w, conv_b):
    s, d = h2.shape
    d_ff = w_up.shape[2] // 2
    tm, tn = min(1024, s), 512
    nj = d_ff // tn
    lo = lambda j, i: (0, j)
    hi = lambda j, i: (0, j + nj)
    w_lo = pl.BlockSpec((None, d, tn), lambda j, i: (layer, 0, j))
    w_hi = pl.BlockSpec((None, d, tn), lambda j, i: (layer, 0, j + nj))
    return pl.pallas_call(
        functools.partial(_ffn_up_kernel, tm=tm),
        out_shape=jax.ShapeDtypeStruct((s, d_ff), BF16),
        grid=(nj, s // tm),
        in_specs=[pl.BlockSpec((tm, d), lambda j, i: (i, 0)),
                  w_lo, w_hi,
                  pl.BlockSpec((FFN_CONV, tn), lo), pl.BlockSpec((FFN_CONV, tn), hi),
                  pl.BlockSpec((1, tn), lo), pl.BlockSpec((1, tn), hi)],
        out_specs=pl.BlockSpec((tm, tn), lambda j, i: (i, j)),
        scratch_shapes=[pltpu.VMEM((d, tn), BF16), pltpu.VMEM((d, tn), BF16),
                        pltpu.VMEM((tm + 8, tn), F32), pltpu.VMEM((tm + 8, tn), F32)],
        compiler_params=_params(("parallel", "arbitrary")),
        name="ffn_up",
    )(h2, w_up, w_up, conv_w, conv_w, conv_b, conv_b)


def _ffn_down_kernel(g_ref, w_ref, x_ref, pn_ref, g2_ref, *rest):
    y = _dot(g_ref[...], w_ref[...])
    x2 = x_ref[...] + g2_ref[...] * (_rms(y) * pn_ref[...])
    if len(rest) == 1:
        rest[0][...] = x2
    else:
        nn_ref, sc_ref, sh_ref, x2_ref, hn_ref = rest
        x2_ref[...] = x2
        hn_ref[...] = _norm_mod(x2, nn_ref[...], sc_ref[...], sh_ref[...]).astype(BF16)


def _ffn_down(g, w_down, layer, x1, post_norm, gate2, next_pre_norm=None):
    s, d = x1.shape
    d_ff = g.shape[1]
    tm = min(256, s)
    row = lambda i: (i, 0)
    vec = pl.BlockSpec((1, d), lambda i: (0, 0))
    tile_f32 = jax.ShapeDtypeStruct((s, d), F32)
    tile = pl.BlockSpec((tm, d), row)
    last = next_pre_norm is None
    return pl.pallas_call(
        _ffn_down_kernel,
        out_shape=tile_f32 if last else (tile_f32, jax.ShapeDtypeStruct((s, d), BF16)),
        grid=(s // tm,),
        in_specs=[pl.BlockSpec((tm, d_ff), row),
                  pl.BlockSpec((None, d_ff, d), lambda i: (layer, 0, 0), pipeline_mode=pl.Buffered(1)),
                  tile, vec, vec] + ([] if last else [vec, vec, vec]),
        out_specs=tile if last else (tile, tile),
        compiler_params=_params(("parallel",)),
        name="ffn_down",
    )(g, w_down, x1, post_norm, gate2, *(() if last else next_pre_norm))


def _layout_w_in(w_in):
    depth, d, _ = w_in.shape
    sizes = (GDN_QK, GDN_QK, GDN_V, GDN_V, GDN_HEADS, GDN_HEADS, MLA_Q_RANK, MLA_KV_RANK, MLA_ROPE,
             SWA_OUT, SWA_KV, SWA_KV)
    offs = np.concatenate([[0], np.cumsum(sizes)])
    part = lambda n: w_in[:, :, offs[n]:offs[n + 1]].astype(BF16)
    zeros = lambda n: jnp.zeros((depth, d, n), BF16)
    w_p = jnp.concatenate([w_in[:, :, :offs[4]].astype(BF16),
                           part(6), zeros(MLA_Q_RANK_PAD - MLA_Q_RANK),
                           part(9), part(7), part(8), zeros(LANES - MLA_ROPE), part(10), part(11)], axis=2)
    w_ab = jnp.concatenate([part(4), part(5), zeros(LANES - 2 * GDN_HEADS)], axis=2)
    return w_p, w_ab


def _layout_w_out(w_out):
    depth, d_mix, d = w_out.shape
    a = d_mix - SWA_OUT
    wc = w_out[:, a:].reshape(depth, SWA_KV_HEADS, SWA_GROUP, SWA_DH, d)[:, :, np.asarray(SWA_GORDER)]
    wc = wc.transpose(0, 2, 1, 3, 4).reshape(depth, SWA_OUT, d)
    return jnp.concatenate([w_out[:, :a].astype(BF16), wc.astype(BF16)], axis=1)


def _layout_mla(q_norm, w_uq, kv_norm, w_ukv):
    dqk = MLA_NOPE + MLA_ROPE
    uq = w_uq.reshape(MLA_Q_RANK, MLA_HEADS, dqk)
    uq = jnp.concatenate([uq[:, :, :MLA_NOPE].reshape(MLA_Q_RANK, -1), uq[:, :, MLA_NOPE:].reshape(MLA_Q_RANK, -1)],
                         axis=1)
    uq = jnp.pad(uq, ((0, MLA_Q_RANK_PAD - MLA_Q_RANK), (0, 0))).astype(BF16)
    qn = jnp.pad(q_norm, (0, MLA_Q_RANK_PAD - MLA_Q_RANK)).reshape(1, MLA_Q_RANK_PAD)
    ukv = w_ukv.reshape(MLA_KV_RANK, MLA_HEADS, MLA_NOPE + MLA_DV)
    uk = ukv[:, :, :MLA_NOPE].reshape(MLA_KV_RANK, -1).astype(BF16)
    uvt = ukv[:, :, MLA_NOPE:].reshape(MLA_KV_RANK, -1).T.astype(BF16)
    return qn, uq, kv_norm.reshape(1, MLA_KV_RANK), uk, uvt


def kernel(x, c, positions, ada_w, ada_b, mix_pre_norm, mix_post_norm, w_in, w_out, gdn_conv, gdn_a_log, gdn_dt_bias, gdn_norm, mla_q_norm, mla_w_uq, mla_kv_norm, mla_w_ukv, swa_sinks, ffn_pre_norm, ffn_post_norm, ffn_w_up, ffn_conv, ffn_conv_b, ffn_w_down):
    batch, s, d = x.shape
    assert batch == 1, "kernels are written for a single sequence"
    depth = ada_w.shape[0]
    xs = x.reshape(s, d)
    mod = _adaln_mod(c, ada_w, ada_b).reshape(depth, N_MOD, 1, d)
    cos, sin = _rope_tables(positions)
    vec = lambda a: a.reshape(1, d)

    w_p, w_ab = _layout_w_in(w_in)
    w_o = _layout_w_out(w_out)
    w_down = ffn_w_down.astype(BF16)

    h = _pre_norm(xs, vec(mix_pre_norm[0]), mod[0, 1], mod[0, 0])
    for l in range(depth):
        shift1, scale1, gate1, shift2, scale2, gate2 = (mod[l, n] for n in range(N_MOD))
        p, ab = _in_proj(h, w_p, w_ab, l)

        q_a, k_a, v_a, gcb, gct = _gdn_prep(p, ab, gdn_conv[l], gdn_a_log[l], gdn_dt_bias[l])
        u, w, qd, kt, qk = _gdn_local(q_a, k_a, v_a, gcb, gct)
        o_a = _gdn_scan(u, w, qd, kt, qk, gcb, p, gdn_norm[l])

        q_b, k_b, v_b = _mla_proj(p, cos, sin, *_layout_mla(mla_q_norm[l], mla_w_uq[l], mla_kv_norm[l],
                                                            mla_w_ukv[l]))
        o_b = _mla_attn(q_b, k_b, v_b)

        o_c = _swa(p, swa_sinks[l])

        x1, h2 = _out_proj(o_a, o_b, o_c, w_o, l, xs, vec(mix_post_norm[l]), gate1,
                           vec(ffn_pre_norm[l]), scale2, shift2)
        g = _ffn_up(h2, ffn_w_up, l, ffn_conv[l], ffn_conv_b[l].reshape(1, -1))
        if l + 1 < depth:
            xs, h = _ffn_down(g, w_down, l, x1, vec(ffn_post_norm[l]), gate2,
                              (vec(mix_pre_norm[l + 1]), mod[l + 1, 1], mod[l + 1, 0]))
        else:
            xs = _ffn_down(g, w_down, l, x1, vec(ffn_post_norm[l]), gate2)
    return xs.reshape(batch, s, d)
```

```python
import functools
import math

import numpy as np
import jax
import jax.numpy as jnp
from jax import lax
from jax.experimental import pallas as pl
from jax.experimental.pallas import tpu as pltpu

F32 = jnp.float32
BF16 = jnp.bfloat16

EPS = 1e-6
N_MOD = 6
GDN_HEADS = 8
GDN_DK = 128
GDN_DV = 128
GDN_CONV = 4
GDN_CHUNK = 64
GDN_QK = GDN_HEADS * GDN_DK
GDN_V = GDN_HEADS * GDN_DV
MLA_HEADS = 4
MLA_Q_RANK = 448
MLA_Q_RANK_PAD = 512
MLA_KV_RANK = 128
MLA_NOPE = 128
MLA_ROPE = 64
MLA_DV = 128
MLA_QK_PAD = 256
MLA_BLOCK = 512
ROPE_THETA = 10000.0
SWA_HEADS = 8
SWA_KV_HEADS = 2
SWA_GROUP = SWA_HEADS // SWA_KV_HEADS
SWA_DH = 64
SWA_WINDOW = 128
SWA_OUT = SWA_HEADS * SWA_DH
SWA_KV = SWA_KV_HEADS * SWA_DH
SWA_GORDER = (0, 2, 1, 3)
FFN_CONV = 3

LANES = 128
GDN_GROUP = 256
VMEM_LIMIT_MB = 56

P_QKV = 0
P_Z = 3072
P_CQ = 4096
P_SWQ = 4608
P_CKV = 5120
P_KROPE = 5248
P_SWK = 5376
P_SWV = 5504
P_WIDTH = 5632


def _params(semantics):
    return pltpu.CompilerParams(dimension_semantics=semantics, vmem_limit_bytes=VMEM_LIMIT_MB << 20)


def _sigmoid(x):
    return 0.5 + 0.5 * jnp.tanh(0.5 * x)


def _silu(x):
    h = 0.5 * x
    return h + h * jnp.tanh(h)


def _softplus(x):
    return jnp.maximum(x, 0.0) + jnp.log(1.0 + jnp.exp(-jnp.abs(x)))


def _rms(x):
    return x * lax.rsqrt(jnp.mean(x * x, axis=-1, keepdims=True) + EPS)


def _dot(a, b):
    return jnp.dot(a, b, preferred_element_type=F32)


def _dot_nt(a, b):
    return lax.dot_general(a, b, (((1,), (1,)), ((), ())), preferred_element_type=F32)


def _dot_tn(a, b):
    return lax.dot_general(a, b, (((0,), (0,)), ((), ())), preferred_element_type=F32)


def _mod_kernel(c_ref, w_ref, b_ref, o_ref, *, d, kc):
    cact = _silu(c_ref[...])
    acc = b_ref[0]
    for k0 in range(0, d, kc):
        acc = acc + jnp.sum(w_ref[0, k0:k0 + kc, :] * cact[k0:k0 + kc], axis=0, keepdims=True)
    o_ref[0] = acc


def _adaln_mod(c, ada_w, ada_b):
    depth, d, n = ada_w.shape
    tn = 1024
    return pl.pallas_call(
        functools.partial(_mod_kernel, d=d, kc=256),
        out_shape=jax.ShapeDtypeStruct((depth, 1, n), F32),
        grid=(depth, n // tn),
        in_specs=[pl.BlockSpec((d, 1), lambda l, j: (0, 0)),
                  pl.BlockSpec((1, d, tn), lambda l, j: (l, 0, j)),
                  pl.BlockSpec((1, 1, tn), lambda l, j: (l, 0, j))],
        out_specs=pl.BlockSpec((1, 1, tn), lambda l, j: (l, 0, j)),
        compiler_params=_params(("parallel", "parallel")),
        name="adaln_mod",
    )(c.reshape(d, 1), ada_w, ada_b.reshape(depth, 1, n))


def _norm_mod(x, w, scale, shift):
    return _rms(x) * w * (1.0 + scale) + shift


def _norm_mod_kernel(x_ref, w_ref, sc_ref, sh_ref, h_ref):
    h_ref[...] = _norm_mod(x_ref[...], w_ref[...], sc_ref[...], sh_ref[...]).astype(BF16)


def _pre_norm(x, w, scale, shift):
    s, d = x.shape
    tm = min(1024, s)
    row = pl.BlockSpec((1, d), lambda i: (0, 0))
    return pl.pallas_call(
        _norm_mod_kernel,
        out_shape=jax.ShapeDtypeStruct((s, d), BF16),
        grid=(s // tm,),
        in_specs=[pl.BlockSpec((tm, d), lambda i: (i, 0)), row, row, row],
        out_specs=pl.BlockSpec((tm, d), lambda i: (i, 0)),
        compiler_params=_params(("parallel",)),
        name="pre_norm",
    )(x, w, scale, shift)


def _inproj_kernel(h_ref, w_ref, wab_ref, p_ref, ab_ref):
    h = h_ref[...]
    p_ref[...] = _dot(h, w_ref[...]).astype(BF16)

    @pl.when(pl.program_id(1) == 0)
    def _():
        ab_ref[...] = _dot(h, wab_ref[...])


def _in_proj(h, w_p, w_ab, layer):
    s, d = h.shape
    n = w_p.shape[2]
    tm, tn = min(2048, s), 512
    return pl.pallas_call(
        _inproj_kernel,
        out_shape=(jax.ShapeDtypeStruct((s, n), BF16), jax.ShapeDtypeStruct((s, LANES), F32)),
        grid=(s // tm, n // tn),
        in_specs=[pl.BlockSpec((tm, d), lambda i, j: (i, 0)),
                  pl.BlockSpec((None, d, tn), lambda i, j: (layer, 0, j)),
                  pl.BlockSpec((None, d, LANES), lambda i, j: (layer, 0, 0))],
        out_specs=(pl.BlockSpec((tm, tn), lambda i, j: (i, j)),
                   pl.BlockSpec((tm, LANES), lambda i, j: (i, 0))),
        compiler_params=_params(("parallel", "arbitrary")),
        name="in_proj",
    )(h, w_p, w_ab)


def _gdn_prep_kernel(x_ref, halo_ref, cw_ref, ab_ref, alog_ref, dtb_ref, tri_ref,
                     q_ref, k_ref, v_ref, gcb_ref, gct_ref, xp_ref, *, tm):
    i = pl.program_id(0)
    outs = (q_ref, k_ref, v_ref)
    for grp in range(3):
        c0 = grp * GDN_QK
        halo = halo_ref[:, c0:c0 + GDN_QK].astype(F32)
        xp_ref[0:8, :] = jnp.where(i > 0, halo, 0.0)
        xp_ref[8:8 + tm, :] = x_ref[:, c0:c0 + GDN_QK].astype(F32)
        y = cw_ref[3:4, c0:c0 + GDN_QK] * xp_ref[8:8 + tm, :]
        for j in range(GDN_CONV - 1):
            off = 8 - (GDN_CONV - 1) + j
            y = y + cw_ref[j:j + 1, c0:c0 + GDN_QK] * xp_ref[off:off + tm, :]
        y = _silu(y)
        for h in range(GDN_HEADS):
            yh = y[:, h * GDN_DK:(h + 1) * GDN_DK]
            if grp < 2:
                yh = yh * lax.rsqrt(jnp.sum(yh * yh, axis=-1, keepdims=True) + EPS)
            if grp == 0:
                yh = yh * (GDN_DK ** -0.5)
            outs[grp][:, h * GDN_DK:(h + 1) * GDN_DK] = yh.astype(BF16)

    ab = ab_ref[...]
    g = -jnp.exp(alog_ref[...]) * _softplus(ab + dtb_ref[...])
    gc = jnp.dot(tri_ref[...], g, preferred_element_type=F32, precision=lax.Precision.HIGHEST)
    lane = lax.broadcasted_iota(jnp.int32, (1, LANES), 1)
    gcb = jnp.where(lane < GDN_HEADS, gc, _sigmoid(ab))
    gcb_ref[...] = gcb
    gct_ref[...] = gcb.T[0:2 * GDN_HEADS, :]


def _gdn_prep(p, ab, conv_w, a_log, dt_bias):
    s = p.shape[0]
    tm = min(512, s)
    c3 = 3 * GDN_QK
    alog_row = jnp.zeros((1, LANES), F32).at[0, :GDN_HEADS].set(a_log)
    dtb_row = jnp.zeros((1, LANES), F32).at[0, :GDN_HEADS].set(dt_bias)
    r = np.arange(tm)
    tri = jnp.asarray(((r[:, None] >= r[None, :]) & (r[:, None] // GDN_CHUNK == r[None, :] // GDN_CHUNK))
                      .astype(np.float32))
    hb = tm // 8
    row = lambda i: (i, 0)
    return pl.pallas_call(
        functools.partial(_gdn_prep_kernel, tm=tm),
        out_shape=(jax.ShapeDtypeStruct((s, GDN_QK), BF16), jax.ShapeDtypeStruct((s, GDN_QK), BF16),
                   jax.ShapeDtypeStruct((s, GDN_V), BF16), jax.ShapeDtypeStruct((s, LANES), F32),
                   jax.ShapeDtypeStruct((2 * GDN_HEADS, s), F32)),
        grid=(s // tm,),
        in_specs=[pl.BlockSpec((tm, c3), row),
                  pl.BlockSpec((8, c3), lambda i: (jnp.maximum(i * hb - 1, 0), 0)),
                  pl.BlockSpec((GDN_CONV, c3), lambda i: (0, 0)),
                  pl.BlockSpec((tm, LANES), row),
                  pl.BlockSpec((1, LANES), lambda i: (0, 0)),
                  pl.BlockSpec((1, LANES), lambda i: (0, 0)),
                  pl.BlockSpec((tm, tm), lambda i: (0, 0))],
        out_specs=(pl.BlockSpec((tm, GDN_QK), row), pl.BlockSpec((tm, GDN_QK), row),
                   pl.BlockSpec((tm, GDN_V), row), pl.BlockSpec((tm, LANES), row),
                   pl.BlockSpec((2 * GDN_HEADS, tm), lambda i: (0, i))),
        scratch_shapes=[pltpu.VMEM((tm + 8, GDN_QK), F32)],
        compiler_params=_params(("parallel",)),
        name="gdn_prep",
    )(p, p, conv_w, ab, alog_row, dtb_row, tri)


def _gdn_local_kernel(q_ref, k_ref, v_ref, gcb_ref, gct_ref,
                      u_ref, w_ref, qd_ref, kt_ref, qk_ref, *, groups):
    n = GDN_GROUP
    c = GDN_CHUNK
    ri = lax.broadcasted_iota(jnp.int32, (n, n), 0)
    ci = lax.broadcasted_iota(jnp.int32, (n, n), 1)
    same_chunk = (ri // c) == (ci // c)
    incl = same_chunk & (ri >= ci)
    strict = same_chunk & (ri > ci)
    eye = (ri == ci).astype(F32)
    last_sel = (ci == (ri // c) * c + (c - 1)).astype(F32)
    heads = range(GDN_HEADS)
    sls = [slice(h * GDN_DK, (h + 1) * GDN_DK) for h in heads]
    lane_chunk = lax.broadcasted_iota(jnp.int32, (c, n), 1) // c
    units = [(g, h) for g in range(groups) for h in heads]

    def to_cat(m):
        out = m[0:c]
        for j in range(1, n // c):
            out = out + m[j * c:(j + 1) * c]
        return out

    def to_bd(m_cat):
        zero = jnp.zeros_like(m_cat)
        return jnp.concatenate([jnp.where(lane_chunk == j, m_cat, zero) for j in range(n // c)], axis=0)

    lmats, tcs, rhss = {}, {}, {}
    for g in range(groups):
        rows = slice(g * n, (g + 1) * n)
        gcb = gcb_ref[rows, :]
        gc_last_all = jnp.dot(last_sel, gcb, preferred_element_type=F32, precision=lax.Precision.HIGHEST)
        for h in heads:
            sl = sls[h]
            q = q_ref[rows, sl].astype(F32)
            k16 = k_ref[rows, sl]
            k = k16.astype(F32)
            v = v_ref[rows, sl].astype(F32)
            gc_col = gcb[:, h:h + 1]
            beta = gcb[:, GDN_HEADS + h:GDN_HEADS + h + 1]
            gc_row = gct_ref[h:h + 1, rows]
            gc_last = gc_last_all[:, h:h + 1]
            diff = gc_col - gc_row
            decay = jnp.where(incl, jnp.exp(jnp.where(incl, diff, 0.0)), 0.0)
            kb = k * beta
            kk = _dot_nt(kb.astype(BF16), k16)
            lmat = jnp.where(strict, kk * decay, 0.0)
            lmats[g, h] = lmat
            tcs[g, h] = to_cat(eye - jnp.where((ri // 2) == (ci // 2), lmat, 0.0))
            eg = jnp.exp(gc_col)
            rhss[g, h] = jnp.concatenate([v * beta, kb * eg], axis=-1).astype(BF16)
            qd_ref[rows, sl] = (q * eg).astype(BF16)
            kt_ref[rows, sl] = (k * jnp.exp(gc_last - gc_col)).astype(BF16)
            qk = _dot_nt(q_ref[rows, sl], k16) * decay
            qk_c = jnp.concatenate([qk[j * c:(j + 1) * c, j * c:(j + 1) * c] for j in range(n // c)], axis=0)
            qk_ref[rows, h * c:(h + 1) * c] = qk_c.astype(BF16)
    b = 2
    while b < c:
        off_diag = ((ri // (2 * b)) == (ci // (2 * b))) & ((ri // b) != (ci // b))
        t16s = {u: tcs[u].astype(BF16) for u in units}
        ys = {u: _dot(t16s[u], jnp.where(off_diag, lmats[u], 0.0).astype(BF16)).astype(BF16) for u in units}
        tcs = {u: tcs[u] - _dot(ys[u], to_bd(t16s[u])) for u in units}
        b *= 2
    for g, h in units:
        rows = slice(g * n, (g + 1) * n)
        uw = _dot(to_bd(tcs[g, h].astype(BF16)), rhss[g, h])
        u_ref[rows, sls[h]] = uw[:, :GDN_DV].astype(BF16)
        w_ref[rows, sls[h]] = uw[:, GDN_DV:].astype(BF16)


def _gdn_local(q, k, v, gcb, gct):
    s = q.shape[0]
    groups = 2 if s % (2 * GDN_GROUP) == 0 else 1
    n = groups * GDN_GROUP
    row = lambda i: (i, 0)
    wide = pl.BlockSpec((n, GDN_QK), row)
    return pl.pallas_call(
        functools.partial(_gdn_local_kernel, groups=groups),
        out_shape=(jax.ShapeDtypeStruct((s, GDN_V), BF16), jax.ShapeDtypeStruct((s, GDN_QK), BF16),
                   jax.ShapeDtypeStruct((s, GDN_QK), BF16), jax.ShapeDtypeStruct((s, GDN_QK), BF16),
                   jax.ShapeDtypeStruct((s, GDN_HEADS * GDN_CHUNK), BF16)),
        grid=(s // n,),
        in_specs=[wide, wide, wide, pl.BlockSpec((n, LANES), row),
                  pl.BlockSpec((2 * GDN_HEADS, n), lambda i: (0, i))],
        out_specs=(wide, wide, wide, wide, pl.BlockSpec((n, GDN_HEADS * GDN_CHUNK), row)),
        compiler_params=_params(("parallel",)),
        name="gdn_local",
    )(q, k, v, gcb, gct)


def _gdn_scan_kernel(u_ref, w_ref, qd_ref, kt_ref, qk_ref, gcb_ref, z_ref, nw_ref, o_ref, st_ref, *, rb):
    c = GDN_CHUNK

    @pl.when(pl.program_id(0) == 0)
    def _():
        st_ref[...] = jnp.zeros_like(st_ref)

    nw = nw_ref[...]

    def chunk(ci, carry):
        r0 = pl.multiple_of(ci * c, c)
        rows = pl.ds(r0, c)
        gt_row = jnp.exp(gcb_ref[pl.ds(r0 + c - 1, 1), :])
        heads = range(GDN_HEADS)
        sls = [slice(h * GDN_DK, (h + 1) * GDN_DK) for h in heads]
        sts = [st_ref[h] for h in heads]
        r1s = [_dot(jnp.concatenate([w_ref[rows, sls[h]], qd_ref[rows, sls[h]]], axis=0),
                    sts[h].astype(BF16)) for h in heads]
        vns = [(u_ref[rows, sls[h]].astype(F32) - r1s[h][0:c]).astype(BF16) for h in heads]
        for h in heads:
            st_ref[h] = sts[h] * gt_row[:, h:h + 1] + _dot_tn(kt_ref[rows, sls[h]], vns[h])
        os_ = [r1s[h][c:2 * c] + _dot(qk_ref[rows, h * c:(h + 1) * c], vns[h]) for h in heads]
        for h in heads:
            o = _rms(os_[h]) * nw * _silu(z_ref[rows, sls[h]].astype(F32))
            o_ref[rows, sls[h]] = o.astype(BF16)
        return carry

    lax.fori_loop(0, rb // c, chunk, 0)


def _gdn_scan(u, w, qd, kt, qk, gcb, p, norm_w):
    s = u.shape[0]
    rb = min(512, s)
    row = lambda i: (i, 0)
    wide = pl.BlockSpec((rb, GDN_QK), row)
    return pl.pallas_call(
        functools.partial(_gdn_scan_kernel, rb=rb),
        out_shape=jax.ShapeDtypeStruct((s, GDN_V), BF16),
        grid=(s // rb,),
        in_specs=[wide, wide, wide, wide, pl.BlockSpec((rb, GDN_HEADS * GDN_CHUNK), row),
                  pl.BlockSpec((rb, LANES), row),
                  pl.BlockSpec((rb, GDN_V), lambda i: (i, P_Z // GDN_V)),
                  pl.BlockSpec((1, GDN_DV), lambda i: (0, 0))],
        out_specs=wide,
        scratch_shapes=[pltpu.VMEM((GDN_HEADS, GDN_DK, GDN_DV), F32)],
        compiler_params=_params(("arbitrary",)),
        name="gdn_scan",
    )(u, w, qd, kt, qk, gcb, p, norm_w.reshape(1, GDN_DV))


def _rope_table_kernel(pos_ref, inv_ref, cos_ref, sin_ref):
    ang = pos_ref[...].astype(F32) * inv_ref[...]
    lane = lax.broadcasted_iota(jnp.int32, ang.shape, 1)
    first_half = (lane % MLA_ROPE) < (MLA_ROPE // 2)
    cos_ref[...] = jnp.cos(ang)
    sin_ref[...] = jnp.where(first_half, -jnp.sin(ang), jnp.sin(ang))


def _rope_tables(positions):
    s = positions.shape[-1]
    tm = min(1024, s)
    half = MLA_ROPE // 2
    inv = (ROPE_THETA ** (-np.arange(half, dtype=np.float32) / half)).astype(np.float32)
    inv_row = jnp.asarray(np.tile(inv, LANES // half).reshape(1, LANES))
    return pl.pallas_call(
        _rope_table_kernel,
        out_shape=(jax.ShapeDtypeStruct((s, LANES), F32), jax.ShapeDtypeStruct((s, LANES), F32)),
        grid=(s // tm,),
        in_specs=[pl.BlockSpec((tm, 1), lambda i: (i, 0)), pl.BlockSpec((1, LANES), lambda i: (0, 0))],
        out_specs=(pl.BlockSpec((tm, LANES), lambda i: (i, 0)), pl.BlockSpec((tm, LANES), lambda i: (i, 0))),
        compiler_params=_params(("parallel",)),
        name="rope_tables",
    )(positions.reshape(s, 1), inv_row)


def _rope_apply(x, cos, sin_signed):
    width = x.shape[-1]
    half = MLA_ROPE // 2
    lane = lax.broadcasted_iota(jnp.int32, x.shape, 1)
    first_half = (lane % MLA_ROPE) < half
    swapped = jnp.where(first_half, pltpu.roll(x, width - half, 1), pltpu.roll(x, half, 1))
    return x * cos + swapped * sin_signed


def _mla_proj_kernel(cq_ref, ckv_ref, kr_ref, cos_ref, sin_ref, qn_ref, wuq_ref, kvn_ref, wuk_ref, wuvt_ref,
                     q_ref, k_ref, vt_ref):
    cos = cos_ref[...]
    sin = sin_ref[...]
    scale = (MLA_NOPE + MLA_ROPE) ** -0.5 * math.log2(math.e)
    cq = cq_ref[...].astype(F32)
    cqn = cq * lax.rsqrt(jnp.sum(cq * cq, axis=-1, keepdims=True) * (1.0 / MLA_Q_RANK) + EPS) * qn_ref[...]
    q = _dot(cqn.astype(BF16), wuq_ref[...])
    nope_w = MLA_HEADS * MLA_NOPE
    q_rope = _rope_apply(q[:, nope_w:], jnp.concatenate([cos, cos], -1), jnp.concatenate([sin, sin], -1))
    zeros = jnp.zeros((q.shape[0], MLA_QK_PAD - MLA_NOPE - MLA_ROPE), F32)
    ckvn = (_rms(ckv_ref[...].astype(F32)) * kvn_ref[...]).astype(BF16)
    k_nope = _dot(ckvn, wuk_ref[...])
    v_t = _dot_nt(wuvt_ref[...], ckvn)
    k_pe = _rope_apply(kr_ref[...].astype(F32), cos, sin)[:, :MLA_ROPE]
    for h in range(MLA_HEADS):
        qh = jnp.concatenate([q[:, h * MLA_NOPE:(h + 1) * MLA_NOPE],
                              q_rope[:, h * MLA_ROPE:(h + 1) * MLA_ROPE], zeros], axis=-1) * scale
        q_ref[h] = qh.astype(BF16)
        kh = jnp.concatenate([k_nope[:, h * MLA_NOPE:(h + 1) * MLA_NOPE], k_pe, zeros], axis=-1)
        k_ref[h] = kh.astype(BF16)
        vt_ref[h, 0] = v_t[h * MLA_DV:(h + 1) * MLA_DV, :].astype(BF16)


def _mla_proj(p, cos, sin, q_norm, w_uq, kv_norm, w_uk, w_uvt):
    s = p.shape[0]
    tm = min(MLA_BLOCK, s)
    const = lambda i: (0, 0)
    return pl.pallas_call(
        _mla_proj_kernel,
        out_shape=(jax.ShapeDtypeStruct((MLA_HEADS, s, MLA_QK_PAD), BF16),
                   jax.ShapeDtypeStruct((MLA_HEADS, s, MLA_QK_PAD), BF16),
                   jax.ShapeDtypeStruct((MLA_HEADS, s // tm, MLA_DV, tm), BF16)),
        grid=(s // tm,),
        in_specs=[pl.BlockSpec((tm, MLA_Q_RANK_PAD), lambda i: (i, P_CQ // MLA_Q_RANK_PAD)),
                  pl.BlockSpec((tm, LANES), lambda i: (i, P_CKV // LANES)),
                  pl.BlockSpec((tm, LANES), lambda i: (i, P_KROPE // LANES)),
                  pl.BlockSpec((tm, LANES), lambda i: (i, 0)),
                  pl.BlockSpec((tm, LANES), lambda i: (i, 0)),
                  pl.BlockSpec(q_norm.shape, const), pl.BlockSpec(w_uq.shape, const),
                  pl.BlockSpec(kv_norm.shape, const), pl.BlockSpec(w_uk.shape, const),
                  pl.BlockSpec(w_uvt.shape, const)],
        out_specs=(pl.BlockSpec((MLA_HEADS, tm, MLA_QK_PAD), lambda i: (0, i, 0)),
                   pl.BlockSpec((MLA_HEADS, tm, MLA_QK_PAD), lambda i: (0, i, 0)),
                   pl.BlockSpec((MLA_HEADS, 1, MLA_DV, tm), lambda i: (0, i, 0, 0))),
        compiler_params=_params(("parallel",)),
        name="mla_proj",
    )(p, p, p, cos, sin, q_norm, w_uq, kv_norm, w_uk, w_uvt)


def _mla_attn_kernel(q_ref, k_ref, vt_ref, o_ref, st0, st1, mb0, mb1, m_ref, l_ref, acc_ref, *, bk, nd):
    qi = pl.program_id(1)
    slots = ((st0, mb0), (st1, mb1))
    bq = nd * bk
    n0 = nd * qi
    tri = lax.broadcasted_iota(jnp.int32, (bk, bk), 0) <= lax.broadcasted_iota(jnp.int32, (bk, bk), 1)

    gq = bk // 2

    def scores(t, slot, c0, tri_mask):
        st_ref, mb_ref = slots[slot]
        r0 = pl.multiple_of(t * bk, bk)
        st = _dot_nt(k_ref[0, pl.ds(r0, bk), :], q_ref[0, c0:c0 + gq, :])
        if tri_mask is not None:
            st = jnp.where(tri_mask, st, -jnp.inf)
        st_ref[:, c0:c0 + gq] = st
        mb_ref[:, c0:c0 + gq] = jnp.max(st, axis=0, keepdims=True)

    def update(t, slot, c0):
        st_ref, mb_ref = slots[slot]
        cols = slice(c0, c0 + gq)
        m = m_ref[:, cols]
        m_new = jnp.maximum(m, mb_ref[:, cols])
        alpha = jnp.exp2(m - m_new)
        pexp = jnp.exp2(st_ref[:, cols] - m_new)
        l_ref[:, cols] = alpha * l_ref[:, cols] + jnp.sum(pexp, axis=0, keepdims=True)
        acc_ref[:, cols] = alpha * acc_ref[:, cols] + _dot(vt_ref[0, t], pexp.astype(BF16))
        m_ref[:, cols] = m_new

    def sweep(ts=None, ss=None, ds=None, tu=None, su=None, du=None):
        for c0 in range(0, bq, gq):
            g = c0 // bk
            if ts is not None and (ds is None or g >= ds):
                scores(ts, ss, c0, tri[:, c0 - g * bk:c0 - g * bk + gq] if g == ds else None)
            if tu is not None and (du is None or g >= du):
                update(tu, su, c0)

    m_ref[...] = jnp.full(m_ref.shape, -jnp.inf, F32)
    l_ref[...] = jnp.zeros(l_ref.shape, F32)
    acc_ref[...] = jnp.zeros(acc_ref.shape, F32)

    @pl.when(qi == 0)
    def _():
        sweep(ts=0, ss=0, ds=0)

    @pl.when(qi > 0)
    def _():
        sweep(ts=0, ss=0)

    def pair(j, carry):
        sweep(ts=2 * j + 1, ss=1, tu=2 * j, su=0)
        sweep(ts=2 * j + 2, ss=0, tu=2 * j + 1, su=1)
        return carry

    lax.fori_loop(0, n0 // 2 - 1, pair, 0)

    @pl.when(qi > 0)
    def _():
        sweep(ts=n0 - 1, ss=1, tu=n0 - 2, su=0)
        sweep(ts=n0, ss=0, ds=0, tu=n0 - 1, su=1)

    for d in range(1, nd):
        sweep(ts=n0 + d, ss=d % 2, ds=d, tu=n0 + d - 1, su=(d - 1) % 2, du=d - 1)
    sweep(tu=n0 + nd - 1, su=(nd - 1) % 2, du=nd - 1)
    o_ref[...] = (acc_ref[...] / l_ref[...]).T.astype(BF16)


def _mla_attn(q, k, vt):
    nh, s, _ = q.shape
    bk = vt.shape[-1]
    nd = 4 if s % (4 * bk) == 0 else 2
    bq = nd * bk
    assert s % bq == 0
    return pl.pallas_call(
        functools.partial(_mla_attn_kernel, bk=bk, nd=nd),
        out_shape=jax.ShapeDtypeStruct((s, nh * MLA_DV), BF16),
        grid=(nh, s // bq),
        in_specs=[pl.BlockSpec((1, bq, MLA_QK_PAD), lambda h, i: (h, i, 0)),
                  pl.BlockSpec((1, s, MLA_QK_PAD), lambda h, i: (h, 0, 0)),
                  pl.BlockSpec((1, s // bk, MLA_DV, bk), lambda h, i: (h, 0, 0, 0))],
        out_specs=pl.BlockSpec((bq, MLA_DV), lambda h, i: (i, h)),
        scratch_shapes=[pltpu.VMEM((bk, bq), F32), pltpu.VMEM((bk, bq), F32),
                        pltpu.VMEM((1, bq), F32), pltpu.VMEM((1, bq), F32),
                        pltpu.VMEM((1, bq), F32), pltpu.VMEM((1, bq), F32), pltpu.VMEM((MLA_DV, bq), F32)],
        compiler_params=_params(("parallel", "arbitrary")),
        name="mla_attn",
    )(q, k, vt)


def _swa_kernel(q_ref, k_ref, v_ref, kh_ref, vh_ref, sink_ref, o_ref, *, nblk):
    w = SWA_WINDOW
    g = SWA_GROUP
    i = pl.program_id(0)
    key = lax.broadcasted_iota(jnp.int32, (2 * w, g * w), 0)
    col = lax.broadcasted_iota(jnp.int32, (2 * w, g * w), 1)
    dist = col % w + w - key
    band = (dist >= 0) & (dist < w)
    first_band = band & ((key >= w) | (i > 0))
    dist_f = dist.astype(F32)
    biases = []
    for hk in range(SWA_KV_HEADS):
        slope = jnp.zeros((2 * w, g * w), F32)
        for gi, gq in enumerate(SWA_GORDER):
            slope = jnp.where(col // w == gi, 2.0 ** (-8.0 * (hk * g + gq + 1.0) / SWA_HEADS), slope)
        biases.append(-slope * dist_f)
    low = lax.broadcasted_iota(jnp.int32, (2 * w, 2 * SWA_DH), 1) < SWA_DH
    top = lax.broadcasted_iota(jnp.int32, (2 * SWA_DH, 1), 0) < SWA_DH
    for blk in range(nblk):
        rows = slice(blk * w, (blk + 1) * w)
        if blk == 0:
            k_prev, v_prev, valid = kh_ref[...], vh_ref[...], first_band
        else:
            k_prev, v_prev, valid = k_ref[(blk - 1) * w:blk * w, :], v_ref[(blk - 1) * w:blk * w, :], band
        kk = jnp.concatenate([k_prev, k_ref[rows, :]], axis=0).astype(F32)
        kk_sw = pltpu.roll(kk, SWA_DH, 1)
        vvt = jnp.concatenate([v_prev, v_ref[rows, :]], axis=0).astype(F32).T
        probs, inv_denoms = [], []
        for hk in range(SWA_KV_HEADS):
            c0 = hk * g * SWA_DH
            q2 = jnp.concatenate([q_ref[rows, c0:c0 + 2 * SWA_DH], q_ref[rows, c0 + 2 * SWA_DH:c0 + 4 * SWA_DH]],
                                 axis=0) * (SWA_DH ** -0.5)
            k_low = jnp.where(low, kk if hk == 0 else kk_sw, 0.0).astype(BF16)
            k_high = jnp.where(low, 0.0, kk_sw if hk == 0 else kk).astype(BF16)
            st = jnp.concatenate([_dot_nt(k_low, q2), _dot_nt(k_high, q2)], axis=1)
            st = jnp.where(valid, st + biases[hk], -jnp.inf)
            sink = sink_ref[hk]
            m = jnp.maximum(jnp.max(st, axis=0, keepdims=True), sink)
            pexp = jnp.exp(st - m)
            denom = jnp.sum(pexp, axis=0, keepdims=True) + jnp.exp(sink - m)
            probs.append(pexp.astype(BF16))
            inv_denoms.append(1.0 / denom)
        v0 = jnp.where(top, vvt, 0.0).astype(BF16)
        v1 = jnp.where(top, 0.0, vvt).astype(BF16)
        o_t = (_dot(v0, probs[0]) + _dot(v1, probs[1])) * jnp.where(top, inv_denoms[0], inv_denoms[1])
        for gi in range(g):
            o_ref[rows, gi * LANES:(gi + 1) * LANES] = o_t[:, gi * w:(gi + 1) * w].T.astype(BF16)


def _swa(p, sinks):
    s = p.shape[0]
    w = SWA_WINDOW
    rb = min(512, s)
    nblk = rb // w
    order = np.asarray([[hk * SWA_GROUP + gq for gq in SWA_GORDER] for hk in range(SWA_KV_HEADS)])
    sink_rows = jnp.repeat(sinks[order], w, axis=1).reshape(SWA_KV_HEADS, 1, SWA_GROUP * w)
    prev = lambda col: (lambda i: (jnp.maximum(i * nblk - 1, 0), col))
    return pl.pallas_call(
        functools.partial(_swa_kernel, nblk=nblk),
        out_shape=jax.ShapeDtypeStruct((s, SWA_OUT), BF16),
        grid=(s // rb,),
        in_specs=[pl.BlockSpec((rb, SWA_OUT), lambda i: (i, P_SWQ // SWA_OUT)),
                  pl.BlockSpec((rb, SWA_KV), lambda i: (i, P_SWK // SWA_KV)),
                  pl.BlockSpec((rb, SWA_KV), lambda i: (i, P_SWV // SWA_KV)),
                  pl.BlockSpec((w, SWA_KV), prev(P_SWK // SWA_KV)),
                  pl.BlockSpec((w, SWA_KV), prev(P_SWV // SWA_KV)),
                  pl.BlockSpec(sink_rows.shape, lambda i: (0, 0, 0))],
        out_specs=pl.BlockSpec((rb, SWA_OUT), lambda i: (i, 0)),
        compiler_params=_params(("parallel",)),
        name="swa",
    )(p, p, p, p, p, sink_rows)


def _outproj_kernel(oa_ref, ob_ref, oc_ref, w_ref, x_ref, pn_ref, g1_ref, fn_ref, sc_ref, sh_ref,
                    x1_ref, h2_ref, *, sub):
    a_w, b_w = oa_ref.shape[1], ob_ref.shape[1]
    for r0 in range(0, x_ref.shape[0], sub):
        rows = slice(r0, r0 + sub)
        mix = (_dot(oa_ref[rows, :], w_ref[0:a_w, :]) + _dot(ob_ref[rows, :], w_ref[a_w:a_w + b_w, :])
               + _dot(oc_ref[rows, :], w_ref[a_w + b_w:, :]))
        x1 = x_ref[rows, :] + g1_ref[...] * (_rms(mix) * pn_ref[...])
        x1_ref[rows, :] = x1
        h2_ref[rows, :] = _norm_mod(x1, fn_ref[...], sc_ref[...], sh_ref[...]).astype(BF16)


def _out_proj(o_a, o_b, o_c, w_out, layer, x, post_norm, gate1, ffn_norm, scale2, shift2):
    s, d = x.shape
    tm = min(512, s)
    row = lambda i: (i, 0)
    vec = pl.BlockSpec((1, d), lambda i: (0, 0))
    return pl.pallas_call(
        functools.partial(_outproj_kernel, sub=min(256, tm)),
        out_shape=(jax.ShapeDtypeStruct((s, d), F32), jax.ShapeDtypeStruct((s, d), BF16)),
        grid=(s // tm,),
        in_specs=[pl.BlockSpec((tm, o_a.shape[1]), row), pl.BlockSpec((tm, o_b.shape[1]), row),
                  pl.BlockSpec((tm, o_c.shape[1]), row),
                  pl.BlockSpec((None,) + w_out.shape[1:], lambda i: (layer, 0, 0), pipeline_mode=pl.Buffered(1)),
                  pl.BlockSpec((tm, d), row), vec, vec, vec, vec, vec],
        out_specs=(pl.BlockSpec((tm, d), row), pl.BlockSpec((tm, d), row)),
        compiler_params=_params(("parallel",)),
        name="out_proj",
    )(o_a, o_b, o_c, w_out, x, post_norm, gate1, ffn_norm, scale2, shift2)


def _gelu_tanh(x):
    return 0.5 * x * (1.0 + jnp.tanh(math.sqrt(2.0 / math.pi) * (x + 0.044715 * (x * x * x))))


def _ffn_up_kernel(h_ref, wg_ref, wu_ref, cg_ref, cu_ref, bg_ref, bu_ref, o_ref,
                   wg16_ref, wu16_ref, xg_ref, xu_ref, *, tm):
    @pl.when(pl.program_id(1) == 0)
    def _():
        wg16_ref[...] = wg_ref[...].astype(BF16)
        wu16_ref[...] = wu_ref[...].astype(BF16)
        xg_ref[0:8, :] = jnp.zeros((8, xg_ref.shape[1]), F32)
        xu_ref[0:8, :] = jnp.zeros((8, xu_ref.shape[1]), F32)

    h = h_ref[...]

    def conv(x_ref, w_ref, cw_ref, b_ref):
        x_ref[8:8 + tm, :] = _dot(h, w_ref[...])
        y = b_ref[...] + cw_ref[2:3, :] * x_ref[8:8 + tm, :]
        y = y + cw_ref[1:2, :] * x_ref[7:7 + tm, :]
        y = y + cw_ref[0:1, :] * x_ref[6:6 + tm, :]
        x_ref[0:8, :] = x_ref[tm:tm + 8, :]
        return y

    gate = conv(xg_ref, wg16_ref, cg_ref, bg_ref)
    up = conv(xu_ref, wu16_ref, cu_ref, bu_ref)
    o_ref[...] = (_gelu_tanh(gate) * up).astype(BF16)


def _ffn_up(h2, w_up, layer, conv_w, conv_b):
    s, d = h2.shape
    d_ff = w_up.shape[2] // 2
    tm, tn = min(1024, s), 512
    nj = d_ff // tn
    lo = lambda j, i: (0, j)
    hi = lambda j, i: (0, j + nj)
    w_lo = pl.BlockSpec((None, d, tn), lambda j, i: (layer, 0, j))
    w_hi = pl.BlockSpec((None, d, tn), lambda j, i: (layer, 0, j + nj))
    return pl.pallas_call(
        functools.partial(_ffn_up_kernel, tm=tm),
        out_shape=jax.ShapeDtypeStruct((s, d_ff), BF16),
        grid=(nj, s // tm),
        in_specs=[pl.BlockSpec((tm, d), lambda j, i: (i, 0)),
                  w_lo, w_hi,
                  pl.BlockSpec((FFN_CONV, tn), lo), pl.BlockSpec((FFN_CONV, tn), hi),
                  pl.BlockSpec((1, tn), lo), pl.BlockSpec((1, tn), hi)],
        out_specs=pl.BlockSpec((tm, tn), lambda j, i: (i, j)),
        scratch_shapes=[pltpu.VMEM((d, tn), BF16), pltpu.VMEM((d, tn), BF16),
                        pltpu.VMEM((tm + 8, tn), F32), pltpu.VMEM((tm + 8, tn), F32)],
        compiler_params=_params(("parallel", "arbitrary")),
        name="ffn_up",
    )(h2, w_up, w_up, conv_w, conv_w, conv_b, conv_b)


def _ffn_down_kernel(g_ref, w_ref, x_ref, pn_ref, g2_ref, *rest):
    y = _dot(g_ref[...], w_ref[...])
    x2 = x_ref[...] + g2_ref[...] * (_rms(y) * pn_ref[...])
    if len(rest) == 1:
        rest[0][...] = x2
    else:
        nn_ref, sc_ref, sh_ref, x2_ref, hn_ref = rest
        x2_ref[...] = x2
        hn_ref[...] = _norm_mod(x2, nn_ref[...], sc_ref[...], sh_ref[...]).astype(BF16)


def _ffn_down(g, w_down, layer, x1, post_norm, gate2, next_pre_norm=None):
    s, d = x1.shape
    d_ff = g.shape[1]
    tm = min(256, s)
    row = lambda i: (i, 0)
    vec = pl.BlockSpec((1, d), lambda i: (0, 0))
    tile_f32 = jax.ShapeDtypeStruct((s, d), F32)
    tile = pl.BlockSpec((tm, d), row)
    last = next_pre_norm is None
    return pl.pallas_call(
        _ffn_down_kernel,
        out_shape=tile_f32 if last else (tile_f32, jax.ShapeDtypeStruct((s, d), BF16)),
        grid=(s // tm,),
        in_specs=[pl.BlockSpec((tm, d_ff), row),
                  pl.BlockSpec((None, d_ff, d), lambda i: (layer, 0, 0), pipeline_mode=pl.Buffered(1)),
                  tile, vec, vec] + ([] if last else [vec, vec, vec]),
        out_specs=tile if last else (tile, tile),
        compiler_params=_params(("parallel",)),
        name="ffn_down",
    )(g, w_down, x1, post_norm, gate2, *(() if last else next_pre_norm))


def _layout_w_in(w_in):
    depth, d, _ = w_in.shape
    sizes = (GDN_QK, GDN_QK, GDN_V, GDN_V, GDN_HEADS, GDN_HEADS, MLA_Q_RANK, MLA_KV_RANK, MLA_ROPE,
             SWA_OUT, SWA_KV, SWA_KV)
    offs = np.concatenate([[0], np.cumsum(sizes)])
    part = lambda n: w_in[:, :, offs[n]:offs[n + 1]].astype(BF16)
    zeros = lambda n: jnp.zeros((depth, d, n), BF16)
    w_p = jnp.concatenate([w_in[:, :, :offs[4]].astype(BF16),
                           part(6), zeros(MLA_Q_RANK_PAD - MLA_Q_RANK),
                           part(9), part(7), part(8), zeros(LANES - MLA_ROPE), part(10), part(11)], axis=2)
    w_ab = jnp.concatenate([part(4), part(5), zeros(LANES - 2 * GDN_HEADS)], axis=2)
    return w_p, w_ab


def _layout_w_out(w_out):
    depth, d_mix, d = w_out.shape
    a = d_mix - SWA_OUT
    wc = w_out[:, a:].reshape(depth, SWA_KV_HEADS, SWA_GROUP, SWA_DH, d)[:, :, np.asarray(SWA_GORDER)]
    wc = wc.transpose(0, 2, 1, 3, 4).reshape(depth, SWA_OUT, d)
    return jnp.concatenate([w_out[:, :a].astype(BF16), wc.astype(BF16)], axis=1)


def _layout_mla(q_norm, w_uq, kv_norm, w_ukv):
    dqk = MLA_NOPE + MLA_ROPE
    uq = w_uq.reshape(MLA_Q_RANK, MLA_HEADS, dqk)
    uq = jnp.concatenate([uq[:, :, :MLA_NOPE].reshape(MLA_Q_RANK, -1), uq[:, :, MLA_NOPE:].reshape(MLA_Q_RANK, -1)],
                         axis=1)
    uq = jnp.pad(uq, ((0, MLA_Q_RANK_PAD - MLA_Q_RANK), (0, 0))).astype(BF16)
    qn = jnp.pad(q_norm, (0, MLA_Q_RANK_PAD - MLA_Q_RANK)).reshape(1, MLA_Q_RANK_PAD)
    ukv = w_ukv.reshape(MLA_KV_RANK, MLA_HEADS, MLA_NOPE + MLA_DV)
    uk = ukv[:, :, :MLA_NOPE].reshape(MLA_KV_RANK, -1).astype(BF16)
    uvt = ukv[:, :, MLA_NOPE:].reshape(MLA_KV_RANK, -1).T.astype(BF16)
    return qn, uq, kv_norm.reshape(1, MLA_KV_RANK), uk, uvt


def kernel(x, c, positions, ada_w, ada_b, mix_pre_norm, mix_post_norm, w_in, w_out, gdn_conv, gdn_a_log, gdn_dt_bias, gdn_norm, mla_q_norm, mla_w_uq, mla_kv_norm, mla_w_ukv, swa_sinks, ffn_pre_norm, ffn_post_norm, ffn_w_up, ffn_conv, ffn_conv_b, ffn_w_down):
    batch, s, d = x.shape
    assert batch == 1, "kernels are written for a single sequence"
    depth = ada_w.shape[0]
    xs = x.reshape(s, d)
    mod = _adaln_mod(c, ada_w, ada_b).reshape(depth, N_MOD, 1, d)
    cos, sin = _rope_tables(positions)
    vec = lambda a: a.reshape(1, d)

    w_p, w_ab = _layout_w_in(w_in)
    w_o = _layout_w_out(w_out)
    w_down = ffn_w_down.astype(BF16)

    h = _pre_norm(xs, vec(mix_pre_norm[0]), mod[0, 1], mod[0, 0])
    for l in range(depth):
        shift1, scale1, gate1, shift2, scale2, gate2 = (mod[l, n] for n in range(N_MOD))
        p, ab = _in_proj(h, w_p, w_ab, l)

        q_a, k_a, v_a, gcb, gct = _gdn_prep(p, ab, gdn_conv[l], gdn_a_log[l], gdn_dt_bias[l])
        u, w, qd, kt, qk = _gdn_local(q_a, k_a, v_a, gcb, gct)
        o_a = _gdn_scan(u, w, qd, kt, qk, gcb, p, gdn_norm[l])

        q_b, k_b, v_b = _mla_proj(p, cos, sin, *_layout_mla(mla_q_norm[l], mla_w_uq[l], mla_kv_norm[l],
                                                            mla_w_ukv[l]))
        o_b = _mla_attn(q_b, k_b, v_b)

        o_c = _swa(p, swa_sinks[l])

        x1, h2 = _out_proj(o_a, o_b, o_c, w_o, l, xs, vec(mix_post_norm[l]), gate1,
                           vec(ffn_pre_norm[l]), scale2, shift2)
        g = _ffn_up(h2, ffn_w_up, l, ffn_conv[l], ffn_conv_b[l].reshape(1, -1))
        if l + 1 < depth:
            xs, h = _ffn_down(g, w_down, l, x1, vec(ffn_post_norm[l]), gate2,
                              (vec(mix_pre_norm[l + 1]), mod[l + 1, 1], mod[l + 1, 0]))
        else:
            xs = _ffn_down(g, w_down, l, x1, vec(ffn_post_norm[l]), gate2)
    return xs.reshape(batch, s, d)
```

```python
import functools
import math

import numpy as np
import jax
import jax.numpy as jnp
from jax import lax
from jax.experimental import pallas as pl
from jax.experimental.pallas import tpu as pltpu

F32 = jnp.float32
BF16 = jnp.bfloat16

EPS = 1e-6
N_MOD = 6
GDN_HEADS = 8
GDN_DK = 128
GDN_DV = 128
GDN_CONV = 4
GDN_CHUNK = 64
GDN_QK = GDN_HEADS * GDN_DK
GDN_V = GDN_HEADS * GDN_DV
MLA_HEADS = 4
MLA_Q_RANK = 448
MLA_Q_RANK_PAD = 512
MLA_KV_RANK = 128
MLA_NOPE = 128
MLA_ROPE = 64
MLA_DV = 128
MLA_QK_PAD = 256
MLA_BLOCK = 512
ROPE_THETA = 10000.0
SWA_HEADS = 8
SWA_KV_HEADS = 2
SWA_GROUP = SWA_HEADS // SWA_KV_HEADS
SWA_DH = 64
SWA_WINDOW = 128
SWA_OUT = SWA_HEADS * SWA_DH
SWA_KV = SWA_KV_HEADS * SWA_DH
SWA_GORDER = (0, 2, 1, 3)
FFN_CONV = 3

LANES = 128
GDN_GROUP = 256
VMEM_LIMIT_MB = 56

P_QKV = 0
P_Z = 3072
P_CQ = 4096
P_SWQ = 4608
P_CKV = 5120
P_KROPE = 5248
P_SWK = 5376
P_SWV = 5504
P_WIDTH = 5632


def _params(semantics):
    return pltpu.CompilerParams(dimension_semantics=semantics, vmem_limit_bytes=VMEM_LIMIT_MB << 20)


def _sigmoid(x):
    return 0.5 + 0.5 * jnp.tanh(0.5 * x)


def _silu(x):
    h = 0.5 * x
    return h + h * jnp.tanh(h)


def _softplus(x):
    return jnp.maximum(x, 0.0) + jnp.log(1.0 + jnp.exp(-jnp.abs(x)))


def _rms(x):
    return x * lax.rsqrt(jnp.mean(x * x, axis=-1, keepdims=True) + EPS)


def _dot(a, b):
    return jnp.dot(a, b, preferred_element_type=F32)


def _dot_nt(a, b):
    return lax.dot_general(a, b, (((1,), (1,)), ((), ())), preferred_element_type=F32)


def _dot_tn(a, b):
    return lax.dot_general(a, b, (((0,), (0,)), ((), ())), preferred_element_type=F32)


def _mod_kernel(c_ref, w_ref, b_ref, o_ref, *, d, kc):
    cact = _silu(c_ref[...])
    acc = b_ref[0]
    for k0 in range(0, d, kc):
        acc = acc + jnp.sum(w_ref[0, k0:k0 + kc, :] * cact[k0:k0 + kc], axis=0, keepdims=True)
    o_ref[0] = acc


def _adaln_mod(c, ada_w, ada_b):
    depth, d, n = ada_w.shape
    tn = 1024
    return pl.pallas_call(
        functools.partial(_mod_kernel, d=d, kc=256),
        out_shape=jax.ShapeDtypeStruct((depth, 1, n), F32),
        grid=(depth, n // tn),
        in_specs=[pl.BlockSpec((d, 1), lambda l, j: (0, 0)),
                  pl.BlockSpec((1, d, tn), lambda l, j: (l, 0, j)),
                  pl.BlockSpec((1, 1, tn), lambda l, j: (l, 0, j))],
        out_specs=pl.BlockSpec((1, 1, tn), lambda l, j: (l, 0, j)),
        compiler_params=_params(("parallel", "parallel")),
        name="adaln_mod",
    )(c.reshape(d, 1), ada_w, ada_b.reshape(depth, 1, n))


def _norm_mod(x, w, scale, shift):
    return _rms(x) * w * (1.0 + scale) + shift


def _norm_mod_kernel(x_ref, w_ref, sc_ref, sh_ref, h_ref):
    h_ref[...] = _norm_mod(x_ref[...], w_ref[...], sc_ref[...], sh_ref[...]).astype(BF16)


def _pre_norm(x, w, scale, shift):
    s, d = x.shape
    tm = min(1024, s)
    row = pl.BlockSpec((1, d), lambda i: (0, 0))
    return pl.pallas_call(
        _norm_mod_kernel,
        out_shape=jax.ShapeDtypeStruct((s, d), BF16),
        grid=(s // tm,),
        in_specs=[pl.BlockSpec((tm, d), lambda i: (i, 0)), row, row, row],
        out_specs=pl.BlockSpec((tm, d), lambda i: (i, 0)),
        compiler_params=_params(("parallel",)),
        name="pre_norm",
    )(x, w, scale, shift)


def _inproj_kernel(h_ref, w_ref, wab_ref, p_ref, ab_ref):
    h = h_ref[...]
    p_ref[...] = _dot(h, w_ref[...]).astype(BF16)

    @pl.when(pl.program_id(1) == 0)
    def _():
        ab_ref[...] = _dot(h, wab_ref[...])


def _in_proj(h, w_p, w_ab, layer):
    s, d = h.shape
    n = w_p.shape[2]
    tm, tn = min(2048, s), 512
    return pl.pallas_call(
        _inproj_kernel,
        out_shape=(jax.ShapeDtypeStruct((s, n), BF16), jax.ShapeDtypeStruct((s, LANES), F32)),
        grid=(s // tm, n // tn),
        in_specs=[pl.BlockSpec((tm, d), lambda i, j: (i, 0)),
                  pl.BlockSpec((None, d, tn), lambda i, j: (layer, 0, j)),
                  pl.BlockSpec((None, d, LANES), lambda i, j: (layer, 0, 0))],
        out_specs=(pl.BlockSpec((tm, tn), lambda i, j: (i, j)),
                   pl.BlockSpec((tm, LANES), lambda i, j: (i, 0))),
        compiler_params=_params(("parallel", "arbitrary")),
        name="in_proj",
    )(h, w_p, w_ab)


def _gdn_prep_kernel(x_ref, halo_ref, cw_ref, ab_ref, alog_ref, dtb_ref, tri_ref,
                     q_ref, k_ref, v_ref, gcb_ref, gct_ref, xp_ref, *, tm):
    i = pl.program_id(0)
    outs = (q_ref, k_ref, v_ref)
    for grp in range(3):
        c0 = grp * GDN_QK
        halo = halo_ref[:, c0:c0 + GDN_QK].astype(F32)
        xp_ref[0:8, :] = jnp.where(i > 0, halo, 0.0)
        xp_ref[8:8 + tm, :] = x_ref[:, c0:c0 + GDN_QK].astype(F32)
        y = cw_ref[3:4, c0:c0 + GDN_QK] * xp_ref[8:8 + tm, :]
        for j in range(GDN_CONV - 1):
            off = 8 - (GDN_CONV - 1) + j
            y = y + cw_ref[j:j + 1, c0:c0 + GDN_QK] * xp_ref[off:off + tm, :]
        y = _silu(y)
        for h in range(GDN_HEADS):
            yh = y[:, h * GDN_DK:(h + 1) * GDN_DK]
            if grp < 2:
                yh = yh * lax.rsqrt(jnp.sum(yh * yh, axis=-1, keepdims=True) + EPS)
            if grp == 0:
                yh = yh * (GDN_DK ** -0.5)
            outs[grp][:, h * GDN_DK:(h + 1) * GDN_DK] = yh.astype(BF16)

    ab = ab_ref[...]
    g = -jnp.exp(alog_ref[...]) * _softplus(ab + dtb_ref[...])
    gc = jnp.dot(tri_ref[...], g, preferred_element_type=F32, precision=lax.Precision.HIGHEST)
    lane = lax.broadcasted_iota(jnp.int32, (1, LANES), 1)
    gcb = jnp.where(lane < GDN_HEADS, gc, _sigmoid(ab))
    gcb_ref[...] = gcb
    gct_ref[...] = gcb.T[0:2 * GDN_HEADS, :]


def _gdn_prep(p, ab, conv_w, a_log, dt_bias):
    s = p.shape[0]
    tm = min(512, s)
    c3 = 3 * GDN_QK
    alog_row = jnp.zeros((1, LANES), F32).at[0, :GDN_HEADS].set(a_log)
    dtb_row = jnp.zeros((1, LANES), F32).at[0, :GDN_HEADS].set(dt_bias)
    r = np.arange(tm)
    tri = jnp.asarray(((r[:, None] >= r[None, :]) & (r[:, None] // GDN_CHUNK == r[None, :] // GDN_CHUNK))
                      .astype(np.float32))
    hb = tm // 8
    row = lambda i: (i, 0)
    return pl.pallas_call(
        functools.partial(_gdn_prep_kernel, tm=tm),
        out_shape=(jax.ShapeDtypeStruct((s, GDN_QK), BF16), jax.ShapeDtypeStruct((s, GDN_QK), BF16),
                   jax.ShapeDtypeStruct((s, GDN_V), BF16), jax.ShapeDtypeStruct((s, LANES), F32),
                   jax.ShapeDtypeStruct((2 * GDN_HEADS, s), F32)),
        grid=(s // tm,),
        in_specs=[pl.BlockSpec((tm, c3), row),
                  pl.BlockSpec((8, c3), lambda i: (jnp.maximum(i * hb - 1, 0), 0)),
                  pl.BlockSpec((GDN_CONV, c3), lambda i: (0, 0)),
                  pl.BlockSpec((tm, LANES), row),
                  pl.BlockSpec((1, LANES), lambda i: (0, 0)),
                  pl.BlockSpec((1, LANES), lambda i: (0, 0)),
                  pl.BlockSpec((tm, tm), lambda i: (0, 0))],
        out_specs=(pl.BlockSpec((tm, GDN_QK), row), pl.BlockSpec((tm, GDN_QK), row),
                   pl.BlockSpec((tm, GDN_V), row), pl.BlockSpec((tm, LANES), row),
                   pl.BlockSpec((2 * GDN_HEADS, tm), lambda i: (0, i))),
        scratch_shapes=[pltpu.VMEM((tm + 8, GDN_QK), F32)],
        compiler_params=_params(("parallel",)),
        name="gdn_prep",
    )(p, p, conv_w, ab, alog_row, dtb_row, tri)


def _gdn_local_kernel(q_ref, k_ref, v_ref, gcb_ref, gct_ref,
                      u_ref, w_ref, qd_ref, kt_ref, qk_ref, *, groups):
    n = GDN_GROUP
    c = GDN_CHUNK
    ri = lax.broadcasted_iota(jnp.int32, (n, n), 0)
    ci = lax.broadcasted_iota(jnp.int32, (n, n), 1)
    same_chunk = (ri // c) == (ci // c)
    incl = same_chunk & (ri >= ci)
    strict = same_chunk & (ri > ci)
    eye = (ri == ci).astype(F32)
    last_sel = (ci == (ri // c) * c + (c - 1)).astype(F32)
    heads = range(GDN_HEADS)
    sls = [slice(h * GDN_DK, (h + 1) * GDN_DK) for h in heads]
    lane_chunk = lax.broadcasted_iota(jnp.int32, (c, n), 1) // c
    units = [(g, h) for g in range(groups) for h in heads]

    def to_cat(m):
        out = m[0:c]
        for j in range(1, n // c):
            out = out + m[j * c:(j + 1) * c]
        return out

    def to_bd(m_cat):
        zero = jnp.zeros_like(m_cat)
        return jnp.concatenate([jnp.where(lane_chunk == j, m_cat, zero) for j in range(n // c)], axis=0)

    lmats, tcs, rhss = {}, {}, {}
    for g in range(groups):
        rows = slice(g * n, (g + 1) * n)
        gcb = gcb_ref[rows, :]
        gc_last_all = jnp.dot(last_sel, gcb, preferred_element_type=F32, precision=lax.Precision.HIGHEST)
        for h in heads:
            sl = sls[h]
            q = q_ref[rows, sl].astype(F32)
            k16 = k_ref[rows, sl]
            k = k16.astype(F32)
            v = v_ref[rows, sl].astype(F32)
            gc_col = gcb[:, h:h + 1]
            beta = gcb[:, GDN_HEADS + h:GDN_HEADS + h + 1]
            gc_row = gct_ref[h:h + 1, rows]
            gc_last = gc_last_all[:, h:h + 1]
            diff = gc_col - gc_row
            decay = jnp.where(incl, jnp.exp(jnp.where(incl, diff, 0.0)), 0.0)
            kb = k * beta
            kk = _dot_nt(kb.astype(BF16), k16)
            lmat = jnp.where(strict, kk * decay, 0.0)
            lmats[g, h] = lmat
            tcs[g, h] = to_cat(eye - jnp.where((ri // 2) == (ci // 2), lmat, 0.0))
            eg = jnp.exp(gc_col)
            rhss[g, h] = jnp.concatenate([v * beta, kb * eg], axis=-1).astype(BF16)
            qd_ref[rows, sl] = (q * eg).astype(BF16)
            kt_ref[rows, sl] = (k * jnp.exp(gc_last - gc_col)).astype(BF16)
            qk = _dot_nt(q_ref[rows, sl], k16) * decay
            qk_c = jnp.concatenate([qk[j * c:(j + 1) * c, j * c:(j + 1) * c] for j in range(n // c)], axis=0)
            qk_ref[rows, h * c:(h + 1) * c] = qk_c.astype(BF16)
    b = 2
    while b < c:
        off_diag = ((ri // (2 * b)) == (ci // (2 * b))) & ((ri // b) != (ci // b))
        t16s = {u: tcs[u].astype(BF16) for u in units}
        ys = {u: _dot(t16s[u], jnp.where(off_diag, lmats[u], 0.0).astype(BF16)).astype(BF16) for u in units}
        tcs = {u: tcs[u] - _dot(ys[u], to_bd(t16s[u])) for u in units}
        b *= 2
    for g, h in units:
        rows = slice(g * n, (g + 1) * n)
        uw = _dot(to_bd(tcs[g, h].astype(BF16)), rhss[g, h])
        u_ref[rows, sls[h]] = uw[:, :GDN_DV].astype(BF16)
        w_ref[rows, sls[h]] = uw[:, GDN_DV:].astype(BF16)


def _gdn_local(q, k, v, gcb, gct):
    s = q.shape[0]
    groups = 2 if s % (2 * GDN_GROUP) == 0 else 1
    n = groups * GDN_GROUP
    row = lambda i: (i, 0)
    wide = pl.BlockSpec((n, GDN_QK), row)
    return pl.pallas_call(
        functools.partial(_gdn_local_kernel, groups=groups),
        out_shape=(jax.ShapeDtypeStruct((s, GDN_V), BF16), jax.ShapeDtypeStruct((s, GDN_QK), BF16),
                   jax.ShapeDtypeStruct((s, GDN_QK), BF16), jax.ShapeDtypeStruct((s, GDN_QK), BF16),
                   jax.ShapeDtypeStruct((s, GDN_HEADS * GDN_CHUNK), BF16)),
        grid=(s // n,),
        in_specs=[wide, wide, wide, pl.BlockSpec((n, LANES), row),
                  pl.BlockSpec((2 * GDN_HEADS, n), lambda i: (0, i))],
        out_specs=(wide, wide, wide, wide, pl.BlockSpec((n, GDN_HEADS * GDN_CHUNK), row)),
        compiler_params=_params(("parallel",)),
        name="gdn_local",
    )(q, k, v, gcb, gct)


def _gdn_scan_kernel(u_ref, w_ref, qd_ref, kt_ref, qk_ref, gcb_ref, z_ref, nw_ref, o_ref, st_ref, *, rb):
    c = GDN_CHUNK

    @pl.when(pl.program_id(0) == 0)
    def _():
        st_ref[...] = jnp.zeros_like(st_ref)

    nw = nw_ref[...]

    def chunk(ci, carry):
        r0 = pl.multiple_of(ci * c, c)
        rows = pl.ds(r0, c)
        gt_row = jnp.exp(gcb_ref[pl.ds(r0 + c - 1, 1), :])
        heads = range(GDN_HEADS)
        sls = [slice(h * GDN_DK, (h + 1) * GDN_DK) for h in heads]
        sts = [st_ref[h] for h in heads]
        r1s = [_dot(jnp.concatenate([w_ref[rows, sls[h]], qd_ref[rows, sls[h]]], axis=0),
                    sts[h].astype(BF16)) for h in heads]
        vns = [(u_ref[rows, sls[h]].astype(F32) - r1s[h][0:c]).astype(BF16) for h in heads]
        for h in heads:
            st_ref[h] = sts[h] * gt_row[:, h:h + 1] + _dot_tn(kt_ref[rows, sls[h]], vns[h])
        os_ = [r1s[h][c:2 * c] + _dot(qk_ref[rows, h * c:(h + 1) * c], vns[h]) for h in heads]
        for h in heads:
            o = _rms(os_[h]) * nw * _silu(z_ref[rows, sls[h]].astype(F32))
            o_ref[rows, sls[h]] = o.astype(BF16)
        return carry

    unroll = min(8, rb // c)
    assert (rb // c) % unroll == 0

    def chunks(i, carry):
        for u in range(unroll):
            carry = chunk(unroll * i + u, carry)
        return carry

    lax.fori_loop(0, rb // (unroll * c), chunks, 0)


def _gdn_scan(u, w, qd, kt, qk, gcb, p, norm_w):
    s = u.shape[0]
    rb = min(512, s)
    row = lambda i: (i, 0)
    wide = pl.BlockSpec((rb, GDN_QK), row)
    return pl.pallas_call(
        functools.partial(_gdn_scan_kernel, rb=rb),
        out_shape=jax.ShapeDtypeStruct((s, GDN_V), BF16),
        grid=(s // rb,),
        in_specs=[wide, wide, wide, wide, pl.BlockSpec((rb, GDN_HEADS * GDN_CHUNK), row),
                  pl.BlockSpec((rb, LANES), row),
                  pl.BlockSpec((rb, GDN_V), lambda i: (i, P_Z // GDN_V)),
                  pl.BlockSpec((1, GDN_DV), lambda i: (0, 0))],
        out_specs=wide,
        scratch_shapes=[pltpu.VMEM((GDN_HEADS, GDN_DK, GDN_DV), F32)],
        compiler_params=_params(("arbitrary",)),
        name="gdn_scan",
    )(u, w, qd, kt, qk, gcb, p, norm_w.reshape(1, GDN_DV))


def _rope_table_kernel(pos_ref, inv_ref, cos_ref, sin_ref):
    ang = pos_ref[...].astype(F32) * inv_ref[...]
    lane = lax.broadcasted_iota(jnp.int32, ang.shape, 1)
    first_half = (lane % MLA_ROPE) < (MLA_ROPE // 2)
    cos_ref[...] = jnp.cos(ang)
    sin_ref[...] = jnp.where(first_half, -jnp.sin(ang), jnp.sin(ang))


def _rope_tables(positions):
    s = positions.shape[-1]
    tm = min(1024, s)
    half = MLA_ROPE // 2
    inv = (ROPE_THETA ** (-np.arange(half, dtype=np.float32) / half)).astype(np.float32)
    inv_row = jnp.asarray(np.tile(inv, LANES // half).reshape(1, LANES))
    return pl.pallas_call(
        _rope_table_kernel,
        out_shape=(jax.ShapeDtypeStruct((s, LANES), F32), jax.ShapeDtypeStruct((s, LANES), F32)),
        grid=(s // tm,),
        in_specs=[pl.BlockSpec((tm, 1), lambda i: (i, 0)), pl.BlockSpec((1, LANES), lambda i: (0, 0))],
        out_specs=(pl.BlockSpec((tm, LANES), lambda i: (i, 0)), pl.BlockSpec((tm, LANES), lambda i: (i, 0))),
        compiler_params=_params(("parallel",)),
        name="rope_tables",
    )(positions.reshape(s, 1), inv_row)


def _rope_apply(x, cos, sin_signed):
    width = x.shape[-1]
    half = MLA_ROPE // 2
    lane = lax.broadcasted_iota(jnp.int32, x.shape, 1)
    first_half = (lane % MLA_ROPE) < half
    swapped = jnp.where(first_half, pltpu.roll(x, width - half, 1), pltpu.roll(x, half, 1))
    return x * cos + swapped * sin_signed


def _mla_proj_kernel(cq_ref, ckv_ref, kr_ref, cos_ref, sin_ref, qn_ref, wuq_ref, kvn_ref, wuk_ref, wuvt_ref,
                     q_ref, k_ref, vt_ref):
    cos = cos_ref[...]
    sin = sin_ref[...]
    scale = (MLA_NOPE + MLA_ROPE) ** -0.5 * math.log2(math.e)
    cq = cq_ref[...].astype(F32)
    cqn = cq * lax.rsqrt(jnp.sum(cq * cq, axis=-1, keepdims=True) * (1.0 / MLA_Q_RANK) + EPS) * qn_ref[...]
    q = _dot(cqn.astype(BF16), wuq_ref[...])
    nope_w = MLA_HEADS * MLA_NOPE
    q_rope = _rope_apply(q[:, nope_w:], jnp.concatenate([cos, cos], -1), jnp.concatenate([sin, sin], -1))
    zeros = jnp.zeros((q.shape[0], MLA_QK_PAD - MLA_NOPE - MLA_ROPE), F32)
    ckvn = (_rms(ckv_ref[...].astype(F32)) * kvn_ref[...]).astype(BF16)
    k_nope = _dot(ckvn, wuk_ref[...])
    v_t = _dot_nt(wuvt_ref[...], ckvn)
    k_pe = _rope_apply(kr_ref[...].astype(F32), cos, sin)[:, :MLA_ROPE]
    for h in range(MLA_HEADS):
        qh = jnp.concatenate([q[:, h * MLA_NOPE:(h + 1) * MLA_NOPE],
                              q_rope[:, h * MLA_ROPE:(h + 1) * MLA_ROPE], zeros], axis=-1) * scale
        q_ref[h] = qh.astype(BF16)
        kh = jnp.concatenate([k_nope[:, h * MLA_NOPE:(h + 1) * MLA_NOPE], k_pe, zeros], axis=-1)
        k_ref[h] = kh.astype(BF16)
        vt_ref[h, 0] = v_t[h * MLA_DV:(h + 1) * MLA_DV, :].astype(BF16)


def _mla_proj(p, cos, sin, q_norm, w_uq, kv_norm, w_uk, w_uvt):
    s = p.shape[0]
    tm = min(MLA_BLOCK, s)
    const = lambda i: (0, 0)
    return pl.pallas_call(
        _mla_proj_kernel,
        out_shape=(jax.ShapeDtypeStruct((MLA_HEADS, s, MLA_QK_PAD), BF16),
                   jax.ShapeDtypeStruct((MLA_HEADS, s, MLA_QK_PAD), BF16),
                   jax.ShapeDtypeStruct((MLA_HEADS, s // tm, MLA_DV, tm), BF16)),
        grid=(s // tm,),
        in_specs=[pl.BlockSpec((tm, MLA_Q_RANK_PAD), lambda i: (i, P_CQ // MLA_Q_RANK_PAD)),
                  pl.BlockSpec((tm, LANES), lambda i: (i, P_CKV // LANES)),
                  pl.BlockSpec((tm, LANES), lambda i: (i, P_KROPE // LANES)),
                  pl.BlockSpec((tm, LANES), lambda i: (i, 0)),
                  pl.BlockSpec((tm, LANES), lambda i: (i, 0)),
                  pl.BlockSpec(q_norm.shape, const), pl.BlockSpec(w_uq.shape, const),
                  pl.BlockSpec(kv_norm.shape, const), pl.BlockSpec(w_uk.shape, const),
                  pl.BlockSpec(w_uvt.shape, const)],
        out_specs=(pl.BlockSpec((MLA_HEADS, tm, MLA_QK_PAD), lambda i: (0, i, 0)),
                   pl.BlockSpec((MLA_HEADS, tm, MLA_QK_PAD), lambda i: (0, i, 0)),
                   pl.BlockSpec((MLA_HEADS, 1, MLA_DV, tm), lambda i: (0, i, 0, 0))),
        compiler_params=_params(("parallel",)),
        name="mla_proj",
    )(p, p, p, cos, sin, q_norm, w_uq, kv_norm, w_uk, w_uvt)


def _mla_attn_kernel(q_ref, k_ref, vt_ref, o_ref, st0, st1, mb0, mb1, m_ref, l_ref, acc_ref, *, bk, nd):
    qi = pl.program_id(1)
    slots = ((st0, mb0), (st1, mb1))
    bq = nd * bk
    n0 = nd * qi
    tri = lax.broadcasted_iota(jnp.int32, (bk, bk), 0) <= lax.broadcasted_iota(jnp.int32, (bk, bk), 1)

    gq = bk // 2

    def scores(t, slot, c0, tri_mask):
        st_ref, mb_ref = slots[slot]
        r0 = pl.multiple_of(t * bk, bk)
        st = _dot_nt(k_ref[0, pl.ds(r0, bk), :], q_ref[0, c0:c0 + gq, :])
        if tri_mask is not None:
            st = jnp.where(tri_mask, st, -jnp.inf)
        st_ref[:, c0:c0 + gq] = st
        mb_ref[:, c0:c0 + gq] = jnp.max(st, axis=0, keepdims=True)

    def update(t, slot, c0):
        st_ref, mb_ref = slots[slot]
        cols = slice(c0, c0 + gq)
        m = m_ref[:, cols]
        m_new = jnp.maximum(m, mb_ref[:, cols])
        alpha = jnp.exp2(m - m_new)
        pexp = jnp.exp2(st_ref[:, cols] - m_new)
        l_ref[:, cols] = alpha * l_ref[:, cols] + jnp.sum(pexp, axis=0, keepdims=True)
        acc_ref[:, cols] = alpha * acc_ref[:, cols] + _dot(vt_ref[0, t], pexp.astype(BF16))
        m_ref[:, cols] = m_new

    def sweep(ts=None, ss=None, ds=None, tu=None, su=None, du=None):
        for c0 in range(0, bq, gq):
            g = c0 // bk
            if ts is not None and (ds is None or g >= ds):
                scores(ts, ss, c0, tri[:, c0 - g * bk:c0 - g * bk + gq] if g == ds else None)
            if tu is not None and (du is None or g >= du):
                update(tu, su, c0)

    m_ref[...] = jnp.full(m_ref.shape, -jnp.inf, F32)
    l_ref[...] = jnp.zeros(l_ref.shape, F32)
    acc_ref[...] = jnp.zeros(acc_ref.shape, F32)

    @pl.when(qi == 0)
    def _():
        sweep(ts=0, ss=0, ds=0)

    @pl.when(qi > 0)
    def _():
        sweep(ts=0, ss=0)

    def pair(j, carry):
        sweep(ts=2 * j + 1, ss=1, tu=2 * j, su=0)
        sweep(ts=2 * j + 2, ss=0, tu=2 * j + 1, su=1)
        return carry

    lax.fori_loop(0, n0 // 2 - 1, pair, 0)

    @pl.when(qi > 0)
    def _():
        sweep(ts=n0 - 1, ss=1, tu=n0 - 2, su=0)
        sweep(ts=n0, ss=0, ds=0, tu=n0 - 1, su=1)

    for d in range(1, nd):
        sweep(ts=n0 + d, ss=d % 2, ds=d, tu=n0 + d - 1, su=(d - 1) % 2, du=d - 1)
    sweep(tu=n0 + nd - 1, su=(nd - 1) % 2, du=nd - 1)
    o_ref[...] = (acc_ref[...] / l_ref[...]).T.astype(BF16)


def _mla_attn(q, k, vt):
    nh, s, _ = q.shape
    bk = vt.shape[-1]
    nd = 4 if s % (4 * bk) == 0 else 2
    bq = nd * bk
    assert s % bq == 0
    return pl.pallas_call(
        functools.partial(_mla_attn_kernel, bk=bk, nd=nd),
        out_shape=jax.ShapeDtypeStruct((s, nh * MLA_DV), BF16),
        grid=(nh, s // bq),
        in_specs=[pl.BlockSpec((1, bq, MLA_QK_PAD), lambda h, i: (h, i, 0)),
                  pl.BlockSpec((1, s, MLA_QK_PAD), lambda h, i: (h, 0, 0)),
                  pl.BlockSpec((1, s // bk, MLA_DV, bk), lambda h, i: (h, 0, 0, 0))],
        out_specs=pl.BlockSpec((bq, MLA_DV), lambda h, i: (i, h)),
        scratch_shapes=[pltpu.VMEM((bk, bq), F32), pltpu.VMEM((bk, bq), F32),
                        pltpu.VMEM((1, bq), F32), pltpu.VMEM((1, bq), F32),
                        pltpu.VMEM((1, bq), F32), pltpu.VMEM((1, bq), F32), pltpu.VMEM((MLA_DV, bq), F32)],
        compiler_params=_params(("parallel", "arbitrary")),
        name="mla_attn",
    )(q, k, vt)


def _swa_kernel(q_ref, k_ref, v_ref, kh_ref, vh_ref, sink_ref, o_ref, *, nblk):
    w = SWA_WINDOW
    g = SWA_GROUP
    i = pl.program_id(0)
    key = lax.broadcasted_iota(jnp.int32, (2 * w, g * w), 0)
    col = lax.broadcasted_iota(jnp.int32, (2 * w, g * w), 1)
    dist = col % w + w - key
    band = (dist >= 0) & (dist < w)
    first_band = band & ((key >= w) | (i > 0))
    dist_f = dist.astype(F32)
    biases = []
    for hk in range(SWA_KV_HEADS):
        slope = jnp.zeros((2 * w, g * w), F32)
        for gi, gq in enumerate(SWA_GORDER):
            slope = jnp.where(col // w == gi, 2.0 ** (-8.0 * (hk * g + gq + 1.0) / SWA_HEADS), slope)
        biases.append(-slope * dist_f)
    low = lax.broadcasted_iota(jnp.int32, (2 * w, 2 * SWA_DH), 1) < SWA_DH
    top = lax.broadcasted_iota(jnp.int32, (2 * SWA_DH, 1), 0) < SWA_DH
    for blk in range(nblk):
        rows = slice(blk * w, (blk + 1) * w)
        if blk == 0:
            k_prev, v_prev, valid = kh_ref[...], vh_ref[...], first_band
        else:
            k_prev, v_prev, valid = k_ref[(blk - 1) * w:blk * w, :], v_ref[(blk - 1) * w:blk * w, :], band
        kk = jnp.concatenate([k_prev, k_ref[rows, :]], axis=0).astype(F32)
        kk_sw = pltpu.roll(kk, SWA_DH, 1)
        vvt = jnp.concatenate([v_prev, v_ref[rows, :]], axis=0).astype(F32).T
        probs, inv_denoms = [], []
        for hk in range(SWA_KV_HEADS):
            c0 = hk * g * SWA_DH
            q2 = jnp.concatenate([q_ref[rows, c0:c0 + 2 * SWA_DH], q_ref[rows, c0 + 2 * SWA_DH:c0 + 4 * SWA_DH]],
                                 axis=0) * (SWA_DH ** -0.5)
            k_low = jnp.where(low, kk if hk == 0 else kk_sw, 0.0).astype(BF16)
            k_high = jnp.where(low, 0.0, kk_sw if hk == 0 else kk).astype(BF16)
            st = jnp.concatenate([_dot_nt(k_low, q2), _dot_nt(k_high, q2)], axis=1)
            st = jnp.where(valid, st + biases[hk], -jnp.inf)
            sink = sink_ref[hk]
            m = jnp.maximum(jnp.max(st, axis=0, keepdims=True), sink)
            pexp = jnp.exp(st - m)
            denom = jnp.sum(pexp, axis=0, keepdims=True) + jnp.exp(sink - m)
            probs.append(pexp.astype(BF16))
            inv_denoms.append(1.0 / denom)
        v0 = jnp.where(top, vvt, 0.0).astype(BF16)
        v1 = jnp.where(top, 0.0, vvt).astype(BF16)
        o_t = (_dot(v0, probs[0]) + _dot(v1, probs[1])) * jnp.where(top, inv_denoms[0], inv_denoms[1])
        for gi in range(g):
            o_ref[rows, gi * LANES:(gi + 1) * LANES] = o_t[:, gi * w:(gi + 1) * w].T.astype(BF16)


def _swa(p, sinks):
    s = p.shape[0]
    w = SWA_WINDOW
    rb = min(512, s)
    nblk = rb // w
    order = np.asarray([[hk * SWA_GROUP + gq for gq in SWA_GORDER] for hk in range(SWA_KV_HEADS)])
    sink_rows = jnp.repeat(sinks[order], w, axis=1).reshape(SWA_KV_HEADS, 1, SWA_GROUP * w)
    prev = lambda col: (lambda i: (jnp.maximum(i * nblk - 1, 0), col))
    return pl.pallas_call(
        functools.partial(_swa_kernel, nblk=nblk),
        out_shape=jax.ShapeDtypeStruct((s, SWA_OUT), BF16),
        grid=(s // rb,),
        in_specs=[pl.BlockSpec((rb, SWA_OUT), lambda i: (i, P_SWQ // SWA_OUT)),
                  pl.BlockSpec((rb, SWA_KV), lambda i: (i, P_SWK // SWA_KV)),
                  pl.BlockSpec((rb, SWA_KV), lambda i: (i, P_SWV // SWA_KV)),
                  pl.BlockSpec((w, SWA_KV), prev(P_SWK // SWA_KV)),
                  pl.BlockSpec((w, SWA_KV), prev(P_SWV // SWA_KV)),
                  pl.BlockSpec(sink_rows.shape, lambda i: (0, 0, 0))],
        out_specs=pl.BlockSpec((rb, SWA_OUT), lambda i: (i, 0)),
        compiler_params=_params(("parallel",)),
        name="swa",
    )(p, p, p, p, p, sink_rows)


def _outproj_kernel(oa_ref, ob_ref, oc_ref, w_ref, x_ref, pn_ref, g1_ref, fn_ref, sc_ref, sh_ref,
                    x1_ref, h2_ref, *, sub):
    a_w, b_w = oa_ref.shape[1], ob_ref.shape[1]
    for r0 in range(0, x_ref.shape[0], sub):
        rows = slice(r0, r0 + sub)
        mix = (_dot(oa_ref[rows, :], w_ref[0:a_w, :]) + _dot(ob_ref[rows, :], w_ref[a_w:a_w + b_w, :])
               + _dot(oc_ref[rows, :], w_ref[a_w + b_w:, :]))
        x1 = x_ref[rows, :] + g1_ref[...] * (_rms(mix) * pn_ref[...])
        x1_ref[rows, :] = x1
        h2_ref[rows, :] = _norm_mod(x1, fn_ref[...], sc_ref[...], sh_ref[...]).astype(BF16)


def _out_proj(o_a, o_b, o_c, w_out, layer, x, post_norm, gate1, ffn_norm, scale2, shift2):
    s, d = x.shape
    tm = min(512, s)
    row = lambda i: (i, 0)
    vec = pl.BlockSpec((1, d), lambda i: (0, 0))
    return pl.pallas_call(
        functools.partial(_outproj_kernel, sub=min(256, tm)),
        out_shape=(jax.ShapeDtypeStruct((s, d), F32), jax.ShapeDtypeStruct((s, d), BF16)),
        grid=(s // tm,),
        in_specs=[pl.BlockSpec((tm, o_a.shape[1]), row), pl.BlockSpec((tm, o_b.shape[1]), row),
                  pl.BlockSpec((tm, o_c.shape[1]), row),
                  pl.BlockSpec((None,) + w_out.shape[1:], lambda i: (layer, 0, 0), pipeline_mode=pl.Buffered(1)),
                  pl.BlockSpec((tm, d), row), vec, vec, vec, vec, vec],
        out_specs=(pl.BlockSpec((tm, d), row), pl.BlockSpec((tm, d), row)),
        compiler_params=_params(("parallel",)),
        name="out_proj",
    )(o_a, o_b, o_c, w_out, x, post_norm, gate1, ffn_norm, scale2, shift2)


def _gelu_tanh(x):
    return 0.5 * x * (1.0 + jnp.tanh(math.sqrt(2.0 / math.pi) * (x + 0.044715 * (x * x * x))))


def _ffn_up_kernel(h_ref, wg_ref, wu_ref, cg_ref, cu_ref, bg_ref, bu_ref, o_ref,
                   wg16_ref, wu16_ref, xg_ref, xu_ref, *, tm):
    @pl.when(pl.program_id(1) == 0)
    def _():
        wg16_ref[...] = wg_ref[...].astype(BF16)
        wu16_ref[...] = wu_ref[...].astype(BF16)
        xg_ref[0:8, :] = jnp.zeros((8, xg_ref.shape[1]), F32)
        xu_ref[0:8, :] = jnp.zeros((8, xu_ref.shape[1]), F32)

    h = h_ref[...]

    def conv(x_ref, w_ref, cw_ref, b_ref):
        x_ref[8:8 + tm, :] = _dot(h, w_ref[...])
        y = b_ref[...] + cw_ref[2:3, :] * x_ref[8:8 + tm, :]
        y = y + cw_ref[1:2, :] * x_ref[7:7 + tm, :]
        y = y + cw_ref[0:1, :] * x_ref[6:6 + tm, :]
        x_ref[0:8, :] = x_ref[tm:tm + 8, :]
        return y

    gate = conv(xg_ref, wg16_ref, cg_ref, bg_ref)
    up = conv(xu_ref, wu16_ref, cu_ref, bu_ref)
    o_ref[...] = (_gelu_tanh(gate) * up).astype(BF16)


def _ffn_up(h2, w_up, layer, conv_w, conv_b):
    s, d = h2.shape
    d_ff = w_up.shape[2] // 2
    tm, tn = min(1024, s), 512
    nj = d_ff // tn
    lo = lambda j, i: (0, j)
    hi = lambda j, i: (0, j + nj)
    w_lo = pl.BlockSpec((None, d, tn), lambda j, i: (layer, 0, j))
    w_hi = pl.BlockSpec((None, d, tn), lambda j, i: (layer, 0, j + nj))
    return pl.pallas_call(
        functools.partial(_ffn_up_kernel, tm=tm),
        out_shape=jax.ShapeDtypeStruct((s, d_ff), BF16),
        grid=(nj, s // tm),
        in_specs=[pl.BlockSpec((tm, d), lambda j, i: (i, 0)),
                  w_lo, w_hi,
                  pl.BlockSpec((FFN_CONV, tn), lo), pl.BlockSpec((FFN_CONV, tn), hi),
                  pl.BlockSpec((1, tn), lo), pl.BlockSpec((1, tn), hi)],
        out_specs=pl.BlockSpec((tm, tn), lambda j, i: (i, j)),
        scratch_shapes=[pltpu.VMEM((d, tn), BF16), pltpu.VMEM((d, tn), BF16),
                        pltpu.VMEM((tm + 8, tn), F32), pltpu.VMEM((tm + 8, tn), F32)],
        compiler_params=_params(("parallel", "arbitrary")),
        name="ffn_up",
    )(h2, w_up, w_up, conv_w, conv_w, conv_b, conv_b)


def _ffn_down_kernel(g_ref, w_ref, x_ref, pn_ref, g2_ref, *rest):
    y = _dot(g_ref[...], w_ref[...])
    x2 = x_ref[...] + g2_ref[...] * (_rms(y) * pn_ref[...])
    if len(rest) == 1:
        rest[0][...] = x2
    else:
        nn_ref, sc_ref, sh_ref, x2_ref, hn_ref = rest
        x2_ref[...] = x2
        hn_ref[...] = _norm_mod(x2, nn_ref[...], sc_ref[...], sh_ref[...]).astype(BF16)


def _ffn_down(g, w_down, layer, x1, post_norm, gate2, next_pre_norm=None):
    s, d = x1.shape
    d_ff = g.shape[1]
    tm = min(256, s)
    row = lambda i: (i, 0)
    vec = pl.BlockSpec((1, d), lambda i: (0, 0))
    tile_f32 = jax.ShapeDtypeStruct((s, d), F32)
    tile = pl.BlockSpec((tm, d), row)
    last = next_pre_norm is None
    return pl.pallas_call(
        _ffn_down_kernel,
        out_shape=tile_f32 if last else (tile_f32, jax.ShapeDtypeStruct((s, d), BF16)),
        grid=(s // tm,),
        in_specs=[pl.BlockSpec((tm, d_ff), row),
                  pl.BlockSpec((None, d_ff, d), lambda i: (layer, 0, 0), pipeline_mode=pl.Buffered(1)),
                  tile, vec, vec] + ([] if last else [vec, vec, vec]),
        out_specs=tile if last else (tile, tile),
        compiler_params=_params(("parallel",)),
        name="ffn_down",
    )(g, w_down, x1, post_norm, gate2, *(() if last else next_pre_norm))


def _layout_w_in(w_in):
    depth, d, _ = w_in.shape
    sizes = (GDN_QK, GDN_QK, GDN_V, GDN_V, GDN_HEADS, GDN_HEADS, MLA_Q_RANK, MLA_KV_RANK, MLA_ROPE,
             SWA_OUT, SWA_KV, SWA_KV)
    offs = np.concatenate([[0], np.cumsum(sizes)])
    part = lambda n: w_in[:, :, offs[n]:offs[n + 1]].astype(BF16)
    zeros = lambda n: jnp.zeros((depth, d, n), BF16)
    w_p = jnp.concatenate([w_in[:, :, :offs[4]].astype(BF16),
                           part(6), zeros(MLA_Q_RANK_PAD - MLA_Q_RANK),
                           part(9), part(7), part(8), zeros(LANES - MLA_ROPE), part(10), part(11)], axis=2)
    w_ab = jnp.concatenate([part(4), part(5), zeros(LANES - 2 * GDN_HEADS)], axis=2)
    return w_p, w_ab


def _layout_w_out(w_out):
    depth, d_mix, d = w_out.shape
    a = d_mix - SWA_OUT
    wc = w_out[:, a:].reshape(depth, SWA_KV_HEADS, SWA_GROUP, SWA_DH, d)[:, :, np.asarray(SWA_GORDER)]
    wc = wc.transpose(0, 2, 1, 3, 4).reshape(depth, SWA_OUT, d)
    return jnp.concatenate([w_out[:, :a].astype(BF16), wc.astype(BF16)], axis=1)


def _layout_mla(q_norm, w_uq, kv_norm, w_ukv):
    dqk = MLA_NOPE + MLA_ROPE
    uq = w_uq.reshape(MLA_Q_RANK, MLA_HEADS, dqk)
    uq = jnp.concatenate([uq[:, :, :MLA_NOPE].reshape(MLA_Q_RANK, -1), uq[:, :, MLA_NOPE:].reshape(MLA_Q_RANK, -1)],
                         axis=1)
    uq = jnp.pad(uq, ((0, MLA_Q_RANK_PAD - MLA_Q_RANK), (0, 0))).astype(BF16)
    qn = jnp.pad(q_norm, (0, MLA_Q_RANK_PAD - MLA_Q_RANK)).reshape(1, MLA_Q_RANK_PAD)
    ukv = w_ukv.reshape(MLA_KV_RANK, MLA_HEADS, MLA_NOPE + MLA_DV)
    uk = ukv[:, :, :MLA_NOPE].reshape(MLA_KV_RANK, -1).astype(BF16)
    uvt = ukv[:, :, MLA_NOPE:].reshape(MLA_KV_RANK, -1).T.astype(BF16)
    return qn, uq, kv_norm.reshape(1, MLA_KV_RANK), uk, uvt


def kernel(x, c, positions, ada_w, ada_b, mix_pre_norm, mix_post_norm, w_in, w_out, gdn_conv, gdn_a_log, gdn_dt_bias, gdn_norm, mla_q_norm, mla_w_uq, mla_kv_norm, mla_w_ukv, swa_sinks, ffn_pre_norm, ffn_post_norm, ffn_w_up, ffn_conv, ffn_conv_b, ffn_w_down):
    batch, s, d = x.shape
    assert batch == 1, "kernels are written for a single sequence"
    depth = ada_w.shape[0]
    xs = x.reshape(s, d)
    mod = _adaln_mod(c, ada_w, ada_b).reshape(depth, N_MOD, 1, d)
    cos, sin = _rope_tables(positions)
    vec = lambda a: a.reshape(1, d)

    w_p, w_ab = _layout_w_in(w_in)
    w_o = _layout_w_out(w_out)
    w_down = ffn_w_down.astype(BF16)

    h = _pre_norm(xs, vec(mix_pre_norm[0]), mod[0, 1], mod[0, 0])
    for l in range(depth):
        shift1, scale1, gate1, shift2, scale2, gate2 = (mod[l, n] for n in range(N_MOD))
        p, ab = _in_proj(h, w_p, w_ab, l)

        q_a, k_a, v_a, gcb, gct = _gdn_prep(p, ab, gdn_conv[l], gdn_a_log[l], gdn_dt_bias[l])
        u, w, qd, kt, qk = _gdn_local(q_a, k_a, v_a, gcb, gct)
        o_a = _gdn_scan(u, w, qd, kt, qk, gcb, p, gdn_norm[l])

        q_b, k_b, v_b = _mla_proj(p, cos, sin, *_layout_mla(mla_q_norm[l], mla_w_uq[l], mla_kv_norm[l],
                                                            mla_w_ukv[l]))
        o_b = _mla_attn(q_b, k_b, v_b)

        o_c = _swa(p, swa_sinks[l])

        x1, h2 = _out_proj(o_a, o_b, o_c, w_o, l, xs, vec(mix_post_norm[l]), gate1,
                           vec(ffn_pre_norm[l]), scale2, shift2)
        g = _ffn_up(h2, ffn_w_up, l, ffn_conv[l], ffn_conv_b[l].reshape(1, -1))
        if l + 1 < depth:
            xs, h = _ffn_down(g, w_down, l, x1, vec(ffn_post_norm[l]), gate2,
                              (vec(mix_pre_norm[l + 1]), mod[l + 1, 1], mod[l + 1, 0]))
        else:
            xs = _ffn_down(g, w_down, l, x1, vec(ffn_post_norm[l]), gate2)
    return xs.reshape(batch, s, d)
```

```python
import functools
import math

import numpy as np
import jax
import jax.numpy as jnp
from jax import lax
from jax.experimental import pallas as pl
from jax.experimental.pallas import tpu as pltpu

F32 = jnp.float32
BF16 = jnp.bfloat16

EPS = 1e-6
N_MOD = 6
GDN_HEADS = 8
GDN_DK = 128
GDN_DV = 128
GDN_CONV = 4
GDN_CHUNK = 64
GDN_QK = GDN_HEADS * GDN_DK
GDN_V = GDN_HEADS * GDN_DV
MLA_HEADS = 4
MLA_Q_RANK = 448
MLA_Q_RANK_PAD = 512
MLA_KV_RANK = 128
MLA_NOPE = 128
MLA_ROPE = 64
MLA_DV = 128
MLA_QK_PAD = 256
MLA_BLOCK = 512
ROPE_THETA = 10000.0
SWA_HEADS = 8
SWA_KV_HEADS = 2
SWA_GROUP = SWA_HEADS // SWA_KV_HEADS
SWA_DH = 64
SWA_WINDOW = 128
SWA_OUT = SWA_HEADS * SWA_DH
SWA_KV = SWA_KV_HEADS * SWA_DH
SWA_GORDER = (0, 2, 1, 3)
FFN_CONV = 3

LANES = 128
GDN_GROUP = 256
VMEM_LIMIT_MB = 56

P_QKV = 0
P_Z = 3072
P_CQ = 4096
P_SWQ = 4608
P_CKV = 5120
P_KROPE = 5248
P_SWK = 5376
P_SWV = 5504
P_WIDTH = 5632


def _params(semantics):
    return pltpu.CompilerParams(dimension_semantics=semantics, vmem_limit_bytes=VMEM_LIMIT_MB << 20)


def _sigmoid(x):
    return 0.5 + 0.5 * jnp.tanh(0.5 * x)


def _silu(x):
    h = 0.5 * x
    return h + h * jnp.tanh(h)


def _softplus(x):
    return jnp.maximum(x, 0.0) + jnp.log(1.0 + jnp.exp(-jnp.abs(x)))


def _rms(x):
    return x * lax.rsqrt(jnp.mean(x * x, axis=-1, keepdims=True) + EPS)


def _dot(a, b):
    return jnp.dot(a, b, preferred_element_type=F32)


def _dot_nt(a, b):
    return lax.dot_general(a, b, (((1,), (1,)), ((), ())), preferred_element_type=F32)


def _dot_tn(a, b):
    return lax.dot_general(a, b, (((0,), (0,)), ((), ())), preferred_element_type=F32)


def _mod_kernel(c_ref, w_ref, b_ref, o_ref, *, d, kc):
    cact = _silu(c_ref[...])
    acc = b_ref[0]
    for k0 in range(0, d, kc):
        acc = acc + jnp.sum(w_ref[0, k0:k0 + kc, :] * cact[k0:k0 + kc], axis=0, keepdims=True)
    o_ref[0] = acc


def _adaln_mod(c, ada_w, ada_b):
    depth, d, n = ada_w.shape
    tn = 1024
    return pl.pallas_call(
        functools.partial(_mod_kernel, d=d, kc=256),
        out_shape=jax.ShapeDtypeStruct((depth, 1, n), F32),
        grid=(depth, n // tn),
        in_specs=[pl.BlockSpec((d, 1), lambda l, j: (0, 0)),
                  pl.BlockSpec((1, d, tn), lambda l, j: (l, 0, j)),
                  pl.BlockSpec((1, 1, tn), lambda l, j: (l, 0, j))],
        out_specs=pl.BlockSpec((1, 1, tn), lambda l, j: (l, 0, j)),
        compiler_params=_params(("parallel", "parallel")),
        name="adaln_mod",
    )(c.reshape(d, 1), ada_w, ada_b.reshape(depth, 1, n))


def _norm_mod(x, w, scale, shift):
    return _rms(x) * w * (1.0 + scale) + shift


def _norm_mod_kernel(x_ref, w_ref, sc_ref, sh_ref, h_ref):
    h_ref[...] = _norm_mod(x_ref[...], w_ref[...], sc_ref[...], sh_ref[...]).astype(BF16)


def _pre_norm(x, w, scale, shift):
    s, d = x.shape
    tm = min(1024, s)
    row = pl.BlockSpec((1, d), lambda i: (0, 0))
    return pl.pallas_call(
        _norm_mod_kernel,
        out_shape=jax.ShapeDtypeStruct((s, d), BF16),
        grid=(s // tm,),
        in_specs=[pl.BlockSpec((tm, d), lambda i: (i, 0)), row, row, row],
        out_specs=pl.BlockSpec((tm, d), lambda i: (i, 0)),
        compiler_params=_params(("parallel",)),
        name="pre_norm",
    )(x, w, scale, shift)


def _inproj_kernel(h_ref, w_ref, wab_ref, p_ref, ab_ref):
    h = h_ref[...]
    p_ref[...] = _dot(h, w_ref[...]).astype(BF16)

    @pl.when(pl.program_id(1) == 0)
    def _():
        ab_ref[...] = _dot(h, wab_ref[...])


def _in_proj(h, w_p, w_ab, layer):
    s, d = h.shape
    n = w_p.shape[2]
    tm, tn = min(2048, s), 512
    return pl.pallas_call(
        _inproj_kernel,
        out_shape=(jax.ShapeDtypeStruct((s, n), BF16), jax.ShapeDtypeStruct((s, LANES), F32)),
        grid=(s // tm, n // tn),
        in_specs=[pl.BlockSpec((tm, d), lambda i, j: (i, 0)),
                  pl.BlockSpec((None, d, tn), lambda i, j: (layer, 0, j)),
                  pl.BlockSpec((None, d, LANES), lambda i, j: (layer, 0, 0))],
        out_specs=(pl.BlockSpec((tm, tn), lambda i, j: (i, j)),
                   pl.BlockSpec((tm, LANES), lambda i, j: (i, 0))),
        compiler_params=_params(("parallel", "arbitrary")),
        name="in_proj",
    )(h, w_p, w_ab)


def _gdn_prep_kernel(x_ref, halo_ref, cw_ref, ab_ref, alog_ref, dtb_ref, tri_ref,
                     q_ref, k_ref, v_ref, gcb_ref, gct_ref, xp_ref, *, tm):
    i = pl.program_id(0)
    outs = (q_ref, k_ref, v_ref)
    for grp in range(3):
        c0 = grp * GDN_QK
        halo = halo_ref[:, c0:c0 + GDN_QK].astype(F32)
        xp_ref[0:8, :] = jnp.where(i > 0, halo, 0.0)
        xp_ref[8:8 + tm, :] = x_ref[:, c0:c0 + GDN_QK].astype(F32)
        y = cw_ref[3:4, c0:c0 + GDN_QK] * xp_ref[8:8 + tm, :]
        for j in range(GDN_CONV - 1):
            off = 8 - (GDN_CONV - 1) + j
            y = y + cw_ref[j:j + 1, c0:c0 + GDN_QK] * xp_ref[off:off + tm, :]
        y = _silu(y)
        for h in range(GDN_HEADS):
            yh = y[:, h * GDN_DK:(h + 1) * GDN_DK]
            if grp < 2:
                yh = yh * lax.rsqrt(jnp.sum(yh * yh, axis=-1, keepdims=True) + EPS)
            if grp == 0:
                yh = yh * (GDN_DK ** -0.5)
            outs[grp][:, h * GDN_DK:(h + 1) * GDN_DK] = yh.astype(BF16)

    ab = ab_ref[...]
    g = -jnp.exp(alog_ref[...]) * _softplus(ab + dtb_ref[...])
    gc = jnp.dot(tri_ref[...], g, preferred_element_type=F32, precision=lax.Precision.HIGHEST)
    lane = lax.broadcasted_iota(jnp.int32, (1, LANES), 1)
    gcb = jnp.where(lane < GDN_HEADS, gc, _sigmoid(ab))
    gcb_ref[...] = gcb
    gct_ref[...] = gcb.T[0:2 * GDN_HEADS, :]


def _gdn_prep(p, ab, conv_w, a_log, dt_bias):
    s = p.shape[0]
    tm = min(512, s)
    c3 = 3 * GDN_QK
    alog_row = jnp.zeros((1, LANES), F32).at[0, :GDN_HEADS].set(a_log)
    dtb_row = jnp.zeros((1, LANES), F32).at[0, :GDN_HEADS].set(dt_bias)
    r = np.arange(tm)
    tri = jnp.asarray(((r[:, None] >= r[None, :]) & (r[:, None] // GDN_CHUNK == r[None, :] // GDN_CHUNK))
                      .astype(np.float32))
    hb = tm // 8
    row = lambda i: (i, 0)
    return pl.pallas_call(
        functools.partial(_gdn_prep_kernel, tm=tm),
        out_shape=(jax.ShapeDtypeStruct((s, GDN_QK), BF16), jax.ShapeDtypeStruct((s, GDN_QK), BF16),
                   jax.ShapeDtypeStruct((s, GDN_V), BF16), jax.ShapeDtypeStruct((s, LANES), F32),
                   jax.ShapeDtypeStruct((2 * GDN_HEADS, s), F32)),
        grid=(s // tm,),
        in_specs=[pl.BlockSpec((tm, c3), row),
                  pl.BlockSpec((8, c3), lambda i: (jnp.maximum(i * hb - 1, 0), 0)),
                  pl.BlockSpec((GDN_CONV, c3), lambda i: (0, 0)),
                  pl.BlockSpec((tm, LANES), row),
                  pl.BlockSpec((1, LANES), lambda i: (0, 0)),
                  pl.BlockSpec((1, LANES), lambda i: (0, 0)),
                  pl.BlockSpec((tm, tm), lambda i: (0, 0))],
        out_specs=(pl.BlockSpec((tm, GDN_QK), row), pl.BlockSpec((tm, GDN_QK), row),
                   pl.BlockSpec((tm, GDN_V), row), pl.BlockSpec((tm, LANES), row),
                   pl.BlockSpec((2 * GDN_HEADS, tm), lambda i: (0, i))),
        scratch_shapes=[pltpu.VMEM((tm + 8, GDN_QK), F32)],
        compiler_params=_params(("parallel",)),
        name="gdn_prep",
    )(p, p, conv_w, ab, alog_row, dtb_row, tri)


def _gdn_local_kernel(q_ref, k_ref, v_ref, gcb_ref, gct_ref,
                      u_ref, w_ref, qd_ref, kt_ref, qk_ref, *, groups):
    n = GDN_GROUP
    c = GDN_CHUNK
    ri = lax.broadcasted_iota(jnp.int32, (n, n), 0)
    ci = lax.broadcasted_iota(jnp.int32, (n, n), 1)
    same_chunk = (ri // c) == (ci // c)
    incl = same_chunk & (ri >= ci)
    strict = same_chunk & (ri > ci)
    eye = (ri == ci).astype(F32)
    last_sel = (ci == (ri // c) * c + (c - 1)).astype(F32)
    heads = range(GDN_HEADS)
    sls = [slice(h * GDN_DK, (h + 1) * GDN_DK) for h in heads]
    lane_chunk = lax.broadcasted_iota(jnp.int32, (c, n), 1) // c
    units = [(g, h) for g in range(groups) for h in heads]

    def to_cat(m):
        out = m[0:c]
        for j in range(1, n // c):
            out = out + m[j * c:(j + 1) * c]
        return out

    def to_bd(m_cat):
        zero = jnp.zeros_like(m_cat)
        return jnp.concatenate([jnp.where(lane_chunk == j, m_cat, zero) for j in range(n // c)], axis=0)

    lmats, tcs, rhss = {}, {}, {}
    for g in range(groups):
        rows = slice(g * n, (g + 1) * n)
        gcb = gcb_ref[rows, :]
        gc_last_all = jnp.dot(last_sel, gcb, preferred_element_type=F32, precision=lax.Precision.HIGHEST)
        for h in heads:
            sl = sls[h]
            q = q_ref[rows, sl].astype(F32)
            k16 = k_ref[rows, sl]
            k = k16.astype(F32)
            v = v_ref[rows, sl].astype(F32)
            gc_col = gcb[:, h:h + 1]
            beta = gcb[:, GDN_HEADS + h:GDN_HEADS + h + 1]
            gc_row = gct_ref[h:h + 1, rows]
            gc_last = gc_last_all[:, h:h + 1]
            diff = gc_col - gc_row
            decay = jnp.where(incl, jnp.exp(jnp.where(incl, diff, 0.0)), 0.0)
            kb = k * beta
            kk = _dot_nt(kb.astype(BF16), k16)
            lmat = jnp.where(strict, kk * decay, 0.0)
            lmats[g, h] = lmat
            tcs[g, h] = to_cat(eye - jnp.where((ri // 2) == (ci // 2), lmat, 0.0))
            eg = jnp.exp(gc_col)
            rhss[g, h] = jnp.concatenate([v * beta, kb * eg], axis=-1).astype(BF16)
            qd_ref[rows, sl] = (q * eg).astype(BF16)
            kt_ref[rows, sl] = (k * jnp.exp(gc_last - gc_col)).astype(BF16)
            qk = _dot_nt(q_ref[rows, sl], k16) * decay
            qk_c = jnp.concatenate([qk[j * c:(j + 1) * c, j * c:(j + 1) * c] for j in range(n // c)], axis=0)
            qk_ref[rows, h * c:(h + 1) * c] = qk_c.astype(BF16)
    b = 2
    while b < c:
        off_diag = ((ri // (2 * b)) == (ci // (2 * b))) & ((ri // b) != (ci // b))
        t16s = {u: tcs[u].astype(BF16) for u in units}
        ys = {u: _dot(t16s[u], jnp.where(off_diag, lmats[u], 0.0).astype(BF16)).astype(BF16) for u in units}
        tcs = {u: tcs[u] - _dot(ys[u], to_bd(t16s[u])) for u in units}
        b *= 2
    for g, h in units:
        rows = slice(g * n, (g + 1) * n)
        uw = _dot(to_bd(tcs[g, h].astype(BF16)), rhss[g, h])
        u_ref[rows, sls[h]] = uw[:, :GDN_DV].astype(BF16)
        w_ref[rows, sls[h]] = uw[:, GDN_DV:].astype(BF16)


def _gdn_local(q, k, v, gcb, gct):
    s = q.shape[0]
    groups = 2 if s % (2 * GDN_GROUP) == 0 else 1
    n = groups * GDN_GROUP
    row = lambda i: (i, 0)
    wide = pl.BlockSpec((n, GDN_QK), row)
    return pl.pallas_call(
        functools.partial(_gdn_local_kernel, groups=groups),
        out_shape=(jax.ShapeDtypeStruct((s, GDN_V), BF16), jax.ShapeDtypeStruct((s, GDN_QK), BF16),
                   jax.ShapeDtypeStruct((s, GDN_QK), BF16), jax.ShapeDtypeStruct((s, GDN_QK), BF16),
                   jax.ShapeDtypeStruct((s, GDN_HEADS * GDN_CHUNK), BF16)),
        grid=(s // n,),
        in_specs=[wide, wide, wide, pl.BlockSpec((n, LANES), row),
                  pl.BlockSpec((2 * GDN_HEADS, n), lambda i: (0, i))],
        out_specs=(wide, wide, wide, wide, pl.BlockSpec((n, GDN_HEADS * GDN_CHUNK), row)),
        compiler_params=_params(("parallel",)),
        name="gdn_local",
    )(q, k, v, gcb, gct)


def _gdn_scan_kernel(u_ref, w_ref, qd_ref, kt_ref, qk_ref, gcb_ref, z_ref, nw_ref, o_ref, st_ref, *, rb):
    c = GDN_CHUNK

    @pl.when(pl.program_id(0) == 0)
    def _():
        st_ref[...] = jnp.zeros_like(st_ref)

    nw = nw_ref[...]

    def chunk(ci, carry):
        r0 = pl.multiple_of(ci * c, c)
        rows = pl.ds(r0, c)
        gt_row = jnp.exp(gcb_ref[pl.ds(r0 + c - 1, 1), :])
        heads = range(GDN_HEADS)
        sls = [slice(h * GDN_DK, (h + 1) * GDN_DK) for h in heads]
        sts = [st_ref[h] for h in heads]
        r1s = [_dot(jnp.concatenate([w_ref[rows, sls[h]], qd_ref[rows, sls[h]]], axis=0),
                    sts[h].astype(BF16)) for h in heads]
        vns = [(u_ref[rows, sls[h]].astype(F32) - r1s[h][0:c]).astype(BF16) for h in heads]
        for h in heads:
            st_ref[h] = sts[h] * gt_row[:, h:h + 1] + _dot_tn(kt_ref[rows, sls[h]], vns[h])
        os_ = [r1s[h][c:2 * c] + _dot(qk_ref[rows, h * c:(h + 1) * c], vns[h]) for h in heads]
        for h in heads:
            o = _rms(os_[h]) * nw * _silu(z_ref[rows, sls[h]].astype(F32))
            o_ref[rows, sls[h]] = o.astype(BF16)
        return carry

    unroll = min(8, rb // c)
    assert (rb // c) % unroll == 0

    def chunks(i, carry):
        for u in range(unroll):
            carry = chunk(unroll * i + u, carry)
        return carry

    lax.fori_loop(0, rb // (unroll * c), chunks, 0)


def _gdn_scan(u, w, qd, kt, qk, gcb, p, norm_w):
    s = u.shape[0]
    rb = min(512, s)
    row = lambda i: (i, 0)
    wide = pl.BlockSpec((rb, GDN_QK), row)
    return pl.pallas_call(
        functools.partial(_gdn_scan_kernel, rb=rb),
        out_shape=jax.ShapeDtypeStruct((s, GDN_V), BF16),
        grid=(s // rb,),
        in_specs=[wide, wide, wide, wide, pl.BlockSpec((rb, GDN_HEADS * GDN_CHUNK), row),
                  pl.BlockSpec((rb, LANES), row),
                  pl.BlockSpec((rb, GDN_V), lambda i: (i, P_Z // GDN_V)),
                  pl.BlockSpec((1, GDN_DV), lambda i: (0, 0))],
        out_specs=wide,
        scratch_shapes=[pltpu.VMEM((GDN_HEADS, GDN_DK, GDN_DV), F32)],
        compiler_params=_params(("arbitrary",)),
        name="gdn_scan",
    )(u, w, qd, kt, qk, gcb, p, norm_w.reshape(1, GDN_DV))


def _rope_table_kernel(pos_ref, inv_ref, cos_ref, sin_ref):
    ang = pos_ref[...].astype(F32) * inv_ref[...]
    lane = lax.broadcasted_iota(jnp.int32, ang.shape, 1)
    first_half = (lane % MLA_ROPE) < (MLA_ROPE // 2)
    cos_ref[...] = jnp.cos(ang)
    sin_ref[...] = jnp.where(first_half, -jnp.sin(ang), jnp.sin(ang))


def _rope_tables(positions):
    s = positions.shape[-1]
    tm = min(1024, s)
    half = MLA_ROPE // 2
    inv = (ROPE_THETA ** (-np.arange(half, dtype=np.float32) / half)).astype(np.float32)
    inv_row = jnp.asarray(np.tile(inv, LANES // half).reshape(1, LANES))
    return pl.pallas_call(
        _rope_table_kernel,
        out_shape=(jax.ShapeDtypeStruct((s, LANES), F32), jax.ShapeDtypeStruct((s, LANES), F32)),
        grid=(s // tm,),
        in_specs=[pl.BlockSpec((tm, 1), lambda i: (i, 0)), pl.BlockSpec((1, LANES), lambda i: (0, 0))],
        out_specs=(pl.BlockSpec((tm, LANES), lambda i: (i, 0)), pl.BlockSpec((tm, LANES), lambda i: (i, 0))),
        compiler_params=_params(("parallel",)),
        name="rope_tables",
    )(positions.reshape(s, 1), inv_row)


def _rope_apply(x, cos, sin_signed):
    width = x.shape[-1]
    half = MLA_ROPE // 2
    lane = lax.broadcasted_iota(jnp.int32, x.shape, 1)
    first_half = (lane % MLA_ROPE) < half
    swapped = jnp.where(first_half, pltpu.roll(x, width - half, 1), pltpu.roll(x, half, 1))
    return x * cos + swapped * sin_signed


def _mla_proj_kernel(cq_ref, ckv_ref, kr_ref, cos_ref, sin_ref, qn_ref, wuq_ref, kvn_ref, wuk_ref, wuvt_ref,
                     q_ref, k_ref, vt_ref):
    cos = cos_ref[...]
    sin = sin_ref[...]
    scale = (MLA_NOPE + MLA_ROPE) ** -0.5 * math.log2(math.e)
    cq = cq_ref[...].astype(F32)
    cqn = cq * lax.rsqrt(jnp.sum(cq * cq, axis=-1, keepdims=True) * (1.0 / MLA_Q_RANK) + EPS) * qn_ref[...]
    q = _dot(cqn.astype(BF16), wuq_ref[...])
    nope_w = MLA_HEADS * MLA_NOPE
    q_rope = _rope_apply(q[:, nope_w:], jnp.concatenate([cos, cos], -1), jnp.concatenate([sin, sin], -1))
    zeros = jnp.zeros((q.shape[0], MLA_QK_PAD - MLA_NOPE - MLA_ROPE), F32)
    ckvn = (_rms(ckv_ref[...].astype(F32)) * kvn_ref[...]).astype(BF16)
    k_nope = _dot(ckvn, wuk_ref[...])
    v_t = _dot_nt(wuvt_ref[...], ckvn)
    k_pe = _rope_apply(kr_ref[...].astype(F32), cos, sin)[:, :MLA_ROPE]
    for h in range(MLA_HEADS):
        qh = jnp.concatenate([q[:, h * MLA_NOPE:(h + 1) * MLA_NOPE],
                              q_rope[:, h * MLA_ROPE:(h + 1) * MLA_ROPE], zeros], axis=-1) * scale
        q_ref[h] = qh.astype(BF16)
        kh = jnp.concatenate([k_nope[:, h * MLA_NOPE:(h + 1) * MLA_NOPE], k_pe, zeros], axis=-1)
        k_ref[h] = kh.astype(BF16)
        vt_ref[h, 0] = v_t[h * MLA_DV:(h + 1) * MLA_DV, :].astype(BF16)


def _mla_proj(p, cos, sin, q_norm, w_uq, kv_norm, w_uk, w_uvt):
    s = p.shape[0]
    tm = min(MLA_BLOCK, s)
    const = lambda i: (0, 0)
    return pl.pallas_call(
        _mla_proj_kernel,
        out_shape=(jax.ShapeDtypeStruct((MLA_HEADS, s, MLA_QK_PAD), BF16),
                   jax.ShapeDtypeStruct((MLA_HEADS, s, MLA_QK_PAD), BF16),
                   jax.ShapeDtypeStruct((MLA_HEADS, s // tm, MLA_DV, tm), BF16)),
        grid=(s // tm,),
        in_specs=[pl.BlockSpec((tm, MLA_Q_RANK_PAD), lambda i: (i, P_CQ // MLA_Q_RANK_PAD)),
                  pl.BlockSpec((tm, LANES), lambda i: (i, P_CKV // LANES)),
                  pl.BlockSpec((tm, LANES), lambda i: (i, P_KROPE // LANES)),
                  pl.BlockSpec((tm, LANES), lambda i: (i, 0)),
                  pl.BlockSpec((tm, LANES), lambda i: (i, 0)),
                  pl.BlockSpec(q_norm.shape, const), pl.BlockSpec(w_uq.shape, const),
                  pl.BlockSpec(kv_norm.shape, const), pl.BlockSpec(w_uk.shape, const),
                  pl.BlockSpec(w_uvt.shape, const)],
        out_specs=(pl.BlockSpec((MLA_HEADS, tm, MLA_QK_PAD), lambda i: (0, i, 0)),
                   pl.BlockSpec((MLA_HEADS, tm, MLA_QK_PAD), lambda i: (0, i, 0)),
                   pl.BlockSpec((MLA_HEADS, 1, MLA_DV, tm), lambda i: (0, i, 0, 0))),
        compiler_params=_params(("parallel",)),
        name="mla_proj",
    )(p, p, p, cos, sin, q_norm, w_uq, kv_norm, w_uk, w_uvt)


def _mla_attn_kernel(q_ref, k_ref, vt_ref, o_ref, st0, st1, mb0, mb1, m_ref, l_ref, acc_ref, *, bk, nd):
    qi = pl.program_id(1)
    slots = ((st0, mb0), (st1, mb1))
    bq = nd * bk
    n0 = nd * qi
    tri = lax.broadcasted_iota(jnp.int32, (bk, bk), 0) <= lax.broadcasted_iota(jnp.int32, (bk, bk), 1)

    gq = bk // 2

    def scores(t, slot, c0, tri_mask):
        st_ref, mb_ref = slots[slot]
        r0 = pl.multiple_of(t * bk, bk)
        st = _dot_nt(k_ref[0, pl.ds(r0, bk), :], q_ref[0, c0:c0 + gq, :])
        if tri_mask is not None:
            st = jnp.where(tri_mask, st, -jnp.inf)
        st_ref[:, c0:c0 + gq] = st
        mb_ref[:, c0:c0 + gq] = jnp.max(st, axis=0, keepdims=True)

    def update(t, slot, c0):
        st_ref, mb_ref = slots[slot]
        cols = slice(c0, c0 + gq)
        m = m_ref[:, cols]
        m_new = jnp.maximum(m, mb_ref[:, cols])
        alpha = jnp.exp2(m - m_new)
        pexp = jnp.exp2(st_ref[:, cols] - m_new)
        l_ref[:, cols] = alpha * l_ref[:, cols] + jnp.sum(pexp, axis=0, keepdims=True)
        acc_ref[:, cols] = alpha * acc_ref[:, cols] + _dot(vt_ref[0, t], pexp.astype(BF16))
        m_ref[:, cols] = m_new

    def sweep(ts=None, ss=None, ds=None, tu=None, su=None, du=None):
        for c0 in range(0, bq, gq):
            g = c0 // bk
            if ts is not None and (ds is None or g >= ds):
                scores(ts, ss, c0, tri[:, c0 - g * bk:c0 - g * bk + gq] if g == ds else None)
            if tu is not None and (du is None or g >= du):
                update(tu, su, c0)

    m_ref[...] = jnp.full(m_ref.shape, -jnp.inf, F32)
    l_ref[...] = jnp.zeros(l_ref.shape, F32)
    acc_ref[...] = jnp.zeros(acc_ref.shape, F32)

    @pl.when(qi == 0)
    def _():
        sweep(ts=0, ss=0, ds=0)

    @pl.when(qi > 0)
    def _():
        sweep(ts=0, ss=0)

    def pair(j, carry):
        sweep(ts=2 * j + 1, ss=1, tu=2 * j, su=0)
        sweep(ts=2 * j + 2, ss=0, tu=2 * j + 1, su=1)
        return carry

    lax.fori_loop(0, n0 // 2 - 1, pair, 0)

    @pl.when(qi > 0)
    def _():
        sweep(ts=n0 - 1, ss=1, tu=n0 - 2, su=0)
        sweep(ts=n0, ss=0, ds=0, tu=n0 - 1, su=1)

    for d in range(1, nd):
        sweep(ts=n0 + d, ss=d % 2, ds=d, tu=n0 + d - 1, su=(d - 1) % 2, du=d - 1)
    sweep(tu=n0 + nd - 1, su=(nd - 1) % 2, du=nd - 1)
    o_ref[...] = (acc_ref[...] / l_ref[...]).T.astype(BF16)


def _mla_attn(q, k, vt):
    nh, s, _ = q.shape
    bk = vt.shape[-1]
    nd = 4 if s % (4 * bk) == 0 else 2
    bq = nd * bk
    assert s % bq == 0
    return pl.pallas_call(
        functools.partial(_mla_attn_kernel, bk=bk, nd=nd),
        out_shape=jax.ShapeDtypeStruct((s, nh * MLA_DV), BF16),
        grid=(nh, s // bq),
        in_specs=[pl.BlockSpec((1, bq, MLA_QK_PAD), lambda h, i: (h, i, 0)),
                  pl.BlockSpec((1, s, MLA_QK_PAD), lambda h, i: (h, 0, 0)),
                  pl.BlockSpec((1, s // bk, MLA_DV, bk), lambda h, i: (h, 0, 0, 0))],
        out_specs=pl.BlockSpec((bq, MLA_DV), lambda h, i: (i, h)),
        scratch_shapes=[pltpu.VMEM((bk, bq), F32), pltpu.VMEM((bk, bq), F32),
                        pltpu.VMEM((1, bq), F32), pltpu.VMEM((1, bq), F32),
                        pltpu.VMEM((1, bq), F32), pltpu.VMEM((1, bq), F32), pltpu.VMEM((MLA_DV, bq), F32)],
        compiler_params=_params(("parallel", "arbitrary")),
        name="mla_attn",
    )(q, k, vt)


def _swa_kernel(q_ref, k_ref, v_ref, kh_ref, vh_ref, sink_ref, o_ref, *, nblk):
    w = SWA_WINDOW
    g = SWA_GROUP
    i = pl.program_id(0)
    key = lax.broadcasted_iota(jnp.int32, (2 * w, g * w), 0)
    col = lax.broadcasted_iota(jnp.int32, (2 * w, g * w), 1)
    dist = col % w + w - key
    band = (dist >= 0) & (dist < w)
    first_band = band & ((key >= w) | (i > 0))
    dist_f = dist.astype(F32)
    biases = []
    for hk in range(SWA_KV_HEADS):
        slope = jnp.zeros((2 * w, g * w), F32)
        for gi, gq in enumerate(SWA_GORDER):
            slope = jnp.where(col // w == gi, 2.0 ** (-8.0 * (hk * g + gq + 1.0) / SWA_HEADS), slope)
        biases.append(-slope * dist_f)
    low = lax.broadcasted_iota(jnp.int32, (2 * w, 2 * SWA_DH), 1) < SWA_DH
    top = lax.broadcasted_iota(jnp.int32, (2 * SWA_DH, 1), 0) < SWA_DH
    for blk in range(nblk):
        rows = slice(blk * w, (blk + 1) * w)
        if blk == 0:
            k_prev, v_prev, valid = kh_ref[...], vh_ref[...], first_band
        else:
            k_prev, v_prev, valid = k_ref[(blk - 1) * w:blk * w, :], v_ref[(blk - 1) * w:blk * w, :], band
        kk = jnp.concatenate([k_prev, k_ref[rows, :]], axis=0).astype(F32)
        kk_sw = pltpu.roll(kk, SWA_DH, 1)
        vvt = jnp.concatenate([v_prev, v_ref[rows, :]], axis=0).astype(F32).T
        probs, inv_denoms = [], []
        for hk in range(SWA_KV_HEADS):
            c0 = hk * g * SWA_DH
            q2 = jnp.concatenate([q_ref[rows, c0:c0 + 2 * SWA_DH], q_ref[rows, c0 + 2 * SWA_DH:c0 + 4 * SWA_DH]],
                                 axis=0) * (SWA_DH ** -0.5)
            k_low = jnp.where(low, kk if hk == 0 else kk_sw, 0.0).astype(BF16)
            k_high = jnp.where(low, 0.0, kk_sw if hk == 0 else kk).astype(BF16)
            st = jnp.concatenate([_dot_nt(k_low, q2), _dot_nt(k_high, q2)], axis=1)
            st = jnp.where(valid, st + biases[hk], -jnp.inf)
            sink = sink_ref[hk]
            m = jnp.maximum(jnp.max(st, axis=0, keepdims=True), sink)
            pexp = jnp.exp(st - m)
            denom = jnp.sum(pexp, axis=0, keepdims=True) + jnp.exp(sink - m)
            probs.append(pexp.astype(BF16))
            inv_denoms.append(1.0 / denom)
        v0 = jnp.where(top, vvt, 0.0).astype(BF16)
        v1 = jnp.where(top, 0.0, vvt).astype(BF16)
        o_t = (_dot(v0, probs[0]) + _dot(v1, probs[1])) * jnp.where(top, inv_denoms[0], inv_denoms[1])
        for gi in range(g):
            o_ref[rows, gi * LANES:(gi + 1) * LANES] = o_t[:, gi * w:(gi + 1) * w].T.astype(BF16)


def _swa(p, sinks):
    s = p.shape[0]
    w = SWA_WINDOW
    rb = min(512, s)
    nblk = rb // w
    order = np.asarray([[hk * SWA_GROUP + gq for gq in SWA_GORDER] for hk in range(SWA_KV_HEADS)])
    sink_rows = jnp.repeat(sinks[order], w, axis=1).reshape(SWA_KV_HEADS, 1, SWA_GROUP * w)
    prev = lambda col: (lambda i: (jnp.maximum(i * nblk - 1, 0), col))
    return pl.pallas_call(
        functools.partial(_swa_kernel, nblk=nblk),
        out_shape=jax.ShapeDtypeStruct((s, SWA_OUT), BF16),
        grid=(s // rb,),
        in_specs=[pl.BlockSpec((rb, SWA_OUT), lambda i: (i, P_SWQ // SWA_OUT)),
                  pl.BlockSpec((rb, SWA_KV), lambda i: (i, P_SWK // SWA_KV)),
                  pl.BlockSpec((rb, SWA_KV), lambda i: (i, P_SWV // SWA_KV)),
                  pl.BlockSpec((w, SWA_KV), prev(P_SWK // SWA_KV)),
                  pl.BlockSpec((w, SWA_KV), prev(P_SWV // SWA_KV)),
                  pl.BlockSpec(sink_rows.shape, lambda i: (0, 0, 0))],
        out_specs=pl.BlockSpec((rb, SWA_OUT), lambda i: (i, 0)),
        compiler_params=_params(("parallel",)),
        name="swa",
    )(p, p, p, p, p, sink_rows)


def _outproj_kernel(oa_ref, ob_ref, oc_ref, w_ref, x_ref, pn_ref, g1_ref, fn_ref, sc_ref, sh_ref,
                    x1_ref, h2_ref, *, sub):
    a_w, b_w = oa_ref.shape[1], ob_ref.shape[1]
    for r0 in range(0, x_ref.shape[0], sub):
        rows = slice(r0, r0 + sub)
        mix = (_dot(oa_ref[rows, :], w_ref[0:a_w, :]) + _dot(ob_ref[rows, :], w_ref[a_w:a_w + b_w, :])
               + _dot(oc_ref[rows, :], w_ref[a_w + b_w:, :]))
        x1 = x_ref[rows, :] + g1_ref[...] * (_rms(mix) * pn_ref[...])
        x1_ref[rows, :] = x1
        h2_ref[rows, :] = _norm_mod(x1, fn_ref[...], sc_ref[...], sh_ref[...]).astype(BF16)


def _out_proj(o_a, o_b, o_c, w_out, layer, x, post_norm, gate1, ffn_norm, scale2, shift2):
    s, d = x.shape
    tm = min(512, s)
    row = lambda i: (i, 0)
    vec = pl.BlockSpec((1, d), lambda i: (0, 0))
    return pl.pallas_call(
        functools.partial(_outproj_kernel, sub=min(256, tm)),
        out_shape=(jax.ShapeDtypeStruct((s, d), F32), jax.ShapeDtypeStruct((s, d), BF16)),
        grid=(s // tm,),
        in_specs=[pl.BlockSpec((tm, o_a.shape[1]), row), pl.BlockSpec((tm, o_b.shape[1]), row),
                  pl.BlockSpec((tm, o_c.shape[1]), row),
                  pl.BlockSpec((None,) + w_out.shape[1:], lambda i: (layer, 0, 0), pipeline_mode=pl.Buffered(1)),
                  pl.BlockSpec((tm, d), row), vec, vec, vec, vec, vec],
        out_specs=(pl.BlockSpec((tm, d), row), pl.BlockSpec((tm, d), row)),
        compiler_params=_params(("parallel",)),
        name="out_proj",
    )(o_a, o_b, o_c, w_out, x, post_norm, gate1, ffn_norm, scale2, shift2)


def _gelu_tanh(x):
    return 0.5 * x * (1.0 + jnp.tanh(math.sqrt(2.0 / math.pi) * (x + 0.044715 * (x * x * x))))


def _ffn_up_kernel(h_ref, wg_ref, wu_ref, cg_ref, cu_ref, bg_ref, bu_ref, o_ref,
                   wg16_ref, wu16_ref, xg_ref, xu_ref, *, tm, th):
    @pl.when(pl.program_id(1) == 0)
    def _():
        wg16_ref[...] = wg_ref[...].astype(BF16)
        wu16_ref[...] = wu_ref[...].astype(BF16)
        xg_ref[0:8, :] = jnp.zeros((8, xg_ref.shape[1]), F32)
        xu_ref[0:8, :] = jnp.zeros((8, xu_ref.shape[1]), F32)

    def half_tile(r0):
        h = h_ref[r0:r0 + th, :]

        def conv(x_ref, w_ref, cw_ref, b_ref):
            x_ref[8:8 + th, :] = _dot(h, w_ref[...])
            y = b_ref[...] + cw_ref[2:3, :] * x_ref[8:8 + th, :]
            y = y + cw_ref[1:2, :] * x_ref[7:7 + th, :]
            y = y + cw_ref[0:1, :] * x_ref[6:6 + th, :]
            x_ref[0:8, :] = x_ref[th:th + 8, :]
            return y

        gate = conv(xg_ref, wg16_ref, cg_ref, bg_ref)
        up = conv(xu_ref, wu16_ref, cu_ref, bu_ref)
        o_ref[r0:r0 + th, :] = (_gelu_tanh(gate) * up).astype(BF16)

    for r0 in range(0, tm, th):
        half_tile(r0)


def _ffn_up(h2, w_up, layer, conv_w, conv_b):
    s, d = h2.shape
    d_ff = w_up.shape[2] // 2
    tm, tn = min(2048, s), 512
    th = min(1024, tm)
    nj = d_ff // tn
    lo = lambda j, i: (0, j)
    hi = lambda j, i: (0, j + nj)
    w_lo = pl.BlockSpec((None, d, tn), lambda j, i: (layer, 0, j))
    w_hi = pl.BlockSpec((None, d, tn), lambda j, i: (layer, 0, j + nj))
    return pl.pallas_call(
        functools.partial(_ffn_up_kernel, tm=tm, th=th),
        out_shape=jax.ShapeDtypeStruct((s, d_ff), BF16),
        grid=(nj, s // tm),
        in_specs=[pl.BlockSpec((tm, d), lambda j, i: (i, 0)),
                  w_lo, w_hi,
                  pl.BlockSpec((FFN_CONV, tn), lo), pl.BlockSpec((FFN_CONV, tn), hi),
                  pl.BlockSpec((1, tn), lo), pl.BlockSpec((1, tn), hi)],
        out_specs=pl.BlockSpec((tm, tn), lambda j, i: (i, j)),
        scratch_shapes=[pltpu.VMEM((d, tn), BF16), pltpu.VMEM((d, tn), BF16),
                        pltpu.VMEM((th + 8, tn), F32), pltpu.VMEM((th + 8, tn), F32)],
        compiler_params=_params(("parallel", "arbitrary")),
        name="ffn_up",
    )(h2, w_up, w_up, conv_w, conv_w, conv_b, conv_b)


def _ffn_down_kernel(g_ref, w_ref, x_ref, pn_ref, g2_ref, *rest):
    y = _dot(g_ref[...], w_ref[...])
    x2 = x_ref[...] + g2_ref[...] * (_rms(y) * pn_ref[...])
    if len(rest) == 1:
        rest[0][...] = x2
    else:
        nn_ref, sc_ref, sh_ref, x2_ref, hn_ref = rest
        x2_ref[...] = x2
        hn_ref[...] = _norm_mod(x2, nn_ref[...], sc_ref[...], sh_ref[...]).astype(BF16)


def _ffn_down(g, w_down, layer, x1, post_norm, gate2, next_pre_norm=None):
    s, d = x1.shape
    d_ff = g.shape[1]
    tm = min(256, s)
    row = lambda i: (i, 0)
    vec = pl.BlockSpec((1, d), lambda i: (0, 0))
    tile_f32 = jax.ShapeDtypeStruct((s, d), F32)
    tile = pl.BlockSpec((tm, d), row)
    last = next_pre_norm is None
    return pl.pallas_call(
        _ffn_down_kernel,
        out_shape=tile_f32 if last else (tile_f32, jax.ShapeDtypeStruct((s, d), BF16)),
        grid=(s // tm,),
        in_specs=[pl.BlockSpec((tm, d_ff), row),
                  pl.BlockSpec((None, d_ff, d), lambda i: (layer, 0, 0), pipeline_mode=pl.Buffered(1)),
                  tile, vec, vec] + ([] if last else [vec, vec, vec]),
        out_specs=tile if last else (tile, tile),
        compiler_params=_params(("parallel",)),
        name="ffn_down",
    )(g, w_down, x1, post_norm, gate2, *(() if last else next_pre_norm))


def _layout_w_in(w_in):
    depth, d, _ = w_in.shape
    sizes = (GDN_QK, GDN_QK, GDN_V, GDN_V, GDN_HEADS, GDN_HEADS, MLA_Q_RANK, MLA_KV_RANK, MLA_ROPE,
             SWA_OUT, SWA_KV, SWA_KV)
    offs = np.concatenate([[0], np.cumsum(sizes)])
    part = lambda n: w_in[:, :, offs[n]:offs[n + 1]].astype(BF16)
    zeros = lambda n: jnp.zeros((depth, d, n), BF16)
    w_p = jnp.concatenate([w_in[:, :, :offs[4]].astype(BF16),
                           part(6), zeros(MLA_Q_RANK_PAD - MLA_Q_RANK),
                           part(9), part(7), part(8), zeros(LANES - MLA_ROPE), part(10), part(11)], axis=2)
    w_ab = jnp.concatenate([part(4), part(5), zeros(LANES - 2 * GDN_HEADS)], axis=2)
    return w_p, w_ab


def _layout_w_out(w_out):
    depth, d_mix, d = w_out.shape
    a = d_mix - SWA_OUT
    wc = w_out[:, a:].reshape(depth, SWA_KV_HEADS, SWA_GROUP, SWA_DH, d)[:, :, np.asarray(SWA_GORDER)]
    wc = wc.transpose(0, 2, 1, 3, 4).reshape(depth, SWA_OUT, d)
    return jnp.concatenate([w_out[:, :a].astype(BF16), wc.astype(BF16)], axis=1)


def _layout_mla(q_norm, w_uq, kv_norm, w_ukv):
    dqk = MLA_NOPE + MLA_ROPE
    uq = w_uq.reshape(MLA_Q_RANK, MLA_HEADS, dqk)
    uq = jnp.concatenate([uq[:, :, :MLA_NOPE].reshape(MLA_Q_RANK, -1), uq[:, :, MLA_NOPE:].reshape(MLA_Q_RANK, -1)],
                         axis=1)
    uq = jnp.pad(uq, ((0, MLA_Q_RANK_PAD - MLA_Q_RANK), (0, 0))).astype(BF16)
    qn = jnp.pad(q_norm, (0, MLA_Q_RANK_PAD - MLA_Q_RANK)).reshape(1, MLA_Q_RANK_PAD)
    ukv = w_ukv.reshape(MLA_KV_RANK, MLA_HEADS, MLA_NOPE + MLA_DV)
    uk = ukv[:, :, :MLA_NOPE].reshape(MLA_KV_RANK, -1).astype(BF16)
    uvt = ukv[:, :, MLA_NOPE:].reshape(MLA_KV_RANK, -1).T.astype(BF16)
    return qn, uq, kv_norm.reshape(1, MLA_KV_RANK), uk, uvt


def kernel(x, c, positions, ada_w, ada_b, mix_pre_norm, mix_post_norm, w_in, w_out, gdn_conv, gdn_a_log, gdn_dt_bias, gdn_norm, mla_q_norm, mla_w_uq, mla_kv_norm, mla_w_ukv, swa_sinks, ffn_pre_norm, ffn_post_norm, ffn_w_up, ffn_conv, ffn_conv_b, ffn_w_down):
    batch, s, d = x.shape
    assert batch == 1, "kernels are written for a single sequence"
    depth = ada_w.shape[0]
    xs = x.reshape(s, d)
    mod = _adaln_mod(c, ada_w, ada_b).reshape(depth, N_MOD, 1, d)
    cos, sin = _rope_tables(positions)
    vec = lambda a: a.reshape(1, d)

    w_p, w_ab = _layout_w_in(w_in)
    w_o = _layout_w_out(w_out)
    w_down = ffn_w_down.astype(BF16)

    h = _pre_norm(xs, vec(mix_pre_norm[0]), mod[0, 1], mod[0, 0])
    for l in range(depth):
        shift1, scale1, gate1, shift2, scale2, gate2 = (mod[l, n] for n in range(N_MOD))
        p, ab = _in_proj(h, w_p, w_ab, l)

        q_a, k_a, v_a, gcb, gct = _gdn_prep(p, ab, gdn_conv[l], gdn_a_log[l], gdn_dt_bias[l])
        u, w, qd, kt, qk = _gdn_local(q_a, k_a, v_a, gcb, gct)
        o_a = _gdn_scan(u, w, qd, kt, qk, gcb, p, gdn_norm[l])

        q_b, k_b, v_b = _mla_proj(p, cos, sin, *_layout_mla(mla_q_norm[l], mla_w_uq[l], mla_kv_norm[l],
                                                            mla_w_ukv[l]))
        o_b = _mla_attn(q_b, k_b, v_b)

        o_c = _swa(p, swa_sinks[l])

        x1, h2 = _out_proj(o_a, o_b, o_c, w_o, l, xs, vec(mix_post_norm[l]), gate1,
                           vec(ffn_pre_norm[l]), scale2, shift2)
        g = _ffn_up(h2, ffn_w_up, l, ffn_conv[l], ffn_conv_b[l].reshape(1, -1))
        if l + 1 < depth:
            xs, h = _ffn_down(g, w_down, l, x1, vec(ffn_post_norm[l]), gate2,
                              (vec(mix_pre_norm[l + 1]), mod[l + 1, 1], mod[l + 1, 0]))
        else:
            xs = _ffn_down(g, w_down, l, x1, vec(ffn_post_norm[l]), gate2)
    return xs.reshape(batch, s, d)
```

```python
import functools
import math

import numpy as np
import jax
import jax.numpy as jnp
from jax import lax
from jax.experimental import pallas as pl
from jax.experimental.pallas import tpu as pltpu

F32 = jnp.float32
BF16 = jnp.bfloat16

EPS = 1e-6
N_MOD = 6
GDN_HEADS = 8
GDN_DK = 128
GDN_DV = 128
GDN_CONV = 4
GDN_CHUNK = 64
GDN_QK = GDN_HEADS * GDN_DK
GDN_V = GDN_HEADS * GDN_DV
MLA_HEADS = 4
MLA_Q_RANK = 448
MLA_Q_RANK_PAD = 512
MLA_KV_RANK = 128
MLA_NOPE = 128
MLA_ROPE = 64
MLA_DV = 128
MLA_QK_PAD = 256
MLA_BLOCK = 512
ROPE_THETA = 10000.0
SWA_HEADS = 8
SWA_KV_HEADS = 2
SWA_GROUP = SWA_HEADS // SWA_KV_HEADS
SWA_DH = 64
SWA_WINDOW = 128
SWA_OUT = SWA_HEADS * SWA_DH
SWA_KV = SWA_KV_HEADS * SWA_DH
SWA_GORDER = (0, 2, 1, 3)
FFN_CONV = 3

LANES = 128
GDN_GROUP = 256
VMEM_LIMIT_MB = 56
VMEM_LIMIT_RESIDENT_MB = 60

P_QKV = 0
P_Z = 3072
P_CQ = 4096
P_SWQ = 4608
P_CKV = 5120
P_KROPE = 5248
P_SWK = 5376
P_SWV = 5504
P_WIDTH = 5632


def _params(semantics, vmem_limit_mb=VMEM_LIMIT_MB):
    return pltpu.CompilerParams(dimension_semantics=semantics, vmem_limit_bytes=vmem_limit_mb << 20)


def _sigmoid(x):
    return 0.5 + 0.5 * jnp.tanh(0.5 * x)


def _silu(x):
    h = 0.5 * x
    return h + h * jnp.tanh(h)


def _softplus(x):
    return jnp.maximum(x, 0.0) + jnp.log(1.0 + jnp.exp(-jnp.abs(x)))


def _rms(x):
    return x * lax.rsqrt(jnp.mean(x * x, axis=-1, keepdims=True) + EPS)


def _dot(a, b):
    return jnp.dot(a, b, preferred_element_type=F32)


def _dot_nt(a, b):
    return lax.dot_general(a, b, (((1,), (1,)), ((), ())), preferred_element_type=F32)


def _dot_tn(a, b):
    return lax.dot_general(a, b, (((0,), (0,)), ((), ())), preferred_element_type=F32)


def _mod_kernel(c_ref, w_ref, b_ref, o_ref, *, d, kc):
    cact = _silu(c_ref[...])
    acc = b_ref[0]
    for k0 in range(0, d, kc):
        acc = acc + jnp.sum(w_ref[0, k0:k0 + kc, :] * cact[k0:k0 + kc], axis=0, keepdims=True)
    o_ref[0] = acc


def _adaln_mod(c, ada_w, ada_b):
    depth, d, n = ada_w.shape
    tn = 1024
    return pl.pallas_call(
        functools.partial(_mod_kernel, d=d, kc=256),
        out_shape=jax.ShapeDtypeStruct((depth, 1, n), F32),
        grid=(depth, n // tn),
        in_specs=[pl.BlockSpec((d, 1), lambda l, j: (0, 0)),
                  pl.BlockSpec((1, d, tn), lambda l, j: (l, 0, j)),
                  pl.BlockSpec((1, 1, tn), lambda l, j: (l, 0, j))],
        out_specs=pl.BlockSpec((1, 1, tn), lambda l, j: (l, 0, j)),
        compiler_params=_params(("parallel", "parallel")),
        name="adaln_mod",
    )(c.reshape(d, 1), ada_w, ada_b.reshape(depth, 1, n))


def _norm_mod(x, w, scale, shift):
    return _rms(x) * w * (1.0 + scale) + shift


def _norm_mod_kernel(x_ref, w_ref, sc_ref, sh_ref, h_ref):
    h_ref[...] = _norm_mod(x_ref[...], w_ref[...], sc_ref[...], sh_ref[...]).astype(BF16)


def _pre_norm(x, w, scale, shift):
    s, d = x.shape
    tm = min(1024, s)
    row = pl.BlockSpec((1, d), lambda i: (0, 0))
    return pl.pallas_call(
        _norm_mod_kernel,
        out_shape=jax.ShapeDtypeStruct((s, d), BF16),
        grid=(s // tm,),
        in_specs=[pl.BlockSpec((tm, d), lambda i: (i, 0)), row, row, row],
        out_specs=pl.BlockSpec((tm, d), lambda i: (i, 0)),
        compiler_params=_params(("parallel",)),
        name="pre_norm",
    )(x, w, scale, shift)


def _inproj_kernel(h_ref, w_ref, wab_ref, p_ref, ab_ref):
    h = h_ref[...]
    p_ref[...] = _dot(h, w_ref[...]).astype(BF16)

    @pl.when(pl.program_id(1) == 0)
    def _():
        ab_ref[...] = _dot(h, wab_ref[...])


def _in_proj(h, w_p, w_ab, layer):
    s, d = h.shape
    n = w_p.shape[2]
    tm, tn = min(2048, s), 512
    return pl.pallas_call(
        _inproj_kernel,
        out_shape=(jax.ShapeDtypeStruct((s, n), BF16), jax.ShapeDtypeStruct((s, LANES), F32)),
        grid=(s // tm, n // tn),
        in_specs=[pl.BlockSpec((tm, d), lambda i, j: (i, 0)),
                  pl.BlockSpec((None, d, tn), lambda i, j: (layer, 0, j)),
                  pl.BlockSpec((None, d, LANES), lambda i, j: (layer, 0, 0))],
        out_specs=(pl.BlockSpec((tm, tn), lambda i, j: (i, j)),
                   pl.BlockSpec((tm, LANES), lambda i, j: (i, 0))),
        compiler_params=_params(("parallel", "arbitrary")),
        name="in_proj",
    )(h, w_p, w_ab)


def _gdn_prep_kernel(x_ref, halo_ref, cw_ref, ab_ref, alog_ref, dtb_ref, tri_ref,
                     q_ref, k_ref, v_ref, gcb_ref, gct_ref, xp_ref, *, tm):
    i = pl.program_id(0)
    outs = (q_ref, k_ref, v_ref)
    for grp in range(3):
        c0 = grp * GDN_QK
        halo = halo_ref[:, c0:c0 + GDN_QK].astype(F32)
        xp_ref[0:8, :] = jnp.where(i > 0, halo, 0.0)
        xp_ref[8:8 + tm, :] = x_ref[:, c0:c0 + GDN_QK].astype(F32)
        y = cw_ref[3:4, c0:c0 + GDN_QK] * xp_ref[8:8 + tm, :]
        for j in range(GDN_CONV - 1):
            off = 8 - (GDN_CONV - 1) + j
            y = y + cw_ref[j:j + 1, c0:c0 + GDN_QK] * xp_ref[off:off + tm, :]
        y = _silu(y)
        for h in range(GDN_HEADS):
            yh = y[:, h * GDN_DK:(h + 1) * GDN_DK]
            if grp < 2:
                yh = yh * lax.rsqrt(jnp.sum(yh * yh, axis=-1, keepdims=True) + EPS)
            if grp == 0:
                yh = yh * (GDN_DK ** -0.5)
            outs[grp][:, h * GDN_DK:(h + 1) * GDN_DK] = yh.astype(BF16)

    ab = ab_ref[...]
    g = -jnp.exp(alog_ref[...]) * _softplus(ab + dtb_ref[...])
    gc = jnp.dot(tri_ref[...], g, preferred_element_type=F32, precision=lax.Precision.HIGHEST)
    lane = lax.broadcasted_iota(jnp.int32, (1, LANES), 1)
    gcb = jnp.where(lane < GDN_HEADS, gc, _sigmoid(ab))
    gcb_ref[...] = gcb
    gct_ref[...] = gcb.T[0:2 * GDN_HEADS, :]


def _gdn_prep(p, ab, conv_w, a_log, dt_bias):
    s = p.shape[0]
    tm = min(512, s)
    c3 = 3 * GDN_QK
    alog_row = jnp.zeros((1, LANES), F32).at[0, :GDN_HEADS].set(a_log)
    dtb_row = jnp.zeros((1, LANES), F32).at[0, :GDN_HEADS].set(dt_bias)
    r = np.arange(tm)
    tri = jnp.asarray(((r[:, None] >= r[None, :]) & (r[:, None] // GDN_CHUNK == r[None, :] // GDN_CHUNK))
                      .astype(np.float32))
    hb = tm // 8
    row = lambda i: (i, 0)
    return pl.pallas_call(
        functools.partial(_gdn_prep_kernel, tm=tm),
        out_shape=(jax.ShapeDtypeStruct((s, GDN_QK), BF16), jax.ShapeDtypeStruct((s, GDN_QK), BF16),
                   jax.ShapeDtypeStruct((s, GDN_V), BF16), jax.ShapeDtypeStruct((s, LANES), F32),
                   jax.ShapeDtypeStruct((2 * GDN_HEADS, s), F32)),
        grid=(s // tm,),
        in_specs=[pl.BlockSpec((tm, c3), row),
                  pl.BlockSpec((8, c3), lambda i: (jnp.maximum(i * hb - 1, 0), 0)),
                  pl.BlockSpec((GDN_CONV, c3), lambda i: (0, 0)),
                  pl.BlockSpec((tm, LANES), row),
                  pl.BlockSpec((1, LANES), lambda i: (0, 0)),
                  pl.BlockSpec((1, LANES), lambda i: (0, 0)),
                  pl.BlockSpec((tm, tm), lambda i: (0, 0))],
        out_specs=(pl.BlockSpec((tm, GDN_QK), row), pl.BlockSpec((tm, GDN_QK), row),
                   pl.BlockSpec((tm, GDN_V), row), pl.BlockSpec((tm, LANES), row),
                   pl.BlockSpec((2 * GDN_HEADS, tm), lambda i: (0, i))),
        scratch_shapes=[pltpu.VMEM((tm + 8, GDN_QK), F32)],
        compiler_params=_params(("parallel",)),
        name="gdn_prep",
    )(p, p, conv_w, ab, alog_row, dtb_row, tri)


def _gdn_local_kernel(q_ref, k_ref, v_ref, gcb_ref, gct_ref,
                      u_ref, w_ref, qd_ref, kt_ref, qk_ref, *, groups):
    n = GDN_GROUP
    c = GDN_CHUNK
    ri = lax.broadcasted_iota(jnp.int32, (n, n), 0)
    ci = lax.broadcasted_iota(jnp.int32, (n, n), 1)
    same_chunk = (ri // c) == (ci // c)
    incl = same_chunk & (ri >= ci)
    strict = same_chunk & (ri > ci)
    eye = (ri == ci).astype(F32)
    last_sel = (ci == (ri // c) * c + (c - 1)).astype(F32)
    heads = range(GDN_HEADS)
    sls = [slice(h * GDN_DK, (h + 1) * GDN_DK) for h in heads]
    lane_chunk = lax.broadcasted_iota(jnp.int32, (c, n), 1) // c
    units = [(g, h) for g in range(groups) for h in heads]

    def to_cat(m):
        out = m[0:c]
        for j in range(1, n // c):
            out = out + m[j * c:(j + 1) * c]
        return out

    def to_bd(m_cat):
        zero = jnp.zeros_like(m_cat)
        return jnp.concatenate([jnp.where(lane_chunk == j, m_cat, zero) for j in range(n // c)], axis=0)

    lmats, tcs, rhss = {}, {}, {}
    for g in range(groups):
        rows = slice(g * n, (g + 1) * n)
        gcb = gcb_ref[rows, :]
        gc_last_all = jnp.dot(last_sel, gcb, preferred_element_type=F32, precision=lax.Precision.HIGHEST)
        for h in heads:
            sl = sls[h]
            q = q_ref[rows, sl].astype(F32)
            k16 = k_ref[rows, sl]
            k = k16.astype(F32)
            v = v_ref[rows, sl].astype(F32)
            gc_col = gcb[:, h:h + 1]
            beta = gcb[:, GDN_HEADS + h:GDN_HEADS + h + 1]
            gc_row = gct_ref[h:h + 1, rows]
            gc_last = gc_last_all[:, h:h + 1]
            diff = gc_col - gc_row
            decay = jnp.where(incl, jnp.exp(jnp.where(incl, diff, 0.0)), 0.0)
            kb = k * beta
            kk = _dot_nt(kb.astype(BF16), k16)
            lmat = jnp.where(strict, kk * decay, 0.0)
            lmats[g, h] = lmat
            tcs[g, h] = to_cat(eye - jnp.where((ri // 2) == (ci // 2), lmat, 0.0))
            eg = jnp.exp(gc_col)
            rhss[g, h] = jnp.concatenate([v * beta, kb * eg], axis=-1).astype(BF16)
            qd_ref[rows, sl] = (q * eg).astype(BF16)
            kt_ref[rows, sl] = (k * jnp.exp(gc_last - gc_col)).astype(BF16)
            qk = _dot_nt(q_ref[rows, sl], k16) * decay
            qk_c = jnp.concatenate([qk[j * c:(j + 1) * c, j * c:(j + 1) * c] for j in range(n // c)], axis=0)
            qk_ref[rows, h * c:(h + 1) * c] = qk_c.astype(BF16)
    b = 2
    while b < c:
        off_diag = ((ri // (2 * b)) == (ci // (2 * b))) & ((ri // b) != (ci // b))
        t16s = {u: tcs[u].astype(BF16) for u in units}
        ys = {u: _dot(t16s[u], jnp.where(off_diag, lmats[u], 0.0).astype(BF16)).astype(BF16) for u in units}
        tcs = {u: tcs[u] - _dot(ys[u], to_bd(t16s[u])) for u in units}
        b *= 2
    for g, h in units:
        rows = slice(g * n, (g + 1) * n)
        uw = _dot(to_bd(tcs[g, h].astype(BF16)), rhss[g, h])
        u_ref[rows, sls[h]] = uw[:, :GDN_DV].astype(BF16)
        w_ref[rows, sls[h]] = uw[:, GDN_DV:].astype(BF16)


def _gdn_local(q, k, v, gcb, gct):
    s = q.shape[0]
    groups = 2 if s % (2 * GDN_GROUP) == 0 else 1
    n = groups * GDN_GROUP
    row = lambda i: (i, 0)
    wide = pl.BlockSpec((n, GDN_QK), row)
    return pl.pallas_call(
        functools.partial(_gdn_local_kernel, groups=groups),
        out_shape=(jax.ShapeDtypeStruct((s, GDN_V), BF16), jax.ShapeDtypeStruct((s, GDN_QK), BF16),
                   jax.ShapeDtypeStruct((s, GDN_QK), BF16), jax.ShapeDtypeStruct((s, GDN_QK), BF16),
                   jax.ShapeDtypeStruct((s, GDN_HEADS * GDN_CHUNK), BF16)),
        grid=(s // n,),
        in_specs=[wide, wide, wide, pl.BlockSpec((n, LANES), row),
                  pl.BlockSpec((2 * GDN_HEADS, n), lambda i: (0, i))],
        out_specs=(wide, wide, wide, wide, pl.BlockSpec((n, GDN_HEADS * GDN_CHUNK), row)),
        compiler_params=_params(("parallel",)),
        name="gdn_local",
    )(q, k, v, gcb, gct)


def _gdn_scan_kernel(u_ref, w_ref, qd_ref, kt_ref, qk_ref, gcb_ref, z_ref, nw_ref, o_ref, st_ref, *, rb):
    c = GDN_CHUNK

    @pl.when(pl.program_id(0) == 0)
    def _():
        st_ref[...] = jnp.zeros_like(st_ref)

    nw = nw_ref[...]

    def chunk(ci, carry):
        r0 = pl.multiple_of(ci * c, c)
        rows = pl.ds(r0, c)
        gt_row = jnp.exp(gcb_ref[pl.ds(r0 + c - 1, 1), :])
        heads = range(GDN_HEADS)
        sls = [slice(h * GDN_DK, (h + 1) * GDN_DK) for h in heads]
        sts = [st_ref[h] for h in heads]
        r1s = [_dot(jnp.concatenate([w_ref[rows, sls[h]], qd_ref[rows, sls[h]]], axis=0),
                    sts[h].astype(BF16)) for h in heads]
        vns = [(u_ref[rows, sls[h]].astype(F32) - r1s[h][0:c]).astype(BF16) for h in heads]
        for h in heads:
            st_ref[h] = sts[h] * gt_row[:, h:h + 1] + _dot_tn(kt_ref[rows, sls[h]], vns[h])
        os_ = [r1s[h][c:2 * c] + _dot(qk_ref[rows, h * c:(h + 1) * c], vns[h]) for h in heads]
        for h in heads:
            o = _rms(os_[h]) * nw * _silu(z_ref[rows, sls[h]].astype(F32))
            o_ref[rows, sls[h]] = o.astype(BF16)
        return carry

    unroll = min(8, rb // c)
    assert (rb // c) % unroll == 0

    def chunks(i, carry):
        for u in range(unroll):
            carry = chunk(unroll * i + u, carry)
        return carry

    lax.fori_loop(0, rb // (unroll * c), chunks, 0)


def _gdn_scan(u, w, qd, kt, qk, gcb, p, norm_w):
    s = u.shape[0]
    rb = min(512, s)
    row = lambda i: (i, 0)
    wide = pl.BlockSpec((rb, GDN_QK), row)
    return pl.pallas_call(
        functools.partial(_gdn_scan_kernel, rb=rb),
        out_shape=jax.ShapeDtypeStruct((s, GDN_V), BF16),
        grid=(s // rb,),
        in_specs=[wide, wide, wide, wide, pl.BlockSpec((rb, GDN_HEADS * GDN_CHUNK), row),
                  pl.BlockSpec((rb, LANES), row),
                  pl.BlockSpec((rb, GDN_V), lambda i: (i, P_Z // GDN_V)),
                  pl.BlockSpec((1, GDN_DV), lambda i: (0, 0))],
        out_specs=wide,
        scratch_shapes=[pltpu.VMEM((GDN_HEADS, GDN_DK, GDN_DV), F32)],
        compiler_params=_params(("arbitrary",)),
        name="gdn_scan",
    )(u, w, qd, kt, qk, gcb, p, norm_w.reshape(1, GDN_DV))


def _rope_table_kernel(pos_ref, inv_ref, cos_ref, sin_ref):
    ang = pos_ref[...].astype(F32) * inv_ref[...]
    lane = lax.broadcasted_iota(jnp.int32, ang.shape, 1)
    first_half = (lane % MLA_ROPE) < (MLA_ROPE // 2)
    cos_ref[...] = jnp.cos(ang)
    sin_ref[...] = jnp.where(first_half, -jnp.sin(ang), jnp.sin(ang))


def _rope_tables(positions):
    s = positions.shape[-1]
    tm = min(1024, s)
    half = MLA_ROPE // 2
    inv = (ROPE_THETA ** (-np.arange(half, dtype=np.float32) / half)).astype(np.float32)
    inv_row = jnp.asarray(np.tile(inv, LANES // half).reshape(1, LANES))
    return pl.pallas_call(
        _rope_table_kernel,
        out_shape=(jax.ShapeDtypeStruct((s, LANES), F32), jax.ShapeDtypeStruct((s, LANES), F32)),
        grid=(s // tm,),
        in_specs=[pl.BlockSpec((tm, 1), lambda i: (i, 0)), pl.BlockSpec((1, LANES), lambda i: (0, 0))],
        out_specs=(pl.BlockSpec((tm, LANES), lambda i: (i, 0)), pl.BlockSpec((tm, LANES), lambda i: (i, 0))),
        compiler_params=_params(("parallel",)),
        name="rope_tables",
    )(positions.reshape(s, 1), inv_row)


def _rope_apply(x, cos, sin_signed):
    width = x.shape[-1]
    half = MLA_ROPE // 2
    lane = lax.broadcasted_iota(jnp.int32, x.shape, 1)
    first_half = (lane % MLA_ROPE) < half
    swapped = jnp.where(first_half, pltpu.roll(x, width - half, 1), pltpu.roll(x, half, 1))
    return x * cos + swapped * sin_signed


def _mla_proj_kernel(cq_ref, ckv_ref, kr_ref, cos_ref, sin_ref, qn_ref, wuq_ref, kvn_ref, wuk_ref, wuvt_ref,
                     q_ref, k_ref, vt_ref):
    cos = cos_ref[...]
    sin = sin_ref[...]
    scale = (MLA_NOPE + MLA_ROPE) ** -0.5 * math.log2(math.e)
    cq = cq_ref[...].astype(F32)
    cqn = cq * lax.rsqrt(jnp.sum(cq * cq, axis=-1, keepdims=True) * (1.0 / MLA_Q_RANK) + EPS) * qn_ref[...]
    q = _dot(cqn.astype(BF16), wuq_ref[...])
    nope_w = MLA_HEADS * MLA_NOPE
    q_rope = _rope_apply(q[:, nope_w:], jnp.concatenate([cos, cos], -1), jnp.concatenate([sin, sin], -1))
    zeros = jnp.zeros((q.shape[0], MLA_QK_PAD - MLA_NOPE - MLA_ROPE), F32)
    ckvn = (_rms(ckv_ref[...].astype(F32)) * kvn_ref[...]).astype(BF16)
    k_nope = _dot(ckvn, wuk_ref[...])
    v_t = _dot_nt(wuvt_ref[...], ckvn)
    k_pe = _rope_apply(kr_ref[...].astype(F32), cos, sin)[:, :MLA_ROPE]
    for h in range(MLA_HEADS):
        qh = jnp.concatenate([q[:, h * MLA_NOPE:(h + 1) * MLA_NOPE],
                              q_rope[:, h * MLA_ROPE:(h + 1) * MLA_ROPE], zeros], axis=-1) * scale
        q_ref[h] = qh.astype(BF16)
        kh = jnp.concatenate([k_nope[:, h * MLA_NOPE:(h + 1) * MLA_NOPE], k_pe, zeros], axis=-1)
        k_ref[h] = kh.astype(BF16)
        vt_ref[h, 0] = v_t[h * MLA_DV:(h + 1) * MLA_DV, :].astype(BF16)


def _mla_proj(p, cos, sin, q_norm, w_uq, kv_norm, w_uk, w_uvt):
    s = p.shape[0]
    tm = min(MLA_BLOCK, s)
    const = lambda i: (0, 0)
    return pl.pallas_call(
        _mla_proj_kernel,
        out_shape=(jax.ShapeDtypeStruct((MLA_HEADS, s, MLA_QK_PAD), BF16),
                   jax.ShapeDtypeStruct((MLA_HEADS, s, MLA_QK_PAD), BF16),
                   jax.ShapeDtypeStruct((MLA_HEADS, s // tm, MLA_DV, tm), BF16)),
        grid=(s // tm,),
        in_specs=[pl.BlockSpec((tm, MLA_Q_RANK_PAD), lambda i: (i, P_CQ // MLA_Q_RANK_PAD)),
                  pl.BlockSpec((tm, LANES), lambda i: (i, P_CKV // LANES)),
                  pl.BlockSpec((tm, LANES), lambda i: (i, P_KROPE // LANES)),
                  pl.BlockSpec((tm, LANES), lambda i: (i, 0)),
                  pl.BlockSpec((tm, LANES), lambda i: (i, 0)),
                  pl.BlockSpec(q_norm.shape, const), pl.BlockSpec(w_uq.shape, const),
                  pl.BlockSpec(kv_norm.shape, const), pl.BlockSpec(w_uk.shape, const),
                  pl.BlockSpec(w_uvt.shape, const)],
        out_specs=(pl.BlockSpec((MLA_HEADS, tm, MLA_QK_PAD), lambda i: (0, i, 0)),
                   pl.BlockSpec((MLA_HEADS, tm, MLA_QK_PAD), lambda i: (0, i, 0)),
                   pl.BlockSpec((MLA_HEADS, 1, MLA_DV, tm), lambda i: (0, i, 0, 0))),
        compiler_params=_params(("parallel",)),
        name="mla_proj",
    )(p, p, p, cos, sin, q_norm, w_uq, kv_norm, w_uk, w_uvt)


def _mla_attn_kernel(q_ref, k_ref, vt_ref, o_ref, st0, st1, mb0, mb1, m_ref, l_ref, acc_ref, *, bk, nd):
    qi = pl.program_id(1)
    slots = ((st0, mb0), (st1, mb1))
    bq = nd * bk
    n0 = nd * qi
    tri = lax.broadcasted_iota(jnp.int32, (bk, bk), 0) <= lax.broadcasted_iota(jnp.int32, (bk, bk), 1)

    gq = bk // 2

    def scores(t, slot, c0, tri_mask):
        st_ref, mb_ref = slots[slot]
        r0 = pl.multiple_of(t * bk, bk)
        st = _dot_nt(k_ref[0, pl.ds(r0, bk), :], q_ref[0, c0:c0 + gq, :])
        if tri_mask is not None:
            st = jnp.where(tri_mask, st, -jnp.inf)
        st_ref[:, c0:c0 + gq] = st
        mb_ref[:, c0:c0 + gq] = jnp.max(st, axis=0, keepdims=True)

    def update(t, slot, c0):
        st_ref, mb_ref = slots[slot]
        cols = slice(c0, c0 + gq)
        m = m_ref[:, cols]
        m_new = jnp.maximum(m, mb_ref[:, cols])
        alpha = jnp.exp2(m - m_new)
        pexp = jnp.exp2(st_ref[:, cols] - m_new)
        l_ref[:, cols] = alpha * l_ref[:, cols] + jnp.sum(pexp, axis=0, keepdims=True)
        acc_ref[:, cols] = alpha * acc_ref[:, cols] + _dot(vt_ref[0, t], pexp.astype(BF16))
        m_ref[:, cols] = m_new

    def sweep(ts=None, ss=None, ds=None, tu=None, su=None, du=None):
        for c0 in range(0, bq, gq):
            g = c0 // bk
            if ts is not None and (ds is None or g >= ds):
                scores(ts, ss, c0, tri[:, c0 - g * bk:c0 - g * bk + gq] if g == ds else None)
            if tu is not None and (du is None or g >= du):
                update(tu, su, c0)

    m_ref[...] = jnp.full(m_ref.shape, -jnp.inf, F32)
    l_ref[...] = jnp.zeros(l_ref.shape, F32)
    acc_ref[...] = jnp.zeros(acc_ref.shape, F32)

    @pl.when(qi == 0)
    def _():
        sweep(ts=0, ss=0, ds=0)

    @pl.when(qi > 0)
    def _():
        sweep(ts=0, ss=0)

    def pair(j, carry):
        sweep(ts=2 * j + 1, ss=1, tu=2 * j, su=0)
        sweep(ts=2 * j + 2, ss=0, tu=2 * j + 1, su=1)
        return carry

    lax.fori_loop(0, n0 // 2 - 1, pair, 0)

    @pl.when(qi > 0)
    def _():
        sweep(ts=n0 - 1, ss=1, tu=n0 - 2, su=0)
        sweep(ts=n0, ss=0, ds=0, tu=n0 - 1, su=1)

    for d in range(1, nd):
        sweep(ts=n0 + d, ss=d % 2, ds=d, tu=n0 + d - 1, su=(d - 1) % 2, du=d - 1)
    sweep(tu=n0 + nd - 1, su=(nd - 1) % 2, du=nd - 1)
    o_ref[...] = (acc_ref[...] / l_ref[...]).T.astype(BF16)


def _mla_attn(q, k, vt):
    nh, s, _ = q.shape
    bk = vt.shape[-1]
    nd = 4 if s % (4 * bk) == 0 else 2
    bq = nd * bk
    assert s % bq == 0
    return pl.pallas_call(
        functools.partial(_mla_attn_kernel, bk=bk, nd=nd),
        out_shape=jax.ShapeDtypeStruct((s, nh * MLA_DV), BF16),
        grid=(nh, s // bq),
        in_specs=[pl.BlockSpec((1, bq, MLA_QK_PAD), lambda h, i: (h, i, 0)),
                  pl.BlockSpec((1, s, MLA_QK_PAD), lambda h, i: (h, 0, 0)),
                  pl.BlockSpec((1, s // bk, MLA_DV, bk), lambda h, i: (h, 0, 0, 0))],
        out_specs=pl.BlockSpec((bq, MLA_DV), lambda h, i: (i, h)),
        scratch_shapes=[pltpu.VMEM((bk, bq), F32), pltpu.VMEM((bk, bq), F32),
                        pltpu.VMEM((1, bq), F32), pltpu.VMEM((1, bq), F32),
                        pltpu.VMEM((1, bq), F32), pltpu.VMEM((1, bq), F32), pltpu.VMEM((MLA_DV, bq), F32)],
        compiler_params=_params(("parallel", "arbitrary")),
        name="mla_attn",
    )(q, k, vt)


def _swa_kernel(q_ref, k_ref, v_ref, kh_ref, vh_ref, sink_ref, o_ref, *, nblk):
    w = SWA_WINDOW
    g = SWA_GROUP
    i = pl.program_id(0)
    key = lax.broadcasted_iota(jnp.int32, (2 * w, g * w), 0)
    col = lax.broadcasted_iota(jnp.int32, (2 * w, g * w), 1)
    dist = col % w + w - key
    band = (dist >= 0) & (dist < w)
    first_band = band & ((key >= w) | (i > 0))
    dist_f = dist.astype(F32)
    biases = []
    for hk in range(SWA_KV_HEADS):
        slope = jnp.zeros((2 * w, g * w), F32)
        for gi, gq in enumerate(SWA_GORDER):
            slope = jnp.where(col // w == gi, 2.0 ** (-8.0 * (hk * g + gq + 1.0) / SWA_HEADS), slope)
        biases.append(-slope * dist_f)
    low = lax.broadcasted_iota(jnp.int32, (2 * w, 2 * SWA_DH), 1) < SWA_DH
    top = lax.broadcasted_iota(jnp.int32, (2 * SWA_DH, 1), 0) < SWA_DH
    for blk in range(nblk):
        rows = slice(blk * w, (blk + 1) * w)
        if blk == 0:
            k_prev, v_prev, valid = kh_ref[...], vh_ref[...], first_band
        else:
            k_prev, v_prev, valid = k_ref[(blk - 1) * w:blk * w, :], v_ref[(blk - 1) * w:blk * w, :], band
        kk = jnp.concatenate([k_prev, k_ref[rows, :]], axis=0).astype(F32)
        kk_sw = pltpu.roll(kk, SWA_DH, 1)
        vvt = jnp.concatenate([v_prev, v_ref[rows, :]], axis=0).astype(F32).T
        probs, inv_denoms = [], []
        for hk in range(SWA_KV_HEADS):
            c0 = hk * g * SWA_DH
            q2 = jnp.concatenate([q_ref[rows, c0:c0 + 2 * SWA_DH], q_ref[rows, c0 + 2 * SWA_DH:c0 + 4 * SWA_DH]],
                                 axis=0) * (SWA_DH ** -0.5)
            k_low = jnp.where(low, kk if hk == 0 else kk_sw, 0.0).astype(BF16)
            k_high = jnp.where(low, 0.0, kk_sw if hk == 0 else kk).astype(BF16)
            st = jnp.concatenate([_dot_nt(k_low, q2), _dot_nt(k_high, q2)], axis=1)
            st = jnp.where(valid, st + biases[hk], -jnp.inf)
            sink = sink_ref[hk]
            m = jnp.maximum(jnp.max(st, axis=0, keepdims=True), sink)
            pexp = jnp.exp(st - m)
            denom = jnp.sum(pexp, axis=0, keepdims=True) + jnp.exp(sink - m)
            probs.append(pexp.astype(BF16))
            inv_denoms.append(1.0 / denom)
        v0 = jnp.where(top, vvt, 0.0).astype(BF16)
        v1 = jnp.where(top, 0.0, vvt).astype(BF16)
        o_t = (_dot(v0, probs[0]) + _dot(v1, probs[1])) * jnp.where(top, inv_denoms[0], inv_denoms[1])
        for gi in range(g):
            o_ref[rows, gi * LANES:(gi + 1) * LANES] = o_t[:, gi * w:(gi + 1) * w].T.astype(BF16)


def _swa(p, sinks):
    s = p.shape[0]
    w = SWA_WINDOW
    rb = min(512, s)
    nblk = rb // w
    order = np.asarray([[hk * SWA_GROUP + gq for gq in SWA_GORDER] for hk in range(SWA_KV_HEADS)])
    sink_rows = jnp.repeat(sinks[order], w, axis=1).reshape(SWA_KV_HEADS, 1, SWA_GROUP * w)
    prev = lambda col: (lambda i: (jnp.maximum(i * nblk - 1, 0), col))
    return pl.pallas_call(
        functools.partial(_swa_kernel, nblk=nblk),
        out_shape=jax.ShapeDtypeStruct((s, SWA_OUT), BF16),
        grid=(s // rb,),
        in_specs=[pl.BlockSpec((rb, SWA_OUT), lambda i: (i, P_SWQ // SWA_OUT)),
                  pl.BlockSpec((rb, SWA_KV), lambda i: (i, P_SWK // SWA_KV)),
                  pl.BlockSpec((rb, SWA_KV), lambda i: (i, P_SWV // SWA_KV)),
                  pl.BlockSpec((w, SWA_KV), prev(P_SWK // SWA_KV)),
                  pl.BlockSpec((w, SWA_KV), prev(P_SWV // SWA_KV)),
                  pl.BlockSpec(sink_rows.shape, lambda i: (0, 0, 0))],
        out_specs=pl.BlockSpec((rb, SWA_OUT), lambda i: (i, 0)),
        compiler_params=_params(("parallel",)),
        name="swa",
    )(p, p, p, p, p, sink_rows)


def _outproj_kernel(oa_ref, ob_ref, oc_ref, w_ref, x_ref, pn_ref, g1_ref, fn_ref, sc_ref, sh_ref,
                    x1_ref, h2_ref, *, sub):
    a_w, b_w = oa_ref.shape[1], ob_ref.shape[1]
    for r0 in range(0, x_ref.shape[0], sub):
        rows = slice(r0, r0 + sub)
        mix = (_dot(oa_ref[rows, :], w_ref[0:a_w, :]) + _dot(ob_ref[rows, :], w_ref[a_w:a_w + b_w, :])
               + _dot(oc_ref[rows, :], w_ref[a_w + b_w:, :]))
        x1 = x_ref[rows, :] + g1_ref[...] * (_rms(mix) * pn_ref[...])
        x1_ref[rows, :] = x1
        h2_ref[rows, :] = _norm_mod(x1, fn_ref[...], sc_ref[...], sh_ref[...]).astype(BF16)


def _out_proj(o_a, o_b, o_c, w_out, layer, x, post_norm, gate1, ffn_norm, scale2, shift2):
    s, d = x.shape
    tm = min(512, s)
    row = lambda i: (i, 0)
    vec = pl.BlockSpec((1, d), lambda i: (0, 0))
    return pl.pallas_call(
        functools.partial(_outproj_kernel, sub=min(256, tm)),
        out_shape=(jax.ShapeDtypeStruct((s, d), F32), jax.ShapeDtypeStruct((s, d), BF16)),
        grid=(s // tm,),
        in_specs=[pl.BlockSpec((tm, o_a.shape[1]), row), pl.BlockSpec((tm, o_b.shape[1]), row),
                  pl.BlockSpec((tm, o_c.shape[1]), row),
                  pl.BlockSpec((None,) + w_out.shape[1:], lambda i: (layer, 0, 0), pipeline_mode=pl.Buffered(1)),
                  pl.BlockSpec((tm, d), row), vec, vec, vec, vec, vec],
        out_specs=(pl.BlockSpec((tm, d), row), pl.BlockSpec((tm, d), row)),
        compiler_params=_params(("parallel",)),
        name="out_proj",
    )(o_a, o_b, o_c, w_out, x, post_norm, gate1, ffn_norm, scale2, shift2)


def _gelu_tanh(x):
    return 0.5 * x * (1.0 + jnp.tanh(math.sqrt(2.0 / math.pi) * (x + 0.044715 * (x * x * x))))


def _ffn_up_kernel(h_ref, wg_ref, wu_ref, cg_ref, cu_ref, bg_ref, bu_ref, o_ref,
                   wg16_ref, wu16_ref, xg_ref, xu_ref, *, tm, th):
    @pl.when(pl.program_id(1) == 0)
    def _():
        wg16_ref[...] = wg_ref[...].astype(BF16)
        wu16_ref[...] = wu_ref[...].astype(BF16)
        xg_ref[0:8, :] = jnp.zeros((8, xg_ref.shape[1]), F32)
        xu_ref[0:8, :] = jnp.zeros((8, xu_ref.shape[1]), F32)

    def half_tile(r0):
        h = h_ref[r0:r0 + th, :]

        def conv(x_ref, w_ref, cw_ref, b_ref):
            x_ref[8:8 + th, :] = _dot(h, w_ref[...])
            y = b_ref[...] + cw_ref[2:3, :] * x_ref[8:8 + th, :]
            y = y + cw_ref[1:2, :] * x_ref[7:7 + th, :]
            y = y + cw_ref[0:1, :] * x_ref[6:6 + th, :]
            x_ref[0:8, :] = x_ref[th:th + 8, :]
            return y

        gate = conv(xg_ref, wg16_ref, cg_ref, bg_ref)
        up = conv(xu_ref, wu16_ref, cu_ref, bu_ref)
        o_ref[r0:r0 + th, :] = (_gelu_tanh(gate) * up).astype(BF16)

    for r0 in range(0, tm, th):
        half_tile(r0)


def _ffn_up(h2, w_up, layer, conv_w, conv_b):
    s, d = h2.shape
    d_ff = w_up.shape[2] // 2
    tm, tn = min(2048, s), 512
    th = min(1024, tm)
    nj = d_ff // tn
    lo = lambda j, i: (0, j)
    hi = lambda j, i: (0, j + nj)
    w_lo = pl.BlockSpec((None, d, tn), lambda j, i: (layer, 0, j))
    w_hi = pl.BlockSpec((None, d, tn), lambda j, i: (layer, 0, j + nj))
    return pl.pallas_call(
        functools.partial(_ffn_up_kernel, tm=tm, th=th),
        out_shape=jax.ShapeDtypeStruct((s, d_ff), BF16),
        grid=(nj, s // tm),
        in_specs=[pl.BlockSpec((tm, d), lambda j, i: (i, 0)),
                  w_lo, w_hi,
                  pl.BlockSpec((FFN_CONV, tn), lo), pl.BlockSpec((FFN_CONV, tn), hi),
                  pl.BlockSpec((1, tn), lo), pl.BlockSpec((1, tn), hi)],
        out_specs=pl.BlockSpec((tm, tn), lambda j, i: (i, j)),
        scratch_shapes=[pltpu.VMEM((d, tn), BF16), pltpu.VMEM((d, tn), BF16),
                        pltpu.VMEM((th + 8, tn), F32), pltpu.VMEM((th + 8, tn), F32)],
        compiler_params=_params(("parallel", "arbitrary")),
        name="ffn_up",
    )(h2, w_up, w_up, conv_w, conv_w, conv_b, conv_b)


def _ffn_down_kernel(g_ref, w_ref, x_ref, pn_ref, g2_ref, *rest, sub):
    for r0 in range(0, x_ref.shape[0], sub):
        rows = slice(r0, r0 + sub)
        y = _dot(g_ref[rows, :], w_ref[...])
        x2 = x_ref[rows, :] + g2_ref[...] * (_rms(y) * pn_ref[...])
        if len(rest) == 1:
            rest[0][rows, :] = x2
        else:
            nn_ref, sc_ref, sh_ref, x2_ref, hn_ref = rest
            x2_ref[rows, :] = x2
            hn_ref[rows, :] = _norm_mod(x2, nn_ref[...], sc_ref[...], sh_ref[...]).astype(BF16)


def _ffn_down(g, w_down, layer, x1, post_norm, gate2, next_pre_norm=None):
    s, d = x1.shape
    d_ff = g.shape[1]
    tm = min(512, s)
    row = lambda i: (i, 0)
    vec = pl.BlockSpec((1, d), lambda i: (0, 0))
    tile_f32 = jax.ShapeDtypeStruct((s, d), F32)
    tile = pl.BlockSpec((tm, d), row)
    last = next_pre_norm is None
    return pl.pallas_call(
        functools.partial(_ffn_down_kernel, sub=min(256, tm)),
        out_shape=tile_f32 if last else (tile_f32, jax.ShapeDtypeStruct((s, d), BF16)),
        grid=(s // tm,),
        in_specs=[pl.BlockSpec((tm, d_ff), row),
                  pl.BlockSpec((None, d_ff, d), lambda i: (layer, 0, 0), pipeline_mode=pl.Buffered(1)),
                  tile, vec, vec] + ([] if last else [vec, vec, vec]),
        out_specs=tile if last else (tile, tile),
        compiler_params=_params(("parallel",), VMEM_LIMIT_RESIDENT_MB),
        name="ffn_down",
    )(g, w_down, x1, post_norm, gate2, *(() if last else next_pre_norm))


def _layout_w_in(w_in):
    depth, d, _ = w_in.shape
    sizes = (GDN_QK, GDN_QK, GDN_V, GDN_V, GDN_HEADS, GDN_HEADS, MLA_Q_RANK, MLA_KV_RANK, MLA_ROPE,
             SWA_OUT, SWA_KV, SWA_KV)
    offs = np.concatenate([[0], np.cumsum(sizes)])
    part = lambda n: w_in[:, :, offs[n]:offs[n + 1]].astype(BF16)
    zeros = lambda n: jnp.zeros((depth, d, n), BF16)
    w_p = jnp.concatenate([w_in[:, :, :offs[4]].astype(BF16),
                           part(6), zeros(MLA_Q_RANK_PAD - MLA_Q_RANK),
                           part(9), part(7), part(8), zeros(LANES - MLA_ROPE), part(10), part(11)], axis=2)
    w_ab = jnp.concatenate([part(4), part(5), zeros(LANES - 2 * GDN_HEADS)], axis=2)
    return w_p, w_ab


def _layout_w_out(w_out):
    depth, d_mix, d = w_out.shape
    a = d_mix - SWA_OUT
    wc = w_out[:, a:].reshape(depth, SWA_KV_HEADS, SWA_GROUP, SWA_DH, d)[:, :, np.asarray(SWA_GORDER)]
    wc = wc.transpose(0, 2, 1, 3, 4).reshape(depth, SWA_OUT, d)
    return jnp.concatenate([w_out[:, :a].astype(BF16), wc.astype(BF16)], axis=1)


def _layout_mla(q_norm, w_uq, kv_norm, w_ukv):
    dqk = MLA_NOPE + MLA_ROPE
    uq = w_uq.reshape(MLA_Q_RANK, MLA_HEADS, dqk)
    uq = jnp.concatenate([uq[:, :, :MLA_NOPE].reshape(MLA_Q_RANK, -1), uq[:, :, MLA_NOPE:].reshape(MLA_Q_RANK, -1)],
                         axis=1)
    uq = jnp.pad(uq, ((0, MLA_Q_RANK_PAD - MLA_Q_RANK), (0, 0))).astype(BF16)
    qn = jnp.pad(q_norm, (0, MLA_Q_RANK_PAD - MLA_Q_RANK)).reshape(1, MLA_Q_RANK_PAD)
    ukv = w_ukv.reshape(MLA_KV_RANK, MLA_HEADS, MLA_NOPE + MLA_DV)
    uk = ukv[:, :, :MLA_NOPE].reshape(MLA_KV_RANK, -1).astype(BF16)
    uvt = ukv[:, :, MLA_NOPE:].reshape(MLA_KV_RANK, -1).T.astype(BF16)
    return qn, uq, kv_norm.reshape(1, MLA_KV_RANK), uk, uvt


def kernel(x, c, positions, ada_w, ada_b, mix_pre_norm, mix_post_norm, w_in, w_out, gdn_conv, gdn_a_log, gdn_dt_bias, gdn_norm, mla_q_norm, mla_w_uq, mla_kv_norm, mla_w_ukv, swa_sinks, ffn_pre_norm, ffn_post_norm, ffn_w_up, ffn_conv, ffn_conv_b, ffn_w_down):
    batch, s, d = x.shape
    assert batch == 1, "kernels are written for a single sequence"
    depth = ada_w.shape[0]
    xs = x.reshape(s, d)
    mod = _adaln_mod(c, ada_w, ada_b).reshape(depth, N_MOD, 1, d)
    cos, sin = _rope_tables(positions)
    vec = lambda a: a.reshape(1, d)

    w_p, w_ab = _layout_w_in(w_in)
    w_o = _layout_w_out(w_out)
    w_down = ffn_w_down.astype(BF16)

    h = _pre_norm(xs, vec(mix_pre_norm[0]), mod[0, 1], mod[0, 0])
    for l in range(depth):
        shift1, scale1, gate1, shift2, scale2, gate2 = (mod[l, n] for n in range(N_MOD))
        p, ab = _in_proj(h, w_p, w_ab, l)

        q_a, k_a, v_a, gcb, gct = _gdn_prep(p, ab, gdn_conv[l], gdn_a_log[l], gdn_dt_bias[l])
        u, w, qd, kt, qk = _gdn_local(q_a, k_a, v_a, gcb, gct)
        o_a = _gdn_scan(u, w, qd, kt, qk, gcb, p, gdn_norm[l])

        q_b, k_b, v_b = _mla_proj(p, cos, sin, *_layout_mla(mla_q_norm[l], mla_w_uq[l], mla_kv_norm[l],
                                                            mla_w_ukv[l]))
        o_b = _mla_attn(q_b, k_b, v_b)

        o_c = _swa(p, swa_sinks[l])

        x1, h2 = _out_proj(o_a, o_b, o_c, w_o, l, xs, vec(mix_post_norm[l]), gate1,
                           vec(ffn_pre_norm[l]), scale2, shift2)
        g = _ffn_up(h2, ffn_w_up, l, ffn_conv[l], ffn_conv_b[l].reshape(1, -1))
        if l + 1 < depth:
            xs, h = _ffn_down(g, w_down, l, x1, vec(ffn_post_norm[l]), gate2,
                              (vec(mix_pre_norm[l + 1]), mod[l + 1, 1], mod[l + 1, 0]))
        else:
            xs = _ffn_down(g, w_down, l, x1, vec(ffn_post_norm[l]), gate2)
    return xs.reshape(batch, s, d)
```

```python
import functools
import math

import numpy as np
import jax
import jax.numpy as jnp
from jax import lax
from jax.experimental import pallas as pl
from jax.experimental.pallas import tpu as pltpu

F32 = jnp.float32
BF16 = jnp.bfloat16

EPS = 1e-6
N_MOD = 6
GDN_HEADS = 8
GDN_DK = 128
GDN_DV = 128
GDN_CONV = 4
GDN_CHUNK = 64
GDN_QK = GDN_HEADS * GDN_DK
GDN_V = GDN_HEADS * GDN_DV
MLA_HEADS = 4
MLA_Q_RANK = 448
MLA_Q_RANK_PAD = 512
MLA_KV_RANK = 128
MLA_NOPE = 128
MLA_ROPE = 64
MLA_DV = 128
MLA_QK_PAD = 256
MLA_BLOCK = 512
ROPE_THETA = 10000.0
SWA_HEADS = 8
SWA_KV_HEADS = 2
SWA_GROUP = SWA_HEADS // SWA_KV_HEADS
SWA_DH = 64
SWA_WINDOW = 128
SWA_OUT = SWA_HEADS * SWA_DH
SWA_KV = SWA_KV_HEADS * SWA_DH
SWA_GORDER = (0, 2, 1, 3)
FFN_CONV = 3

LANES = 128
GDN_GROUP = 256
VMEM_LIMIT_MB = 56
VMEM_LIMIT_RESIDENT_MB = 60

P_QKV = 0
P_Z = 3072
P_CQ = 4096
P_SWQ = 4608
P_CKV = 5120
P_KROPE = 5248
P_SWK = 5376
P_SWV = 5504
P_WIDTH = 5632


def _params(semantics, vmem_limit_mb=VMEM_LIMIT_MB):
    return pltpu.CompilerParams(dimension_semantics=semantics, vmem_limit_bytes=vmem_limit_mb << 20)


def _sigmoid(x):
    return 0.5 + 0.5 * jnp.tanh(0.5 * x)


def _silu(x):
    h = 0.5 * x
    return h + h * jnp.tanh(h)


def _softplus(x):
    return jnp.maximum(x, 0.0) + jnp.log(1.0 + jnp.exp(-jnp.abs(x)))


def _rms(x):
    return x * lax.rsqrt(jnp.mean(x * x, axis=-1, keepdims=True) + EPS)


def _dot(a, b):
    return jnp.dot(a, b, preferred_element_type=F32)


def _dot_nt(a, b):
    return lax.dot_general(a, b, (((1,), (1,)), ((), ())), preferred_element_type=F32)


def _dot_tn(a, b):
    return lax.dot_general(a, b, (((0,), (0,)), ((), ())), preferred_element_type=F32)


def _mod_kernel(c_ref, w_ref, b_ref, o_ref, *, d, kc):
    cact = _silu(c_ref[...])
    acc = b_ref[0]
    for k0 in range(0, d, kc):
        acc = acc + jnp.sum(w_ref[0, k0:k0 + kc, :] * cact[k0:k0 + kc], axis=0, keepdims=True)
    o_ref[0] = acc


def _adaln_mod(c, ada_w, ada_b):
    depth, d, n = ada_w.shape
    tn = 1024
    return pl.pallas_call(
        functools.partial(_mod_kernel, d=d, kc=256),
        out_shape=jax.ShapeDtypeStruct((depth, 1, n), F32),
        grid=(depth, n // tn),
        in_specs=[pl.BlockSpec((d, 1), lambda l, j: (0, 0)),
                  pl.BlockSpec((1, d, tn), lambda l, j: (l, 0, j)),
                  pl.BlockSpec((1, 1, tn), lambda l, j: (l, 0, j))],
        out_specs=pl.BlockSpec((1, 1, tn), lambda l, j: (l, 0, j)),
        compiler_params=_params(("parallel", "parallel")),
        name="adaln_mod",
    )(c.reshape(d, 1), ada_w, ada_b.reshape(depth, 1, n))


def _norm_mod(x, w, scale, shift):
    return _rms(x) * w * (1.0 + scale) + shift


def _norm_mod_kernel(x_ref, w_ref, sc_ref, sh_ref, h_ref):
    h_ref[...] = _norm_mod(x_ref[...], w_ref[...], sc_ref[...], sh_ref[...]).astype(BF16)


def _pre_norm(x, w, scale, shift):
    s, d = x.shape
    tm = min(1024, s)
    row = pl.BlockSpec((1, d), lambda i: (0, 0))
    return pl.pallas_call(
        _norm_mod_kernel,
        out_shape=jax.ShapeDtypeStruct((s, d), BF16),
        grid=(s // tm,),
        in_specs=[pl.BlockSpec((tm, d), lambda i: (i, 0)), row, row, row],
        out_specs=pl.BlockSpec((tm, d), lambda i: (i, 0)),
        compiler_params=_params(("parallel",)),
        name="pre_norm",
    )(x, w, scale, shift)


def _inproj_kernel(h_ref, w_ref, wab_ref, p_ref, ab_ref):
    h = h_ref[...]
    p_ref[...] = _dot(h, w_ref[...]).astype(BF16)

    @pl.when(pl.program_id(1) == 0)
    def _():
        ab_ref[...] = _dot(h, wab_ref[...])


def _in_proj(h, w_p, w_ab, layer):
    s, d = h.shape
    n = w_p.shape[2]
    tm, tn = min(2048, s), 512
    return pl.pallas_call(
        _inproj_kernel,
        out_shape=(jax.ShapeDtypeStruct((s, n), BF16), jax.ShapeDtypeStruct((s, LANES), F32)),
        grid=(s // tm, n // tn),
        in_specs=[pl.BlockSpec((tm, d), lambda i, j: (i, 0)),
                  pl.BlockSpec((None, d, tn), lambda i, j: (layer, 0, j)),
                  pl.BlockSpec((None, d, LANES), lambda i, j: (layer, 0, 0))],
        out_specs=(pl.BlockSpec((tm, tn), lambda i, j: (i, j)),
                   pl.BlockSpec((tm, LANES), lambda i, j: (i, 0))),
        compiler_params=_params(("parallel", "arbitrary")),
        name="in_proj",
    )(h, w_p, w_ab)


def _gdn_prep_kernel(x_ref, halo_ref, cw_ref, ab_ref, alog_ref, dtb_ref, tri_ref,
                     q_ref, k_ref, v_ref, gcb_ref, gct_ref, xp_ref, *, tm):
    i = pl.program_id(0)
    outs = (q_ref, k_ref, v_ref)
    for grp in range(3):
        c0 = grp * GDN_QK
        halo = halo_ref[:, c0:c0 + GDN_QK].astype(F32)
        xp_ref[0:8, :] = jnp.where(i > 0, halo, 0.0)
        xp_ref[8:8 + tm, :] = x_ref[:, c0:c0 + GDN_QK].astype(F32)
        y = cw_ref[3:4, c0:c0 + GDN_QK] * xp_ref[8:8 + tm, :]
        for j in range(GDN_CONV - 1):
            off = 8 - (GDN_CONV - 1) + j
            y = y + cw_ref[j:j + 1, c0:c0 + GDN_QK] * xp_ref[off:off + tm, :]
        y = _silu(y)
        for h in range(GDN_HEADS):
            yh = y[:, h * GDN_DK:(h + 1) * GDN_DK]
            if grp < 2:
                yh = yh * lax.rsqrt(jnp.sum(yh * yh, axis=-1, keepdims=True) + EPS)
            if grp == 0:
                yh = yh * (GDN_DK ** -0.5)
            outs[grp][:, h * GDN_DK:(h + 1) * GDN_DK] = yh.astype(BF16)

    ab = ab_ref[...]
    g = -jnp.exp(alog_ref[...]) * _softplus(ab + dtb_ref[...])
    gc = jnp.dot(tri_ref[...], g, preferred_element_type=F32, precision=lax.Precision.HIGHEST)
    lane = lax.broadcasted_iota(jnp.int32, (1, LANES), 1)
    gcb = jnp.where(lane < GDN_HEADS, gc, _sigmoid(ab))
    gcb_ref[...] = gcb
    gct_ref[...] = gcb.T[0:2 * GDN_HEADS, :]


def _gdn_prep(p, ab, conv_w, a_log, dt_bias):
    s = p.shape[0]
    tm = min(512, s)
    c3 = 3 * GDN_QK
    alog_row = jnp.zeros((1, LANES), F32).at[0, :GDN_HEADS].set(a_log)
    dtb_row = jnp.zeros((1, LANES), F32).at[0, :GDN_HEADS].set(dt_bias)
    r = np.arange(tm)
    tri = jnp.asarray(((r[:, None] >= r[None, :]) & (r[:, None] // GDN_CHUNK == r[None, :] // GDN_CHUNK))
                      .astype(np.float32))
    hb = tm // 8
    row = lambda i: (i, 0)
    return pl.pallas_call(
        functools.partial(_gdn_prep_kernel, tm=tm),
        out_shape=(jax.ShapeDtypeStruct((s, GDN_QK), BF16), jax.ShapeDtypeStruct((s, GDN_QK), BF16),
                   jax.ShapeDtypeStruct((s, GDN_V), BF16), jax.ShapeDtypeStruct((s, LANES), F32),
                   jax.ShapeDtypeStruct((2 * GDN_HEADS, s), F32)),
        grid=(s // tm,),
        in_specs=[pl.BlockSpec((tm, c3), row),
                  pl.BlockSpec((8, c3), lambda i: (jnp.maximum(i * hb - 1, 0), 0)),
                  pl.BlockSpec((GDN_CONV, c3), lambda i: (0, 0)),
                  pl.BlockSpec((tm, LANES), row),
                  pl.BlockSpec((1, LANES), lambda i: (0, 0)),
                  pl.BlockSpec((1, LANES), lambda i: (0, 0)),
                  pl.BlockSpec((tm, tm), lambda i: (0, 0))],
        out_specs=(pl.BlockSpec((tm, GDN_QK), row), pl.BlockSpec((tm, GDN_QK), row),
                   pl.BlockSpec((tm, GDN_V), row), pl.BlockSpec((tm, LANES), row),
                   pl.BlockSpec((2 * GDN_HEADS, tm), lambda i: (0, i))),
        scratch_shapes=[pltpu.VMEM((tm + 8, GDN_QK), F32)],
        compiler_params=_params(("parallel",)),
        name="gdn_prep",
    )(p, p, conv_w, ab, alog_row, dtb_row, tri)


def _gdn_local_kernel(q_ref, k_ref, v_ref, gcb_ref, gct_ref,
                      u_ref, w_ref, qd_ref, kt_ref, qk_ref, *, groups):
    n = GDN_GROUP
    c = GDN_CHUNK
    ri = lax.broadcasted_iota(jnp.int32, (n, n), 0)
    ci = lax.broadcasted_iota(jnp.int32, (n, n), 1)
    same_chunk = (ri // c) == (ci // c)
    incl = same_chunk & (ri >= ci)
    strict = same_chunk & (ri > ci)
    eye = (ri == ci).astype(F32)
    last_sel = (ci == (ri // c) * c + (c - 1)).astype(F32)
    heads = range(GDN_HEADS)
    sls = [slice(h * GDN_DK, (h + 1) * GDN_DK) for h in heads]
    lane_chunk = lax.broadcasted_iota(jnp.int32, (c, n), 1) // c
    units = [(g, h) for g in range(groups) for h in heads]

    def to_cat(m):
        out = m[0:c]
        for j in range(1, n // c):
            out = out + m[j * c:(j + 1) * c]
        return out

    def to_bd(m_cat):
        zero = jnp.zeros_like(m_cat)
        return jnp.concatenate([jnp.where(lane_chunk == j, m_cat, zero) for j in range(n // c)], axis=0)

    lmats, tcs, rhss = {}, {}, {}
    for g in range(groups):
        rows = slice(g * n, (g + 1) * n)
        gcb = gcb_ref[rows, :]
        gc_last_all = jnp.dot(last_sel, gcb, preferred_element_type=F32, precision=lax.Precision.HIGHEST)
        for h in heads:
            sl = sls[h]
            q = q_ref[rows, sl].astype(F32)
            k16 = k_ref[rows, sl]
            k = k16.astype(F32)
            v = v_ref[rows, sl].astype(F32)
            gc_col = gcb[:, h:h + 1]
            beta = gcb[:, GDN_HEADS + h:GDN_HEADS + h + 1]
            gc_row = gct_ref[h:h + 1, rows]
            gc_last = gc_last_all[:, h:h + 1]
            diff = gc_col - gc_row
            decay = jnp.where(incl, jnp.exp(jnp.where(incl, diff, 0.0)), 0.0)
            kb = k * beta
            kk = _dot_nt(kb.astype(BF16), k16)
            lmat = jnp.where(strict, kk * decay, 0.0)
            lmats[g, h] = lmat
            tcs[g, h] = to_cat(eye - jnp.where((ri // 2) == (ci // 2), lmat, 0.0))
            eg = jnp.exp(gc_col)
            rhss[g, h] = jnp.concatenate([v * beta, kb * eg], axis=-1).astype(BF16)
            qd_ref[rows, sl] = (q * eg).astype(BF16)
            kt_ref[rows, sl] = (k * jnp.exp(gc_last - gc_col)).astype(BF16)
            qk = _dot_nt(q_ref[rows, sl], k16) * decay
            qk_c = jnp.concatenate([qk[j * c:(j + 1) * c, j * c:(j + 1) * c] for j in range(n // c)], axis=0)
            qk_ref[rows, h * c:(h + 1) * c] = qk_c.astype(BF16)
    b = 2
    while b < c:
        off_diag = ((ri // (2 * b)) == (ci // (2 * b))) & ((ri // b) != (ci // b))
        t16s = {u: tcs[u].astype(BF16) for u in units}
        ys = {u: _dot(t16s[u], jnp.where(off_diag, lmats[u], 0.0).astype(BF16)).astype(BF16) for u in units}
        tcs = {u: tcs[u] - _dot(ys[u], to_bd(t16s[u])) for u in units}
        b *= 2
    for g, h in units:
        rows = slice(g * n, (g + 1) * n)
        uw = _dot(to_bd(tcs[g, h].astype(BF16)), rhss[g, h])
        u_ref[rows, sls[h]] = uw[:, :GDN_DV].astype(BF16)
        w_ref[rows, sls[h]] = uw[:, GDN_DV:].astype(BF16)


def _gdn_local(q, k, v, gcb, gct):
    s = q.shape[0]
    groups = 2 if s % (2 * GDN_GROUP) == 0 else 1
    n = groups * GDN_GROUP
    row = lambda i: (i, 0)
    wide = pl.BlockSpec((n, GDN_QK), row)
    return pl.pallas_call(
        functools.partial(_gdn_local_kernel, groups=groups),
        out_shape=(jax.ShapeDtypeStruct((s, GDN_V), BF16), jax.ShapeDtypeStruct((s, GDN_QK), BF16),
                   jax.ShapeDtypeStruct((s, GDN_QK), BF16), jax.ShapeDtypeStruct((s, GDN_QK), BF16),
                   jax.ShapeDtypeStruct((s, GDN_HEADS * GDN_CHUNK), BF16)),
        grid=(s // n,),
        in_specs=[wide, wide, wide, pl.BlockSpec((n, LANES), row),
                  pl.BlockSpec((2 * GDN_HEADS, n), lambda i: (0, i))],
        out_specs=(wide, wide, wide, wide, pl.BlockSpec((n, GDN_HEADS * GDN_CHUNK), row)),
        compiler_params=_params(("parallel",)),
        name="gdn_local",
    )(q, k, v, gcb, gct)


def _gdn_scan_kernel(u_ref, w_ref, qd_ref, kt_ref, qk_ref, gcb_ref, z_ref, nw_ref, o_ref, st_ref, *, rb):
    c = GDN_CHUNK

    @pl.when(pl.program_id(0) == 0)
    def _():
        st_ref[...] = jnp.zeros_like(st_ref)

    nw = nw_ref[...]

    def chunk(ci, carry):
        r0 = pl.multiple_of(ci * c, c)
        rows = pl.ds(r0, c)
        gt_row = jnp.exp(gcb_ref[pl.ds(r0 + c - 1, 1), :])
        heads = range(GDN_HEADS)
        sls = [slice(h * GDN_DK, (h + 1) * GDN_DK) for h in heads]
        sts = [st_ref[h] for h in heads]
        r1s = [_dot(jnp.concatenate([w_ref[rows, sls[h]], qd_ref[rows, sls[h]]], axis=0),
                    sts[h].astype(BF16)) for h in heads]
        vns = [(u_ref[rows, sls[h]].astype(F32) - r1s[h][0:c]).astype(BF16) for h in heads]
        for h in heads:
            st_ref[h] = sts[h] * gt_row[:, h:h + 1] + _dot_tn(kt_ref[rows, sls[h]], vns[h])
        os_ = [r1s[h][c:2 * c] + _dot(qk_ref[rows, h * c:(h + 1) * c], vns[h]) for h in heads]
        for h in heads:
            o = _rms(os_[h]) * nw * _silu(z_ref[rows, sls[h]].astype(F32))
            o_ref[rows, sls[h]] = o.astype(BF16)
        return carry

    unroll = min(16, rb // c)
    assert (rb // c) % unroll == 0

    def chunks(i, carry):
        for u in range(unroll):
            carry = chunk(unroll * i + u, carry)
        return carry

    lax.fori_loop(0, rb // (unroll * c), chunks, 0)


def _gdn_scan(u, w, qd, kt, qk, gcb, p, norm_w):
    s = u.shape[0]
    rb = min(1024, s)
    row = lambda i: (i, 0)
    wide = pl.BlockSpec((rb, GDN_QK), row)
    return pl.pallas_call(
        functools.partial(_gdn_scan_kernel, rb=rb),
        out_shape=jax.ShapeDtypeStruct((s, GDN_V), BF16),
        grid=(s // rb,),
        in_specs=[wide, wide, wide, wide, pl.BlockSpec((rb, GDN_HEADS * GDN_CHUNK), row),
                  pl.BlockSpec((rb, LANES), row),
                  pl.BlockSpec((rb, GDN_V), lambda i: (i, P_Z // GDN_V)),
                  pl.BlockSpec((1, GDN_DV), lambda i: (0, 0))],
        out_specs=wide,
        scratch_shapes=[pltpu.VMEM((GDN_HEADS, GDN_DK, GDN_DV), F32)],
        compiler_params=_params(("arbitrary",)),
        name="gdn_scan",
    )(u, w, qd, kt, qk, gcb, p, norm_w.reshape(1, GDN_DV))


def _rope_table_kernel(pos_ref, inv_ref, cos_ref, sin_ref):
    ang = pos_ref[...].astype(F32) * inv_ref[...]
    lane = lax.broadcasted_iota(jnp.int32, ang.shape, 1)
    first_half = (lane % MLA_ROPE) < (MLA_ROPE // 2)
    cos_ref[...] = jnp.cos(ang)
    sin_ref[...] = jnp.where(first_half, -jnp.sin(ang), jnp.sin(ang))


def _rope_tables(positions):
    s = positions.shape[-1]
    tm = min(1024, s)
    half = MLA_ROPE // 2
    inv = (ROPE_THETA ** (-np.arange(half, dtype=np.float32) / half)).astype(np.float32)
    inv_row = jnp.asarray(np.tile(inv, LANES // half).reshape(1, LANES))
    return pl.pallas_call(
        _rope_table_kernel,
        out_shape=(jax.ShapeDtypeStruct((s, LANES), F32), jax.ShapeDtypeStruct((s, LANES), F32)),
        grid=(s // tm,),
        in_specs=[pl.BlockSpec((tm, 1), lambda i: (i, 0)), pl.BlockSpec((1, LANES), lambda i: (0, 0))],
        out_specs=(pl.BlockSpec((tm, LANES), lambda i: (i, 0)), pl.BlockSpec((tm, LANES), lambda i: (i, 0))),
        compiler_params=_params(("parallel",)),
        name="rope_tables",
    )(positions.reshape(s, 1), inv_row)


def _rope_apply(x, cos, sin_signed):
    width = x.shape[-1]
    half = MLA_ROPE // 2
    lane = lax.broadcasted_iota(jnp.int32, x.shape, 1)
    first_half = (lane % MLA_ROPE) < half
    swapped = jnp.where(first_half, pltpu.roll(x, width - half, 1), pltpu.roll(x, half, 1))
    return x * cos + swapped * sin_signed


def _mla_proj_kernel(cq_ref, ckv_ref, kr_ref, cos_ref, sin_ref, qn_ref, wuq_ref, kvn_ref, wuk_ref, wuvt_ref,
                     q_ref, k_ref, vt_ref):
    cos = cos_ref[...]
    sin = sin_ref[...]
    scale = (MLA_NOPE + MLA_ROPE) ** -0.5 * math.log2(math.e)
    cq = cq_ref[...].astype(F32)
    cqn = cq * lax.rsqrt(jnp.sum(cq * cq, axis=-1, keepdims=True) * (1.0 / MLA_Q_RANK) + EPS) * qn_ref[...]
    q = _dot(cqn.astype(BF16), wuq_ref[...])
    nope_w = MLA_HEADS * MLA_NOPE
    q_rope = _rope_apply(q[:, nope_w:], jnp.concatenate([cos, cos], -1), jnp.concatenate([sin, sin], -1))
    zeros = jnp.zeros((q.shape[0], MLA_QK_PAD - MLA_NOPE - MLA_ROPE), F32)
    ckvn = (_rms(ckv_ref[...].astype(F32)) * kvn_ref[...]).astype(BF16)
    k_nope = _dot(ckvn, wuk_ref[...])
    v_t = _dot_nt(wuvt_ref[...], ckvn)
    k_pe = _rope_apply(kr_ref[...].astype(F32), cos, sin)[:, :MLA_ROPE]
    for h in range(MLA_HEADS):
        qh = jnp.concatenate([q[:, h * MLA_NOPE:(h + 1) * MLA_NOPE],
                              q_rope[:, h * MLA_ROPE:(h + 1) * MLA_ROPE], zeros], axis=-1) * scale
        q_ref[h] = qh.astype(BF16)
        kh = jnp.concatenate([k_nope[:, h * MLA_NOPE:(h + 1) * MLA_NOPE], k_pe, zeros], axis=-1)
        k_ref[h] = kh.astype(BF16)
        vt_ref[h, 0] = v_t[h * MLA_DV:(h + 1) * MLA_DV, :].astype(BF16)


def _mla_proj(p, cos, sin, q_norm, w_uq, kv_norm, w_uk, w_uvt):
    s = p.shape[0]
    tm = min(MLA_BLOCK, s)
    const = lambda i: (0, 0)
    return pl.pallas_call(
        _mla_proj_kernel,
        out_shape=(jax.ShapeDtypeStruct((MLA_HEADS, s, MLA_QK_PAD), BF16),
                   jax.ShapeDtypeStruct((MLA_HEADS, s, MLA_QK_PAD), BF16),
                   jax.ShapeDtypeStruct((MLA_HEADS, s // tm, MLA_DV, tm), BF16)),
        grid=(s // tm,),
        in_specs=[pl.BlockSpec((tm, MLA_Q_RANK_PAD), lambda i: (i, P_CQ // MLA_Q_RANK_PAD)),
                  pl.BlockSpec((tm, LANES), lambda i: (i, P_CKV // LANES)),
                  pl.BlockSpec((tm, LANES), lambda i: (i, P_KROPE // LANES)),
                  pl.BlockSpec((tm, LANES), lambda i: (i, 0)),
                  pl.BlockSpec((tm, LANES), lambda i: (i, 0)),
                  pl.BlockSpec(q_norm.shape, const), pl.BlockSpec(w_uq.shape, const),
                  pl.BlockSpec(kv_norm.shape, const), pl.BlockSpec(w_uk.shape, const),
                  pl.BlockSpec(w_uvt.shape, const)],
        out_specs=(pl.BlockSpec((MLA_HEADS, tm, MLA_QK_PAD), lambda i: (0, i, 0)),
                   pl.BlockSpec((MLA_HEADS, tm, MLA_QK_PAD), lambda i: (0, i, 0)),
                   pl.BlockSpec((MLA_HEADS, 1, MLA_DV, tm), lambda i: (0, i, 0, 0))),
        compiler_params=_params(("parallel",)),
        name="mla_proj",
    )(p, p, p, cos, sin, q_norm, w_uq, kv_norm, w_uk, w_uvt)


def _mla_attn_kernel(q_ref, k_ref, vt_ref, o_ref, st0, st1, mb0, mb1, m_ref, l_ref, acc_ref, *, bk, nd):
    qi = pl.program_id(1)
    slots = ((st0, mb0), (st1, mb1))
    bq = nd * bk
    n0 = nd * qi
    tri = lax.broadcasted_iota(jnp.int32, (bk, bk), 0) <= lax.broadcasted_iota(jnp.int32, (bk, bk), 1)

    gq = bk // 2

    def scores(t, slot, c0, tri_mask):
        st_ref, mb_ref = slots[slot]
        r0 = pl.multiple_of(t * bk, bk)
        st = _dot_nt(k_ref[0, pl.ds(r0, bk), :], q_ref[0, c0:c0 + gq, :])
        if tri_mask is not None:
            st = jnp.where(tri_mask, st, -jnp.inf)
        st_ref[:, c0:c0 + gq] = st
        mb_ref[:, c0:c0 + gq] = jnp.max(st, axis=0, keepdims=True)

    def update(t, slot, c0):
        st_ref, mb_ref = slots[slot]
        cols = slice(c0, c0 + gq)
        m = m_ref[:, cols]
        m_new = jnp.maximum(m, mb_ref[:, cols])
        alpha = jnp.exp2(m - m_new)
        pexp = jnp.exp2(st_ref[:, cols] - m_new)
        l_ref[:, cols] = alpha * l_ref[:, cols] + jnp.sum(pexp, axis=0, keepdims=True)
        acc_ref[:, cols] = alpha * acc_ref[:, cols] + _dot(vt_ref[0, t], pexp.astype(BF16))
        m_ref[:, cols] = m_new

    def sweep(ts=None, ss=None, ds=None, tu=None, su=None, du=None):
        for c0 in range(0, bq, gq):
            g = c0 // bk
            if ts is not None and (ds is None or g >= ds):
                scores(ts, ss, c0, tri[:, c0 - g * bk:c0 - g * bk + gq] if g == ds else None)
            if tu is not None and (du is None or g >= du):
                update(tu, su, c0)

    m_ref[...] = jnp.full(m_ref.shape, -jnp.inf, F32)
    l_ref[...] = jnp.zeros(l_ref.shape, F32)
    acc_ref[...] = jnp.zeros(acc_ref.shape, F32)

    @pl.when(qi == 0)
    def _():
        sweep(ts=0, ss=0, ds=0)

    @pl.when(qi > 0)
    def _():
        sweep(ts=0, ss=0)

    def pair(j):
        sweep(ts=2 * j + 1, ss=1, tu=2 * j, su=0)
        sweep(ts=2 * j + 2, ss=0, tu=2 * j + 1, su=1)

    assert nd % 4 == 0

    @pl.when(qi > 0)
    def _():
        pair(0)

    def two_pairs(j, carry):
        pair(2 * j + 1)
        pair(2 * j + 2)
        return carry

    lax.fori_loop(0, n0 // 4 - 1, two_pairs, 0)

    @pl.when(qi > 0)
    def _():
        sweep(ts=n0 - 1, ss=1, tu=n0 - 2, su=0)
        sweep(ts=n0, ss=0, ds=0, tu=n0 - 1, su=1)

    for d in range(1, nd):
        sweep(ts=n0 + d, ss=d % 2, ds=d, tu=n0 + d - 1, su=(d - 1) % 2, du=d - 1)
    sweep(tu=n0 + nd - 1, su=(nd - 1) % 2, du=nd - 1)
    o_ref[...] = (acc_ref[...] / l_ref[...]).T.astype(BF16)


def _mla_attn(q, k, vt):
    nh, s, _ = q.shape
    bk = vt.shape[-1]
    nd = 4 if s % (4 * bk) == 0 else 2
    bq = nd * bk
    assert s % bq == 0
    return pl.pallas_call(
        functools.partial(_mla_attn_kernel, bk=bk, nd=nd),
        out_shape=jax.ShapeDtypeStruct((s, nh * MLA_DV), BF16),
        grid=(nh, s // bq),
        in_specs=[pl.BlockSpec((1, bq, MLA_QK_PAD), lambda h, i: (h, i, 0)),
                  pl.BlockSpec((1, s, MLA_QK_PAD), lambda h, i: (h, 0, 0)),
                  pl.BlockSpec((1, s // bk, MLA_DV, bk), lambda h, i: (h, 0, 0, 0))],
        out_specs=pl.BlockSpec((bq, MLA_DV), lambda h, i: (i, h)),
        scratch_shapes=[pltpu.VMEM((bk, bq), F32), pltpu.VMEM((bk, bq), F32),
                        pltpu.VMEM((1, bq), F32), pltpu.VMEM((1, bq), F32),
                        pltpu.VMEM((1, bq), F32), pltpu.VMEM((1, bq), F32), pltpu.VMEM((MLA_DV, bq), F32)],
        compiler_params=_params(("parallel", "arbitrary")),
        name="mla_attn",
    )(q, k, vt)


def _swa_kernel(q_ref, k_ref, v_ref, kh_ref, vh_ref, sink_ref, o_ref, *, nblk):
    w = SWA_WINDOW
    g = SWA_GROUP
    i = pl.program_id(0)
    key = lax.broadcasted_iota(jnp.int32, (2 * w, g * w), 0)
    col = lax.broadcasted_iota(jnp.int32, (2 * w, g * w), 1)
    dist = col % w + w - key
    band = (dist >= 0) & (dist < w)
    first_band = band & ((key >= w) | (i > 0))
    dist_f = dist.astype(F32)
    biases = []
    for hk in range(SWA_KV_HEADS):
        slope = jnp.zeros((2 * w, g * w), F32)
        for gi, gq in enumerate(SWA_GORDER):
            slope = jnp.where(col // w == gi, 2.0 ** (-8.0 * (hk * g + gq + 1.0) / SWA_HEADS), slope)
        biases.append(-slope * dist_f)
    low = lax.broadcasted_iota(jnp.int32, (2 * w, 2 * SWA_DH), 1) < SWA_DH
    top = lax.broadcasted_iota(jnp.int32, (2 * SWA_DH, 1), 0) < SWA_DH
    for blk in range(nblk):
        rows = slice(blk * w, (blk + 1) * w)
        if blk == 0:
            k_prev, v_prev, valid = kh_ref[...], vh_ref[...], first_band
        else:
            k_prev, v_prev, valid = k_ref[(blk - 1) * w:blk * w, :], v_ref[(blk - 1) * w:blk * w, :], band
        kk = jnp.concatenate([k_prev, k_ref[rows, :]], axis=0).astype(F32)
        kk_sw = pltpu.roll(kk, SWA_DH, 1)
        vvt = jnp.concatenate([v_prev, v_ref[rows, :]], axis=0).astype(F32).T
        probs, inv_denoms = [], []
        for hk in range(SWA_KV_HEADS):
            c0 = hk * g * SWA_DH
            q2 = jnp.concatenate([q_ref[rows, c0:c0 + 2 * SWA_DH], q_ref[rows, c0 + 2 * SWA_DH:c0 + 4 * SWA_DH]],
                                 axis=0) * (SWA_DH ** -0.5)
            k_low = jnp.where(low, kk if hk == 0 else kk_sw, 0.0).astype(BF16)
            k_high = jnp.where(low, 0.0, kk_sw if hk == 0 else kk).astype(BF16)
            st = jnp.concatenate([_dot_nt(k_low, q2), _dot_nt(k_high, q2)], axis=1)
            st = jnp.where(valid, st + biases[hk], -jnp.inf)
            sink = sink_ref[hk]
            m = jnp.maximum(jnp.max(st, axis=0, keepdims=True), sink)
            pexp = jnp.exp(st - m)
            denom = jnp.sum(pexp, axis=0, keepdims=True) + jnp.exp(sink - m)
            probs.append(pexp.astype(BF16))
            inv_denoms.append(1.0 / denom)
        v0 = jnp.where(top, vvt, 0.0).astype(BF16)
        v1 = jnp.where(top, 0.0, vvt).astype(BF16)
        o_t = (_dot(v0, probs[0]) + _dot(v1, probs[1])) * jnp.where(top, inv_denoms[0], inv_denoms[1])
        for gi in range(g):
            o_ref[rows, gi * LANES:(gi + 1) * LANES] = o_t[:, gi * w:(gi + 1) * w].T.astype(BF16)


def _swa(p, sinks):
    s = p.shape[0]
    w = SWA_WINDOW
    rb = min(512, s)
    nblk = rb // w
    order = np.asarray([[hk * SWA_GROUP + gq for gq in SWA_GORDER] for hk in range(SWA_KV_HEADS)])
    sink_rows = jnp.repeat(sinks[order], w, axis=1).reshape(SWA_KV_HEADS, 1, SWA_GROUP * w)
    prev = lambda col: (lambda i: (jnp.maximum(i * nblk - 1, 0), col))
    return pl.pallas_call(
        functools.partial(_swa_kernel, nblk=nblk),
        out_shape=jax.ShapeDtypeStruct((s, SWA_OUT), BF16),
        grid=(s // rb,),
        in_specs=[pl.BlockSpec((rb, SWA_OUT), lambda i: (i, P_SWQ // SWA_OUT)),
                  pl.BlockSpec((rb, SWA_KV), lambda i: (i, P_SWK // SWA_KV)),
                  pl.BlockSpec((rb, SWA_KV), lambda i: (i, P_SWV // SWA_KV)),
                  pl.BlockSpec((w, SWA_KV), prev(P_SWK // SWA_KV)),
                  pl.BlockSpec((w, SWA_KV), prev(P_SWV // SWA_KV)),
                  pl.BlockSpec(sink_rows.shape, lambda i: (0, 0, 0))],
        out_specs=pl.BlockSpec((rb, SWA_OUT), lambda i: (i, 0)),
        compiler_params=_params(("parallel",)),
        name="swa",
    )(p, p, p, p, p, sink_rows)


def _outproj_kernel(oa_ref, ob_ref, oc_ref, w_ref, x_ref, pn_ref, g1_ref, fn_ref, sc_ref, sh_ref,
                    x1_ref, h2_ref, *, sub):
    a_w, b_w = oa_ref.shape[1], ob_ref.shape[1]
    for r0 in range(0, x_ref.shape[0], sub):
        rows = slice(r0, r0 + sub)
        mix = (_dot(oa_ref[rows, :], w_ref[0:a_w, :]) + _dot(ob_ref[rows, :], w_ref[a_w:a_w + b_w, :])
               + _dot(oc_ref[rows, :], w_ref[a_w + b_w:, :]))
        x1 = x_ref[rows, :] + g1_ref[...] * (_rms(mix) * pn_ref[...])
        x1_ref[rows, :] = x1
        h2_ref[rows, :] = _norm_mod(x1, fn_ref[...], sc_ref[...], sh_ref[...]).astype(BF16)


def _out_proj(o_a, o_b, o_c, w_out, layer, x, post_norm, gate1, ffn_norm, scale2, shift2):
    s, d = x.shape
    tm = min(512, s)
    row = lambda i: (i, 0)
    vec = pl.BlockSpec((1, d), lambda i: (0, 0))
    return pl.pallas_call(
        functools.partial(_outproj_kernel, sub=min(256, tm)),
        out_shape=(jax.ShapeDtypeStruct((s, d), F32), jax.ShapeDtypeStruct((s, d), BF16)),
        grid=(s // tm,),
        in_specs=[pl.BlockSpec((tm, o_a.shape[1]), row), pl.BlockSpec((tm, o_b.shape[1]), row),
                  pl.BlockSpec((tm, o_c.shape[1]), row),
                  pl.BlockSpec((None,) + w_out.shape[1:], lambda i: (layer, 0, 0), pipeline_mode=pl.Buffered(1)),
                  pl.BlockSpec((tm, d), row), vec, vec, vec, vec, vec],
        out_specs=(pl.BlockSpec((tm, d), row), pl.BlockSpec((tm, d), row)),
        compiler_params=_params(("parallel",)),
        name="out_proj",
    )(o_a, o_b, o_c, w_out, x, post_norm, gate1, ffn_norm, scale2, shift2)


def _gelu_tanh(x):
    return 0.5 * x * (1.0 + jnp.tanh(math.sqrt(2.0 / math.pi) * (x + 0.044715 * (x * x * x))))


def _ffn_up_kernel(h_ref, wg_ref, wu_ref, cg_ref, cu_ref, bg_ref, bu_ref, o_ref,
                   wg16_ref, wu16_ref, xg_ref, xu_ref, *, tm, th):
    @pl.when(pl.program_id(1) == 0)
    def _():
        wg16_ref[...] = wg_ref[...].astype(BF16)
        wu16_ref[...] = wu_ref[...].astype(BF16)
        xg_ref[0:8, :] = jnp.zeros((8, xg_ref.shape[1]), F32)
        xu_ref[0:8, :] = jnp.zeros((8, xu_ref.shape[1]), F32)

    def half_tile(r0):
        h = h_ref[r0:r0 + th, :]

        def conv(x_ref, w_ref, cw_ref, b_ref):
            x_ref[8:8 + th, :] = _dot(h, w_ref[...])
            y = b_ref[...] + cw_ref[2:3, :] * x_ref[8:8 + th, :]
            y = y + cw_ref[1:2, :] * x_ref[7:7 + th, :]
            y = y + cw_ref[0:1, :] * x_ref[6:6 + th, :]
            x_ref[0:8, :] = x_ref[th:th + 8, :]
            return y

        gate = conv(xg_ref, wg16_ref, cg_ref, bg_ref)
        up = conv(xu_ref, wu16_ref, cu_ref, bu_ref)
        o_ref[r0:r0 + th, :] = (_gelu_tanh(gate) * up).astype(BF16)

    for r0 in range(0, tm, th):
        half_tile(r0)


def _ffn_up(h2, w_up, layer, conv_w, conv_b):
    s, d = h2.shape
    d_ff = w_up.shape[2] // 2
    tm, tn = min(2048, s), 512
    th = min(1024, tm)
    nj = d_ff // tn
    lo = lambda j, i: (0, j)
    hi = lambda j, i: (0, j + nj)
    w_lo = pl.BlockSpec((None, d, tn), lambda j, i: (layer, 0, j))
    w_hi = pl.BlockSpec((None, d, tn), lambda j, i: (layer, 0, j + nj))
    return pl.pallas_call(
        functools.partial(_ffn_up_kernel, tm=tm, th=th),
        out_shape=jax.ShapeDtypeStruct((s, d_ff), BF16),
        grid=(nj, s // tm),
        in_specs=[pl.BlockSpec((tm, d), lambda j, i: (i, 0)),
                  w_lo, w_hi,
                  pl.BlockSpec((FFN_CONV, tn), lo), pl.BlockSpec((FFN_CONV, tn), hi),
                  pl.BlockSpec((1, tn), lo), pl.BlockSpec((1, tn), hi)],
        out_specs=pl.BlockSpec((tm, tn), lambda j, i: (i, j)),
        scratch_shapes=[pltpu.VMEM((d, tn), BF16), pltpu.VMEM((d, tn), BF16),
                        pltpu.VMEM((th + 8, tn), F32), pltpu.VMEM((th + 8, tn), F32)],
        compiler_params=_params(("parallel", "arbitrary")),
        name="ffn_up",
    )(h2, w_up, w_up, conv_w, conv_w, conv_b, conv_b)


def _ffn_down_kernel(g_ref, w_ref, x_ref, pn_ref, g2_ref, *rest, sub):
    for r0 in range(0, x_ref.shape[0], sub):
        rows = slice(r0, r0 + sub)
        y = _dot(g_ref[rows, :], w_ref[...])
        x2 = x_ref[rows, :] + g2_ref[...] * (_rms(y) * pn_ref[...])
        if len(rest) == 1:
            rest[0][rows, :] = x2
        else:
            nn_ref, sc_ref, sh_ref, x2_ref, hn_ref = rest
            x2_ref[rows, :] = x2
            hn_ref[rows, :] = _norm_mod(x2, nn_ref[...], sc_ref[...], sh_ref[...]).astype(BF16)


def _ffn_down(g, w_down, layer, x1, post_norm, gate2, next_pre_norm=None):
    s, d = x1.shape
    d_ff = g.shape[1]
    tm = min(512, s)
    row = lambda i: (i, 0)
    vec = pl.BlockSpec((1, d), lambda i: (0, 0))
    tile_f32 = jax.ShapeDtypeStruct((s, d), F32)
    tile = pl.BlockSpec((tm, d), row)
    last = next_pre_norm is None
    return pl.pallas_call(
        functools.partial(_ffn_down_kernel, sub=min(256, tm)),
        out_shape=tile_f32 if last else (tile_f32, jax.ShapeDtypeStruct((s, d), BF16)),
        grid=(s // tm,),
        in_specs=[pl.BlockSpec((tm, d_ff), row),
                  pl.BlockSpec((None, d_ff, d), lambda i: (layer, 0, 0), pipeline_mode=pl.Buffered(1)),
                  tile, vec, vec] + ([] if last else [vec, vec, vec]),
        out_specs=tile if last else (tile, tile),
        compiler_params=_params(("parallel",), VMEM_LIMIT_RESIDENT_MB),
        name="ffn_down",
    )(g, w_down, x1, post_norm, gate2, *(() if last else next_pre_norm))


def _layout_w_in(w_in):
    depth, d, _ = w_in.shape
    sizes = (GDN_QK, GDN_QK, GDN_V, GDN_V, GDN_HEADS, GDN_HEADS, MLA_Q_RANK, MLA_KV_RANK, MLA_ROPE,
             SWA_OUT, SWA_KV, SWA_KV)
    offs = np.concatenate([[0], np.cumsum(sizes)])
    part = lambda n: w_in[:, :, offs[n]:offs[n + 1]].astype(BF16)
    zeros = lambda n: jnp.zeros((depth, d, n), BF16)
    w_p = jnp.concatenate([w_in[:, :, :offs[4]].astype(BF16),
                           part(6), zeros(MLA_Q_RANK_PAD - MLA_Q_RANK),
                           part(9), part(7), part(8), zeros(LANES - MLA_ROPE), part(10), part(11)], axis=2)
    w_ab = jnp.concatenate([part(4), part(5), zeros(LANES - 2 * GDN_HEADS)], axis=2)
    return w_p, w_ab


def _layout_w_out(w_out):
    depth, d_mix, d = w_out.shape
    a = d_mix - SWA_OUT
    wc = w_out[:, a:].reshape(depth, SWA_KV_HEADS, SWA_GROUP, SWA_DH, d)[:, :, np.asarray(SWA_GORDER)]
    wc = wc.transpose(0, 2, 1, 3, 4).reshape(depth, SWA_OUT, d)
    return jnp.concatenate([w_out[:, :a].astype(BF16), wc.astype(BF16)], axis=1)


def _layout_mla(q_norm, w_uq, kv_norm, w_ukv):
    dqk = MLA_NOPE + MLA_ROPE
    uq = w_uq.reshape(MLA_Q_RANK, MLA_HEADS, dqk)
    uq = jnp.concatenate([uq[:, :, :MLA_NOPE].reshape(MLA_Q_RANK, -1), uq[:, :, MLA_NOPE:].reshape(MLA_Q_RANK, -1)],
                         axis=1)
    uq = jnp.pad(uq, ((0, MLA_Q_RANK_PAD - MLA_Q_RANK), (0, 0))).astype(BF16)
    qn = jnp.pad(q_norm, (0, MLA_Q_RANK_PAD - MLA_Q_RANK)).reshape(1, MLA_Q_RANK_PAD)
    ukv = w_ukv.reshape(MLA_KV_RANK, MLA_HEADS, MLA_NOPE + MLA_DV)
    uk = ukv[:, :, :MLA_NOPE].reshape(MLA_KV_RANK, -1).astype(BF16)
    uvt = ukv[:, :, MLA_NOPE:].reshape(MLA_KV_RANK, -1).T.astype(BF16)
    return qn, uq, kv_norm.reshape(1, MLA_KV_RANK), uk, uvt


def kernel(x, c, positions, ada_w, ada_b, mix_pre_norm, mix_post_norm, w_in, w_out, gdn_conv, gdn_a_log, gdn_dt_bias, gdn_norm, mla_q_norm, mla_w_uq, mla_kv_norm, mla_w_ukv, swa_sinks, ffn_pre_norm, ffn_post_norm, ffn_w_up, ffn_conv, ffn_conv_b, ffn_w_down):
    batch, s, d = x.shape
    assert batch == 1, "kernels are written for a single sequence"
    depth = ada_w.shape[0]
    xs = x.reshape(s, d)
    mod = _adaln_mod(c, ada_w, ada_b).reshape(depth, N_MOD, 1, d)
    cos, sin = _rope_tables(positions)
    vec = lambda a: a.reshape(1, d)

    w_p, w_ab = _layout_w_in(w_in)
    w_o = _layout_w_out(w_out)
    w_down = ffn_w_down.astype(BF16)

    h = _pre_norm(xs, vec(mix_pre_norm[0]), mod[0, 1], mod[0, 0])
    for l in range(depth):
        shift1, scale1, gate1, shift2, scale2, gate2 = (mod[l, n] for n in range(N_MOD))
        p, ab = _in_proj(h, w_p, w_ab, l)

        q_a, k_a, v_a, gcb, gct = _gdn_prep(p, ab, gdn_conv[l], gdn_a_log[l], gdn_dt_bias[l])
        u, w, qd, kt, qk = _gdn_local(q_a, k_a, v_a, gcb, gct)
        o_a = _gdn_scan(u, w, qd, kt, qk, gcb, p, gdn_norm[l])

        q_b, k_b, v_b = _mla_proj(p, cos, sin, *_layout_mla(mla_q_norm[l], mla_w_uq[l], mla_kv_norm[l],
                                                            mla_w_ukv[l]))
        o_b = _mla_attn(q_b, k_b, v_b)

        o_c = _swa(p, swa_sinks[l])

        x1, h2 = _out_proj(o_a, o_b, o_c, w_o, l, xs, vec(mix_post_norm[l]), gate1,
                           vec(ffn_pre_norm[l]), scale2, shift2)
        g = _ffn_up(h2, ffn_w_up, l, ffn_conv[l], ffn_conv_b[l].reshape(1, -1))
        if l + 1 < depth:
            xs, h = _ffn_down(g, w_down, l, x1, vec(ffn_post_norm[l]), gate2,
                              (vec(mix_pre_norm[l + 1]), mod[l + 1, 1], mod[l + 1, 0]))
        else:
            xs = _ffn_down(g, w_down, l, x1, vec(ffn_post_norm[l]), gate2)
    return xs.reshape(batch, s, d)
```

```python
import functools
import math

import numpy as np
import jax
import jax.numpy as jnp
from jax import lax
from jax.experimental import pallas as pl
from jax.experimental.pallas import tpu as pltpu

F32 = jnp.float32
BF16 = jnp.bfloat16

EPS = 1e-6
N_MOD = 6
GDN_HEADS = 8
GDN_DK = 128
GDN_DV = 128
GDN_CONV = 4
GDN_CHUNK = 64
GDN_QK = GDN_HEADS * GDN_DK
GDN_V = GDN_HEADS * GDN_DV
MLA_HEADS = 4
MLA_Q_RANK = 448
MLA_Q_RANK_PAD = 512
MLA_KV_RANK = 128
MLA_NOPE = 128
MLA_ROPE = 64
MLA_DV = 128
MLA_QK_PAD = 256
MLA_BLOCK = 512
ROPE_THETA = 10000.0
SWA_HEADS = 8
SWA_KV_HEADS = 2
SWA_GROUP = SWA_HEADS // SWA_KV_HEADS
SWA_DH = 64
SWA_WINDOW = 128
SWA_OUT = SWA_HEADS * SWA_DH
SWA_KV = SWA_KV_HEADS * SWA_DH
SWA_GORDER = (0, 2, 1, 3)
FFN_CONV = 3

LANES = 128
GDN_GROUP = 256
VMEM_LIMIT_MB = 56
VMEM_LIMIT_RESIDENT_MB = 60

P_QKV = 0
P_Z = 3072
P_CQ = 4096
P_SWQ = 4608
P_CKV = 5120
P_KROPE = 5248
P_SWK = 5376
P_SWV = 5504
P_WIDTH = 5632


def _params(semantics, vmem_limit_mb=VMEM_LIMIT_MB):
    return pltpu.CompilerParams(dimension_semantics=semantics, vmem_limit_bytes=vmem_limit_mb << 20)


def _sigmoid(x):
    return 0.5 + 0.5 * jnp.tanh(0.5 * x)


def _silu(x):
    h = 0.5 * x
    return h + h * jnp.tanh(h)


def _softplus(x):
    return jnp.maximum(x, 0.0) + jnp.log(1.0 + jnp.exp(-jnp.abs(x)))


def _rms(x):
    return x * lax.rsqrt(jnp.mean(x * x, axis=-1, keepdims=True) + EPS)


def _dot(a, b):
    return jnp.dot(a, b, preferred_element_type=F32)


def _dot_nt(a, b):
    return lax.dot_general(a, b, (((1,), (1,)), ((), ())), preferred_element_type=F32)


def _dot_tn(a, b):
    return lax.dot_general(a, b, (((0,), (0,)), ((), ())), preferred_element_type=F32)


def _mod_kernel(c_ref, w_ref, b_ref, o_ref, *, d, kc):
    cact = _silu(c_ref[...])
    acc = b_ref[0]
    for k0 in range(0, d, kc):
        acc = acc + jnp.sum(w_ref[0, k0:k0 + kc, :] * cact[k0:k0 + kc], axis=0, keepdims=True)
    o_ref[0] = acc


def _adaln_mod(c, ada_w, ada_b):
    depth, d, n = ada_w.shape
    tn = 1024
    return pl.pallas_call(
        functools.partial(_mod_kernel, d=d, kc=256),
        out_shape=jax.ShapeDtypeStruct((depth, 1, n), F32),
        grid=(depth, n // tn),
        in_specs=[pl.BlockSpec((d, 1), lambda l, j: (0, 0)),
                  pl.BlockSpec((1, d, tn), lambda l, j: (l, 0, j)),
                  pl.BlockSpec((1, 1, tn), lambda l, j: (l, 0, j))],
        out_specs=pl.BlockSpec((1, 1, tn), lambda l, j: (l, 0, j)),
        compiler_params=_params(("parallel", "parallel")),
        name="adaln_mod",
    )(c.reshape(d, 1), ada_w, ada_b.reshape(depth, 1, n))


def _norm_mod(x, w, scale, shift):
    return _rms(x) * w * (1.0 + scale) + shift


def _norm_mod_kernel(x_ref, w_ref, sc_ref, sh_ref, h_ref):
    h_ref[...] = _norm_mod(x_ref[...], w_ref[...], sc_ref[...], sh_ref[...]).astype(BF16)


def _pre_norm(x, w, scale, shift):
    s, d = x.shape
    tm = min(1024, s)
    row = pl.BlockSpec((1, d), lambda i: (0, 0))
    return pl.pallas_call(
        _norm_mod_kernel,
        out_shape=jax.ShapeDtypeStruct((s, d), BF16),
        grid=(s // tm,),
        in_specs=[pl.BlockSpec((tm, d), lambda i: (i, 0)), row, row, row],
        out_specs=pl.BlockSpec((tm, d), lambda i: (i, 0)),
        compiler_params=_params(("parallel",)),
        name="pre_norm",
    )(x, w, scale, shift)


def _inproj_kernel(h_ref, w_ref, wab_ref, p_ref, ab_ref):
    h = h_ref[...]
    p_ref[...] = _dot(h, w_ref[...]).astype(BF16)

    @pl.when(pl.program_id(1) == 0)
    def _():
        ab_ref[...] = _dot(h, wab_ref[...])


def _in_proj(h, w_p, w_ab, layer):
    s, d = h.shape
    n = w_p.shape[2]
    tm, tn = min(2048, s), 512
    return pl.pallas_call(
        _inproj_kernel,
        out_shape=(jax.ShapeDtypeStruct((s, n), BF16), jax.ShapeDtypeStruct((s, LANES), F32)),
        grid=(s // tm, n // tn),
        in_specs=[pl.BlockSpec((tm, d), lambda i, j: (i, 0)),
                  pl.BlockSpec((None, d, tn), lambda i, j: (layer, 0, j)),
                  pl.BlockSpec((None, d, LANES), lambda i, j: (layer, 0, 0))],
        out_specs=(pl.BlockSpec((tm, tn), lambda i, j: (i, j)),
                   pl.BlockSpec((tm, LANES), lambda i, j: (i, 0))),
        compiler_params=_params(("parallel", "arbitrary")),
        name="in_proj",
    )(h, w_p, w_ab)


def _gdn_prep_kernel(x_ref, halo_ref, cw_ref, ab_ref, alog_ref, dtb_ref, tri_ref,
                     q_ref, k_ref, v_ref, gcb_ref, gct_ref, xp_ref, *, tm):
    i = pl.program_id(0)
    outs = (q_ref, k_ref, v_ref)
    for grp in range(3):
        c0 = grp * GDN_QK
        halo = halo_ref[:, c0:c0 + GDN_QK].astype(F32)
        xp_ref[0:8, :] = jnp.where(i > 0, halo, 0.0)
        xp_ref[8:8 + tm, :] = x_ref[:, c0:c0 + GDN_QK].astype(F32)
        cwh = 0.5 * cw_ref[:, c0:c0 + GDN_QK]
        hy = cwh[3:4] * xp_ref[8:8 + tm, :]
        for j in range(GDN_CONV - 1):
            off = 8 - (GDN_CONV - 1) + j
            hy = hy + cwh[j:j + 1] * xp_ref[off:off + tm, :]
        y = hy + hy * jnp.tanh(hy)
        for h in range(GDN_HEADS):
            yh = y[:, h * GDN_DK:(h + 1) * GDN_DK]
            if grp < 2:
                inv = lax.rsqrt(jnp.sum(yh * yh, axis=-1, keepdims=True) + EPS)
                if grp == 0:
                    inv = inv * (GDN_DK ** -0.5)
                yh = yh * inv
            outs[grp][:, h * GDN_DK:(h + 1) * GDN_DK] = yh.astype(BF16)

    ab = ab_ref[...]
    g = -jnp.exp(alog_ref[...]) * _softplus(ab + dtb_ref[...])
    gc = jnp.dot(tri_ref[...], g, preferred_element_type=F32, precision=lax.Precision.HIGHEST)
    lane = lax.broadcasted_iota(jnp.int32, (1, LANES), 1)
    gcb = jnp.where(lane < GDN_HEADS, gc, _sigmoid(ab))
    gcb_ref[...] = gcb
    gct_ref[...] = gcb.T[0:2 * GDN_HEADS, :]


def _gdn_prep(p, ab, conv_w, a_log, dt_bias):
    s = p.shape[0]
    tm = min(512, s)
    c3 = 3 * GDN_QK
    alog_row = jnp.zeros((1, LANES), F32).at[0, :GDN_HEADS].set(a_log)
    dtb_row = jnp.zeros((1, LANES), F32).at[0, :GDN_HEADS].set(dt_bias)
    r = np.arange(tm)
    tri = jnp.asarray(((r[:, None] >= r[None, :]) & (r[:, None] // GDN_CHUNK == r[None, :] // GDN_CHUNK))
                      .astype(np.float32))
    hb = tm // 8
    row = lambda i: (i, 0)
    return pl.pallas_call(
        functools.partial(_gdn_prep_kernel, tm=tm),
        out_shape=(jax.ShapeDtypeStruct((s, GDN_QK), BF16), jax.ShapeDtypeStruct((s, GDN_QK), BF16),
                   jax.ShapeDtypeStruct((s, GDN_V), BF16), jax.ShapeDtypeStruct((s, LANES), F32),
                   jax.ShapeDtypeStruct((2 * GDN_HEADS, s), F32)),
        grid=(s // tm,),
        in_specs=[pl.BlockSpec((tm, c3), row),
                  pl.BlockSpec((8, c3), lambda i: (jnp.maximum(i * hb - 1, 0), 0)),
                  pl.BlockSpec((GDN_CONV, c3), lambda i: (0, 0)),
                  pl.BlockSpec((tm, LANES), row),
                  pl.BlockSpec((1, LANES), lambda i: (0, 0)),
                  pl.BlockSpec((1, LANES), lambda i: (0, 0)),
                  pl.BlockSpec((tm, tm), lambda i: (0, 0))],
        out_specs=(pl.BlockSpec((tm, GDN_QK), row), pl.BlockSpec((tm, GDN_QK), row),
                   pl.BlockSpec((tm, GDN_V), row), pl.BlockSpec((tm, LANES), row),
                   pl.BlockSpec((2 * GDN_HEADS, tm), lambda i: (0, i))),
        scratch_shapes=[pltpu.VMEM((tm + 8, GDN_QK), F32)],
        compiler_params=_params(("parallel",)),
        name="gdn_prep",
    )(p, p, conv_w, ab, alog_row, dtb_row, tri)


def _gdn_local_kernel(q_ref, k_ref, v_ref, gcb_ref, gct_ref,
                      u_ref, w_ref, qd_ref, kt_ref, qk_ref, *, groups):
    n = GDN_GROUP
    c = GDN_CHUNK
    ri = lax.broadcasted_iota(jnp.int32, (n, n), 0)
    ci = lax.broadcasted_iota(jnp.int32, (n, n), 1)
    same_chunk = (ri // c) == (ci // c)
    incl = same_chunk & (ri >= ci)
    strict = same_chunk & (ri > ci)
    eye = (ri == ci).astype(F32)
    last_sel = (ci == (ri // c) * c + (c - 1)).astype(F32)
    heads = range(GDN_HEADS)
    sls = [slice(h * GDN_DK, (h + 1) * GDN_DK) for h in heads]
    lane_chunk = lax.broadcasted_iota(jnp.int32, (c, n), 1) // c
    units = [(g, h) for g in range(groups) for h in heads]

    def to_cat(m):
        out = m[0:c]
        for j in range(1, n // c):
            out = out + m[j * c:(j + 1) * c]
        return out

    def to_bd(m_cat):
        zero = jnp.zeros_like(m_cat)
        return jnp.concatenate([jnp.where(lane_chunk == j, m_cat, zero) for j in range(n // c)], axis=0)

    lmats, tcs, rhss = {}, {}, {}
    for g in range(groups):
        rows = slice(g * n, (g + 1) * n)
        gcb = gcb_ref[rows, :]
        gc_last_all = jnp.dot(last_sel, gcb, preferred_element_type=F32, precision=lax.Precision.HIGHEST)
        for h in heads:
            sl = sls[h]
            q = q_ref[rows, sl].astype(F32)
            k16 = k_ref[rows, sl]
            k = k16.astype(F32)
            v = v_ref[rows, sl].astype(F32)
            gc_col = gcb[:, h:h + 1]
            beta = gcb[:, GDN_HEADS + h:GDN_HEADS + h + 1]
            gc_row = gct_ref[h:h + 1, rows]
            gc_last = gc_last_all[:, h:h + 1]
            diff = gc_col - gc_row
            decay = jnp.where(incl, jnp.exp(jnp.where(incl, diff, 0.0)), 0.0)
            kb = k * beta
            kk = _dot_nt(kb.astype(BF16), k16)
            lmat = jnp.where(strict, kk * decay, 0.0)
            lmats[g, h] = lmat
            tcs[g, h] = to_cat(eye - jnp.where((ri // 2) == (ci // 2), lmat, 0.0))
            eg = jnp.exp(gc_col)
            rhss[g, h] = jnp.concatenate([v * beta, kb * eg], axis=-1).astype(BF16)
            qd_ref[rows, sl] = (q * eg).astype(BF16)
            kt_ref[rows, sl] = (k * jnp.exp(gc_last - gc_col)).astype(BF16)
            qk = _dot_nt(q_ref[rows, sl], k16) * decay
            qk_c = jnp.concatenate([qk[j * c:(j + 1) * c, j * c:(j + 1) * c] for j in range(n // c)], axis=0)
            qk_ref[rows, h * c:(h + 1) * c] = qk_c.astype(BF16)
    b = 2
    while b < c:
        off_diag = ((ri // (2 * b)) == (ci // (2 * b))) & ((ri // b) != (ci // b))
        t16s = {u: tcs[u].astype(BF16) for u in units}
        ys = {u: _dot(t16s[u], jnp.where(off_diag, lmats[u], 0.0).astype(BF16)).astype(BF16) for u in units}
        tcs = {u: tcs[u] - _dot(ys[u], to_bd(t16s[u])) for u in units}
        b *= 2
    for g, h in units:
        rows = slice(g * n, (g + 1) * n)
        uw = _dot(to_bd(tcs[g, h].astype(BF16)), rhss[g, h])
        u_ref[rows, sls[h]] = uw[:, :GDN_DV].astype(BF16)
        w_ref[rows, sls[h]] = uw[:, GDN_DV:].astype(BF16)


def _gdn_local(q, k, v, gcb, gct):
    s = q.shape[0]
    groups = 2 if s % (2 * GDN_GROUP) == 0 else 1
    n = groups * GDN_GROUP
    row = lambda i: (i, 0)
    wide = pl.BlockSpec((n, GDN_QK), row)
    return pl.pallas_call(
        functools.partial(_gdn_local_kernel, groups=groups),
        out_shape=(jax.ShapeDtypeStruct((s, GDN_V), BF16), jax.ShapeDtypeStruct((s, GDN_QK), BF16),
                   jax.ShapeDtypeStruct((s, GDN_QK), BF16), jax.ShapeDtypeStruct((s, GDN_QK), BF16),
                   jax.ShapeDtypeStruct((s, GDN_HEADS * GDN_CHUNK), BF16)),
        grid=(s // n,),
        in_specs=[wide, wide, wide, pl.BlockSpec((n, LANES), row),
                  pl.BlockSpec((2 * GDN_HEADS, n), lambda i: (0, i))],
        out_specs=(wide, wide, wide, wide, pl.BlockSpec((n, GDN_HEADS * GDN_CHUNK), row)),
        compiler_params=_params(("parallel",)),
        name="gdn_local",
    )(q, k, v, gcb, gct)


def _gdn_scan_kernel(u_ref, w_ref, qd_ref, kt_ref, qk_ref, gcb_ref, z_ref, nw_ref, o_ref, st_ref, *, rb):
    c = GDN_CHUNK

    @pl.when(pl.program_id(0) == 0)
    def _():
        st_ref[...] = jnp.zeros_like(st_ref)

    nw = nw_ref[...]

    def chunk(ci, carry):
        r0 = pl.multiple_of(ci * c, c)
        rows = pl.ds(r0, c)
        gt_row = jnp.exp(gcb_ref[pl.ds(r0 + c - 1, 1), :])
        heads = range(GDN_HEADS)
        sls = [slice(h * GDN_DK, (h + 1) * GDN_DK) for h in heads]
        sts = [st_ref[h] for h in heads]
        r1s = [_dot(jnp.concatenate([w_ref[rows, sls[h]], qd_ref[rows, sls[h]]], axis=0),
                    sts[h].astype(BF16)) for h in heads]
        vns = [(u_ref[rows, sls[h]].astype(F32) - r1s[h][0:c]).astype(BF16) for h in heads]
        for h in heads:
            st_ref[h] = sts[h] * gt_row[:, h:h + 1] + _dot_tn(kt_ref[rows, sls[h]], vns[h])
        os_ = [r1s[h][c:2 * c] + _dot(qk_ref[rows, h * c:(h + 1) * c], vns[h]) for h in heads]
        for h in heads:
            o = _rms(os_[h]) * nw * _silu(z_ref[rows, sls[h]].astype(F32))
            o_ref[rows, sls[h]] = o.astype(BF16)
        return carry

    unroll = min(16, rb // c)
    assert (rb // c) % unroll == 0

    def chunks(i, carry):
        for u in range(unroll):
            carry = chunk(unroll * i + u, carry)
        return carry

    lax.fori_loop(0, rb // (unroll * c), chunks, 0)


def _gdn_scan(u, w, qd, kt, qk, gcb, p, norm_w):
    s = u.shape[0]
    rb = min(1024, s)
    row = lambda i: (i, 0)
    wide = pl.BlockSpec((rb, GDN_QK), row)
    return pl.pallas_call(
        functools.partial(_gdn_scan_kernel, rb=rb),
        out_shape=jax.ShapeDtypeStruct((s, GDN_V), BF16),
        grid=(s // rb,),
        in_specs=[wide, wide, wide, wide, pl.BlockSpec((rb, GDN_HEADS * GDN_CHUNK), row),
                  pl.BlockSpec((rb, LANES), row),
                  pl.BlockSpec((rb, GDN_V), lambda i: (i, P_Z // GDN_V)),
                  pl.BlockSpec((1, GDN_DV), lambda i: (0, 0))],
        out_specs=wide,
        scratch_shapes=[pltpu.VMEM((GDN_HEADS, GDN_DK, GDN_DV), F32)],
        compiler_params=_params(("arbitrary",)),
        name="gdn_scan",
    )(u, w, qd, kt, qk, gcb, p, norm_w.reshape(1, GDN_DV))


def _rope_table_kernel(pos_ref, inv_ref, cos_ref, sin_ref):
    ang = pos_ref[...].astype(F32) * inv_ref[...]
    lane = lax.broadcasted_iota(jnp.int32, ang.shape, 1)
    first_half = (lane % MLA_ROPE) < (MLA_ROPE // 2)
    cos_ref[...] = jnp.cos(ang)
    sin_ref[...] = jnp.where(first_half, -jnp.sin(ang), jnp.sin(ang))


def _rope_tables(positions):
    s = positions.shape[-1]
    tm = min(1024, s)
    half = MLA_ROPE // 2
    inv = (ROPE_THETA ** (-np.arange(half, dtype=np.float32) / half)).astype(np.float32)
    inv_row = jnp.asarray(np.tile(inv, LANES // half).reshape(1, LANES))
    return pl.pallas_call(
        _rope_table_kernel,
        out_shape=(jax.ShapeDtypeStruct((s, LANES), F32), jax.ShapeDtypeStruct((s, LANES), F32)),
        grid=(s // tm,),
        in_specs=[pl.BlockSpec((tm, 1), lambda i: (i, 0)), pl.BlockSpec((1, LANES), lambda i: (0, 0))],
        out_specs=(pl.BlockSpec((tm, LANES), lambda i: (i, 0)), pl.BlockSpec((tm, LANES), lambda i: (i, 0))),
        compiler_params=_params(("parallel",)),
        name="rope_tables",
    )(positions.reshape(s, 1), inv_row)


def _rope_apply(x, cos, sin_signed):
    width = x.shape[-1]
    half = MLA_ROPE // 2
    lane = lax.broadcasted_iota(jnp.int32, x.shape, 1)
    first_half = (lane % MLA_ROPE) < half
    swapped = jnp.where(first_half, pltpu.roll(x, width - half, 1), pltpu.roll(x, half, 1))
    return x * cos + swapped * sin_signed


def _mla_proj_kernel(cq_ref, ckv_ref, kr_ref, cos_ref, sin_ref, qn_ref, wuq_ref, kvn_ref, wuk_ref, wuvt_ref,
                     q_ref, k_ref, vt_ref):
    cos = cos_ref[...]
    sin = sin_ref[...]
    scale = (MLA_NOPE + MLA_ROPE) ** -0.5 * math.log2(math.e)
    cq = cq_ref[...].astype(F32)
    cqn = cq * lax.rsqrt(jnp.sum(cq * cq, axis=-1, keepdims=True) * (1.0 / MLA_Q_RANK) + EPS) * qn_ref[...]
    q = _dot(cqn.astype(BF16), wuq_ref[...])
    nope_w = MLA_HEADS * MLA_NOPE
    q_rope = _rope_apply(q[:, nope_w:], jnp.concatenate([cos, cos], -1), jnp.concatenate([sin, sin], -1))
    zeros = jnp.zeros((q.shape[0], MLA_QK_PAD - MLA_NOPE - MLA_ROPE), F32)
    ckvn = (_rms(ckv_ref[...].astype(F32)) * kvn_ref[...]).astype(BF16)
    k_nope = _dot(ckvn, wuk_ref[...])
    v_t = _dot_nt(wuvt_ref[...], ckvn)
    k_pe = _rope_apply(kr_ref[...].astype(F32), cos, sin)[:, :MLA_ROPE]
    for h in range(MLA_HEADS):
        qh = jnp.concatenate([q[:, h * MLA_NOPE:(h + 1) * MLA_NOPE],
                              q_rope[:, h * MLA_ROPE:(h + 1) * MLA_ROPE], zeros], axis=-1) * scale
        q_ref[h] = qh.astype(BF16)
        kh = jnp.concatenate([k_nope[:, h * MLA_NOPE:(h + 1) * MLA_NOPE], k_pe, zeros], axis=-1)
        k_ref[h] = kh.astype(BF16)
        vt_ref[h, 0] = v_t[h * MLA_DV:(h + 1) * MLA_DV, :].astype(BF16)


def _mla_proj(p, cos, sin, q_norm, w_uq, kv_norm, w_uk, w_uvt):
    s = p.shape[0]
    tm = min(MLA_BLOCK, s)
    const = lambda i: (0, 0)
    return pl.pallas_call(
        _mla_proj_kernel,
        out_shape=(jax.ShapeDtypeStruct((MLA_HEADS, s, MLA_QK_PAD), BF16),
                   jax.ShapeDtypeStruct((MLA_HEADS, s, MLA_QK_PAD), BF16),
                   jax.ShapeDtypeStruct((MLA_HEADS, s // tm, MLA_DV, tm), BF16)),
        grid=(s // tm,),
        in_specs=[pl.BlockSpec((tm, MLA_Q_RANK_PAD), lambda i: (i, P_CQ // MLA_Q_RANK_PAD)),
                  pl.BlockSpec((tm, LANES), lambda i: (i, P_CKV // LANES)),
                  pl.BlockSpec((tm, LANES), lambda i: (i, P_KROPE // LANES)),
                  pl.BlockSpec((tm, LANES), lambda i: (i, 0)),
                  pl.BlockSpec((tm, LANES), lambda i: (i, 0)),
                  pl.BlockSpec(q_norm.shape, const), pl.BlockSpec(w_uq.shape, const),
                  pl.BlockSpec(kv_norm.shape, const), pl.BlockSpec(w_uk.shape, const),
                  pl.BlockSpec(w_uvt.shape, const)],
        out_specs=(pl.BlockSpec((MLA_HEADS, tm, MLA_QK_PAD), lambda i: (0, i, 0)),
                   pl.BlockSpec((MLA_HEADS, tm, MLA_QK_PAD), lambda i: (0, i, 0)),
                   pl.BlockSpec((MLA_HEADS, 1, MLA_DV, tm), lambda i: (0, i, 0, 0))),
        compiler_params=_params(("parallel",)),
        name="mla_proj",
    )(p, p, p, cos, sin, q_norm, w_uq, kv_norm, w_uk, w_uvt)


def _mla_attn_kernel(q_ref, k_ref, vt_ref, o_ref, st0, st1, mb0, mb1, m_ref, l_ref, acc_ref, *, bk, nd):
    qi = pl.program_id(1)
    slots = ((st0, mb0), (st1, mb1))
    bq = nd * bk
    n0 = nd * qi
    tri = lax.broadcasted_iota(jnp.int32, (bk, bk), 0) <= lax.broadcasted_iota(jnp.int32, (bk, bk), 1)

    gq = bk // 2

    def scores(t, slot, c0, tri_mask):
        st_ref, mb_ref = slots[slot]
        r0 = pl.multiple_of(t * bk, bk)
        st = _dot_nt(k_ref[0, pl.ds(r0, bk), :], q_ref[0, c0:c0 + gq, :])
        if tri_mask is not None:
            st = jnp.where(tri_mask, st, -jnp.inf)
        st_ref[:, c0:c0 + gq] = st
        mb_ref[:, c0:c0 + gq] = jnp.max(st, axis=0, keepdims=True)

    def update(t, slot, c0):
        st_ref, mb_ref = slots[slot]
        cols = slice(c0, c0 + gq)
        m = m_ref[:, cols]
        m_new = jnp.maximum(m, mb_ref[:, cols])
        alpha = jnp.exp2(m - m_new)
        pexp = jnp.exp2(st_ref[:, cols] - m_new)
        l_ref[:, cols] = alpha * l_ref[:, cols] + jnp.sum(pexp, axis=0, keepdims=True)
        acc_ref[:, cols] = alpha * acc_ref[:, cols] + _dot(vt_ref[0, t], pexp.astype(BF16))
        m_ref[:, cols] = m_new

    def sweep(ts=None, ss=None, ds=None, tu=None, su=None, du=None):
        for c0 in range(0, bq, gq):
            g = c0 // bk
            if ts is not None and (ds is None or g >= ds):
                scores(ts, ss, c0, tri[:, c0 - g * bk:c0 - g * bk + gq] if g == ds else None)
            if tu is not None and (du is None or g >= du):
                update(tu, su, c0)

    m_ref[...] = jnp.full(m_ref.shape, -jnp.inf, F32)
    l_ref[...] = jnp.zeros(l_ref.shape, F32)
    acc_ref[...] = jnp.zeros(acc_ref.shape, F32)

    @pl.when(qi == 0)
    def _():
        sweep(ts=0, ss=0, ds=0)

    @pl.when(qi > 0)
    def _():
        sweep(ts=0, ss=0)

    def pair(j):
        sweep(ts=2 * j + 1, ss=1, tu=2 * j, su=0)
        sweep(ts=2 * j + 2, ss=0, tu=2 * j + 1, su=1)

    assert nd % 4 == 0

    @pl.when(qi > 0)
    def _():
        pair(0)

    def two_pairs(j, carry):
        pair(2 * j + 1)
        pair(2 * j + 2)
        return carry

    lax.fori_loop(0, n0 // 4 - 1, two_pairs, 0)

    @pl.when(qi > 0)
    def _():
        sweep(ts=n0 - 1, ss=1, tu=n0 - 2, su=0)
        sweep(ts=n0, ss=0, ds=0, tu=n0 - 1, su=1)

    for d in range(1, nd):
        sweep(ts=n0 + d, ss=d % 2, ds=d, tu=n0 + d - 1, su=(d - 1) % 2, du=d - 1)
    sweep(tu=n0 + nd - 1, su=(nd - 1) % 2, du=nd - 1)
    o_ref[...] = (acc_ref[...] / l_ref[...]).T.astype(BF16)


def _mla_attn(q, k, vt):
    nh, s, _ = q.shape
    bk = vt.shape[-1]
    nd = 4 if s % (4 * bk) == 0 else 2
    bq = nd * bk
    assert s % bq == 0
    return pl.pallas_call(
        functools.partial(_mla_attn_kernel, bk=bk, nd=nd),
        out_shape=jax.ShapeDtypeStruct((s, nh * MLA_DV), BF16),
        grid=(nh, s // bq),
        in_specs=[pl.BlockSpec((1, bq, MLA_QK_PAD), lambda h, i: (h, i, 0)),
                  pl.BlockSpec((1, s, MLA_QK_PAD), lambda h, i: (h, 0, 0)),
                  pl.BlockSpec((1, s // bk, MLA_DV, bk), lambda h, i: (h, 0, 0, 0))],
        out_specs=pl.BlockSpec((bq, MLA_DV), lambda h, i: (i, h)),
        scratch_shapes=[pltpu.VMEM((bk, bq), F32), pltpu.VMEM((bk, bq), F32),
                        pltpu.VMEM((1, bq), F32), pltpu.VMEM((1, bq), F32),
                        pltpu.VMEM((1, bq), F32), pltpu.VMEM((1, bq), F32), pltpu.VMEM((MLA_DV, bq), F32)],
        compiler_params=_params(("parallel", "arbitrary")),
        name="mla_attn",
    )(q, k, vt)


def _swa_kernel(q_ref, k_ref, v_ref, kh_ref, vh_ref, sink_ref, o_ref, *, nblk):
    w = SWA_WINDOW
    g = SWA_GROUP
    i = pl.program_id(0)
    key = lax.broadcasted_iota(jnp.int32, (2 * w, g * w), 0)
    col = lax.broadcasted_iota(jnp.int32, (2 * w, g * w), 1)
    dist = col % w + w - key
    band = (dist >= 0) & (dist < w)
    first_band = band & ((key >= w) | (i > 0))
    dist_f = dist.astype(F32)
    biases = []
    for hk in range(SWA_KV_HEADS):
        slope = jnp.zeros((2 * w, g * w), F32)
        for gi, gq in enumerate(SWA_GORDER):
            slope = jnp.where(col // w == gi, 2.0 ** (-8.0 * (hk * g + gq + 1.0) / SWA_HEADS), slope)
        biases.append(-slope * dist_f)
    low = lax.broadcasted_iota(jnp.int32, (2 * w, 2 * SWA_DH), 1) < SWA_DH
    top = lax.broadcasted_iota(jnp.int32, (2 * SWA_DH, 1), 0) < SWA_DH
    for blk in range(nblk):
        rows = slice(blk * w, (blk + 1) * w)
        if blk == 0:
            k_prev, v_prev, valid = kh_ref[...], vh_ref[...], first_band
        else:
            k_prev, v_prev, valid = k_ref[(blk - 1) * w:blk * w, :], v_ref[(blk - 1) * w:blk * w, :], band
        kk = jnp.concatenate([k_prev, k_ref[rows, :]], axis=0).astype(F32)
        kk_sw = pltpu.roll(kk, SWA_DH, 1)
        vvt = jnp.concatenate([v_prev, v_ref[rows, :]], axis=0).astype(F32).T
        probs, inv_denoms = [], []
        for hk in range(SWA_KV_HEADS):
            c0 = hk * g * SWA_DH
            q2 = jnp.concatenate([q_ref[rows, c0:c0 + 2 * SWA_DH], q_ref[rows, c0 + 2 * SWA_DH:c0 + 4 * SWA_DH]],
                                 axis=0) * (SWA_DH ** -0.5)
            k_low = jnp.where(low, kk if hk == 0 else kk_sw, 0.0).astype(BF16)
            k_high = jnp.where(low, 0.0, kk_sw if hk == 0 else kk).astype(BF16)
            st = jnp.concatenate([_dot_nt(k_low, q2), _dot_nt(k_high, q2)], axis=1)
            st = jnp.where(valid, st + biases[hk], -jnp.inf)
            sink = sink_ref[hk]
            m = jnp.maximum(jnp.max(st, axis=0, keepdims=True), sink)
            pexp = jnp.exp(st - m)
            denom = jnp.sum(pexp, axis=0, keepdims=True) + jnp.exp(sink - m)
            probs.append(pexp.astype(BF16))
            inv_denoms.append(1.0 / denom)
        v0 = jnp.where(top, vvt, 0.0).astype(BF16)
        v1 = jnp.where(top, 0.0, vvt).astype(BF16)
        o_t = (_dot(v0, probs[0]) + _dot(v1, probs[1])) * jnp.where(top, inv_denoms[0], inv_denoms[1])
        for gi in range(g):
            o_ref[rows, gi * LANES:(gi + 1) * LANES] = o_t[:, gi * w:(gi + 1) * w].T.astype(BF16)


def _swa(p, sinks):
    s = p.shape[0]
    w = SWA_WINDOW
    rb = min(512, s)
    nblk = rb // w
    order = np.asarray([[hk * SWA_GROUP + gq for gq in SWA_GORDER] for hk in range(SWA_KV_HEADS)])
    sink_rows = jnp.repeat(sinks[order], w, axis=1).reshape(SWA_KV_HEADS, 1, SWA_GROUP * w)
    prev = lambda col: (lambda i: (jnp.maximum(i * nblk - 1, 0), col))
    return pl.pallas_call(
        functools.partial(_swa_kernel, nblk=nblk),
        out_shape=jax.ShapeDtypeStruct((s, SWA_OUT), BF16),
        grid=(s // rb,),
        in_specs=[pl.BlockSpec((rb, SWA_OUT), lambda i: (i, P_SWQ // SWA_OUT)),
                  pl.BlockSpec((rb, SWA_KV), lambda i: (i, P_SWK // SWA_KV)),
                  pl.BlockSpec((rb, SWA_KV), lambda i: (i, P_SWV // SWA_KV)),
                  pl.BlockSpec((w, SWA_KV), prev(P_SWK // SWA_KV)),
                  pl.BlockSpec((w, SWA_KV), prev(P_SWV // SWA_KV)),
                  pl.BlockSpec(sink_rows.shape, lambda i: (0, 0, 0))],
        out_specs=pl.BlockSpec((rb, SWA_OUT), lambda i: (i, 0)),
        compiler_params=_params(("parallel",)),
        name="swa",
    )(p, p, p, p, p, sink_rows)


def _outproj_kernel(oa_ref, ob_ref, oc_ref, w_ref, x_ref, pn_ref, g1_ref, fn_ref, sc_ref, sh_ref,
                    x1_ref, h2_ref, *, sub):
    a_w, b_w = oa_ref.shape[1], ob_ref.shape[1]
    for r0 in range(0, x_ref.shape[0], sub):
        rows = slice(r0, r0 + sub)
        mix = (_dot(oa_ref[rows, :], w_ref[0:a_w, :]) + _dot(ob_ref[rows, :], w_ref[a_w:a_w + b_w, :])
               + _dot(oc_ref[rows, :], w_ref[a_w + b_w:, :]))
        x1 = x_ref[rows, :] + g1_ref[...] * (_rms(mix) * pn_ref[...])
        x1_ref[rows, :] = x1
        h2_ref[rows, :] = _norm_mod(x1, fn_ref[...], sc_ref[...], sh_ref[...]).astype(BF16)


def _out_proj(o_a, o_b, o_c, w_out, layer, x, post_norm, gate1, ffn_norm, scale2, shift2):
    s, d = x.shape
    tm = min(512, s)
    row = lambda i: (i, 0)
    vec = pl.BlockSpec((1, d), lambda i: (0, 0))
    return pl.pallas_call(
        functools.partial(_outproj_kernel, sub=min(256, tm)),
        out_shape=(jax.ShapeDtypeStruct((s, d), F32), jax.ShapeDtypeStruct((s, d), BF16)),
        grid=(s // tm,),
        in_specs=[pl.BlockSpec((tm, o_a.shape[1]), row), pl.BlockSpec((tm, o_b.shape[1]), row),
                  pl.BlockSpec((tm, o_c.shape[1]), row),
                  pl.BlockSpec((None,) + w_out.shape[1:], lambda i: (layer, 0, 0), pipeline_mode=pl.Buffered(1)),
                  pl.BlockSpec((tm, d), row), vec, vec, vec, vec, vec],
        out_specs=(pl.BlockSpec((tm, d), row), pl.BlockSpec((tm, d), row)),
        compiler_params=_params(("parallel",)),
        name="out_proj",
    )(o_a, o_b, o_c, w_out, x, post_norm, gate1, ffn_norm, scale2, shift2)


def _gelu_tanh(x):
    return 0.5 * x * (1.0 + jnp.tanh(math.sqrt(2.0 / math.pi) * (x + 0.044715 * (x * x * x))))


def _ffn_up_kernel(h_ref, wg_ref, wu_ref, cg_ref, cu_ref, bg_ref, bu_ref, o_ref,
                   wg16_ref, wu16_ref, xg_ref, xu_ref, *, tm, th):
    @pl.when(pl.program_id(1) == 0)
    def _():
        wg16_ref[...] = wg_ref[...].astype(BF16)
        wu16_ref[...] = wu_ref[...].astype(BF16)
        xg_ref[0:8, :] = jnp.zeros((8, xg_ref.shape[1]), F32)
        xu_ref[0:8, :] = jnp.zeros((8, xu_ref.shape[1]), F32)

    def half_tile(r0):
        h = h_ref[r0:r0 + th, :]

        def conv(x_ref, w_ref, cw_ref, b_ref):
            x_ref[8:8 + th, :] = _dot(h, w_ref[...])
            y = b_ref[...] + cw_ref[2:3, :] * x_ref[8:8 + th, :]
            y = y + cw_ref[1:2, :] * x_ref[7:7 + th, :]
            y = y + cw_ref[0:1, :] * x_ref[6:6 + th, :]
            x_ref[0:8, :] = x_ref[th:th + 8, :]
            return y

        gate = conv(xg_ref, wg16_ref, cg_ref, bg_ref)
        up = conv(xu_ref, wu16_ref, cu_ref, bu_ref)
        o_ref[r0:r0 + th, :] = (_gelu_tanh(gate) * up).astype(BF16)

    for r0 in range(0, tm, th):
        half_tile(r0)


def _ffn_up(h2, w_up, layer, conv_w, conv_b):
    s, d = h2.shape
    d_ff = w_up.shape[2] // 2
    tm, tn = min(2048, s), 512
    th = min(1024, tm)
    nj = d_ff // tn
    lo = lambda j, i: (0, j)
    hi = lambda j, i: (0, j + nj)
    w_lo = pl.BlockSpec((None, d, tn), lambda j, i: (layer, 0, j))
    w_hi = pl.BlockSpec((None, d, tn), lambda j, i: (layer, 0, j + nj))
    return pl.pallas_call(
        functools.partial(_ffn_up_kernel, tm=tm, th=th),
        out_shape=jax.ShapeDtypeStruct((s, d_ff), BF16),
        grid=(nj, s // tm),
        in_specs=[pl.BlockSpec((tm, d), lambda j, i: (i, 0)),
                  w_lo, w_hi,
                  pl.BlockSpec((FFN_CONV, tn), lo), pl.BlockSpec((FFN_CONV, tn), hi),
                  pl.BlockSpec((1, tn), lo), pl.BlockSpec((1, tn), hi)],
        out_specs=pl.BlockSpec((tm, tn), lambda j, i: (i, j)),
        scratch_shapes=[pltpu.VMEM((d, tn), BF16), pltpu.VMEM((d, tn), BF16),
                        pltpu.VMEM((th + 8, tn), F32), pltpu.VMEM((th + 8, tn), F32)],
        compiler_params=_params(("parallel", "arbitrary")),
        name="ffn_up",
    )(h2, w_up, w_up, conv_w, conv_w, conv_b, conv_b)


def _ffn_down_kernel(g_ref, w_ref, x_ref, pn_ref, g2_ref, *rest, sub):
    for r0 in range(0, x_ref.shape[0], sub):
        rows = slice(r0, r0 + sub)
        y = _dot(g_ref[rows, :], w_ref[...])
        x2 = x_ref[rows, :] + g2_ref[...] * (_rms(y) * pn_ref[...])
        if len(rest) == 1:
            rest[0][rows, :] = x2
        else:
            nn_ref, sc_ref, sh_ref, x2_ref, hn_ref = rest
            x2_ref[rows, :] = x2
            hn_ref[rows, :] = _norm_mod(x2, nn_ref[...], sc_ref[...], sh_ref[...]).astype(BF16)


def _ffn_down(g, w_down, layer, x1, post_norm, gate2, next_pre_norm=None):
    s, d = x1.shape
    d_ff = g.shape[1]
    tm = min(512, s)
    row = lambda i: (i, 0)
    vec = pl.BlockSpec((1, d), lambda i: (0, 0))
    tile_f32 = jax.ShapeDtypeStruct((s, d), F32)
    tile = pl.BlockSpec((tm, d), row)
    last = next_pre_norm is None
    return pl.pallas_call(
        functools.partial(_ffn_down_kernel, sub=min(256, tm)),
        out_shape=tile_f32 if last else (tile_f32, jax.ShapeDtypeStruct((s, d), BF16)),
        grid=(s // tm,),
        in_specs=[pl.BlockSpec((tm, d_ff), row),
                  pl.BlockSpec((None, d_ff, d), lambda i: (layer, 0, 0), pipeline_mode=pl.Buffered(1)),
                  tile, vec, vec] + ([] if last else [vec, vec, vec]),
        out_specs=tile if last else (tile, tile),
        compiler_params=_params(("parallel",), VMEM_LIMIT_RESIDENT_MB),
        name="ffn_down",
    )(g, w_down, x1, post_norm, gate2, *(() if last else next_pre_norm))


def _layout_w_in(w_in):
    depth, d, _ = w_in.shape
    sizes = (GDN_QK, GDN_QK, GDN_V, GDN_V, GDN_HEADS, GDN_HEADS, MLA_Q_RANK, MLA_KV_RANK, MLA_ROPE,
             SWA_OUT, SWA_KV, SWA_KV)
    offs = np.concatenate([[0], np.cumsum(sizes)])
    part = lambda n: w_in[:, :, offs[n]:offs[n + 1]].astype(BF16)
    zeros = lambda n: jnp.zeros((depth, d, n), BF16)
    w_p = jnp.concatenate([w_in[:, :, :offs[4]].astype(BF16),
                           part(6), zeros(MLA_Q_RANK_PAD - MLA_Q_RANK),
                           part(9), part(7), part(8), zeros(LANES - MLA_ROPE), part(10), part(11)], axis=2)
    w_ab = jnp.concatenate([part(4), part(5), zeros(LANES - 2 * GDN_HEADS)], axis=2)
    return w_p, w_ab


def _layout_w_out(w_out):
    depth, d_mix, d = w_out.shape
    a = d_mix - SWA_OUT
    wc = w_out[:, a:].reshape(depth, SWA_KV_HEADS, SWA_GROUP, SWA_DH, d)[:, :, np.asarray(SWA_GORDER)]
    wc = wc.transpose(0, 2, 1, 3, 4).reshape(depth, SWA_OUT, d)
    return jnp.concatenate([w_out[:, :a].astype(BF16), wc.astype(BF16)], axis=1)


def _layout_mla(q_norm, w_uq, kv_norm, w_ukv):
    dqk = MLA_NOPE + MLA_ROPE
    uq = w_uq.reshape(MLA_Q_RANK, MLA_HEADS, dqk)
    uq = jnp.concatenate([uq[:, :, :MLA_NOPE].reshape(MLA_Q_RANK, -1), uq[:, :, MLA_NOPE:].reshape(MLA_Q_RANK, -1)],
                         axis=1)
    uq = jnp.pad(uq, ((0, MLA_Q_RANK_PAD - MLA_Q_RANK), (0, 0))).astype(BF16)
    qn = jnp.pad(q_norm, (0, MLA_Q_RANK_PAD - MLA_Q_RANK)).reshape(1, MLA_Q_RANK_PAD)
    ukv = w_ukv.reshape(MLA_KV_RANK, MLA_HEADS, MLA_NOPE + MLA_DV)
    uk = ukv[:, :, :MLA_NOPE].reshape(MLA_KV_RANK, -1).astype(BF16)
    uvt = ukv[:, :, MLA_NOPE:].reshape(MLA_KV_RANK, -1).T.astype(BF16)
    return qn, uq, kv_norm.reshape(1, MLA_KV_RANK), uk, uvt


def kernel(x, c, positions, ada_w, ada_b, mix_pre_norm, mix_post_norm, w_in, w_out, gdn_conv, gdn_a_log, gdn_dt_bias, gdn_norm, mla_q_norm, mla_w_uq, mla_kv_norm, mla_w_ukv, swa_sinks, ffn_pre_norm, ffn_post_norm, ffn_w_up, ffn_conv, ffn_conv_b, ffn_w_down):
    batch, s, d = x.shape
    assert batch == 1, "kernels are written for a single sequence"
    depth = ada_w.shape[0]
    xs = x.reshape(s, d)
    mod = _adaln_mod(c, ada_w, ada_b).reshape(depth, N_MOD, 1, d)
    cos, sin = _rope_tables(positions)
    vec = lambda a: a.reshape(1, d)

    w_p, w_ab = _layout_w_in(w_in)
    w_o = _layout_w_out(w_out)
    w_down = ffn_w_down.astype(BF16)

    h = _pre_norm(xs, vec(mix_pre_norm[0]), mod[0, 1], mod[0, 0])
    for l in range(depth):
        shift1, scale1, gate1, shift2, scale2, gate2 = (mod[l, n] for n in range(N_MOD))
        p, ab = _in_proj(h, w_p, w_ab, l)

        q_a, k_a, v_a, gcb, gct = _gdn_prep(p, ab, gdn_conv[l], gdn_a_log[l], gdn_dt_bias[l])
        u, w, qd, kt, qk = _gdn_local(q_a, k_a, v_a, gcb, gct)
        o_a = _gdn_scan(u, w, qd, kt, qk, gcb, p, gdn_norm[l])

        q_b, k_b, v_b = _mla_proj(p, cos, sin, *_layout_mla(mla_q_norm[l], mla_w_uq[l], mla_kv_norm[l],
                                                            mla_w_ukv[l]))
        o_b = _mla_attn(q_b, k_b, v_b)

        o_c = _swa(p, swa_sinks[l])

        x1, h2 = _out_proj(o_a, o_b, o_c, w_o, l, xs, vec(mix_post_norm[l]), gate1,
                           vec(ffn_pre_norm[l]), scale2, shift2)
        g = _ffn_up(h2, ffn_w_up, l, ffn_conv[l], ffn_conv_b[l].reshape(1, -1))
        if l + 1 < depth:
            xs, h = _ffn_down(g, w_down, l, x1, vec(ffn_post_norm[l]), gate2,
                              (vec(mix_pre_norm[l + 1]), mod[l + 1, 1], mod[l + 1, 0]))
        else:
            xs = _ffn_down(g, w_down, l, x1, vec(ffn_post_norm[l]), gate2)
    return xs.reshape(batch, s, d)
```
